```python
import math
import jax, jax.numpy as jnp
from jax import lax
import numpy as np

D_MODEL = 4096
BATCH = 4
SEQ = 2048
DEPTH = 2
DEC_BATCH = 8
DEC_SEQ = 1
PAST_LEN = 16384
PAGE_SIZE = 128

N_EVEN = (DEPTH + 1) // 2
N_ODD = DEPTH // 2
EPS = 1e-6
NEG_INF = -1e30
F32 = jnp.float32

HG_HEADS = 16
HG_DK = 128
HG_DV = (D_MODEL // 2) // HG_HEADS
HG_QK = HG_HEADS * HG_DK
HG_WIDTH = HG_HEADS * HG_DV
HG_CHUNK = 64

NSA_HEADS = 16
NSA_KVH = 2
NSA_G = NSA_HEADS // NSA_KVH
NSA_HD = (D_MODEL // 2) // NSA_HEADS
NSA_WIDTH = NSA_HEADS * NSA_HD
NSA_KVW = NSA_KVH * NSA_HD
CMP_BLOCK = 32
CMP_HID = 256
SEL_BLOCK = 64
N_SELECT = 16
WINDOW = 512
SEL_QBLOCK = 64
WIN_QBLOCK = 128
NUM_BUCKETS = 32
MAX_DISTANCE = 1024

RET_HEADS = 16
RET_DK = D_MODEL // RET_HEADS
RET_DV = 2 * RET_DK
RET_QK = RET_HEADS * RET_DK
RET_V = RET_HEADS * RET_DV
RET_CHUNK = 128
ROPE_BASE = 10000.0

MEM_LEN = 256
MEM_HEADS = 4
MEM_HD = 128
MEM_W = MEM_HEADS * MEM_HD

EVEN_SPLITS = (HG_QK, HG_QK, HG_WIDTH, HG_WIDTH, NSA_WIDTH, 6 * NSA_KVW, 3 * NSA_HEADS, NSA_WIDTH)
EVEN_IN = sum(EVEN_SPLITS)
EVEN_OUT = HG_WIDTH + NSA_WIDTH
ODD_SPLITS = (RET_QK, RET_QK, RET_V, RET_V)
ODD_IN = sum(ODD_SPLITS)

kernel_name = "hybrid_hgrn2_nsa_retention_step"


def split_cols(x, sizes):
    return jnp.split(x, [int(c) for c in np.cumsum(sizes)[:-1]], axis=-1)


def rmsnorm(x, g):
    xf = x.astype(F32)
    y = xf * lax.rsqrt(jnp.mean(xf * xf, axis=-1, keepdims=True) + EPS)
    return (y * g.astype(F32)).astype(x.dtype)


def head_rmsnorm(o, g):
    B, T, H, D = o.shape
    y = o * lax.rsqrt(jnp.mean(o * o, axis=-1, keepdims=True) + EPS)
    return y.reshape(B, T, H * D) * g.astype(F32)


def head_groupnorm(o, g):
    B, T, H, D = o.shape
    c = o - jnp.mean(o, axis=-1, keepdims=True)
    y = c * lax.rsqrt(jnp.mean(c * c, axis=-1, keepdims=True) + EPS)
    return y.reshape(B, T, H * D) * g.astype(F32)


def masked_softmax(s, valid):
    p = jax.nn.softmax(jnp.where(valid, s, NEG_INF), axis=-1)
    return p * valid.astype(F32)


def rel_bucket(dist):
    d = jnp.maximum(dist, 0)
    me = NUM_BUCKETS // 2
    logd = jnp.log(jnp.maximum(d, 1).astype(F32) / me)
    large = me + (logd / math.log(MAX_DISTANCE / me) * (NUM_BUCKETS - me)).astype(jnp.int32)
    return jnp.where(d < me, d, jnp.minimum(large, NUM_BUCKETS - 1))


def heads_bias(table, bucket):
    b = table.astype(F32)[bucket]
    return b.reshape(bucket.shape + (NSA_KVH, NSA_G)).transpose(2, 3, 0, 1)


def rotary(x, pos):
    half = x.shape[-1] // 2
    inv = ROPE_BASE ** (-jnp.arange(half, dtype=F32) / half)
    ang = pos.astype(F32)[:, None] * inv[None, :]
    cos = jnp.cos(ang)[None, :, None, :]
    sin = jnp.sin(ang)[None, :, None, :]
    x1, x2 = x[..., :half], x[..., half:]
    return jnp.concatenate([x1 * cos - x2 * sin, x1 * sin + x2 * cos], axis=-1)


def chunked_scan(step, S0, xs, chunk):
    B, T = xs[0].shape[:2]
    nc = T // chunk
    xs_c = tuple(jnp.moveaxis(a.reshape((B, nc, chunk) + a.shape[2:]), 1, 0) for a in xs)
    S, o = lax.scan(lambda St, c: step(St, *c), S0, xs_c)
    o = jnp.moveaxis(o, 0, 1)
    return o.reshape((B, T) + o.shape[3:]), S


def hgrn_prep(hq, hf, hv, lb):
    B, T, _ = hq.shape
    f = lb[None, None, :] + (1.0 - lb[None, None, :]) * jax.nn.sigmoid(hf.astype(F32))
    q = hq.astype(F32).reshape(B, T, HG_HEADS, HG_DK) * HG_DK ** -0.5
    logf = jnp.log(f).reshape(B, T, HG_HEADS, HG_DK)
    k = (1.0 - f).reshape(B, T, HG_HEADS, HG_DK)
    v = hv.astype(F32).reshape(B, T, HG_HEADS, HG_DV)
    return q, logf, k, v


def hgrn_chunk(S0, q, logf, k, v):
    C = q.shape[1]
    b = jnp.cumsum(logf, axis=1)
    o_inter = jnp.einsum('bthd,bhde->bthe', q * jnp.exp(b), S0)
    causal = jnp.tril(jnp.ones((C, C), dtype=bool))
    diff = b[:, :, None] - b[:, None, :]
    decay = jnp.exp(jnp.where(causal[None, :, :, None, None], diff, -jnp.inf))
    a = jnp.einsum('bthd,bshd,btshd->bhts', q, k, decay)
    o_intra = jnp.einsum('bhts,bshe->bthe', a, v)
    b_last = b[:, -1]
    S_new = jnp.exp(b_last)[..., None] * S0 + jnp.einsum('bshd,bshe->bhde', k * jnp.exp(b_last[:, None] - b), v)
    return S_new, o_inter + o_intra


def hgrn_out(o, gain, gate):
    return head_rmsnorm(o, gain) * jax.nn.silu(gate.astype(F32))


def nsa_prep(nq, kvs):
    B, T, _ = nq.shape
    q = nq.reshape(B, T, NSA_KVH, NSA_G, NSA_HD) * NSA_HD ** -0.5
    kv = [a.reshape(B, T, NSA_KVH, NSA_HD) for a in split_cols(kvs, (NSA_KVW,) * 6)]
    return q, kv


def compress(k, pe, w1, w2):
    B, L = k.shape[:2]
    n_cb = L // CMP_BLOCK
    kb = k[:, : n_cb * CMP_BLOCK].reshape(B, n_cb, CMP_BLOCK, NSA_KVH, NSA_HD) + pe[None, None, :, None, :]
    kb = kb.transpose(0, 1, 3, 2, 4).reshape(B, n_cb, NSA_KVH, CMP_BLOCK * NSA_HD)
    return jax.nn.silu(kb @ w1) @ w2


def nsa_compressed(q, qpos, kc, vc, table):
    n_cb = kc.shape[1]
    end = (jnp.arange(n_cb, dtype=jnp.int32) + 1) * CMP_BLOCK - 1
    dist = qpos[:, None] - end[None, :]
    s = jnp.einsum('bqkgd,bnkd->bkgqn', q, kc).astype(F32) + heads_bias(table, rel_bucket(dist))
    p = masked_softmax(s, dist >= 0)
    o = jnp.einsum('bkgqn,bnkd->bqkgd', p.astype(vc.dtype), vc)
    return o, p


def select_blocks(p, qpos, L):
    B, KVH, G, Tq, n_cb = p.shape
    ratio = SEL_BLOCK // CMP_BLOCK
    n_sel = -(-L // SEL_BLOCK)
    imp = jnp.pad(p.sum(2), ((0, 0), (0, 0), (0, 0), (0, n_sel * ratio - n_cb)))
    imp = imp.reshape(B, KVH, Tq, n_sel, ratio).sum(-1)
    blk = jnp.arange(n_sel, dtype=jnp.int32)[None, :]
    cur = (qpos // SEL_BLOCK)[:, None]
    forced = (blk == 0) | (blk == cur) | (blk == cur - 1)
    valid = blk * SEL_BLOCK <= qpos[:, None]
    score = jnp.where(valid, jnp.where(forced, jnp.inf, imp), -jnp.inf)
    _, idx = lax.top_k(score, min(N_SELECT, n_sel))
    return idx


def pad_blocks(k):
    L = k.shape[1]
    n_sel = -(-L // SEL_BLOCK)
    return jnp.pad(k, ((0, 0), (0, n_sel * SEL_BLOCK - L), (0, 0), (0, 0)))


def nsa_selected(q, qpos, idx, ks, vs, table):
    B, Tq = q.shape[:2]
    n = idx.shape[-1] * SEL_BLOCK
    tok = (idx[..., None] * SEL_BLOCK + jnp.arange(SEL_BLOCK, dtype=jnp.int32)).reshape(B, NSA_KVH, Tq, n)
    bi = jnp.arange(B)[:, None, None, None]
    hi = jnp.arange(NSA_KVH)[None, :, None, None]
    kg = ks.transpose(0, 2, 1, 3)[bi, hi, tok]
    vg = vs.transpose(0, 2, 1, 3)[bi, hi, tok]
    dist = qpos[None, None, :, None] - tok
    bias = table.astype(F32).reshape(NUM_BUCKETS, NSA_KVH, NSA_G)[rel_bucket(dist), hi]
    s = jnp.einsum('bqkgd,bkqsd->bkgqs', q, kg).astype(F32) + jnp.moveaxis(bias, 4, 2)
    p = masked_softmax(s, (dist >= 0)[:, :, None])
    return jnp.einsum('bkgqs,bkqsd->bqkgd', p.astype(vg.dtype), vg)


def window_banded(q, kw, vw, table):
    B, T = q.shape[:2]
    nb = T // WIN_QBLOCK
    span = WIN_QBLOCK + WINDOW
    pad = ((0, 0), (WINDOW, 0), (0, 0), (0, 0))
    gidx = jnp.arange(nb, dtype=jnp.int32)[:, None] * WIN_QBLOCK + jnp.arange(span, dtype=jnp.int32)[None, :]
    kb = jnp.pad(kw, pad)[:, gidx]
    vb = jnp.pad(vw, pad)[:, gidx]
    qb = q.reshape(B, nb, WIN_QBLOCK, NSA_KVH, NSA_G, NSA_HD)
    dist = jnp.arange(WIN_QBLOCK, dtype=jnp.int32)[:, None] + WINDOW - jnp.arange(span, dtype=jnp.int32)[None, :]
    kpos = gidx - WINDOW
    valid = (dist >= 0) & (dist < WINDOW) & (kpos[:, None, :] >= 0)
    s = jnp.einsum('bnqkgd,bnskd->bnkgqs', qb, kb).astype(F32) + heads_bias(table, rel_bucket(dist))
    p = masked_softmax(s, valid[None, :, None, None])
    o = jnp.einsum('bnkgqs,bnskd->bnqkgd', p.astype(vb.dtype), vb)
    return o.reshape(B, T, NSA_KVH, NSA_G, NSA_HD)


def window_dense(q, qpos, kw, vw, kpos, table):
    dist = qpos[:, None] - kpos[None, :]
    valid = (dist >= 0) & (dist < WINDOW) & (kpos[None, :] >= 0)
    s = jnp.einsum('bqkgd,bskd->bkgqs', q, kw).astype(F32) + heads_bias(table, rel_bucket(dist))
    p = masked_softmax(s, valid)
    return jnp.einsum('bkgqs,bskd->bqkgd', p.astype(vw.dtype), vw)


def nsa_combine(o_c, o_s, o_w, gate_logits, path):
    B, T = o_c.shape[:2]
    g = jax.nn.sigmoid(gate_logits.astype(F32)).reshape(B, T, NSA_KVH, NSA_G, 3)
    o = g[..., 0:1] * o_c.astype(F32) + g[..., 1:2] * o_s.astype(F32) + g[..., 2:3] * o_w.astype(F32)
    return o.reshape(B, T, NSA_WIDTH) * jax.nn.silu(path.astype(F32))


def gather_pages(pool, page_table):
    B, NP = page_table.shape
    g = pool[page_table]
    return g.reshape((B, NP * PAGE_SIZE) + pool.shape[2:])


def even_prompt(h, w_in, w_out, lb, hg_gain, pe, w1k, w2k, w1v, w2v, table):
    B, T, _ = h.shape
    pos = jnp.arange(T, dtype=jnp.int32)
    hq, hf, hv, hgate, nq, kvs, ngate, npath = split_cols(h @ w_in, EVEN_SPLITS)
    q, logf, k, v = hgrn_prep(hq, hf, hv, lb)
    S0 = jnp.zeros((B, HG_HEADS, HG_DK, HG_DV), F32)
    o_hg, S = chunked_scan(hgrn_chunk, S0, (q, logf, k, v), HG_CHUNK)
    y_hg = hgrn_out(o_hg, hg_gain, hgate)
    qn, (kc, vc, ks, vs, kw, vw) = nsa_prep(nq, kvs)
    o_c, p_c = nsa_compressed(qn, pos, compress(kc, pe, w1k, w2k), compress(vc, pe, w1v, w2v), table)
    idx = select_blocks(p_c, pos, T)
    ks_p, vs_p = pad_blocks(ks), pad_blocks(vs)
    nqb = T // SEL_QBLOCK
    qb = jnp.moveaxis(qn.reshape(B, nqb, SEL_QBLOCK, NSA_KVH, NSA_G, NSA_HD), 1, 0)
    pb = pos.reshape(nqb, SEL_QBLOCK)
    ib = jnp.moveaxis(idx.reshape(B, NSA_KVH, nqb, SEL_QBLOCK, idx.shape[-1]), 2, 0)
    o_s = lax.map(lambda a: nsa_selected(a[0], a[1], a[2], ks_p, vs_p, table), (qb, pb, ib))
    o_s = jnp.moveaxis(o_s, 0, 1).reshape(B, T, NSA_KVH, NSA_G, NSA_HD)
    o_w = window_banded(qn, kw, vw, table)
    y_nsa = nsa_combine(o_c, o_s, o_w, ngate, npath)
    y = jnp.concatenate([y_hg, y_nsa], axis=-1).astype(h.dtype) @ w_out
    wb = min(WINDOW, T)
    return y, S, kc, vc, ks, vs, kw[:, T - wb:], vw[:, T - wb:]


def even_sample(h, pos, past_len, S0, page_table, pool_ck, pool_cv, pool_sk, pool_sv, win_k, win_v,
                w_in, w_out, lb, hg_gain, pe, w1k, w2k, w1v, w2v, table):
    B, T, _ = h.shape
    hq, hf, hv, hgate, nq, kvs, ngate, npath = split_cols(h @ w_in, EVEN_SPLITS)
    q, logf, k, v = hgrn_prep(hq, hf, hv, lb)
    o_hg, S = chunked_scan(hgrn_chunk, S0.astype(F32), (q, logf, k, v), T)
    y_hg = hgrn_out(o_hg, hg_gain, hgate)
    qn, (kc, vc, ks, vs, kw, vw) = nsa_prep(nq, kvs)
    kc_all = jnp.concatenate([gather_pages(pool_ck, page_table).astype(kc.dtype), kc], axis=1)
    vc_all = jnp.concatenate([gather_pages(pool_cv, page_table).astype(vc.dtype), vc], axis=1)
    ks_all = jnp.concatenate([gather_pages(pool_sk, page_table).astype(ks.dtype), ks], axis=1)
    vs_all = jnp.concatenate([gather_pages(pool_sv, page_table).astype(vs.dtype), vs], axis=1)
    L = kc_all.shape[1]
    o_c, p_c = nsa_compressed(qn, pos, compress(kc_all, pe, w1k, w2k), compress(vc_all, pe, w1v, w2v), table)
    idx = select_blocks(p_c, pos, L)
    o_s = nsa_selected(qn, pos, idx, pad_blocks(ks_all), pad_blocks(vs_all), table)
    wb = win_k.shape[1]
    kw_all = jnp.concatenate([win_k.astype(kw.dtype), kw], axis=1)
    vw_all = jnp.concatenate([win_v.astype(vw.dtype), vw], axis=1)
    kpos = past_len - wb + jnp.arange(wb + T, dtype=jnp.int32)
    o_w = window_dense(qn, pos, kw_all, vw_all, kpos, table)
    y_nsa = nsa_combine(o_c, o_s, o_w, ngate, npath)
    y = jnp.concatenate([y_hg, y_nsa], axis=-1).astype(h.dtype) @ w_out
    return y, S, kc, vc, ks, vs, kw_all[:, T:], vw_all[:, T:]


def retention_chunk(S0, q, k, v, log_gamma):
    C = q.shape[1]
    i = jnp.arange(C, dtype=F32)
    diff = i[:, None] - i[None, :]
    decay = jnp.where(diff[None] >= 0, jnp.exp(diff[None] * log_gamma[:, None, None]), 0.0)
    a = jnp.einsum('bthd,bshd->bhts', q, k) * decay[None]
    inner = jnp.einsum('bhts,bshe->bthe', a, v)
    q_dec = q * jnp.exp((i + 1.0)[:, None] * log_gamma[None, :])[None, :, :, None]
    cross = jnp.einsum('bthd,bhde->bthe', q_dec, S0)
    k_dec = k * jnp.exp((C - 1.0 - i)[:, None] * log_gamma[None, :])[None, :, :, None]
    S_new = jnp.exp(C * log_gamma)[None, :, None, None] * S0 + jnp.einsum('bshd,bshe->bhde', k_dec, v)
    return S_new, inner + cross


def retention_layer(h, pos, S0, w_in, w_out, gain, chunk):
    B, T, _ = h.shape
    rq, rk, rv, rg = split_cols(h @ w_in, ODD_SPLITS)
    q = rotary(rq.astype(F32).reshape(B, T, RET_HEADS, RET_DK), pos) * RET_DK ** -0.5
    k = rotary(rk.astype(F32).reshape(B, T, RET_HEADS, RET_DK), pos)
    v = rv.astype(F32).reshape(B, T, RET_HEADS, RET_DV)
    log_gamma = jnp.log1p(-jnp.exp2(-5.0 - jnp.arange(RET_HEADS, dtype=F32)))
    o, S = chunked_scan(lambda St, a, b, c: retention_chunk(St, a, b, c, log_gamma), S0, (q, k, v), chunk)
    y = head_groupnorm(o, gain) * jax.nn.silu(rg.astype(F32))
    return y.astype(h.dtype) @ w_out, S


def mem_attend(h, mk, mv, w_q, w_o):
    B, T, _ = h.shape
    qh, gh = split_cols(h @ w_q, (MEM_W, MEM_W))
    q = qh.reshape(B, T, MEM_HEADS, MEM_HD) * MEM_HD ** -0.5
    s = jnp.einsum('bqhd,bmhd->bhqm', q, mk).astype(F32)
    p = jax.nn.softmax(s, axis=-1)
    o = jnp.einsum('bhqm,bmhd->bqhd', p.astype(mv.dtype), mv).reshape(B, T, MEM_W)
    return (o.astype(F32) * jax.nn.silu(gh.astype(F32))).astype(h.dtype) @ w_o


def setup_inputs(seed: int = 0) -> dict:
    key = jax.random.key(seed)
    n_pages = PAST_LEN // PAGE_SIZE
    n_pool = (DEC_BATCH * n_pages * 5) // 4
    wb = min(WINDOW, PAST_LEN)
    ks = jax.random.split(key, 48)
    ctr = [0]

    def nrm(shape, scale=1.0):
        kk = ks[ctr[0]]
        ctr[0] += 1
        return jax.random.normal(kk, shape, F32) * scale

    def gain(shape):
        return 1.0 + 0.1 * nrm(shape)

    page_table = jax.random.permutation(ks[47], n_pool)[: DEC_BATCH * n_pages].reshape(DEC_BATCH, n_pages).astype(jnp.int32)
    return {
        'x_prompt': nrm((BATCH, SEQ, D_MODEL)),
        'x_sample': nrm((DEC_BATCH, DEC_SEQ, D_MODEL)),
        'mem_prompt': nrm((BATCH, MEM_LEN, D_MODEL)),
        'state_hgrn': nrm((N_EVEN, DEC_BATCH, HG_HEADS, HG_DK, HG_DV)),
        'cache_cmp_k': nrm((N_EVEN, n_pool, PAGE_SIZE, NSA_KVH, NSA_HD)),
        'cache_cmp_v': nrm((N_EVEN, n_pool, PAGE_SIZE, NSA_KVH, NSA_HD)),
        'cache_sel_k': nrm((N_EVEN, n_pool, PAGE_SIZE, NSA_KVH, NSA_HD)),
        'cache_sel_v': nrm((N_EVEN, n_pool, PAGE_SIZE, NSA_KVH, NSA_HD)),
        'cache_win_k': nrm((N_EVEN, DEC_BATCH, wb, NSA_KVH, NSA_HD)),
        'cache_win_v': nrm((N_EVEN, DEC_BATCH, wb, NSA_KVH, NSA_HD)),
        'state_ret': nrm((N_ODD, DEC_BATCH, RET_HEADS, RET_DK, RET_DV)),
        'cache_mem_k': nrm((DEPTH, DEC_BATCH, MEM_LEN, MEM_HEADS, MEM_HD)),
        'cache_mem_v': nrm((DEPTH, DEC_BATCH, MEM_LEN, MEM_HEADS, MEM_HD)),
        'page_table': page_table,
        'rel_table': nrm((NUM_BUCKETS, NSA_HEADS), 0.5),
        'norm_mix_pre': gain((DEPTH, D_MODEL)),
        'norm_mix_post': gain((DEPTH, D_MODEL)),
        'norm_mem_pre': gain((DEPTH, D_MODEL)),
        'norm_mem_post': gain((DEPTH, D_MODEL)),
        'ev_w_in': nrm((N_EVEN, D_MODEL, EVEN_IN), D_MODEL ** -0.5),
        'ev_w_out': nrm((N_EVEN, EVEN_OUT, D_MODEL), EVEN_OUT ** -0.5),
        'hgrn_lb': nrm((N_EVEN + 1, HG_QK)),
        'hgrn_norm': gain((N_EVEN, HG_WIDTH)),
        'cmp_pe': nrm((N_EVEN, CMP_BLOCK, NSA_HD), 0.5),
        'cmp_w1_k': nrm((N_EVEN, CMP_BLOCK * NSA_HD, CMP_HID), (CMP_BLOCK * NSA_HD) ** -0.5),
        'cmp_w2_k': nrm((N_EVEN, CMP_HID, NSA_HD), CMP_HID ** -0.5),
        'cmp_w1_v': nrm((N_EVEN, CMP_BLOCK * NSA_HD, CMP_HID), (CMP_BLOCK * NSA_HD) ** -0.5),
        'cmp_w2_v': nrm((N_EVEN, CMP_HID, NSA_HD), CMP_HID ** -0.5),
        'od_w_in': nrm((N_ODD, D_MODEL, ODD_IN), D_MODEL ** -0.5),
        'od_w_out': nrm((N_ODD, RET_V, D_MODEL), RET_V ** -0.5),
        'ret_norm': gain((N_ODD, RET_V)),
        'mem_w_q': nrm((DEPTH, D_MODEL, 2 * MEM_W), D_MODEL ** -0.5),
        'mem_w_k': nrm((DEPTH, D_MODEL, MEM_W), D_MODEL ** -0.5),
        'mem_w_v': nrm((DEPTH, D_MODEL, MEM_W), D_MODEL ** -0.5),
        'mem_w_o': nrm((DEPTH, MEM_W, D_MODEL), MEM_W ** -0.5),
    }


def reference(x_prompt, x_sample, mem_prompt, state_hgrn, cache_cmp_k, cache_cmp_v, cache_sel_k, cache_sel_v,
              cache_win_k, cache_win_v, state_ret, cache_mem_k, cache_mem_v, page_table, rel_table,
              norm_mix_pre, norm_mix_post, norm_mem_pre, norm_mem_post, ev_w_in, ev_w_out, hgrn_lb, hgrn_norm,
              cmp_pe, cmp_w1_k, cmp_w2_k, cmp_w1_v, cmp_w2_v, od_w_in, od_w_out, ret_norm,
              mem_w_q, mem_w_k, mem_w_v, mem_w_o):
    B, T, _ = x_prompt.shape
    Ts = x_sample.shape[1]
    past_len = page_table.shape[1] * PAGE_SIZE
    pos_p = jnp.arange(T, dtype=jnp.int32)
    pos_s = past_len + jnp.arange(Ts, dtype=jnp.int32)
    lb_all = jnp.cumsum(jax.nn.softmax(hgrn_lb.astype(F32), axis=0), axis=0)
    even_p, even_s, ret_p, ret_s, memk_p, memv_p = [], [], [], [], [], []
    xp, xs = x_prompt, x_sample
    for l in range(DEPTH):
        hp = rmsnorm(xp, norm_mix_pre[l])
        hs = rmsnorm(xs, norm_mix_pre[l])
        if l % 2 == 0:
            e = l // 2
            w = (ev_w_in[e], ev_w_out[e], lb_all[e], hgrn_norm[e], cmp_pe[e],
                 cmp_w1_k[e], cmp_w2_k[e], cmp_w1_v[e], cmp_w2_v[e], rel_table)
            yp, *sp = even_prompt(hp, *w)
            ys, *ss = even_sample(hs, pos_s, past_len, state_hgrn[e], page_table, cache_cmp_k[e], cache_cmp_v[e],
                                  cache_sel_k[e], cache_sel_v[e], cache_win_k[e], cache_win_v[e], *w)
            even_p.append(sp)
            even_s.append(ss)
        else:
            o = l // 2
            S0p = jnp.zeros((B, RET_HEADS, RET_DK, RET_DV), F32)
            yp, sp = retention_layer(hp, pos_p, S0p, od_w_in[o], od_w_out[o], ret_norm[o], RET_CHUNK)
            ys, ss = retention_layer(hs, pos_s, state_ret[o].astype(F32), od_w_in[o], od_w_out[o], ret_norm[o], Ts)
            ret_p.append(sp)
            ret_s.append(ss)
        xp = xp + rmsnorm(yp, norm_mix_post[l])
        xs = xs + rmsnorm(ys, norm_mix_post[l])
        mk_p = (mem_prompt @ mem_w_k[l]).reshape(B, mem_prompt.shape[1], MEM_HEADS, MEM_HD)
        mv_p = (mem_prompt @ mem_w_v[l]).reshape(B, mem_prompt.shape[1], MEM_HEADS, MEM_HD)
        memk_p.append(mk_p)
        memv_p.append(mv_p)
        hp = rmsnorm(xp, norm_mem_pre[l])
        hs = rmsnorm(xs, norm_mem_pre[l])
        yp = mem_attend(hp, mk_p, mv_p, mem_w_q[l], mem_w_o[l])
        ys = mem_attend(hs, cache_mem_k[l], cache_mem_v[l], mem_w_q[l], mem_w_o[l])
        xp = xp + rmsnorm(yp, norm_mem_post[l])
        xs = xs + rmsnorm(ys, norm_mem_post[l])
    p_hgrn, p_cmp_k, p_cmp_v, p_sel_k, p_sel_v, p_win_k, p_win_v = [jnp.stack(a) for a in zip(*even_p)]
    s_hgrn, s_cmp_k, s_cmp_v, s_sel_k, s_sel_v, s_win_k, s_win_v = [jnp.stack(a) for a in zip(*even_s)]
    p_ret = jnp.stack(ret_p)
    s_ret = jnp.stack(ret_s)
    p_mem_k = jnp.stack(memk_p)
    p_mem_v = jnp.stack(memv_p)
    return (xp, xs, p_hgrn, p_cmp_k, p_cmp_v, p_sel_k, p_sel_v, p_win_k, p_win_v, p_ret, p_mem_k, p_mem_v,
            s_hgrn, s_cmp_k, s_cmp_v, s_sel_k, s_sel_v, s_win_k, s_win_v, s_ret)
```

```python
import functools
import math

import jax
import jax.numpy as jnp
import numpy as np
from jax import lax
from jax.experimental import pallas as pl
from jax.experimental.pallas import tpu as pltpu

F32 = jnp.float32
BF16 = jnp.bfloat16
EPS = 1e-6
NEG_INF = -1e30

PAGE_SIZE = 128
HG_HEADS = 16
HG_DK = 128
HG_CHUNK = 64
NSA_HEADS = 16
NSA_KVH = 2
NSA_G = NSA_HEADS // NSA_KVH
CMP_BLOCK = 32
SEL_BLOCK = 64
N_SELECT = 16
WINDOW = 512
SEL_QBLOCK = 64
WIN_QBLOCK = 128
NUM_BUCKETS = 32
MAX_DISTANCE = 1024
RET_HEADS = 16
RET_CHUNK = 128
ROPE_BASE = 10000.0
MEM_HEADS = 4
MEM_HD = 128
MEM_W = MEM_HEADS * MEM_HD

VMEM_LIMIT_BYTES = 56 * 1024 * 1024


def _params(*sem):
    return pltpu.CompilerParams(dimension_semantics=sem, vmem_limit_bytes=VMEM_LIMIT_BYTES)


def _rmsnorm_kernel(x_ref, g_ref, o_ref):
    x = x_ref[...]
    y = x * lax.rsqrt(jnp.mean(x * x, axis=-1, keepdims=True) + EPS)
    o_ref[...] = (y * g_ref[...]).astype(o_ref.dtype)


def rmsnorm_bf16(x, g):
    M, D = x.shape
    tm = min(M, 512)
    return pl.pallas_call(
        _rmsnorm_kernel,
        grid=(M // tm,),
        in_specs=[pl.BlockSpec((tm, D), lambda i: (i, 0)), pl.BlockSpec((1, D), lambda i: (0, 0))],
        out_specs=pl.BlockSpec((tm, D), lambda i: (i, 0)),
        out_shape=jax.ShapeDtypeStruct((M, D), BF16),
        compiler_params=_params("parallel"),
        name="rmsnorm_bf16",
    )(x, g.reshape(1, D))


def _post_kernel(x_ref, y_ref, g_ref, o_ref):
    y = y_ref[...]
    n = y * lax.rsqrt(jnp.mean(y * y, axis=-1, keepdims=True) + EPS)
    o_ref[...] = x_ref[...] + n * g_ref[...]


def residual_post(x, y, g):
    M, D = x.shape
    tm = min(M, 512)
    return pl.pallas_call(
        _post_kernel,
        grid=(M // tm,),
        in_specs=[pl.BlockSpec((tm, D), lambda i: (i, 0)), pl.BlockSpec((tm, D), lambda i: (i, 0)),
                  pl.BlockSpec((1, D), lambda i: (0, 0))],
        out_specs=pl.BlockSpec((tm, D), lambda i: (i, 0)),
        out_shape=jax.ShapeDtypeStruct((M, D), F32),
        compiler_params=_params("parallel"),
        name="residual_post",
    )(x, y, g.reshape(1, D))


def _matmul_kernel(x_ref, w_ref, o_ref, wb_ref):
    @pl.when(pl.program_id(1) == 0)
    def _():
        wb_ref[...] = w_ref[...].astype(BF16)

    o_ref[...] = jnp.dot(x_ref[...], wb_ref[...], preferred_element_type=F32)


def matmul(x, w, n_cols=None):
    M, K = x.shape
    N = w.shape[1] if n_cols is None else n_cols
    tn = 512 if K <= 4096 else 256
    tn = min(tn, N)
    tm = min(M, 1024 if K <= 4096 else 512)
    assert N % tn == 0 and M % tm == 0, (M, K, N)
    return pl.pallas_call(
        _matmul_kernel,
        grid=(N // tn, M // tm),
        in_specs=[pl.BlockSpec((tm, K), lambda n, m: (m, 0)), pl.BlockSpec((K, tn), lambda n, m: (0, n))],
        out_specs=pl.BlockSpec((tm, tn), lambda n, m: (m, n)),
        out_shape=jax.ShapeDtypeStruct((M, N), F32),
        scratch_shapes=[pltpu.VMEM((K, tn), BF16)],
        compiler_params=_params("parallel", "arbitrary"),
        name="matmul",
    )(x, w)


def split_cols(x, sizes):
    return jnp.split(x, [int(c) for c in np.cumsum(sizes)[:-1]], axis=-1)


def head_rmsnorm(o, g):
    B, T, H, D = o.shape
    y = o * lax.rsqrt(jnp.mean(o * o, axis=-1, keepdims=True) + EPS)
    return y.reshape(B, T, H * D) * g.astype(F32)


def head_groupnorm(o, g):
    B, T, H, D = o.shape
    c = o - jnp.mean(o, axis=-1, keepdims=True)
    y = c * lax.rsqrt(jnp.mean(c * c, axis=-1, keepdims=True) + EPS)
    return y.reshape(B, T, H * D) * g.astype(F32)


def masked_softmax(s, valid):
    p = jax.nn.softmax(jnp.where(valid, s, NEG_INF), axis=-1)
    return p * valid.astype(F32)


def rel_bucket(dist):
    d = jnp.maximum(dist, 0)
    me = NUM_BUCKETS // 2
    logd = jnp.log(jnp.maximum(d, 1).astype(F32) / me)
    large = me + (logd / math.log(MAX_DISTANCE / me) * (NUM_BUCKETS - me)).astype(jnp.int32)
    return jnp.where(d < me, d, jnp.minimum(large, NUM_BUCKETS - 1))


def heads_bias(table, bucket):
    b = table.astype(F32)[bucket]
    return b.reshape(bucket.shape + (NSA_KVH, NSA_G)).transpose(2, 3, 0, 1)


def rotary(x, pos):
    half = x.shape[-1] // 2
    inv = ROPE_BASE ** (-jnp.arange(half, dtype=F32) / half)
    ang = pos.astype(F32)[:, None] * inv[None, :]
    cos = jnp.cos(ang)[None, :, None, :]
    sin = jnp.sin(ang)[None, :, None, :]
    x1, x2 = x[..., :half], x[..., half:]
    return jnp.concatenate([x1 * cos - x2 * sin, x1 * sin + x2 * cos], axis=-1)


def chunked_scan(step, S0, xs, chunk):
    B, T = xs[0].shape[:2]
    nc = T // chunk
    xs_c = tuple(jnp.moveaxis(a.reshape((B, nc, chunk) + a.shape[2:]), 1, 0) for a in xs)
    S, o = lax.scan(lambda St, c: step(St, *c), S0, xs_c)
    o = jnp.moveaxis(o, 0, 1)
    return o.reshape((B, T) + o.shape[3:]), S


def hgrn_prep(hq, hf, hv, lb, dv):
    B, T, _ = hq.shape
    f = lb[None, None, :] + (1.0 - lb[None, None, :]) * jax.nn.sigmoid(hf.astype(F32))
    q = hq.astype(F32).reshape(B, T, HG_HEADS, HG_DK) * HG_DK ** -0.5
    logf = jnp.log(f).reshape(B, T, HG_HEADS, HG_DK)
    k = (1.0 - f).reshape(B, T, HG_HEADS, HG_DK)
    v = hv.astype(F32).reshape(B, T, HG_HEADS, dv)
    return q, logf, k, v


def hgrn_chunk(S0, q, logf, k, v):
    C = q.shape[1]
    b = jnp.cumsum(logf, axis=1)
    o_inter = jnp.einsum('bthd,bhde->bthe', q * jnp.exp(b), S0)
    causal = jnp.tril(jnp.ones((C, C), dtype=bool))
    diff = b[:, :, None] - b[:, None, :]
    decay = jnp.exp(jnp.where(causal[None, :, :, None, None], diff, -jnp.inf))
    a = jnp.einsum('bthd,bshd,btshd->bhts', q, k, decay)
    o_intra = jnp.einsum('bhts,bshe->bthe', a, v)
    b_last = b[:, -1]
    S_new = jnp.exp(b_last)[..., None] * S0 + jnp.einsum('bshd,bshe->bhde', k * jnp.exp(b_last[:, None] - b), v)
    return S_new, o_inter + o_intra


def hgrn_out(o, gain, gate):
    return head_rmsnorm(o, gain) * jax.nn.silu(gate.astype(F32))


def nsa_prep(nq, kvs, hd):
    B, T, _ = nq.shape
    q = nq.reshape(B, T, NSA_KVH, NSA_G, hd) * hd ** -0.5
    kv = [a.reshape(B, T, NSA_KVH, hd) for a in split_cols(kvs, (NSA_KVH * hd,) * 6)]
    return q, kv


def compress(k, pe, w1, w2):
    B, L, KVH, HD = k.shape
    n_cb = L // CMP_BLOCK
    kb = k[:, : n_cb * CMP_BLOCK].reshape(B, n_cb, CMP_BLOCK, KVH, HD) + pe[None, None, :, None, :]
    kb = kb.transpose(0, 1, 3, 2, 4).reshape(B, n_cb, KVH, CMP_BLOCK * HD)
    return jax.nn.silu(kb @ w1) @ w2


def nsa_compressed(q, qpos, kc, vc, table):
    n_cb = kc.shape[1]
    end = (jnp.arange(n_cb, dtype=jnp.int32) + 1) * CMP_BLOCK - 1
    dist = qpos[:, None] - end[None, :]
    s = jnp.einsum('bqkgd,bnkd->bkgqn', q, kc).astype(F32) + heads_bias(table, rel_bucket(dist))
    p = masked_softmax(s, dist >= 0)
    o = jnp.einsum('bkgqn,bnkd->bqkgd', p.astype(vc.dtype), vc)
    return o, p


def select_blocks(p, qpos, L):
    B, KVH, G, Tq, n_cb = p.shape
    ratio = SEL_BLOCK // CMP_BLOCK
    n_sel = -(-L // SEL_BLOCK)
    imp = jnp.pad(p.sum(2), ((0, 0), (0, 0), (0, 0), (0, n_sel * ratio - n_cb)))
    imp = imp.reshape(B, KVH, Tq, n_sel, ratio).sum(-1)
    blk = jnp.arange(n_sel, dtype=jnp.int32)[None, :]
    cur = (qpos // SEL_BLOCK)[:, None]
    forced = (blk == 0) | (blk == cur) | (blk == cur - 1)
    valid = blk * SEL_BLOCK <= qpos[:, None]
    score = jnp.where(valid, jnp.where(forced, jnp.inf, imp), -jnp.inf)
    _, idx = lax.top_k(score, min(N_SELECT, n_sel))
    return idx


def pad_blocks(k):
    L = k.shape[1]
    n_sel = -(-L // SEL_BLOCK)
    return jnp.pad(k, ((0, 0), (0, n_sel * SEL_BLOCK - L), (0, 0), (0, 0)))


def nsa_selected(q, qpos, idx, ks, vs, table):
    B, Tq = q.shape[:2]
    n = idx.shape[-1] * SEL_BLOCK
    tok = (idx[..., None] * SEL_BLOCK + jnp.arange(SEL_BLOCK, dtype=jnp.int32)).reshape(B, NSA_KVH, Tq, n)
    bi = jnp.arange(B)[:, None, None, None]
    hi = jnp.arange(NSA_KVH)[None, :, None, None]
    kg = ks.transpose(0, 2, 1, 3)[bi, hi, tok]
    vg = vs.transpose(0, 2, 1, 3)[bi, hi, tok]
    dist = qpos[None, None, :, None] - tok
    bias = table.astype(F32).reshape(NUM_BUCKETS, NSA_KVH, NSA_G)[rel_bucket(dist), hi]
    s = jnp.einsum('bqkgd,bkqsd->bkgqs', q, kg).astype(F32) + jnp.moveaxis(bias, 4, 2)
    p = masked_softmax(s, (dist >= 0)[:, :, None])
    return jnp.einsum('bkgqs,bkqsd->bqkgd', p.astype(vg.dtype), vg)


def window_banded(q, kw, vw, table):
    B, T = q.shape[:2]
    hd = q.shape[-1]
    nb = T // WIN_QBLOCK
    span = WIN_QBLOCK + WINDOW
    pad = ((0, 0), (WINDOW, 0), (0, 0), (0, 0))
    gidx = jnp.arange(nb, dtype=jnp.int32)[:, None] * WIN_QBLOCK + jnp.arange(span, dtype=jnp.int32)[None, :]
    kb = jnp.pad(kw, pad)[:, gidx]
    vb = jnp.pad(vw, pad)[:, gidx]
    qb = q.reshape(B, nb, WIN_QBLOCK, NSA_KVH, NSA_G, hd)
    dist = jnp.arange(WIN_QBLOCK, dtype=jnp.int32)[:, None] + WINDOW - jnp.arange(span, dtype=jnp.int32)[None, :]
    kpos = gidx - WINDOW
    valid = (dist >= 0) & (dist < WINDOW) & (kpos[:, None, :] >= 0)
    s = jnp.einsum('bnqkgd,bnskd->bnkgqs', qb, kb).astype(F32) + heads_bias(table, rel_bucket(dist))
    p = masked_softmax(s, valid[None, :, None, None])
    o = jnp.einsum('bnkgqs,bnskd->bnqkgd', p.astype(vb.dtype), vb)
    return o.reshape(B, T, NSA_KVH, NSA_G, hd)


def window_dense(q, qpos, kw, vw, kpos, table):
    dist = qpos[:, None] - kpos[None, :]
    valid = (dist >= 0) & (dist < WINDOW) & (kpos[None, :] >= 0)
    s = jnp.einsum('bqkgd,bskd->bkgqs', q, kw).astype(F32) + heads_bias(table, rel_bucket(dist))
    p = masked_softmax(s, valid)
    return jnp.einsum('bkgqs,bskd->bqkgd', p.astype(vw.dtype), vw)


def nsa_combine(o_c, o_s, o_w, gate_logits, path):
    B, T = o_c.shape[:2]
    g = jax.nn.sigmoid(gate_logits.astype(F32)).reshape(B, T, NSA_KVH, NSA_G, 3)
    o = g[..., 0:1] * o_c.astype(F32) + g[..., 1:2] * o_s.astype(F32) + g[..., 2:3] * o_w.astype(F32)
    return o.reshape(B, T, -1) * jax.nn.silu(path.astype(F32))


def gather_pages(pool, page_table):
    B, NP = page_table.shape
    g = pool[page_table]
    return g.reshape((B, NP * PAGE_SIZE) + pool.shape[2:])


def retention_chunk(S0, q, k, v, log_gamma):
    C = q.shape[1]
    i = jnp.arange(C, dtype=F32)
    diff = i[:, None] - i[None, :]
    decay = jnp.where(diff[None] >= 0, jnp.exp(diff[None] * log_gamma[:, None, None]), 0.0)
    a = jnp.einsum('bthd,bshd->bhts', q, k) * decay[None]
    inner = jnp.einsum('bhts,bshe->bthe', a, v)
    q_dec = q * jnp.exp((i + 1.0)[:, None] * log_gamma[None, :])[None, :, :, None]
    cross = jnp.einsum('bthd,bhde->bthe', q_dec, S0)
    k_dec = k * jnp.exp((C - 1.0 - i)[:, None] * log_gamma[None, :])[None, :, :, None]
    S_new = jnp.exp(C * log_gamma)[None, :, None, None] * S0 + jnp.einsum('bshd,bshe->bhde', k_dec, v)
    return S_new, inner + cross


def _proj(h2, w, splits):
    N = w.shape[1]
    main = (N // 512) * 512
    parts = [matmul(h2, w, n_cols=main)]
    if main < N:
        tail = N - main
        padded = -(-tail // 512) * 512
        w_tail = jnp.pad(w[:, main:], ((0, 0), (0, padded - tail)))
        parts.append(matmul(h2, w_tail)[:, :tail])
    full = jnp.concatenate(parts, axis=-1) if len(parts) > 1 else parts[0]
    return split_cols(full, splits)


def even_prompt(h2, B, T, w_in, w_out, lb, hg_gain, pe, w1k, w2k, w1v, w2v, table, dims):
    hg_qk, hg_w, nsa_w, nsa_kvw = dims
    hd = nsa_w // NSA_HEADS
    dv = hg_w // HG_HEADS
    splits = (hg_qk, hg_qk, hg_w, hg_w, nsa_w, 6 * nsa_kvw, 3 * NSA_HEADS, nsa_w)
    pos = jnp.arange(T, dtype=jnp.int32)
    hq, hf, hv, hgate, nq, kvs, ngate, npath = [a.reshape(B, T, -1) for a in _proj(h2, w_in, splits)]
    q, logf, k, v = hgrn_prep(hq, hf, hv, lb, dv)
    S0 = jnp.zeros((B, HG_HEADS, HG_DK, dv), F32)
    o_hg, S = chunked_scan(hgrn_chunk, S0, (q, logf, k, v), HG_CHUNK)
    y_hg = hgrn_out(o_hg, hg_gain, hgate)
    qn, (kc, vc, ks, vs, kw, vw) = nsa_prep(nq, kvs, hd)
    o_c, p_c = nsa_compressed(qn, pos, compress(kc, pe, w1k, w2k), compress(vc, pe, w1v, w2v), table)
    idx = select_blocks(p_c, pos, T)
    ks_p, vs_p = pad_blocks(ks), pad_blocks(vs)
    nqb = T // SEL_QBLOCK
    qb = jnp.moveaxis(qn.reshape(B, nqb, SEL_QBLOCK, NSA_KVH, NSA_G, hd), 1, 0)
    pb = pos.reshape(nqb, SEL_QBLOCK)
    ib = jnp.moveaxis(idx.reshape(B, NSA_KVH, nqb, SEL_QBLOCK, idx.shape[-1]), 2, 0)
    o_s = lax.map(lambda a: nsa_selected(a[0], a[1], a[2], ks_p, vs_p, table), (qb, pb, ib))
    o_s = jnp.moveaxis(o_s, 0, 1).reshape(B, T, NSA_KVH, NSA_G, hd)
    o_w = window_banded(qn, kw, vw, table)
    y_nsa = nsa_combine(o_c, o_s, o_w, ngate, npath)
    a = jnp.concatenate([y_hg, y_nsa], axis=-1).astype(BF16).reshape(B * T, -1)
    y = matmul(a, w_out)
    wb = min(WINDOW, T)
    return y, S, kc, vc, ks, vs, kw[:, T - wb:], vw[:, T - wb:]


def even_sample(h2, B, T, pos, past_len, S0, page_table, pool_ck, pool_cv, pool_sk, pool_sv, win_k, win_v,
                w_in, w_out, lb, hg_gain, pe, w1k, w2k, w1v, w2v, table, dims):
    hg_qk, hg_w, nsa_w, nsa_kvw = dims
    hd = nsa_w // NSA_HEADS
    dv = hg_w // HG_HEADS
    splits = (hg_qk, hg_qk, hg_w, hg_w, nsa_w, 6 * nsa_kvw, 3 * NSA_HEADS, nsa_w)
    hq, hf, hv, hgate, nq, kvs, ngate, npath = [a[:B * T].reshape(B, T, -1) for a in _proj(h2, w_in, splits)]
    q, logf, k, v = hgrn_prep(hq, hf, hv, lb, dv)
    o_hg, S = chunked_scan(hgrn_chunk, S0.astype(F32), (q, logf, k, v), T)
    y_hg = hgrn_out(o_hg, hg_gain, hgate)
    qn, (kc, vc, ks, vs, kw, vw) = nsa_prep(nq, kvs, hd)
    kc_all = jnp.concatenate([gather_pages(pool_ck, page_table).astype(kc.dtype), kc], axis=1)
    vc_all = jnp.concatenate([gather_pages(pool_cv, page_table).astype(vc.dtype), vc], axis=1)
    ks_all = jnp.concatenate([gather_pages(pool_sk, page_table).astype(ks.dtype), ks], axis=1)
    vs_all = jnp.concatenate([gather_pages(pool_sv, page_table).astype(vs.dtype), vs], axis=1)
    L = kc_all.shape[1]
    o_c, p_c = nsa_compressed(qn, pos, compress(kc_all, pe, w1k, w2k), compress(vc_all, pe, w1v, w2v), table)
    idx = select_blocks(p_c, pos, L)
    o_s = nsa_selected(qn, pos, idx, pad_blocks(ks_all), pad_blocks(vs_all), table)
    wb = win_k.shape[1]
    kw_all = jnp.concatenate([win_k.astype(kw.dtype), kw], axis=1)
    vw_all = jnp.concatenate([win_v.astype(vw.dtype), vw], axis=1)
    kpos = past_len - wb + jnp.arange(wb + T, dtype=jnp.int32)
    o_w = window_dense(qn, pos, kw_all, vw_all, kpos, table)
    y_nsa = nsa_combine(o_c, o_s, o_w, ngate, npath)
    a = jnp.concatenate([y_hg, y_nsa], axis=-1).astype(BF16).reshape(B * T, -1)
    y = matmul(_pad_rows(a), w_out)
    return y, S, kc, vc, ks, vs, kw_all[:, T:], vw_all[:, T:]


def retention_layer(h2, B, T, pos, S0, w_in, w_out, gain, chunk, pad):
    ret_qk = S0.shape[2] * RET_HEADS
    ret_v = S0.shape[3] * RET_HEADS
    rq, rk, rv, rg = [a[:B * T].reshape(B, T, -1) for a in _proj(h2, w_in, (ret_qk, ret_qk, ret_v, ret_v))]
    dk, dv = ret_qk // RET_HEADS, ret_v // RET_HEADS
    q = rotary(rq.astype(F32).reshape(B, T, RET_HEADS, dk), pos) * dk ** -0.5
    k = rotary(rk.astype(F32).reshape(B, T, RET_HEADS, dk), pos)
    v = rv.astype(F32).reshape(B, T, RET_HEADS, dv)
    log_gamma = jnp.log1p(-jnp.exp2(-5.0 - jnp.arange(RET_HEADS, dtype=F32)))
    o, S = chunked_scan(lambda St, a, b, c: retention_chunk(St, a, b, c, log_gamma), S0, (q, k, v), chunk)
    y = head_groupnorm(o, gain) * jax.nn.silu(rg.astype(F32))
    a = y.astype(BF16).reshape(B * T, -1)
    if pad:
        a = _pad_rows(a)
    return matmul(a, w_out), S


def mem_attend(h2, B, T, mk, mv, w_q, w_o, pad):
    qh, gh = [a[:B * T].reshape(B, T, -1) for a in _proj(h2, w_q, (MEM_W, MEM_W))]
    q = qh.reshape(B, T, MEM_HEADS, MEM_HD) * MEM_HD ** -0.5
    s = jnp.einsum('bqhd,bmhd->bhqm', q, mk).astype(F32)
    p = jax.nn.softmax(s, axis=-1)
    o = jnp.einsum('bhqm,bmhd->bqhd', p.astype(mv.dtype), mv).reshape(B, T, MEM_W)
    a = (o.astype(F32) * jax.nn.silu(gh.astype(F32))).astype(BF16).reshape(B * T, -1)
    if pad:
        a = _pad_rows(a)
    return matmul(a, w_o)


SAMPLE_ROWS = 8


def _pad_rows(a):
    M = a.shape[0]
    padded = -(-M // SAMPLE_ROWS) * SAMPLE_ROWS
    return a if padded == M else jnp.pad(a, ((0, padded - M), (0, 0)))


def kernel(x_prompt, x_sample, mem_prompt, state_hgrn, cache_cmp_k, cache_cmp_v, cache_sel_k, cache_sel_v,
           cache_win_k, cache_win_v, state_ret, cache_mem_k, cache_mem_v, page_table, rel_table,
           norm_mix_pre, norm_mix_post, norm_mem_pre, norm_mem_post, ev_w_in, ev_w_out, hgrn_lb, hgrn_norm,
           cmp_pe, cmp_w1_k, cmp_w2_k, cmp_w1_v, cmp_w2_v, od_w_in, od_w_out, ret_norm,
           mem_w_q, mem_w_k, mem_w_v, mem_w_o):
    B, T, D = x_prompt.shape
    Bs, Ts, _ = x_sample.shape
    depth = norm_mix_pre.shape[0]
    past_len = page_table.shape[1] * PAGE_SIZE
    pos_p = jnp.arange(T, dtype=jnp.int32)
    pos_s = past_len + jnp.arange(Ts, dtype=jnp.int32)
    lb_all = jnp.cumsum(jax.nn.softmax(hgrn_lb.astype(F32), axis=0), axis=0)
    hg_qk = hgrn_lb.shape[1]
    hg_w = hgrn_norm.shape[1]
    nsa_w = ev_w_out.shape[1] - hg_w
    nsa_kvw = NSA_KVH * (nsa_w // NSA_HEADS)
    dims = (hg_qk, hg_w, nsa_w, nsa_kvw)
    Ms = Bs * Ts

    even_p, even_s, ret_p, ret_s, memk_p, memv_p = [], [], [], [], [], []
    xp = x_prompt.reshape(B * T, D)
    xs = _pad_rows(x_sample.reshape(Ms, D))
    mem2 = mem_prompt.reshape(-1, D).astype(BF16)
    ML = mem_prompt.shape[1]
    for l in range(depth):
        hp = rmsnorm_bf16(xp, norm_mix_pre[l])
        hs = rmsnorm_bf16(xs, norm_mix_pre[l])
        if l % 2 == 0:
            e = l // 2
            w = (ev_w_in[e], ev_w_out[e], lb_all[e], hgrn_norm[e], cmp_pe[e],
                 cmp_w1_k[e], cmp_w2_k[e], cmp_w1_v[e], cmp_w2_v[e], rel_table)
            yp, *sp = even_prompt(hp, B, T, *w, dims)
            ys, *ss = even_sample(hs, Bs, Ts, pos_s, past_len, state_hgrn[e], page_table, cache_cmp_k[e],
                                  cache_cmp_v[e], cache_sel_k[e], cache_sel_v[e], cache_win_k[e], cache_win_v[e],
                                  *w, dims)
            even_p.append(sp)
            even_s.append(ss)
        else:
            o = l // 2
            S0p = jnp.zeros((B,) + state_ret.shape[2:], F32)
            yp, sp = retention_layer(hp, B, T, pos_p, S0p, od_w_in[o], od_w_out[o], ret_norm[o], RET_CHUNK, False)
            ys, ss = retention_layer(hs, Bs, Ts, pos_s, state_ret[o].astype(F32), od_w_in[o], od_w_out[o],
                                     ret_norm[o], Ts, True)
            ret_p.append(sp)
            ret_s.append(ss)
        xp = residual_post(xp, yp, norm_mix_post[l])
        xs = residual_post(xs, ys, norm_mix_post[l])
        mk_p = matmul(mem2, mem_w_k[l]).reshape(B, ML, MEM_HEADS, MEM_HD)
        mv_p = matmul(mem2, mem_w_v[l]).reshape(B, ML, MEM_HEADS, MEM_HD)
        memk_p.append(mk_p)
        memv_p.append(mv_p)
        hp = rmsnorm_bf16(xp, norm_mem_pre[l])
        hs = rmsnorm_bf16(xs, norm_mem_pre[l])
        yp = mem_attend(hp, B, T, mk_p, mv_p, mem_w_q[l], mem_w_o[l], False)
        ys = mem_attend(hs, Bs, Ts, cache_mem_k[l], cache_mem_v[l], mem_w_q[l], mem_w_o[l], True)
        xp = residual_post(xp, yp, norm_mem_post[l])
        xs = residual_post(xs, ys, norm_mem_post[l])
    p_hgrn, p_cmp_k, p_cmp_v, p_sel_k, p_sel_v, p_win_k, p_win_v = [jnp.stack(a) for a in zip(*even_p)]
    s_hgrn, s_cmp_k, s_cmp_v, s_sel_k, s_sel_v, s_win_k, s_win_v = [jnp.stack(a) for a in zip(*even_s)]
    p_ret = jnp.stack(ret_p)
    s_ret = jnp.stack(ret_s)
    p_mem_k = jnp.stack(memk_p)
    p_mem_v = jnp.stack(memv_p)
    return (xp.reshape(B, T, D), xs[:Ms].reshape(Bs, Ts, D), p_hgrn, p_cmp_k, p_cmp_v, p_sel_k, p_sel_v,
            p_win_k, p_win_v, p_ret, p_mem_k, p_mem_v,
            s_hgrn, s_cmp_k, s_cmp_v, s_sel_k, s_sel_v, s_win_k, s_win_v, s_ret)
```

```python
import functools
import math

import jax
import jax.numpy as jnp
import numpy as np
from jax import lax
from jax.experimental import pallas as pl
from jax.experimental.pallas import tpu as pltpu

F32 = jnp.float32
BF16 = jnp.bfloat16
EPS = 1e-6
NEG_INF = -1e30

PAGE_SIZE = 128
HG_HEADS = 16
HG_DK = 128
HG_CHUNK = 64
HGRN_SUB = 16
NSA_HEADS = 16
NSA_KVH = 2
NSA_G = NSA_HEADS // NSA_KVH
CMP_BLOCK = 32
SEL_BLOCK = 64
N_SELECT = 16
WINDOW = 512
SEL_QBLOCK = 64
WIN_QBLOCK = 128
NUM_BUCKETS = 32
MAX_DISTANCE = 1024
RET_HEADS = 16
RET_CHUNK = 128
ROPE_BASE = 10000.0
MEM_HEADS = 4
MEM_HD = 128
MEM_W = MEM_HEADS * MEM_HD

VMEM_LIMIT_BYTES = 56 * 1024 * 1024


def _params(*sem):
    return pltpu.CompilerParams(dimension_semantics=sem, vmem_limit_bytes=VMEM_LIMIT_BYTES)


def _rmsnorm_kernel(x_ref, g_ref, o_ref):
    x = x_ref[...]
    y = x * lax.rsqrt(jnp.mean(x * x, axis=-1, keepdims=True) + EPS)
    o_ref[...] = (y * g_ref[...]).astype(o_ref.dtype)


def rmsnorm_bf16(x, g):
    M, D = x.shape
    tm = min(M, 512)
    return pl.pallas_call(
        _rmsnorm_kernel,
        grid=(M // tm,),
        in_specs=[pl.BlockSpec((tm, D), lambda i: (i, 0)), pl.BlockSpec((1, D), lambda i: (0, 0))],
        out_specs=pl.BlockSpec((tm, D), lambda i: (i, 0)),
        out_shape=jax.ShapeDtypeStruct((M, D), BF16),
        compiler_params=_params("parallel"),
        name="rmsnorm_bf16",
    )(x, g.reshape(1, D))


def _post_kernel(x_ref, y_ref, g_ref, o_ref):
    y = y_ref[...]
    n = y * lax.rsqrt(jnp.mean(y * y, axis=-1, keepdims=True) + EPS)
    o_ref[...] = x_ref[...] + n * g_ref[...]


def residual_post(x, y, g):
    M, D = x.shape
    tm = min(M, 512)
    return pl.pallas_call(
        _post_kernel,
        grid=(M // tm,),
        in_specs=[pl.BlockSpec((tm, D), lambda i: (i, 0)), pl.BlockSpec((tm, D), lambda i: (i, 0)),
                  pl.BlockSpec((1, D), lambda i: (0, 0))],
        out_specs=pl.BlockSpec((tm, D), lambda i: (i, 0)),
        out_shape=jax.ShapeDtypeStruct((M, D), F32),
        compiler_params=_params("parallel"),
        name="residual_post",
    )(x, y, g.reshape(1, D))


def _matmul_kernel(x_ref, w_ref, o_ref, wb_ref):
    @pl.when(pl.program_id(1) == 0)
    def _():
        wb_ref[...] = w_ref[...].astype(BF16)

    o_ref[...] = jnp.dot(x_ref[...], wb_ref[...], preferred_element_type=F32)


def matmul(x, w, n_cols=None):
    M, K = x.shape
    N = w.shape[1] if n_cols is None else n_cols
    tn = 512 if K <= 4096 else 256
    tn = min(tn, N)
    tm = min(M, 1024 if K <= 4096 else 512)
    assert N % tn == 0 and M % tm == 0, (M, K, N)
    return pl.pallas_call(
        _matmul_kernel,
        grid=(N // tn, M // tm),
        in_specs=[pl.BlockSpec((tm, K), lambda n, m: (m, 0)), pl.BlockSpec((K, tn), lambda n, m: (0, n))],
        out_specs=pl.BlockSpec((tm, tn), lambda n, m: (m, n)),
        out_shape=jax.ShapeDtypeStruct((M, N), F32),
        scratch_shapes=[pltpu.VMEM((K, tn), BF16)],
        compiler_params=_params("parallel", "arbitrary"),
        name="matmul",
    )(x, w)


LANES = 128
NSA_TQ = 128
_NT = (((1,), (1,)), ((), ()))


def _nsa_prompt_kernel(q_ref, kc_ref, vc_ref, ks_ref, vs_ref, kw_ref, vw_ref, cb_ref, tb_ref, gate_ref, path_ref,
                       o_ref, qb_s, oc_s, m_s, l_s, acc_s, *, hd, n_win_chunks, n_pick):
    i = pl.program_id(2)
    tq = NSA_TQ
    G = NSA_G
    scale = hd ** -0.5
    row = lax.broadcasted_iota(jnp.int32, (tq, LANES), 0)
    lane = lax.broadcasted_iota(jnp.int32, (tq, LANES), 1)
    qpos = i * tq + row

    kcmp = kc_ref[0].astype(BF16)
    vcmp = vc_ref[0].astype(BF16)
    valid_c = qpos >= (lane + 1) * CMP_BLOCK - 1
    valid_cf = valid_c.astype(F32)
    imp = jnp.zeros((tq, LANES), F32)
    for g in range(G):
        qg = (q_ref[:, g * hd:(g + 1) * hd] * scale).astype(BF16)
        qb_s[g] = qg
        s = lax.dot_general(qg, kcmp, _NT, preferred_element_type=F32) + cb_ref[g]
        s = jnp.where(valid_c, s, NEG_INF)
        e = jnp.exp(s - jnp.max(s, axis=-1, keepdims=True))
        p = e / jnp.sum(e, axis=-1, keepdims=True) * valid_cf
        imp = imp + p
        oc_s[g] = jnp.dot(p.astype(BF16), vcmp, preferred_element_type=F32)

    imp2 = imp + pltpu.roll(imp, LANES - 1, axis=1)
    blk = lane >> 1
    cur = qpos // SEL_BLOCK
    forced = (blk == 0) | (blk == cur) | (blk == cur - 1)
    usable = ((lane & 1) == 0) & (blk * SEL_BLOCK <= qpos)
    work = jnp.where(usable, jnp.where(forced, jnp.inf, imp2), -jnp.inf)
    sel = jnp.zeros((tq, LANES), F32)
    for _ in range(n_pick):
        mx = jnp.max(work, axis=-1, keepdims=True)
        first = jnp.min(jnp.where(work == mx, lane, LANES), axis=-1, keepdims=True)
        pick = lane == first
        sel = jnp.where(pick, 1.0, sel)
        work = jnp.where(pick, -jnp.inf, work)
    sel_b = sel.astype(BF16)

    m_s[...] = jnp.full(m_s.shape, NEG_INF, F32)
    l_s[...] = jnp.zeros(l_s.shape, F32)
    acc_s[...] = jnp.zeros(acc_s.shape, F32)
    erow = lax.broadcasted_iota(jnp.int32, (LANES, LANES), 0)
    ecol = lax.broadcasted_iota(jnp.int32, (LANES, LANES), 1)

    def online(slot, g, k_b, v_b, bias, valid):
        s = lax.dot_general(qb_s[g], k_b, _NT, preferred_element_type=F32) + bias
        s = jnp.where(valid, s, NEG_INF)
        m_old = m_s[slot, g]
        m_new = jnp.maximum(m_old, jnp.max(s, axis=-1, keepdims=True))
        alpha = jnp.exp(m_old - m_new)
        p = jnp.where(valid, jnp.exp(s - m_new), 0.0)
        l_s[slot, g] = alpha * l_s[slot, g] + jnp.sum(p, axis=-1, keepdims=True)
        acc_s[slot, g] = alpha * acc_s[slot, g] + jnp.dot(p.astype(BF16), v_b, preferred_element_type=F32)
        m_s[slot, g] = m_new

    def chunk(c, carry):
        k0 = pl.multiple_of(c * LANES, LANES)
        delta = i - c
        dist = qpos - (k0 + lane)
        causal = dist >= 0
        expand = (erow == 2 * (c * (LANES // SEL_BLOCK) + (ecol // SEL_BLOCK))).astype(BF16)
        chosen = jnp.dot(sel_b, expand, preferred_element_type=F32) > 0.5
        valid_s = chosen & causal
        k_b = ks_ref[pl.ds(k0, LANES), :].astype(BF16)
        v_b = vs_ref[pl.ds(k0, LANES), :].astype(BF16)
        for g in range(G):
            online(0, g, k_b, v_b, tb_ref[g, delta], valid_s)

        @pl.when(delta < n_win_chunks)
        def _():
            valid_w = causal & (dist < WINDOW)
            kw_b = kw_ref[pl.ds(k0, LANES), :].astype(BF16)
            vw_b = vw_ref[pl.ds(k0, LANES), :].astype(BF16)
            for g in range(G):
                online(1, g, kw_b, vw_b, tb_ref[g, delta], valid_w)

        return carry

    lax.fori_loop(0, i + 1, chunk, 0)

    gates = jax.nn.sigmoid(gate_ref[...])
    for g in range(G):
        o_sel = acc_s[0, g] / l_s[0, g]
        o_win = acc_s[1, g] / l_s[1, g]
        o = (gates[:, 3 * g:3 * g + 1] * oc_s[g] + gates[:, 3 * g + 1:3 * g + 2] * o_sel
             + gates[:, 3 * g + 2:3 * g + 3] * o_win)
        o_ref[:, g * hd:(g + 1) * hd] = (o * jax.nn.silu(path_ref[:, g * hd:(g + 1) * hd])).astype(o_ref.dtype)


def _bias_by_distance(table, n):
    return table.astype(F32)[rel_bucket(jnp.arange(n, dtype=jnp.int32))]


def nsa_prompt(main, tail, kcmp, vcmp, table, B, T, cols):
    q_col, kv_col, path_col, gate_col, hd = cols
    G = NSA_G
    tq = NSA_TQ
    n_cb = kcmp.shape[1]
    assert n_cb <= LANES and T % tq == 0 and hd == LANES
    nd = T // LANES
    n_win_chunks = WINDOW // LANES + 1
    by_dist = _bias_by_distance(table, T)
    r = jnp.arange(LANES, dtype=jnp.int32)
    toe = jnp.arange(nd, dtype=jnp.int32)[:, None, None] * LANES + r[None, :, None] - r[None, None, :]
    tb = jnp.moveaxis(by_dist[jnp.clip(toe, 0, T - 1)], -1, 0)
    cdist = jnp.arange(T, dtype=jnp.int32)[:, None] - ((jnp.arange(LANES, dtype=jnp.int32) + 1) * CMP_BLOCK - 1)[None]
    cb = jnp.moveaxis(by_dist[jnp.clip(cdist, 0, T - 1)], -1, 0)
    pad = ((0, 0), (0, LANES - n_cb), (0, 0))
    kcmp = jnp.pad(kcmp, pad)
    vcmp = jnp.pad(vcmp, pad)
    nblk = T // tq
    gw = G * hd
    row_map = lambda b, h, i: (b * nblk + i, 0)
    kv_spec = lambda j: pl.BlockSpec((T, hd), lambda b, h, i: (b, kv_col // hd + 2 * j + h))
    return pl.pallas_call(
        functools.partial(_nsa_prompt_kernel, hd=hd, n_win_chunks=n_win_chunks,
                          n_pick=min(N_SELECT, -(-T // SEL_BLOCK))),
        grid=(B, NSA_KVH, nblk),
        in_specs=[
            pl.BlockSpec((tq, gw), lambda b, h, i: (b * nblk + i, q_col // gw + h)),
            pl.BlockSpec((1, LANES, hd), lambda b, h, i: (b, 0, h)),
            pl.BlockSpec((1, LANES, hd), lambda b, h, i: (b, 0, h)),
            kv_spec(2), kv_spec(3), kv_spec(4), kv_spec(5),
            pl.BlockSpec((G, tq, LANES), lambda b, h, i: (h, i, 0)),
            pl.BlockSpec((G, nd, LANES, LANES), lambda b, h, i: (h, 0, 0, 0)),
            pl.BlockSpec((tq, LANES), lambda b, h, i: (b * nblk + i, gate_col // LANES + h)),
            pl.BlockSpec((tq, gw), lambda b, h, i: (b * nblk + i, path_col // gw + h)),
        ],
        out_specs=pl.BlockSpec((tq, gw), lambda b, h, i: (b * nblk + i, h)),
        out_shape=jax.ShapeDtypeStruct((B * T, NSA_KVH * gw), BF16),
        scratch_shapes=[pltpu.VMEM((G, tq, hd), BF16), pltpu.VMEM((G, tq, hd), F32),
                        pltpu.VMEM((2, G, tq, 1), F32), pltpu.VMEM((2, G, tq, 1), F32),
                        pltpu.VMEM((2, G, tq, hd), F32)],
        compiler_params=_params("parallel", "parallel", "arbitrary"),
        name="nsa_prompt",
    )(main, kcmp, vcmp, main, main, main, main, cb, tb, tail, tail)


_TN = (((0,), (0,)), ((), ()))


def _cumsum_rows(x, n):
    row = lax.broadcasted_iota(jnp.int32, x.shape, 0)
    sh = 1
    while sh < n:
        x = x + jnp.where(row >= sh, pltpu.roll(x, sh, axis=0), 0.0)
        sh *= 2
    return x


def _hgrn_kernel(q_ref, f_ref, v_ref, gate_ref, lb_ref, gain_ref, s0_ref, y_ref, s_out_ref, st_s,
                 *, C, SB, c_eff, dk):
    c = pl.program_id(2)

    @pl.when(c == 0)
    def _():
        st_s[...] = s0_ref[0, 0].T

    lb = lb_ref[...]
    f = lb + (1.0 - lb) * jax.nn.sigmoid(f_ref[...])
    logf = jnp.log(f)
    k = 1.0 - f
    if c_eff < C:
        real = lax.broadcasted_iota(jnp.int32, (C, dk), 0) < c_eff
        logf = jnp.where(real, logf, 0.0)
        k = jnp.where(real, k, 0.0)
    q = q_ref[...] * dk ** -0.5
    v_b = v_ref[...].astype(BF16)
    b = _cumsum_rows(logf, C)
    st = st_s[...]
    o = lax.dot_general((q * jnp.exp(b)).astype(BF16), st.astype(BF16), _NT, preferred_element_type=F32)

    lane_c = lax.broadcasted_iota(jnp.int32, (SB, C), 1)
    row_c = lax.broadcasted_iota(jnp.int32, (SB, C), 0)
    outs = []
    for I in range(C // SB):
        r0 = I * SB
        q_i = q[r0:r0 + SB]
        b_i = b[r0:r0 + SB]
        if I > 0:
            b_r = b[r0 - 1:r0]
            qq = (q_i * jnp.exp(b_i - b_r)).astype(BF16)
            kk = (k * jnp.exp(jnp.minimum(b_r - b, 0.0))).astype(BF16)
            a = lax.dot_general(qq, kk, _NT, preferred_element_type=F32)
            a = jnp.where(lane_c < r0, a, 0.0)
        else:
            a = jnp.zeros((SB, C), F32)
        for s in range(SB):
            z = q_i * k[r0 + s:r0 + s + 1] * jnp.exp(jnp.minimum(b_i - b[r0 + s:r0 + s + 1], 0.0))
            col = jnp.sum(z, axis=-1, keepdims=True)
            a = jnp.where((lane_c == r0 + s) & (row_c >= s), col, a)
        outs.append(jnp.dot(a.astype(BF16), v_b, preferred_element_type=F32))
    o = o + jnp.concatenate(outs, axis=0)

    b_last = b[C - 1:C]
    kd = (k * jnp.exp(b_last - b)).astype(BF16)
    st_new = jnp.exp(b_last) * st + lax.dot_general(v_b, kd, _TN, preferred_element_type=F32)
    st_s[...] = st_new

    y = o * lax.rsqrt(jnp.mean(o * o, axis=-1, keepdims=True) + EPS) * gain_ref[...]
    y_ref[...] = (y * jax.nn.silu(gate_ref[...])).astype(y_ref.dtype)

    @pl.when(c == pl.num_programs(2) - 1)
    def _():
        s_out_ref[0, 0] = st_new.T


def hgrn_mix(main, lb, gain, s0, B, T, cols, C, SB, c_eff):
    q_col, f_col, v_col, g_col = cols
    _, H, dk, dv = s0.shape
    assert dk == LANES and dv == LANES and T % C == 0 and C % SB == 0
    nc = T // C
    col = lambda off: pl.BlockSpec((C, LANES), lambda b, h, c: (b * nc + c, off // LANES + h))
    vec = pl.BlockSpec((1, LANES), lambda b, h, c: (0, h))
    st = pl.BlockSpec((1, 1, dk, dv), lambda b, h, c: (b, h, 0, 0))
    return pl.pallas_call(
        functools.partial(_hgrn_kernel, C=C, SB=SB, c_eff=c_eff, dk=dk),
        grid=(B, H, nc),
        in_specs=[col(q_col), col(f_col), col(v_col), col(g_col), vec, vec, st],
        out_specs=[pl.BlockSpec((C, LANES), lambda b, h, c: (b * nc + c, h)), st],
        out_shape=[jax.ShapeDtypeStruct((B * T, H * dv), BF16), jax.ShapeDtypeStruct(s0.shape, F32)],
        scratch_shapes=[pltpu.VMEM((dv, dk), F32)],
        compiler_params=_params("parallel", "parallel", "arbitrary"),
        name="hgrn_mix",
    )(main, main, main, main, lb.reshape(1, -1), gain.reshape(1, -1), s0)


def _retention_kernel(q_ref, k_ref, v_ref, g_ref, cos_ref, sin_ref, lg_ref, gain_ref, s0_ref, y_ref, s_out_ref, s_s,
                      *, C, c_eff, dk):
    c = pl.program_id(2)

    @pl.when(c == 0)
    def _():
        s_s[...] = s0_ref[0, 0]

    half = dk // 2
    cos = cos_ref[...]
    sin = sin_ref[...]
    lg_w = lg_ref[0]
    lg = lg_w[:, :LANES]
    row = lax.broadcasted_iota(jnp.int32, (C, LANES), 0).astype(F32)

    def rot(ref, w):
        x1 = ref[:, :half]
        x2 = ref[:, half:]
        return jnp.concatenate([(x1 * cos - x2 * sin) * w, (x1 * sin + x2 * cos) * w], axis=1)

    q = rot(q_ref, dk ** -0.5)
    k = rot(k_ref, 1.0)
    v_b = v_ref[...].astype(BF16)
    s_old = s_s[...]

    a = lax.dot_general(q.astype(BF16), k.astype(BF16), _NT, preferred_element_type=F32)
    ti = lax.broadcasted_iota(jnp.int32, (C, C), 0)
    si = lax.broadcasted_iota(jnp.int32, (C, C), 1)
    diff = (ti - si).astype(F32)
    a = a * jnp.where(diff >= 0, jnp.exp(diff * lg_w[:, :C]), 0.0)
    inner = jnp.dot(a.astype(BF16), v_b, preferred_element_type=F32)
    q_w = jnp.exp((row + 1.0) * lg)
    q_dec = q * jnp.concatenate([q_w] * (dk // LANES), axis=1)
    cross = jnp.dot(q_dec.astype(BF16), s_old.astype(BF16), preferred_element_type=F32)
    k_w = jnp.where(row < c_eff, jnp.exp((c_eff - 1.0 - row) * lg), 0.0)
    k_dec = k * jnp.concatenate([k_w] * (dk // LANES), axis=1)
    s_new = jnp.exp(c_eff * lg_w) * s_old + lax.dot_general(k_dec.astype(BF16), v_b, _TN, preferred_element_type=F32)
    s_s[...] = s_new

    o = inner + cross
    cen = o - jnp.mean(o, axis=-1, keepdims=True)
    y = cen * lax.rsqrt(jnp.mean(cen * cen, axis=-1, keepdims=True) + EPS) * gain_ref[...]
    y_ref[...] = (y * jax.nn.silu(g_ref[...])).astype(y_ref.dtype)

    @pl.when(c == pl.num_programs(2) - 1)
    def _():
        s_out_ref[0, 0] = s_new


def retention_mix(proj, pos, gain, s0, B, T, C, c_eff):
    _, H, dk, dv = s0.shape
    assert T % C == 0 and C <= LANES and dk % LANES == 0
    nc = T // C
    half = dk // 2
    inv = ROPE_BASE ** (-jnp.arange(half, dtype=F32) / half)
    ang = pos.astype(F32)[:, None] * inv[None, :]
    log_gamma = jnp.log1p(-jnp.exp2(-5.0 - jnp.arange(H, dtype=F32)))
    lg = jnp.broadcast_to(log_gamma[:, None, None], (H, 1, dv))
    qk = lambda j: pl.BlockSpec((C, dk), lambda b, h, c: (b * nc + c, j * H + h))
    vg = lambda j: pl.BlockSpec((C, dv), lambda b, h, c: (b * nc + c, (2 * H * dk) // dv + j * H + h))
    tab = pl.BlockSpec((C, half), lambda b, h, c: (c, 0))
    st = pl.BlockSpec((1, 1, dk, dv), lambda b, h, c: (b, h, 0, 0))
    return pl.pallas_call(
        functools.partial(_retention_kernel, C=C, c_eff=c_eff, dk=dk),
        grid=(B, H, nc),
        in_specs=[qk(0), qk(1), vg(0), vg(1), tab, tab,
                  pl.BlockSpec((1, 1, dv), lambda b, h, c: (h, 0, 0)),
                  pl.BlockSpec((1, dv), lambda b, h, c: (0, h)), st],
        out_specs=[pl.BlockSpec((C, dv), lambda b, h, c: (b * nc + c, h)), st],
        out_shape=[jax.ShapeDtypeStruct((B * T, H * dv), BF16), jax.ShapeDtypeStruct(s0.shape, F32)],
        scratch_shapes=[pltpu.VMEM((dk, dv), F32)],
        compiler_params=_params("parallel", "parallel", "arbitrary"),
        name="retention_mix",
    )(proj, proj, proj, proj, jnp.cos(ang), jnp.sin(ang), lg, gain.reshape(1, -1), s0)


def _mem_kernel(qg_ref, mk_ref, mv_ref, o_ref, *, heads, hd):
    for h in range(heads):
        sl = slice(h * hd, (h + 1) * hd)
        q = (qg_ref[:, sl] * hd ** -0.5).astype(BF16)
        s = lax.dot_general(q, mk_ref[0, :, sl].astype(BF16), _NT, preferred_element_type=F32)
        e = jnp.exp(s - jnp.max(s, axis=-1, keepdims=True))
        p = e / jnp.sum(e, axis=-1, keepdims=True)
        o = jnp.dot(p.astype(BF16), mv_ref[0, :, sl].astype(BF16), preferred_element_type=F32)
        gate = qg_ref[:, heads * hd + h * hd:heads * hd + (h + 1) * hd]
        o_ref[:, sl] = (o * jax.nn.silu(gate)).astype(o_ref.dtype)


def mem_mix(qg, mk, mv, B, T):
    W = mk.shape[2]
    tq = min(T, 256)
    nb = T // tq
    kv = pl.BlockSpec((1, mk.shape[1], W), lambda b, i: (b, 0, 0))
    return pl.pallas_call(
        functools.partial(_mem_kernel, heads=MEM_HEADS, hd=W // MEM_HEADS),
        grid=(B, nb),
        in_specs=[pl.BlockSpec((tq, 2 * W), lambda b, i: (b * nb + i, 0)), kv, kv],
        out_specs=pl.BlockSpec((tq, W), lambda b, i: (b * nb + i, 0)),
        out_shape=jax.ShapeDtypeStruct((B * T, W), BF16),
        compiler_params=_params("parallel", "arbitrary"),
        name="mem_mix",
    )(qg, mk, mv)


def split_cols(x, sizes):
    return jnp.split(x, [int(c) for c in np.cumsum(sizes)[:-1]], axis=-1)


def head_rmsnorm(o, g):
    B, T, H, D = o.shape
    y = o * lax.rsqrt(jnp.mean(o * o, axis=-1, keepdims=True) + EPS)
    return y.reshape(B, T, H * D) * g.astype(F32)


def head_groupnorm(o, g):
    B, T, H, D = o.shape
    c = o - jnp.mean(o, axis=-1, keepdims=True)
    y = c * lax.rsqrt(jnp.mean(c * c, axis=-1, keepdims=True) + EPS)
    return y.reshape(B, T, H * D) * g.astype(F32)


def masked_softmax(s, valid):
    p = jax.nn.softmax(jnp.where(valid, s, NEG_INF), axis=-1)
    return p * valid.astype(F32)


def rel_bucket(dist):
    d = jnp.maximum(dist, 0)
    me = NUM_BUCKETS // 2
    logd = jnp.log(jnp.maximum(d, 1).astype(F32) / me)
    large = me + (logd / math.log(MAX_DISTANCE / me) * (NUM_BUCKETS - me)).astype(jnp.int32)
    return jnp.where(d < me, d, jnp.minimum(large, NUM_BUCKETS - 1))


def heads_bias(table, bucket):
    b = table.astype(F32)[bucket]
    return b.reshape(bucket.shape + (NSA_KVH, NSA_G)).transpose(2, 3, 0, 1)


def rotary(x, pos):
    half = x.shape[-1] // 2
    inv = ROPE_BASE ** (-jnp.arange(half, dtype=F32) / half)
    ang = pos.astype(F32)[:, None] * inv[None, :]
    cos = jnp.cos(ang)[None, :, None, :]
    sin = jnp.sin(ang)[None, :, None, :]
    x1, x2 = x[..., :half], x[..., half:]
    return jnp.concatenate([x1 * cos - x2 * sin, x1 * sin + x2 * cos], axis=-1)


def chunked_scan(step, S0, xs, chunk):
    B, T = xs[0].shape[:2]
    nc = T // chunk
    xs_c = tuple(jnp.moveaxis(a.reshape((B, nc, chunk) + a.shape[2:]), 1, 0) for a in xs)
    S, o = lax.scan(lambda St, c: step(St, *c), S0, xs_c)
    o = jnp.moveaxis(o, 0, 1)
    return o.reshape((B, T) + o.shape[3:]), S


def hgrn_prep(hq, hf, hv, lb, dv):
    B, T, _ = hq.shape
    f = lb[None, None, :] + (1.0 - lb[None, None, :]) * jax.nn.sigmoid(hf.astype(F32))
    q = hq.astype(F32).reshape(B, T, HG_HEADS, HG_DK) * HG_DK ** -0.5
    logf = jnp.log(f).reshape(B, T, HG_HEADS, HG_DK)
    k = (1.0 - f).reshape(B, T, HG_HEADS, HG_DK)
    v = hv.astype(F32).reshape(B, T, HG_HEADS, dv)
    return q, logf, k, v


def hgrn_chunk(S0, q, logf, k, v):
    C = q.shape[1]
    b = jnp.cumsum(logf, axis=1)
    o_inter = jnp.einsum('bthd,bhde->bthe', q * jnp.exp(b), S0)
    causal = jnp.tril(jnp.ones((C, C), dtype=bool))
    diff = b[:, :, None] - b[:, None, :]
    decay = jnp.exp(jnp.where(causal[None, :, :, None, None], diff, -jnp.inf))
    a = jnp.einsum('bthd,bshd,btshd->bhts', q, k, decay)
    o_intra = jnp.einsum('bhts,bshe->bthe', a, v)
    b_last = b[:, -1]
    S_new = jnp.exp(b_last)[..., None] * S0 + jnp.einsum('bshd,bshe->bhde', k * jnp.exp(b_last[:, None] - b), v)
    return S_new, o_inter + o_intra


def hgrn_out(o, gain, gate):
    return head_rmsnorm(o, gain) * jax.nn.silu(gate.astype(F32))


def nsa_prep(nq, kvs, hd):
    B, T, _ = nq.shape
    q = nq.reshape(B, T, NSA_KVH, NSA_G, hd) * hd ** -0.5
    kv = [a.reshape(B, T, NSA_KVH, hd) for a in split_cols(kvs, (NSA_KVH * hd,) * 6)]
    return q, kv


def compress(k, pe, w1, w2):
    B, L, KVH, HD = k.shape
    n_cb = L // CMP_BLOCK
    kb = k[:, : n_cb * CMP_BLOCK].reshape(B, n_cb, CMP_BLOCK, KVH, HD) + pe[None, None, :, None, :]
    kb = kb.transpose(0, 1, 3, 2, 4).reshape(B, n_cb, KVH, CMP_BLOCK * HD)
    return jax.nn.silu(kb @ w1) @ w2


def nsa_compressed(q, qpos, kc, vc, table):
    n_cb = kc.shape[1]
    end = (jnp.arange(n_cb, dtype=jnp.int32) + 1) * CMP_BLOCK - 1
    dist = qpos[:, None] - end[None, :]
    s = jnp.einsum('bqkgd,bnkd->bkgqn', q, kc).astype(F32) + heads_bias(table, rel_bucket(dist))
    p = masked_softmax(s, dist >= 0)
    o = jnp.einsum('bkgqn,bnkd->bqkgd', p.astype(vc.dtype), vc)
    return o, p


def select_blocks(p, qpos, L):
    B, KVH, G, Tq, n_cb = p.shape
    ratio = SEL_BLOCK // CMP_BLOCK
    n_sel = -(-L // SEL_BLOCK)
    imp = jnp.pad(p.sum(2), ((0, 0), (0, 0), (0, 0), (0, n_sel * ratio - n_cb)))
    imp = imp.reshape(B, KVH, Tq, n_sel, ratio).sum(-1)
    blk = jnp.arange(n_sel, dtype=jnp.int32)[None, :]
    cur = (qpos // SEL_BLOCK)[:, None]
    forced = (blk == 0) | (blk == cur) | (blk == cur - 1)
    valid = blk * SEL_BLOCK <= qpos[:, None]
    score = jnp.where(valid, jnp.where(forced, jnp.inf, imp), -jnp.inf)
    _, idx = lax.top_k(score, min(N_SELECT, n_sel))
    return idx


def pad_blocks(k):
    L = k.shape[1]
    n_sel = -(-L // SEL_BLOCK)
    return jnp.pad(k, ((0, 0), (0, n_sel * SEL_BLOCK - L), (0, 0), (0, 0)))


def nsa_selected(q, qpos, idx, ks, vs, table):
    B, Tq = q.shape[:2]
    n = idx.shape[-1] * SEL_BLOCK
    tok = (idx[..., None] * SEL_BLOCK + jnp.arange(SEL_BLOCK, dtype=jnp.int32)).reshape(B, NSA_KVH, Tq, n)
    bi = jnp.arange(B)[:, None, None, None]
    hi = jnp.arange(NSA_KVH)[None, :, None, None]
    kg = ks.transpose(0, 2, 1, 3)[bi, hi, tok]
    vg = vs.transpose(0, 2, 1, 3)[bi, hi, tok]
    dist = qpos[None, None, :, None] - tok
    bias = table.astype(F32).reshape(NUM_BUCKETS, NSA_KVH, NSA_G)[rel_bucket(dist), hi]
    s = jnp.einsum('bqkgd,bkqsd->bkgqs', q, kg).astype(F32) + jnp.moveaxis(bias, 4, 2)
    p = masked_softmax(s, (dist >= 0)[:, :, None])
    return jnp.einsum('bkgqs,bkqsd->bqkgd', p.astype(vg.dtype), vg)


def window_banded(q, kw, vw, table):
    B, T = q.shape[:2]
    hd = q.shape[-1]
    nb = T // WIN_QBLOCK
    span = WIN_QBLOCK + WINDOW
    pad = ((0, 0), (WINDOW, 0), (0, 0), (0, 0))
    gidx = jnp.arange(nb, dtype=jnp.int32)[:, None] * WIN_QBLOCK + jnp.arange(span, dtype=jnp.int32)[None, :]
    kb = jnp.pad(kw, pad)[:, gidx]
    vb = jnp.pad(vw, pad)[:, gidx]
    qb = q.reshape(B, nb, WIN_QBLOCK, NSA_KVH, NSA_G, hd)
    dist = jnp.arange(WIN_QBLOCK, dtype=jnp.int32)[:, None] + WINDOW - jnp.arange(span, dtype=jnp.int32)[None, :]
    kpos = gidx - WINDOW
    valid = (dist >= 0) & (dist < WINDOW) & (kpos[:, None, :] >= 0)
    s = jnp.einsum('bnqkgd,bnskd->bnkgqs', qb, kb).astype(F32) + heads_bias(table, rel_bucket(dist))
    p = masked_softmax(s, valid[None, :, None, None])
    o = jnp.einsum('bnkgqs,bnskd->bnqkgd', p.astype(vb.dtype), vb)
    return o.reshape(B, T, NSA_KVH, NSA_G, hd)


def window_dense(q, qpos, kw, vw, kpos, table):
    dist = qpos[:, None] - kpos[None, :]
    valid = (dist >= 0) & (dist < WINDOW) & (kpos[None, :] >= 0)
    s = jnp.einsum('bqkgd,bskd->bkgqs', q, kw).astype(F32) + heads_bias(table, rel_bucket(dist))
    p = masked_softmax(s, valid)
    return jnp.einsum('bkgqs,bskd->bqkgd', p.astype(vw.dtype), vw)


def nsa_combine(o_c, o_s, o_w, gate_logits, path):
    B, T = o_c.shape[:2]
    g = jax.nn.sigmoid(gate_logits.astype(F32)).reshape(B, T, NSA_KVH, NSA_G, 3)
    o = g[..., 0:1] * o_c.astype(F32) + g[..., 1:2] * o_s.astype(F32) + g[..., 2:3] * o_w.astype(F32)
    return o.reshape(B, T, -1) * jax.nn.silu(path.astype(F32))


def gather_pages(pool, page_table):
    B, NP = page_table.shape
    g = pool[page_table]
    return g.reshape((B, NP * PAGE_SIZE) + pool.shape[2:])


def retention_chunk(S0, q, k, v, log_gamma):
    C = q.shape[1]
    i = jnp.arange(C, dtype=F32)
    diff = i[:, None] - i[None, :]
    decay = jnp.where(diff[None] >= 0, jnp.exp(diff[None] * log_gamma[:, None, None]), 0.0)
    a = jnp.einsum('bthd,bshd->bhts', q, k) * decay[None]
    inner = jnp.einsum('bhts,bshe->bthe', a, v)
    q_dec = q * jnp.exp((i + 1.0)[:, None] * log_gamma[None, :])[None, :, :, None]
    cross = jnp.einsum('bthd,bhde->bthe', q_dec, S0)
    k_dec = k * jnp.exp((C - 1.0 - i)[:, None] * log_gamma[None, :])[None, :, :, None]
    S_new = jnp.exp(C * log_gamma)[None, :, None, None] * S0 + jnp.einsum('bshd,bshe->bhde', k_dec, v)
    return S_new, inner + cross


SAMPLE_ROWS = 16
MATMUL_TN = 512


class EvenCols:
    def __init__(self, hg_qk, hg_w, nsa_w, nsa_kvw):
        self.hd = nsa_w // NSA_HEADS
        self.kvw = nsa_kvw
        self.nsa_w = nsa_w
        self.q, self.f, self.v, self.g = 0, hg_qk, 2 * hg_qk, 2 * hg_qk + hg_w
        self.nq = 2 * hg_qk + 2 * hg_w
        self.kv = self.nq + nsa_w
        self.main_w = self.kv + 6 * nsa_kvw
        self.gate_w = 3 * NSA_G
        self.tail_path, self.tail_gate = 0, nsa_w
        used = nsa_w + NSA_KVH * LANES
        self.tail_w = -(-used // MATMUL_TN) * MATMUL_TN
        assert self.main_w % MATMUL_TN == 0

    def tail_weight(self, w_in):
        gate0 = self.main_w
        path0 = gate0 + NSA_KVH * self.gate_w
        parts = [w_in[:, path0:path0 + self.nsa_w]]
        for h in range(NSA_KVH):
            parts.append(jnp.pad(w_in[:, gate0 + h * self.gate_w:gate0 + (h + 1) * self.gate_w],
                                 ((0, 0), (0, LANES - self.gate_w))))
        w = jnp.concatenate(parts, axis=1)
        return jnp.pad(w, ((0, 0), (0, self.tail_w - w.shape[1])))


def even_prompt(h2, B, T, w_in, w_tail, w_out, lb, hg_gain, pe, w1k, w2k, w1v, w2v, table, ec):
    hd = ec.hd
    main = matmul(h2, w_in, n_cols=ec.main_w)
    tail = matmul(h2, w_tail)
    S0 = jnp.zeros((B, HG_HEADS, HG_DK, hg_gain.shape[0] // HG_HEADS), F32)
    y_hg, S = hgrn_mix(main, lb, hg_gain, S0, B, T, (ec.q, ec.f, ec.v, ec.g), HG_CHUNK, HGRN_SUB, HG_CHUNK)
    kvs = main[:, ec.kv:ec.main_w].reshape(B, T, 6, NSA_KVH, hd)
    kc, vc, ks, vs, kw, vw = [kvs[:, :, j] for j in range(6)]
    kcmp = compress(kc, pe, w1k, w2k).reshape(B, -1, NSA_KVH * hd)
    vcmp = compress(vc, pe, w1v, w2v).reshape(B, -1, NSA_KVH * hd)
    y_nsa = nsa_prompt(main, tail, kcmp, vcmp, table, B, T, (ec.nq, ec.kv, ec.tail_path, ec.tail_gate, hd))
    y = matmul(jnp.concatenate([y_hg, y_nsa], axis=-1), w_out)
    wb = min(WINDOW, T)
    return y, S, kc, vc, ks, vs, kw[:, T - wb:], vw[:, T - wb:]


def even_sample(h2, B, T, pos, past_len, S0, page_table, pool_ck, pool_cv, pool_sk, pool_sv, win_k, win_v,
                w_in, w_tail, w_out, lb, hg_gain, pe, w1k, w2k, w1v, w2v, table, ec):
    hd = ec.hd
    R = SAMPLE_ROWS
    main = matmul(h2, w_in, n_cols=ec.main_w)
    tail = matmul(h2, w_tail)
    y_hg, S = hgrn_mix(main, lb, hg_gain, S0.astype(F32), B, R, (ec.q, ec.f, ec.v, ec.g), R, R, T)
    m3 = main.reshape(B, R, -1)[:, :T]
    t3 = tail.reshape(B, R, -1)[:, :T]
    nq = m3[..., ec.nq:ec.kv]
    kvs = m3[..., ec.kv:ec.main_w]
    npath = t3[..., ec.tail_path:ec.tail_path + ec.nsa_w]
    ngate = jnp.concatenate([t3[..., ec.tail_gate + h * LANES:ec.tail_gate + h * LANES + ec.gate_w]
                             for h in range(NSA_KVH)], axis=-1)
    qn, (kc, vc, ks, vs, kw, vw) = nsa_prep(nq, kvs, hd)
    kc_all = jnp.concatenate([gather_pages(pool_ck, page_table).astype(kc.dtype), kc], axis=1)
    vc_all = jnp.concatenate([gather_pages(pool_cv, page_table).astype(vc.dtype), vc], axis=1)
    ks_all = jnp.concatenate([gather_pages(pool_sk, page_table).astype(ks.dtype), ks], axis=1)
    vs_all = jnp.concatenate([gather_pages(pool_sv, page_table).astype(vs.dtype), vs], axis=1)
    L = kc_all.shape[1]
    o_c, p_c = nsa_compressed(qn, pos, compress(kc_all, pe, w1k, w2k), compress(vc_all, pe, w1v, w2v), table)
    idx = select_blocks(p_c, pos, L)
    o_s = nsa_selected(qn, pos, idx, pad_blocks(ks_all), pad_blocks(vs_all), table)
    wb = win_k.shape[1]
    kw_all = jnp.concatenate([win_k.astype(kw.dtype), kw], axis=1)
    vw_all = jnp.concatenate([win_v.astype(vw.dtype), vw], axis=1)
    kpos = past_len - wb + jnp.arange(wb + T, dtype=jnp.int32)
    o_w = window_dense(qn, pos, kw_all, vw_all, kpos, table)
    y_nsa = nsa_combine(o_c, o_s, o_w, ngate, npath)
    y_nsa = jnp.pad(y_nsa.astype(BF16), ((0, 0), (0, R - T), (0, 0))).reshape(B * R, -1)
    y = matmul(jnp.concatenate([y_hg, y_nsa], axis=-1), w_out)
    return y, S, kc, vc, ks, vs, kw_all[:, T:], vw_all[:, T:]


def retention_layer(h2, B, T, pos, S0, w_in, w_out, gain, C, c_eff):
    y, S = retention_mix(matmul(h2, w_in), pos, gain, S0, B, T, C, c_eff)
    return matmul(y, w_out), S


def mem_attend(h2, B, T, mk, mv, w_q, w_o):
    slots = mk.shape[1]
    o = mem_mix(matmul(h2, w_q), mk.reshape(B, slots, -1), mv.reshape(B, slots, -1), B, T)
    return matmul(o, w_o)


def kernel(x_prompt, x_sample, mem_prompt, state_hgrn, cache_cmp_k, cache_cmp_v, cache_sel_k, cache_sel_v,
           cache_win_k, cache_win_v, state_ret, cache_mem_k, cache_mem_v, page_table, rel_table,
           norm_mix_pre, norm_mix_post, norm_mem_pre, norm_mem_post, ev_w_in, ev_w_out, hgrn_lb, hgrn_norm,
           cmp_pe, cmp_w1_k, cmp_w2_k, cmp_w1_v, cmp_w2_v, od_w_in, od_w_out, ret_norm,
           mem_w_q, mem_w_k, mem_w_v, mem_w_o):
    B, T, D = x_prompt.shape
    Bs, Ts, _ = x_sample.shape
    depth = norm_mix_pre.shape[0]
    past_len = page_table.shape[1] * PAGE_SIZE
    pos_p = jnp.arange(T, dtype=jnp.int32)
    pos_s = past_len + jnp.arange(Ts, dtype=jnp.int32)
    lb_all = jnp.cumsum(jax.nn.softmax(hgrn_lb.astype(F32), axis=0), axis=0)
    hg_qk = hgrn_lb.shape[1]
    hg_w = hgrn_norm.shape[1]
    nsa_w = ev_w_out.shape[1] - hg_w
    nsa_kvw = NSA_KVH * (nsa_w // NSA_HEADS)
    ec = EvenCols(hg_qk, hg_w, nsa_w, nsa_kvw)
    R = SAMPLE_ROWS
    assert Ts <= R

    even_p, even_s, ret_p, ret_s, memk_p, memv_p = [], [], [], [], [], []
    xp = x_prompt.reshape(B * T, D)
    xs = jnp.pad(x_sample, ((0, 0), (0, R - Ts), (0, 0))).reshape(Bs * R, D)
    pos_sr = past_len + jnp.arange(R, dtype=jnp.int32)
    mem2 = mem_prompt.reshape(-1, D).astype(BF16)
    ML = mem_prompt.shape[1]
    for l in range(depth):
        hp = rmsnorm_bf16(xp, norm_mix_pre[l])
        hs = rmsnorm_bf16(xs, norm_mix_pre[l])
        if l % 2 == 0:
            e = l // 2
            w = (ev_w_in[e], ec.tail_weight(ev_w_in[e]), ev_w_out[e], lb_all[e], hgrn_norm[e], cmp_pe[e],
                 cmp_w1_k[e], cmp_w2_k[e], cmp_w1_v[e], cmp_w2_v[e], rel_table)
            yp, *sp = even_prompt(hp, B, T, *w, ec)
            ys, *ss = even_sample(hs, Bs, Ts, pos_s, past_len, state_hgrn[e], page_table, cache_cmp_k[e],
                                  cache_cmp_v[e], cache_sel_k[e], cache_sel_v[e], cache_win_k[e], cache_win_v[e],
                                  *w, ec)
            even_p.append(sp)
            even_s.append(ss)
        else:
            o = l // 2
            S0p = jnp.zeros((B,) + state_ret.shape[2:], F32)
            yp, sp = retention_layer(hp, B, T, pos_p, S0p, od_w_in[o], od_w_out[o], ret_norm[o],
                                     RET_CHUNK, RET_CHUNK)
            ys, ss = retention_layer(hs, Bs, R, pos_sr, state_ret[o].astype(F32), od_w_in[o], od_w_out[o],
                                     ret_norm[o], R, Ts)
            ret_p.append(sp)
            ret_s.append(ss)
        xp = residual_post(xp, yp, norm_mix_post[l])
        xs = residual_post(xs, ys, norm_mix_post[l])
        mk_p = matmul(mem2, mem_w_k[l]).reshape(B, ML, MEM_HEADS, MEM_HD)
        mv_p = matmul(mem2, mem_w_v[l]).reshape(B, ML, MEM_HEADS, MEM_HD)
        memk_p.append(mk_p)
        memv_p.append(mv_p)
        hp = rmsnorm_bf16(xp, norm_mem_pre[l])
        hs = rmsnorm_bf16(xs, norm_mem_pre[l])
        yp = mem_attend(hp, B, T, mk_p, mv_p, mem_w_q[l], mem_w_o[l])
        ys = mem_attend(hs, Bs, R, cache_mem_k[l], cache_mem_v[l], mem_w_q[l], mem_w_o[l])
        xp = residual_post(xp, yp, norm_mem_post[l])
        xs = residual_post(xs, ys, norm_mem_post[l])
    p_hgrn, p_cmp_k, p_cmp_v, p_sel_k, p_sel_v, p_win_k, p_win_v = [jnp.stack(a) for a in zip(*even_p)]
    s_hgrn, s_cmp_k, s_cmp_v, s_sel_k, s_sel_v, s_win_k, s_win_v = [jnp.stack(a) for a in zip(*even_s)]
    p_ret = jnp.stack(ret_p)
    s_ret = jnp.stack(ret_s)
    p_mem_k = jnp.stack(memk_p)
    p_mem_v = jnp.stack(memv_p)
    return (xp.reshape(B, T, D), xs.reshape(Bs, R, D)[:, :Ts], p_hgrn, p_cmp_k, p_cmp_v, p_sel_k, p_sel_v,
            p_win_k, p_win_v, p_ret, p_mem_k, p_mem_v,
            s_hgrn, s_cmp_k, s_cmp_v, s_sel_k, s_sel_v, s_win_k, s_win_v, s_ret)
```

```python
import functools
import math

import jax
import jax.numpy as jnp
import numpy as np
from jax import lax
from jax.experimental import pallas as pl
from jax.experimental.pallas import tpu as pltpu

F32 = jnp.float32
BF16 = jnp.bfloat16
EPS = 1e-6
NEG_INF = -1e30

PAGE_SIZE = 128
HG_HEADS = 16
HG_DK = 128
HG_CHUNK = 64
HGRN_SUB = 16
NSA_HEADS = 16
NSA_KVH = 2
NSA_G = NSA_HEADS // NSA_KVH
CMP_BLOCK = 32
SEL_BLOCK = 64
N_SELECT = 16
WINDOW = 512
SEL_QBLOCK = 64
WIN_QBLOCK = 128
NUM_BUCKETS = 32
MAX_DISTANCE = 1024
RET_HEADS = 16
RET_CHUNK = 128
ROPE_BASE = 10000.0
MEM_HEADS = 4
MEM_HD = 128
MEM_W = MEM_HEADS * MEM_HD

VMEM_LIMIT_BYTES = 56 * 1024 * 1024


def _params(*sem):
    return pltpu.CompilerParams(dimension_semantics=sem, vmem_limit_bytes=VMEM_LIMIT_BYTES)


def _rmsnorm_kernel(x_ref, g_ref, o_ref):
    x = x_ref[...]
    y = x * lax.rsqrt(jnp.mean(x * x, axis=-1, keepdims=True) + EPS)
    o_ref[...] = (y * g_ref[...]).astype(o_ref.dtype)


def rmsnorm_bf16(x, g):
    M, D = x.shape
    tm = min(M, 512)
    return pl.pallas_call(
        _rmsnorm_kernel,
        grid=(M // tm,),
        in_specs=[pl.BlockSpec((tm, D), lambda i: (i, 0)), pl.BlockSpec((1, D), lambda i: (0, 0))],
        out_specs=pl.BlockSpec((tm, D), lambda i: (i, 0)),
        out_shape=jax.ShapeDtypeStruct((M, D), BF16),
        compiler_params=_params("parallel"),
        name="rmsnorm_bf16",
    )(x, g.reshape(1, D))


def _post_kernel(x_ref, y_ref, g_ref, o_ref):
    y = y_ref[...]
    n = y * lax.rsqrt(jnp.mean(y * y, axis=-1, keepdims=True) + EPS)
    o_ref[...] = x_ref[...] + n * g_ref[...]


def residual_post(x, y, g):
    M, D = x.shape
    tm = min(M, 512)
    return pl.pallas_call(
        _post_kernel,
        grid=(M // tm,),
        in_specs=[pl.BlockSpec((tm, D), lambda i: (i, 0)), pl.BlockSpec((tm, D), lambda i: (i, 0)),
                  pl.BlockSpec((1, D), lambda i: (0, 0))],
        out_specs=pl.BlockSpec((tm, D), lambda i: (i, 0)),
        out_shape=jax.ShapeDtypeStruct((M, D), F32),
        compiler_params=_params("parallel"),
        name="residual_post",
    )(x, y, g.reshape(1, D))


def _matmul_kernel(x_ref, w_ref, o_ref, wb_ref):
    @pl.when(pl.program_id(1) == 0)
    def _():
        wb_ref[...] = w_ref[...].astype(BF16)

    o_ref[...] = jnp.dot(x_ref[...], wb_ref[...], preferred_element_type=F32)


def matmul(x, w, n_cols=None):
    M, K = x.shape
    N = w.shape[1] if n_cols is None else n_cols
    tn = 512 if K <= 4096 else 256
    tn = min(tn, N)
    tm = min(M, 1024 if K <= 4096 else 512)
    assert N % tn == 0 and M % tm == 0, (M, K, N)
    return pl.pallas_call(
        _matmul_kernel,
        grid=(N // tn, M // tm),
        in_specs=[pl.BlockSpec((tm, K), lambda n, m: (m, 0)), pl.BlockSpec((K, tn), lambda n, m: (0, n))],
        out_specs=pl.BlockSpec((tm, tn), lambda n, m: (m, n)),
        out_shape=jax.ShapeDtypeStruct((M, N), F32),
        scratch_shapes=[pltpu.VMEM((K, tn), BF16)],
        compiler_params=_params("parallel", "arbitrary"),
        name="matmul",
    )(x, w)


LANES = 128
NSA_TQ = 128
_NT = (((1,), (1,)), ((), ()))


_TN = (((0,), (0,)), ((), ()))


def _bias_tile_kernel(thr_ref, table_ref, o_ref, *, row_stride, row_offset):
    kvh = pl.program_id(0)
    step = pl.program_id(1)
    row = lax.broadcasted_iota(jnp.int32, (LANES, LANES), 0)
    lane = lax.broadcasted_iota(jnp.int32, (LANES, LANES), 1)
    dist = LANES * step + lane - row_stride * row - row_offset
    for g in range(NSA_G):
        h = kvh * NSA_G + g
        bias = jnp.full((LANES, LANES), table_ref[0, h], F32)
        for k in range(1, NUM_BUCKETS):
            bias = jnp.where(dist >= thr_ref[k], table_ref[k, h], bias)
        o_ref[0, 0, :, g * LANES:(g + 1) * LANES] = bias


def bias_tiles(table, n_steps, max_dist, row_stride, row_offset):
    bucket = rel_bucket(jnp.arange(max_dist, dtype=jnp.int32))
    thr = jnp.sum(bucket[None, :] < jnp.arange(NUM_BUCKETS, dtype=jnp.int32)[:, None], axis=1).astype(jnp.int32)
    smem = pl.BlockSpec(memory_space=pltpu.SMEM)
    return pl.pallas_call(
        functools.partial(_bias_tile_kernel, row_stride=row_stride, row_offset=row_offset),
        grid=(NSA_KVH, n_steps),
        in_specs=[smem, smem],
        out_specs=pl.BlockSpec((1, 1, LANES, NSA_G * LANES), lambda h, s: (h, s, 0, 0)),
        out_shape=jax.ShapeDtypeStruct((NSA_KVH, n_steps, LANES, NSA_G * LANES), F32),
        compiler_params=_params("parallel", "parallel"),
        name="bias_tiles",
    )(thr, table.astype(F32))


def _nsa_prompt_kernel(q_ref, kc_ref, vc_ref, ks_ref, vs_ref, kw_ref, vw_ref, cb_ref, tb_ref, gate_ref, path_ref,
                       o_ref, qb_s, oc_s, m_s, l_s, acc_s, *, hd, n_win_chunks, n_pick):
    i = pl.program_id(2)
    tq = NSA_TQ
    G = NSA_G
    scale = hd ** -0.5
    row = lax.broadcasted_iota(jnp.int32, (LANES, tq), 0)
    qpos = i * tq + lax.broadcasted_iota(jnp.int32, (LANES, tq), 1)
    slab = lambda g: slice(g * tq, (g + 1) * tq)

    for g in range(G):
        qb_s[slab(g), :] = (q_ref[:, g * hd:(g + 1) * hd] * scale).astype(BF16)
    qb = qb_s[...]

    s_all = lax.dot_general(kc_ref[0].astype(BF16), qb, _NT, preferred_element_type=F32) + cb_ref[0, 0]
    valid_c = qpos >= (row + 1) * CMP_BLOCK - 1
    valid_cf = valid_c.astype(F32)
    imp = jnp.zeros((LANES, tq), F32)
    ps = []
    for g in range(G):
        s = jnp.where(valid_c, s_all[:, slab(g)], NEG_INF)
        e = jnp.exp(s - jnp.max(s, axis=0, keepdims=True))
        p = e / jnp.sum(e, axis=0, keepdims=True) * valid_cf
        imp = imp + p
        ps.append(p.astype(BF16))
    oc_s[...] = lax.dot_general(vc_ref[0].astype(BF16), jnp.concatenate(ps, axis=1), _TN,
                                preferred_element_type=F32)

    imp2 = imp + pltpu.roll(imp, LANES - 1, axis=0)
    blk = row >> 1
    cur = qpos // SEL_BLOCK
    forced = (blk == 0) | (blk == cur) | (blk == cur - 1)
    usable = ((row & 1) == 0) & (blk * SEL_BLOCK <= qpos)
    work = jnp.where(usable, jnp.where(forced, jnp.inf, imp2), -jnp.inf)
    sel = jnp.zeros((LANES, tq), F32)
    for _ in range(n_pick):
        mx = jnp.max(work, axis=0, keepdims=True)
        first = jnp.min(jnp.where(work == mx, row, LANES), axis=0, keepdims=True)
        pick = row == first
        sel = jnp.where(pick, 1.0, sel)
        work = jnp.where(pick, -jnp.inf, work)
    sel_b = sel.astype(BF16)

    m_s[...] = jnp.full(m_s.shape, NEG_INF, F32)
    l_s[...] = jnp.zeros(l_s.shape, F32)
    acc_s[...] = jnp.zeros(acc_s.shape, F32)
    ekey = lax.broadcasted_iota(jnp.int32, (LANES, LANES), 0)
    eblk = lax.broadcasted_iota(jnp.int32, (LANES, LANES), 1)

    def online(slot, k_b, v_b, bias, valid):
        s_all = lax.dot_general(k_b, qb, _NT, preferred_element_type=F32) + bias
        ps, alphas = [], []
        for g in range(G):
            s = jnp.where(valid, s_all[:, slab(g)], NEG_INF)
            m_old = m_s[slot, :, slab(g)]
            m_new = jnp.maximum(m_old, jnp.max(s, axis=0, keepdims=True))
            alpha = jnp.exp(m_old - m_new)
            p = jnp.where(valid, jnp.exp(s - m_new), 0.0)
            l_s[slot, :, slab(g)] = alpha * l_s[slot, :, slab(g)] + jnp.sum(p, axis=0, keepdims=True)
            m_s[slot, :, slab(g)] = m_new
            ps.append(p.astype(BF16))
            alphas.append(alpha)
        pv = lax.dot_general(v_b, jnp.concatenate(ps, axis=1), _TN, preferred_element_type=F32)
        acc_s[slot] = jnp.concatenate(alphas, axis=1) * acc_s[slot] + pv

    def chunk(c, carry):
        k0 = pl.multiple_of(c * LANES, LANES)
        delta = i - c
        dist = qpos - (k0 + row)
        causal = dist >= 0
        expand = (eblk == 2 * (c * (LANES // SEL_BLOCK) + (ekey // SEL_BLOCK))).astype(BF16)
        chosen = jnp.dot(expand, sel_b, preferred_element_type=F32) > 0.5
        bias = tb_ref[0, delta]
        online(0, ks_ref[pl.ds(k0, LANES), :].astype(BF16), vs_ref[pl.ds(k0, LANES), :].astype(BF16), bias,
               chosen & causal)

        @pl.when(delta < n_win_chunks)
        def _():
            online(1, kw_ref[pl.ds(k0, LANES), :].astype(BF16), vw_ref[pl.ds(k0, LANES), :].astype(BF16), bias,
                   causal & (dist < WINDOW))

        return carry

    lax.fori_loop(0, i + 1, chunk, 0)

    gates = jax.nn.sigmoid(gate_ref[...]).T
    for g in range(G):
        o_sel = acc_s[0, :, slab(g)] / l_s[0, :, slab(g)]
        o_win = acc_s[1, :, slab(g)] / l_s[1, :, slab(g)]
        o = (gates[3 * g:3 * g + 1] * oc_s[:, slab(g)] + gates[3 * g + 1:3 * g + 2] * o_sel
             + gates[3 * g + 2:3 * g + 3] * o_win)
        o_ref[:, g * hd:(g + 1) * hd] = (o.T * jax.nn.silu(path_ref[:, g * hd:(g + 1) * hd])).astype(o_ref.dtype)


def nsa_prompt(main, tail, kcmp, vcmp, table, B, T, cols):
    q_col, kv_col, path_col, gate_col, hd = cols
    G = NSA_G
    tq = NSA_TQ
    n_cb = kcmp.shape[1]
    assert n_cb <= LANES and T % tq == 0 and hd == LANES
    nd = T // LANES
    n_win_chunks = WINDOW // LANES + 1
    tb = bias_tiles(table, nd, T, 1, 0)
    cb = bias_tiles(table, T // tq, T, CMP_BLOCK, CMP_BLOCK - 1)
    pad = ((0, 0), (0, LANES - n_cb), (0, 0))
    kcmp = jnp.pad(kcmp, pad)
    vcmp = jnp.pad(vcmp, pad)
    nblk = T // tq
    gw = G * hd
    row_map = lambda b, h, i: (b * nblk + i, 0)
    kv_spec = lambda j: pl.BlockSpec((T, hd), lambda b, h, i: (b, kv_col // hd + 2 * j + h))
    return pl.pallas_call(
        functools.partial(_nsa_prompt_kernel, hd=hd, n_win_chunks=n_win_chunks,
                          n_pick=min(N_SELECT, -(-T // SEL_BLOCK))),
        grid=(B, NSA_KVH, nblk),
        in_specs=[
            pl.BlockSpec((tq, gw), lambda b, h, i: (b * nblk + i, q_col // gw + h)),
            pl.BlockSpec((1, LANES, hd), lambda b, h, i: (b, 0, h)),
            pl.BlockSpec((1, LANES, hd), lambda b, h, i: (b, 0, h)),
            kv_spec(2), kv_spec(3), kv_spec(4), kv_spec(5),
            pl.BlockSpec((1, 1, LANES, gw), lambda b, h, i: (h, i, 0, 0)),
            pl.BlockSpec((1, nd, LANES, gw), lambda b, h, i: (h, 0, 0, 0)),
            pl.BlockSpec((tq, LANES), lambda b, h, i: (b * nblk + i, gate_col // LANES + h)),
            pl.BlockSpec((tq, gw), lambda b, h, i: (b * nblk + i, path_col // gw + h)),
        ],
        out_specs=pl.BlockSpec((tq, gw), lambda b, h, i: (b * nblk + i, h)),
        out_shape=jax.ShapeDtypeStruct((B * T, NSA_KVH * gw), BF16),
        scratch_shapes=[pltpu.VMEM((G * tq, hd), BF16), pltpu.VMEM((hd, G * tq), F32),
                        pltpu.VMEM((2, 1, G * tq), F32), pltpu.VMEM((2, 1, G * tq), F32),
                        pltpu.VMEM((2, hd, G * tq), F32)],
        compiler_params=_params("parallel", "parallel", "arbitrary"),
        name="nsa_prompt",
    )(main, kcmp, vcmp, main, main, main, main, cb, tb, tail, tail)


_TN = (((0,), (0,)), ((), ()))


def _cumsum_rows(x, n):
    row = lax.broadcasted_iota(jnp.int32, x.shape, 0)
    sh = 1
    while sh < n:
        x = x + jnp.where(row >= sh, pltpu.roll(x, sh, axis=0), 0.0)
        sh *= 2
    return x


def _hgrn_kernel(q_ref, f_ref, v_ref, gate_ref, lb_ref, gain_ref, s0_ref, y_ref, s_out_ref, st_s,
                 *, C, SB, c_eff, dk, HB):
    c = pl.program_id(2)

    @pl.when(c == 0)
    def _():
        for hh in range(HB):
            st_s[hh] = s0_ref[0, hh].T

    for hh in range(HB):
        _hgrn_head(hh, slice(hh * dk, (hh + 1) * dk), q_ref, f_ref, v_ref, gate_ref, lb_ref, gain_ref,
                   y_ref, st_s, C=C, SB=SB, c_eff=c_eff, dk=dk)

    @pl.when(c == pl.num_programs(2) - 1)
    def _():
        for hh in range(HB):
            s_out_ref[0, hh] = st_s[hh].T


def _hgrn_head(hh, sl, q_ref, f_ref, v_ref, gate_ref, lb_ref, gain_ref, y_ref, st_s, *, C, SB, c_eff, dk):
    lb = lb_ref[:, sl]
    f = lb + (1.0 - lb) * jax.nn.sigmoid(f_ref[:, sl])
    logf = jnp.log(f)
    k = 1.0 - f
    if c_eff < C:
        real = lax.broadcasted_iota(jnp.int32, (C, dk), 0) < c_eff
        logf = jnp.where(real, logf, 0.0)
        k = jnp.where(real, k, 0.0)
    q = q_ref[:, sl] * dk ** -0.5
    v_b = v_ref[:, sl].astype(BF16)
    b = _cumsum_rows(logf, C)
    st = st_s[hh]
    o = lax.dot_general((q * jnp.exp(b)).astype(BF16), st.astype(BF16), _NT, preferred_element_type=F32)

    lane_c = lax.broadcasted_iota(jnp.int32, (SB, C), 1)
    row_c = lax.broadcasted_iota(jnp.int32, (SB, C), 0)
    outs = []
    for I in range(C // SB):
        r0 = I * SB
        q_i = q[r0:r0 + SB]
        b_i = b[r0:r0 + SB]
        if I > 0:
            b_r = b[r0 - 1:r0]
            qq = (q_i * jnp.exp(b_i - b_r)).astype(BF16)
            kk = (k * jnp.exp(jnp.minimum(b_r - b, 0.0))).astype(BF16)
            a = lax.dot_general(qq, kk, _NT, preferred_element_type=F32)
            a = jnp.where(lane_c < r0, a, 0.0)
        else:
            a = jnp.zeros((SB, C), F32)
        for s in range(SB):
            z = q_i * k[r0 + s:r0 + s + 1] * jnp.exp(jnp.minimum(b_i - b[r0 + s:r0 + s + 1], 0.0))
            col = jnp.sum(z, axis=-1, keepdims=True)
            a = jnp.where((lane_c == r0 + s) & (row_c >= s), col, a)
        outs.append(jnp.dot(a.astype(BF16), v_b, preferred_element_type=F32))
    o = o + jnp.concatenate(outs, axis=0)

    b_last = b[C - 1:C]
    kd = (k * jnp.exp(b_last - b)).astype(BF16)
    st_new = jnp.exp(b_last) * st + lax.dot_general(v_b, kd, _TN, preferred_element_type=F32)
    st_s[hh] = st_new

    y = o * lax.rsqrt(jnp.mean(o * o, axis=-1, keepdims=True) + EPS) * gain_ref[:, sl]
    y_ref[:, sl] = (y * jax.nn.silu(gate_ref[:, sl])).astype(y_ref.dtype)


HGRN_HEADS_PER_STEP = 8


def hgrn_mix(main, lb, gain, s0, B, T, cols, C, SB, c_eff):
    q_col, f_col, v_col, g_col = cols
    _, H, dk, dv = s0.shape
    HB = HGRN_HEADS_PER_STEP
    W = HB * LANES
    assert dk == LANES and dv == LANES and T % C == 0 and C % SB == 0 and H % HB == 0
    assert all(off % W == 0 for off in cols)
    nc = T // C
    col = lambda off: pl.BlockSpec((C, W), lambda b, h, c: (b * nc + c, off // W + h))
    vec = pl.BlockSpec((1, W), lambda b, h, c: (0, h))
    st = pl.BlockSpec((1, HB, dk, dv), lambda b, h, c: (b, h, 0, 0))
    return pl.pallas_call(
        functools.partial(_hgrn_kernel, C=C, SB=SB, c_eff=c_eff, dk=dk, HB=HB),
        grid=(B, H // HB, nc),
        in_specs=[col(q_col), col(f_col), col(v_col), col(g_col), vec, vec, st],
        out_specs=[pl.BlockSpec((C, W), lambda b, h, c: (b * nc + c, h)), st],
        out_shape=[jax.ShapeDtypeStruct((B * T, H * dv), BF16), jax.ShapeDtypeStruct(s0.shape, F32)],
        scratch_shapes=[pltpu.VMEM((HB, dv, dk), F32)],
        compiler_params=_params("parallel", "parallel", "arbitrary"),
        name="hgrn_mix",
    )(main, main, main, main, lb.reshape(1, -1), gain.reshape(1, -1), s0)


def _retention_kernel(q_ref, k_ref, v_ref, g_ref, cos_ref, sin_ref, lg_ref, gain_ref, s0_ref, y_ref, s_out_ref, s_s,
                      *, C, c_eff, dk):
    c = pl.program_id(2)

    @pl.when(c == 0)
    def _():
        s_s[...] = s0_ref[0, 0]

    half = dk // 2
    cos = cos_ref[...]
    sin = sin_ref[...]
    lg_w = lg_ref[0]
    lg = lg_w[:, :LANES]
    row = lax.broadcasted_iota(jnp.int32, (C, LANES), 0).astype(F32)

    def rot(ref, w):
        x1 = ref[:, :half]
        x2 = ref[:, half:]
        return jnp.concatenate([(x1 * cos - x2 * sin) * w, (x1 * sin + x2 * cos) * w], axis=1)

    q = rot(q_ref, dk ** -0.5)
    k = rot(k_ref, 1.0)
    v_b = v_ref[...].astype(BF16)
    s_old = s_s[...]

    a = lax.dot_general(q.astype(BF16), k.astype(BF16), _NT, preferred_element_type=F32)
    ti = lax.broadcasted_iota(jnp.int32, (C, C), 0)
    si = lax.broadcasted_iota(jnp.int32, (C, C), 1)
    diff = (ti - si).astype(F32)
    a = a * jnp.where(diff >= 0, jnp.exp(diff * lg_w[:, :C]), 0.0)
    inner = jnp.dot(a.astype(BF16), v_b, preferred_element_type=F32)
    q_w = jnp.exp((row + 1.0) * lg)
    q_dec = q * jnp.concatenate([q_w] * (dk // LANES), axis=1)
    cross = jnp.dot(q_dec.astype(BF16), s_old.astype(BF16), preferred_element_type=F32)
    k_w = jnp.where(row < c_eff, jnp.exp((c_eff - 1.0 - row) * lg), 0.0)
    k_dec = k * jnp.concatenate([k_w] * (dk // LANES), axis=1)
    s_new = jnp.exp(c_eff * lg_w) * s_old + lax.dot_general(k_dec.astype(BF16), v_b, _TN, preferred_element_type=F32)
    s_s[...] = s_new

    o = inner + cross
    cen = o - jnp.mean(o, axis=-1, keepdims=True)
    y = cen * lax.rsqrt(jnp.mean(cen * cen, axis=-1, keepdims=True) + EPS) * gain_ref[...]
    y_ref[...] = (y * jax.nn.silu(g_ref[...])).astype(y_ref.dtype)

    @pl.when(c == pl.num_programs(2) - 1)
    def _():
        s_out_ref[0, 0] = s_new


def retention_mix(proj, pos, gain, s0, B, T, C, c_eff):
    _, H, dk, dv = s0.shape
    assert T % C == 0 and C <= LANES and dk % LANES == 0
    nc = T // C
    half = dk // 2
    inv = ROPE_BASE ** (-jnp.arange(half, dtype=F32) / half)
    ang = pos.astype(F32)[:, None] * inv[None, :]
    log_gamma = jnp.log1p(-jnp.exp2(-5.0 - jnp.arange(H, dtype=F32)))
    lg = jnp.broadcast_to(log_gamma[:, None, None], (H, 1, dv))
    qk = lambda j: pl.BlockSpec((C, dk), lambda b, h, c: (b * nc + c, j * H + h))
    vg = lambda j: pl.BlockSpec((C, dv), lambda b, h, c: (b * nc + c, (2 * H * dk) // dv + j * H + h))
    tab = pl.BlockSpec((C, half), lambda b, h, c: (c, 0))
    st = pl.BlockSpec((1, 1, dk, dv), lambda b, h, c: (b, h, 0, 0))
    return pl.pallas_call(
        functools.partial(_retention_kernel, C=C, c_eff=c_eff, dk=dk),
        grid=(B, H, nc),
        in_specs=[qk(0), qk(1), vg(0), vg(1), tab, tab,
                  pl.BlockSpec((1, 1, dv), lambda b, h, c: (h, 0, 0)),
                  pl.BlockSpec((1, dv), lambda b, h, c: (0, h)), st],
        out_specs=[pl.BlockSpec((C, dv), lambda b, h, c: (b * nc + c, h)), st],
        out_shape=[jax.ShapeDtypeStruct((B * T, H * dv), BF16), jax.ShapeDtypeStruct(s0.shape, F32)],
        scratch_shapes=[pltpu.VMEM((dk, dv), F32)],
        compiler_params=_params("parallel", "parallel", "arbitrary"),
        name="retention_mix",
    )(proj, proj, proj, proj, jnp.cos(ang), jnp.sin(ang), lg, gain.reshape(1, -1), s0)


def _mem_kernel(qg_ref, mk_ref, mv_ref, o_ref, *, heads, hd):
    for h in range(heads):
        sl = slice(h * hd, (h + 1) * hd)
        q = (qg_ref[:, sl] * hd ** -0.5).astype(BF16)
        s = lax.dot_general(q, mk_ref[0, :, sl].astype(BF16), _NT, preferred_element_type=F32)
        e = jnp.exp(s - jnp.max(s, axis=-1, keepdims=True))
        p = e / jnp.sum(e, axis=-1, keepdims=True)
        o = jnp.dot(p.astype(BF16), mv_ref[0, :, sl].astype(BF16), preferred_element_type=F32)
        gate = qg_ref[:, heads * hd + h * hd:heads * hd + (h + 1) * hd]
        o_ref[:, sl] = (o * jax.nn.silu(gate)).astype(o_ref.dtype)


def mem_mix(qg, mk, mv, B, T):
    W = mk.shape[2]
    tq = min(T, 256)
    nb = T // tq
    kv = pl.BlockSpec((1, mk.shape[1], W), lambda b, i: (b, 0, 0))
    return pl.pallas_call(
        functools.partial(_mem_kernel, heads=MEM_HEADS, hd=W // MEM_HEADS),
        grid=(B, nb),
        in_specs=[pl.BlockSpec((tq, 2 * W), lambda b, i: (b * nb + i, 0)), kv, kv],
        out_specs=pl.BlockSpec((tq, W), lambda b, i: (b * nb + i, 0)),
        out_shape=jax.ShapeDtypeStruct((B * T, W), BF16),
        compiler_params=_params("parallel", "arbitrary"),
        name="mem_mix",
    )(qg, mk, mv)


def split_cols(x, sizes):
    return jnp.split(x, [int(c) for c in np.cumsum(sizes)[:-1]], axis=-1)


def head_rmsnorm(o, g):
    B, T, H, D = o.shape
    y = o * lax.rsqrt(jnp.mean(o * o, axis=-1, keepdims=True) + EPS)
    return y.reshape(B, T, H * D) * g.astype(F32)


def head_groupnorm(o, g):
    B, T, H, D = o.shape
    c = o - jnp.mean(o, axis=-1, keepdims=True)
    y = c * lax.rsqrt(jnp.mean(c * c, axis=-1, keepdims=True) + EPS)
    return y.reshape(B, T, H * D) * g.astype(F32)


def masked_softmax(s, valid):
    p = jax.nn.softmax(jnp.where(valid, s, NEG_INF), axis=-1)
    return p * valid.astype(F32)


def rel_bucket(dist):
    d = jnp.maximum(dist, 0)
    me = NUM_BUCKETS // 2
    logd = jnp.log(jnp.maximum(d, 1).astype(F32) / me)
    large = me + (logd / math.log(MAX_DISTANCE / me) * (NUM_BUCKETS - me)).astype(jnp.int32)
    return jnp.where(d < me, d, jnp.minimum(large, NUM_BUCKETS - 1))


def heads_bias(table, bucket):
    b = table.astype(F32)[bucket]
    return b.reshape(bucket.shape + (NSA_KVH, NSA_G)).transpose(2, 3, 0, 1)


def rotary(x, pos):
    half = x.shape[-1] // 2
    inv = ROPE_BASE ** (-jnp.arange(half, dtype=F32) / half)
    ang = pos.astype(F32)[:, None] * inv[None, :]
    cos = jnp.cos(ang)[None, :, None, :]
    sin = jnp.sin(ang)[None, :, None, :]
    x1, x2 = x[..., :half], x[..., half:]
    return jnp.concatenate([x1 * cos - x2 * sin, x1 * sin + x2 * cos], axis=-1)


def chunked_scan(step, S0, xs, chunk):
    B, T = xs[0].shape[:2]
    nc = T // chunk
    xs_c = tuple(jnp.moveaxis(a.reshape((B, nc, chunk) + a.shape[2:]), 1, 0) for a in xs)
    S, o = lax.scan(lambda St, c: step(St, *c), S0, xs_c)
    o = jnp.moveaxis(o, 0, 1)
    return o.reshape((B, T) + o.shape[3:]), S


def hgrn_prep(hq, hf, hv, lb, dv):
    B, T, _ = hq.shape
    f = lb[None, None, :] + (1.0 - lb[None, None, :]) * jax.nn.sigmoid(hf.astype(F32))
    q = hq.astype(F32).reshape(B, T, HG_HEADS, HG_DK) * HG_DK ** -0.5
    logf = jnp.log(f).reshape(B, T, HG_HEADS, HG_DK)
    k = (1.0 - f).reshape(B, T, HG_HEADS, HG_DK)
    v = hv.astype(F32).reshape(B, T, HG_HEADS, dv)
    return q, logf, k, v


def hgrn_chunk(S0, q, logf, k, v):
    C = q.shape[1]
    b = jnp.cumsum(logf, axis=1)
    o_inter = jnp.einsum('bthd,bhde->bthe', q * jnp.exp(b), S0)
    causal = jnp.tril(jnp.ones((C, C), dtype=bool))
    diff = b[:, :, None] - b[:, None, :]
    decay = jnp.exp(jnp.where(causal[None, :, :, None, None], diff, -jnp.inf))
    a = jnp.einsum('bthd,bshd,btshd->bhts', q, k, decay)
    o_intra = jnp.einsum('bhts,bshe->bthe', a, v)
    b_last = b[:, -1]
    S_new = jnp.exp(b_last)[..., None] * S0 + jnp.einsum('bshd,bshe->bhde', k * jnp.exp(b_last[:, None] - b), v)
    return S_new, o_inter + o_intra


def hgrn_out(o, gain, gate):
    return head_rmsnorm(o, gain) * jax.nn.silu(gate.astype(F32))


def nsa_prep(nq, kvs, hd):
    B, T, _ = nq.shape
    q = nq.reshape(B, T, NSA_KVH, NSA_G, hd) * hd ** -0.5
    kv = [a.reshape(B, T, NSA_KVH, hd) for a in split_cols(kvs, (NSA_KVH * hd,) * 6)]
    return q, kv


def compress(k, pe, w1, w2):
    B, L, KVH, HD = k.shape
    n_cb = L // CMP_BLOCK
    kb = k[:, : n_cb * CMP_BLOCK].reshape(B, n_cb, CMP_BLOCK, KVH, HD) + pe[None, None, :, None, :]
    kb = kb.transpose(0, 1, 3, 2, 4).reshape(B, n_cb, KVH, CMP_BLOCK * HD)
    return jax.nn.silu(kb @ w1) @ w2


def nsa_compressed(q, qpos, kc, vc, table):
    n_cb = kc.shape[1]
    end = (jnp.arange(n_cb, dtype=jnp.int32) + 1) * CMP_BLOCK - 1
    dist = qpos[:, None] - end[None, :]
    s = jnp.einsum('bqkgd,bnkd->bkgqn', q, kc).astype(F32) + heads_bias(table, rel_bucket(dist))
    p = masked_softmax(s, dist >= 0)
    o = jnp.einsum('bkgqn,bnkd->bqkgd', p.astype(vc.dtype), vc)
    return o, p


def select_blocks(p, qpos, L):
    B, KVH, G, Tq, n_cb = p.shape
    ratio = SEL_BLOCK // CMP_BLOCK
    n_sel = -(-L // SEL_BLOCK)
    imp = jnp.pad(p.sum(2), ((0, 0), (0, 0), (0, 0), (0, n_sel * ratio - n_cb)))
    imp = imp.reshape(B, KVH, Tq, n_sel, ratio).sum(-1)
    blk = jnp.arange(n_sel, dtype=jnp.int32)[None, :]
    cur = (qpos // SEL_BLOCK)[:, None]
    forced = (blk == 0) | (blk == cur) | (blk == cur - 1)
    valid = blk * SEL_BLOCK <= qpos[:, None]
    score = jnp.where(valid, jnp.where(forced, jnp.inf, imp), -jnp.inf)
    _, idx = lax.top_k(score, min(N_SELECT, n_sel))
    return idx


def pad_blocks(k):
    L = k.shape[1]
    n_sel = -(-L // SEL_BLOCK)
    return jnp.pad(k, ((0, 0), (0, n_sel * SEL_BLOCK - L), (0, 0), (0, 0)))


def nsa_selected(q, qpos, idx, ks, vs, table):
    B, Tq = q.shape[:2]
    n = idx.shape[-1] * SEL_BLOCK
    tok = (idx[..., None] * SEL_BLOCK + jnp.arange(SEL_BLOCK, dtype=jnp.int32)).reshape(B, NSA_KVH, Tq, n)
    bi = jnp.arange(B)[:, None, None, None]
    hi = jnp.arange(NSA_KVH)[None, :, None, None]
    kg = ks.transpose(0, 2, 1, 3)[bi, hi, tok]
    vg = vs.transpose(0, 2, 1, 3)[bi, hi, tok]
    dist = qpos[None, None, :, None] - tok
    bias = table.astype(F32).reshape(NUM_BUCKETS, NSA_KVH, NSA_G)[rel_bucket(dist), hi]
    s = jnp.einsum('bqkgd,bkqsd->bkgqs', q, kg).astype(F32) + jnp.moveaxis(bias, 4, 2)
    p = masked_softmax(s, (dist >= 0)[:, :, None])
    return jnp.einsum('bkgqs,bkqsd->bqkgd', p.astype(vg.dtype), vg)


def window_banded(q, kw, vw, table):
    B, T = q.shape[:2]
    hd = q.shape[-1]
    nb = T // WIN_QBLOCK
    span = WIN_QBLOCK + WINDOW
    pad = ((0, 0), (WINDOW, 0), (0, 0), (0, 0))
    gidx = jnp.arange(nb, dtype=jnp.int32)[:, None] * WIN_QBLOCK + jnp.arange(span, dtype=jnp.int32)[None, :]
    kb = jnp.pad(kw, pad)[:, gidx]
    vb = jnp.pad(vw, pad)[:, gidx]
    qb = q.reshape(B, nb, WIN_QBLOCK, NSA_KVH, NSA_G, hd)
    dist = jnp.arange(WIN_QBLOCK, dtype=jnp.int32)[:, None] + WINDOW - jnp.arange(span, dtype=jnp.int32)[None, :]
    kpos = gidx - WINDOW
    valid = (dist >= 0) & (dist < WINDOW) & (kpos[:, None, :] >= 0)
    s = jnp.einsum('bnqkgd,bnskd->bnkgqs', qb, kb).astype(F32) + heads_bias(table, rel_bucket(dist))
    p = masked_softmax(s, valid[None, :, None, None])
    o = jnp.einsum('bnkgqs,bnskd->bnqkgd', p.astype(vb.dtype), vb)
    return o.reshape(B, T, NSA_KVH, NSA_G, hd)


def window_dense(q, qpos, kw, vw, kpos, table):
    dist = qpos[:, None] - kpos[None, :]
    valid = (dist >= 0) & (dist < WINDOW) & (kpos[None, :] >= 0)
    s = jnp.einsum('bqkgd,bskd->bkgqs', q, kw).astype(F32) + heads_bias(table, rel_bucket(dist))
    p = masked_softmax(s, valid)
    return jnp.einsum('bkgqs,bskd->bqkgd', p.astype(vw.dtype), vw)


def nsa_combine(o_c, o_s, o_w, gate_logits, path):
    B, T = o_c.shape[:2]
    g = jax.nn.sigmoid(gate_logits.astype(F32)).reshape(B, T, NSA_KVH, NSA_G, 3)
    o = g[..., 0:1] * o_c.astype(F32) + g[..., 1:2] * o_s.astype(F32) + g[..., 2:3] * o_w.astype(F32)
    return o.reshape(B, T, -1) * jax.nn.silu(path.astype(F32))


def gather_pages(pool, page_table):
    B, NP = page_table.shape
    g = pool[page_table]
    return g.reshape((B, NP * PAGE_SIZE) + pool.shape[2:])


def retention_chunk(S0, q, k, v, log_gamma):
    C = q.shape[1]
    i = jnp.arange(C, dtype=F32)
    diff = i[:, None] - i[None, :]
    decay = jnp.where(diff[None] >= 0, jnp.exp(diff[None] * log_gamma[:, None, None]), 0.0)
    a = jnp.einsum('bthd,bshd->bhts', q, k) * decay[None]
    inner = jnp.einsum('bhts,bshe->bthe', a, v)
    q_dec = q * jnp.exp((i + 1.0)[:, None] * log_gamma[None, :])[None, :, :, None]
    cross = jnp.einsum('bthd,bhde->bthe', q_dec, S0)
    k_dec = k * jnp.exp((C - 1.0 - i)[:, None] * log_gamma[None, :])[None, :, :, None]
    S_new = jnp.exp(C * log_gamma)[None, :, None, None] * S0 + jnp.einsum('bshd,bshe->bhde', k_dec, v)
    return S_new, inner + cross


SAMPLE_ROWS = 16
MATMUL_TN = 512


class EvenCols:
    def __init__(self, hg_qk, hg_w, nsa_w, nsa_kvw):
        self.hd = nsa_w // NSA_HEADS
        self.kvw = nsa_kvw
        self.nsa_w = nsa_w
        self.q, self.f, self.v, self.g = 0, hg_qk, 2 * hg_qk, 2 * hg_qk + hg_w
        self.nq = 2 * hg_qk + 2 * hg_w
        self.kv = self.nq + nsa_w
        self.main_w = self.kv + 6 * nsa_kvw
        self.gate_w = 3 * NSA_G
        self.tail_path, self.tail_gate = 0, nsa_w
        used = nsa_w + NSA_KVH * LANES
        self.tail_w = -(-used // MATMUL_TN) * MATMUL_TN
        assert self.main_w % MATMUL_TN == 0

    def tail_weight(self, w_in):
        gate0 = self.main_w
        path0 = gate0 + NSA_KVH * self.gate_w
        parts = [w_in[:, path0:path0 + self.nsa_w]]
        for h in range(NSA_KVH):
            parts.append(jnp.pad(w_in[:, gate0 + h * self.gate_w:gate0 + (h + 1) * self.gate_w],
                                 ((0, 0), (0, LANES - self.gate_w))))
        w = jnp.concatenate(parts, axis=1)
        return jnp.pad(w, ((0, 0), (0, self.tail_w - w.shape[1])))


def even_prompt(h2, B, T, w_in, w_tail, w_out, lb, hg_gain, pe, w1k, w2k, w1v, w2v, table, ec):
    hd = ec.hd
    main = matmul(h2, w_in, n_cols=ec.main_w)
    tail = matmul(h2, w_tail)
    S0 = jnp.zeros((B, HG_HEADS, HG_DK, hg_gain.shape[0] // HG_HEADS), F32)
    y_hg, S = hgrn_mix(main, lb, hg_gain, S0, B, T, (ec.q, ec.f, ec.v, ec.g), HG_CHUNK, HGRN_SUB, HG_CHUNK)
    kvs = main[:, ec.kv:ec.main_w].reshape(B, T, 6, NSA_KVH, hd)
    kc, vc, ks, vs, kw, vw = [kvs[:, :, j] for j in range(6)]
    kcmp = compress(kc, pe, w1k, w2k).reshape(B, -1, NSA_KVH * hd)
    vcmp = compress(vc, pe, w1v, w2v).reshape(B, -1, NSA_KVH * hd)
    y_nsa = nsa_prompt(main, tail, kcmp, vcmp, table, B, T, (ec.nq, ec.kv, ec.tail_path, ec.tail_gate, hd))
    y = matmul(jnp.concatenate([y_hg, y_nsa], axis=-1), w_out)
    wb = min(WINDOW, T)
    return y, S, kc, vc, ks, vs, kw[:, T - wb:], vw[:, T - wb:]


def even_sample(h2, B, T, pos, past_len, S0, page_table, pool_ck, pool_cv, pool_sk, pool_sv, win_k, win_v,
                w_in, w_tail, w_out, lb, hg_gain, pe, w1k, w2k, w1v, w2v, table, ec):
    hd = ec.hd
    R = SAMPLE_ROWS
    main = matmul(h2, w_in, n_cols=ec.main_w)
    tail = matmul(h2, w_tail)
    y_hg, S = hgrn_mix(main, lb, hg_gain, S0.astype(F32), B, R, (ec.q, ec.f, ec.v, ec.g), R, R, T)
    m3 = main.reshape(B, R, -1)[:, :T]
    t3 = tail.reshape(B, R, -1)[:, :T]
    nq = m3[..., ec.nq:ec.kv]
    kvs = m3[..., ec.kv:ec.main_w]
    npath = t3[..., ec.tail_path:ec.tail_path + ec.nsa_w]
    ngate = jnp.concatenate([t3[..., ec.tail_gate + h * LANES:ec.tail_gate + h * LANES + ec.gate_w]
                             for h in range(NSA_KVH)], axis=-1)
    qn, (kc, vc, ks, vs, kw, vw) = nsa_prep(nq, kvs, hd)
    kc_all = jnp.concatenate([gather_pages(pool_ck, page_table).astype(kc.dtype), kc], axis=1)
    vc_all = jnp.concatenate([gather_pages(pool_cv, page_table).astype(vc.dtype), vc], axis=1)
    ks_all = jnp.concatenate([gather_pages(pool_sk, page_table).astype(ks.dtype), ks], axis=1)
    vs_all = jnp.concatenate([gather_pages(pool_sv, page_table).astype(vs.dtype), vs], axis=1)
    L = kc_all.shape[1]
    o_c, p_c = nsa_compressed(qn, pos, compress(kc_all, pe, w1k, w2k), compress(vc_all, pe, w1v, w2v), table)
    idx = select_blocks(p_c, pos, L)
    o_s = nsa_selected(qn, pos, idx, pad_blocks(ks_all), pad_blocks(vs_all), table)
    wb = win_k.shape[1]
    kw_all = jnp.concatenate([win_k.astype(kw.dtype), kw], axis=1)
    vw_all = jnp.concatenate([win_v.astype(vw.dtype), vw], axis=1)
    kpos = past_len - wb + jnp.arange(wb + T, dtype=jnp.int32)
    o_w = window_dense(qn, pos, kw_all, vw_all, kpos, table)
    y_nsa = nsa_combine(o_c, o_s, o_w, ngate, npath)
    y_nsa = jnp.pad(y_nsa.astype(BF16), ((0, 0), (0, R - T), (0, 0))).reshape(B * R, -1)
    y = matmul(jnp.concatenate([y_hg, y_nsa], axis=-1), w_out)
    return y, S, kc, vc, ks, vs, kw_all[:, T:], vw_all[:, T:]


def retention_layer(h2, B, T, pos, S0, w_in, w_out, gain, C, c_eff):
    y, S = retention_mix(matmul(h2, w_in), pos, gain, S0, B, T, C, c_eff)
    return matmul(y, w_out), S


def mem_attend(h2, B, T, mk, mv, w_q, w_o):
    slots = mk.shape[1]
    o = mem_mix(matmul(h2, w_q), mk.reshape(B, slots, -1), mv.reshape(B, slots, -1), B, T)
    return matmul(o, w_o)


def kernel(x_prompt, x_sample, mem_prompt, state_hgrn, cache_cmp_k, cache_cmp_v, cache_sel_k, cache_sel_v,
           cache_win_k, cache_win_v, state_ret, cache_mem_k, cache_mem_v, page_table, rel_table,
           norm_mix_pre, norm_mix_post, norm_mem_pre, norm_mem_post, ev_w_in, ev_w_out, hgrn_lb, hgrn_norm,
           cmp_pe, cmp_w1_k, cmp_w2_k, cmp_w1_v, cmp_w2_v, od_w_in, od_w_out, ret_norm,
           mem_w_q, mem_w_k, mem_w_v, mem_w_o):
    B, T, D = x_prompt.shape
    Bs, Ts, _ = x_sample.shape
    depth = norm_mix_pre.shape[0]
    past_len = page_table.shape[1] * PAGE_SIZE
    pos_p = jnp.arange(T, dtype=jnp.int32)
    pos_s = past_len + jnp.arange(Ts, dtype=jnp.int32)
    lb_all = jnp.cumsum(jax.nn.softmax(hgrn_lb.astype(F32), axis=0), axis=0)
    hg_qk = hgrn_lb.shape[1]
    hg_w = hgrn_norm.shape[1]
    nsa_w = ev_w_out.shape[1] - hg_w
    nsa_kvw = NSA_KVH * (nsa_w // NSA_HEADS)
    ec = EvenCols(hg_qk, hg_w, nsa_w, nsa_kvw)
    R = SAMPLE_ROWS
    assert Ts <= R

    even_p, even_s, ret_p, ret_s, memk_p, memv_p = [], [], [], [], [], []
    xp = x_prompt.reshape(B * T, D)
    xs = jnp.pad(x_sample, ((0, 0), (0, R - Ts), (0, 0))).reshape(Bs * R, D)
    pos_sr = past_len + jnp.arange(R, dtype=jnp.int32)
    mem2 = mem_prompt.reshape(-1, D).astype(BF16)
    ML = mem_prompt.shape[1]
    for l in range(depth):
        hp = rmsnorm_bf16(xp, norm_mix_pre[l])
        hs = rmsnorm_bf16(xs, norm_mix_pre[l])
        if l % 2 == 0:
            e = l // 2
            w = (ev_w_in[e], ec.tail_weight(ev_w_in[e]), ev_w_out[e], lb_all[e], hgrn_norm[e], cmp_pe[e],
                 cmp_w1_k[e], cmp_w2_k[e], cmp_w1_v[e], cmp_w2_v[e], rel_table)
            yp, *sp = even_prompt(hp, B, T, *w, ec)
            ys, *ss = even_sample(hs, Bs, Ts, pos_s, past_len, state_hgrn[e], page_table, cache_cmp_k[e],
                                  cache_cmp_v[e], cache_sel_k[e], cache_sel_v[e], cache_win_k[e], cache_win_v[e],
                                  *w, ec)
            even_p.append(sp)
            even_s.append(ss)
        else:
            o = l // 2
            S0p = jnp.zeros((B,) + state_ret.shape[2:], F32)
            yp, sp = retention_layer(hp, B, T, pos_p, S0p, od_w_in[o], od_w_out[o], ret_norm[o],
                                     RET_CHUNK, RET_CHUNK)
            ys, ss = retention_layer(hs, Bs, R, pos_sr, state_ret[o].astype(F32), od_w_in[o], od_w_out[o],
                                     ret_norm[o], R, Ts)
            ret_p.append(sp)
            ret_s.append(ss)
        xp = residual_post(xp, yp, norm_mix_post[l])
        xs = residual_post(xs, ys, norm_mix_post[l])
        mk_p = matmul(mem2, mem_w_k[l]).reshape(B, ML, MEM_HEADS, MEM_HD)
        mv_p = matmul(mem2, mem_w_v[l]).reshape(B, ML, MEM_HEADS, MEM_HD)
        memk_p.append(mk_p)
        memv_p.append(mv_p)
        hp = rmsnorm_bf16(xp, norm_mem_pre[l])
        hs = rmsnorm_bf16(xs, norm_mem_pre[l])
        yp = mem_attend(hp, B, T, mk_p, mv_p, mem_w_q[l], mem_w_o[l])
        ys = mem_attend(hs, Bs, R, cache_mem_k[l], cache_mem_v[l], mem_w_q[l], mem_w_o[l])
        xp = residual_post(xp, yp, norm_mem_post[l])
        xs = residual_post(xs, ys, norm_mem_post[l])
    p_hgrn, p_cmp_k, p_cmp_v, p_sel_k, p_sel_v, p_win_k, p_win_v = [jnp.stack(a) for a in zip(*even_p)]
    s_hgrn, s_cmp_k, s_cmp_v, s_sel_k, s_sel_v, s_win_k, s_win_v = [jnp.stack(a) for a in zip(*even_s)]
    p_ret = jnp.stack(ret_p)
    s_ret = jnp.stack(ret_s)
    p_mem_k = jnp.stack(memk_p)
    p_mem_v = jnp.stack(memv_p)
    return (xp.reshape(B, T, D), xs.reshape(Bs, R, D)[:, :Ts], p_hgrn, p_cmp_k, p_cmp_v, p_sel_k, p_sel_v,
            p_win_k, p_win_v, p_ret, p_mem_k, p_mem_v,
            s_hgrn, s_cmp_k, s_cmp_v, s_sel_k, s_sel_v, s_win_k, s_win_v, s_ret)
```

```python
import functools
import math

import jax
import jax.numpy as jnp
import numpy as np
from jax import lax
from jax.experimental import pallas as pl
from jax.experimental.pallas import tpu as pltpu

F32 = jnp.float32
BF16 = jnp.bfloat16
EPS = 1e-6
NEG_INF = -1e30

PAGE_SIZE = 128
HG_HEADS = 16
HG_DK = 128
HG_CHUNK = 64
HGRN_SUB = 16
NSA_HEADS = 16
NSA_KVH = 2
NSA_G = NSA_HEADS // NSA_KVH
CMP_BLOCK = 32
SEL_BLOCK = 64
N_SELECT = 16
WINDOW = 512
SEL_QBLOCK = 64
WIN_QBLOCK = 128
NUM_BUCKETS = 32
MAX_DISTANCE = 1024
RET_HEADS = 16
RET_CHUNK = 128
ROPE_BASE = 10000.0
MEM_HEADS = 4
MEM_HD = 128
MEM_W = MEM_HEADS * MEM_HD

VMEM_LIMIT_BYTES = 56 * 1024 * 1024


def _params(*sem):
    return pltpu.CompilerParams(dimension_semantics=sem, vmem_limit_bytes=VMEM_LIMIT_BYTES)


def _rmsnorm_kernel(x_ref, g_ref, o_ref):
    x = x_ref[...]
    y = x * lax.rsqrt(jnp.mean(x * x, axis=-1, keepdims=True) + EPS)
    o_ref[...] = (y * g_ref[...]).astype(o_ref.dtype)


def rmsnorm_bf16(x, g):
    M, D = x.shape
    tm = min(M, 512)
    return pl.pallas_call(
        _rmsnorm_kernel,
        grid=(M // tm,),
        in_specs=[pl.BlockSpec((tm, D), lambda i: (i, 0)), pl.BlockSpec((1, D), lambda i: (0, 0))],
        out_specs=pl.BlockSpec((tm, D), lambda i: (i, 0)),
        out_shape=jax.ShapeDtypeStruct((M, D), BF16),
        compiler_params=_params("parallel"),
        name="rmsnorm_bf16",
    )(x, g.reshape(1, D))


def _post_kernel(x_ref, y_ref, g_ref, o_ref):
    y = y_ref[...]
    n = y * lax.rsqrt(jnp.mean(y * y, axis=-1, keepdims=True) + EPS)
    o_ref[...] = x_ref[...] + n * g_ref[...]


def residual_post(x, y, g):
    M, D = x.shape
    tm = min(M, 512)
    return pl.pallas_call(
        _post_kernel,
        grid=(M // tm,),
        in_specs=[pl.BlockSpec((tm, D), lambda i: (i, 0)), pl.BlockSpec((tm, D), lambda i: (i, 0)),
                  pl.BlockSpec((1, D), lambda i: (0, 0))],
        out_specs=pl.BlockSpec((tm, D), lambda i: (i, 0)),
        out_shape=jax.ShapeDtypeStruct((M, D), F32),
        compiler_params=_params("parallel"),
        name="residual_post",
    )(x, y, g.reshape(1, D))


def _matmul_kernel(x_ref, w_ref, o_ref, wb_ref):
    @pl.when(pl.program_id(1) == 0)
    def _():
        wb_ref[...] = w_ref[...].astype(BF16)

    o_ref[...] = jnp.dot(x_ref[...], wb_ref[...], preferred_element_type=F32)


def matmul(x, w, n_cols=None):
    M, K = x.shape
    N = w.shape[1] if n_cols is None else n_cols
    tn = 512 if K <= 4096 else 256
    tn = min(tn, N)
    tm = min(M, 1024 if K <= 4096 else 512)
    assert N % tn == 0 and M % tm == 0, (M, K, N)
    return pl.pallas_call(
        _matmul_kernel,
        grid=(N // tn, M // tm),
        in_specs=[pl.BlockSpec((tm, K), lambda n, m: (m, 0)), pl.BlockSpec((K, tn), lambda n, m: (0, n))],
        out_specs=pl.BlockSpec((tm, tn), lambda n, m: (m, n)),
        out_shape=jax.ShapeDtypeStruct((M, N), F32),
        scratch_shapes=[pltpu.VMEM((K, tn), BF16)],
        compiler_params=_params("parallel", "arbitrary"),
        name="matmul",
    )(x, w)


LANES = 128
NSA_TQ = 128
_NT = (((1,), (1,)), ((), ()))


_TN = (((0,), (0,)), ((), ()))


def _bias_tile_kernel(thr_ref, table_ref, o_ref, *, row_stride, row_offset):
    kvh = pl.program_id(0)
    step = pl.program_id(1)
    row = lax.broadcasted_iota(jnp.int32, (LANES, LANES), 0)
    lane = lax.broadcasted_iota(jnp.int32, (LANES, LANES), 1)
    dist = LANES * step + lane - row_stride * row - row_offset
    for g in range(NSA_G):
        h = kvh * NSA_G + g
        bias = jnp.full((LANES, LANES), table_ref[0, h], F32)
        for k in range(1, NUM_BUCKETS):
            bias = jnp.where(dist >= thr_ref[k], table_ref[k, h], bias)
        o_ref[0, 0, :, g * LANES:(g + 1) * LANES] = bias


def bias_tiles(table, n_steps, max_dist, row_stride, row_offset):
    bucket = rel_bucket(jnp.arange(max_dist, dtype=jnp.int32))
    thr = jnp.sum(bucket[None, :] < jnp.arange(NUM_BUCKETS, dtype=jnp.int32)[:, None], axis=1).astype(jnp.int32)
    smem = pl.BlockSpec(memory_space=pltpu.SMEM)
    return pl.pallas_call(
        functools.partial(_bias_tile_kernel, row_stride=row_stride, row_offset=row_offset),
        grid=(NSA_KVH, n_steps),
        in_specs=[smem, smem],
        out_specs=pl.BlockSpec((1, 1, LANES, NSA_G * LANES), lambda h, s: (h, s, 0, 0)),
        out_shape=jax.ShapeDtypeStruct((NSA_KVH, n_steps, LANES, NSA_G * LANES), F32),
        compiler_params=_params("parallel", "parallel"),
        name="bias_tiles",
    )(thr, table.astype(F32))


def _nsa_prompt_kernel(q_ref, kc_ref, vc_ref, ks_ref, vs_ref, kw_ref, vw_ref, cb_ref, tb_ref, gate_ref, path_ref,
                       o_ref, qb_s, oc_s, m_s, l_s, acc_s, *, hd, n_win_chunks, n_pick):
    i = pl.program_id(2)
    tq = NSA_TQ
    G = NSA_G
    scale = hd ** -0.5
    row = lax.broadcasted_iota(jnp.int32, (LANES, tq), 0)
    qpos = i * tq + lax.broadcasted_iota(jnp.int32, (LANES, tq), 1)
    slab = lambda g: slice(g * tq, (g + 1) * tq)

    for g in range(G):
        qb_s[slab(g), :] = (q_ref[:, g * hd:(g + 1) * hd] * scale).astype(BF16)
    qb = qb_s[...]

    s_all = lax.dot_general(kc_ref[0].astype(BF16), qb, _NT, preferred_element_type=F32) + cb_ref[0, 0]
    valid_c = qpos >= (row + 1) * CMP_BLOCK - 1
    valid_cf = valid_c.astype(F32)
    imp = jnp.zeros((LANES, tq), F32)
    ps = []
    for g in range(G):
        s = jnp.where(valid_c, s_all[:, slab(g)], NEG_INF)
        e = jnp.exp(s - jnp.max(s, axis=0, keepdims=True))
        p = e / jnp.sum(e, axis=0, keepdims=True) * valid_cf
        imp = imp + p
        ps.append(p.astype(BF16))
    oc_s[...] = lax.dot_general(vc_ref[0].astype(BF16), jnp.concatenate(ps, axis=1), _TN,
                                preferred_element_type=F32)

    imp2 = imp + pltpu.roll(imp, LANES - 1, axis=0)
    blk = row >> 1
    cur = qpos // SEL_BLOCK
    forced = (blk == 0) | (blk == cur) | (blk == cur - 1)
    usable = ((row & 1) == 0) & (blk * SEL_BLOCK <= qpos)
    work = jnp.where(usable, jnp.where(forced, jnp.inf, imp2), -jnp.inf)
    sel = jnp.zeros((LANES, tq), F32)
    for _ in range(n_pick):
        mx = jnp.max(work, axis=0, keepdims=True)
        first = jnp.min(jnp.where(work == mx, row, LANES), axis=0, keepdims=True)
        pick = row == first
        sel = jnp.where(pick, 1.0, sel)
        work = jnp.where(pick, -jnp.inf, work)
    sel_b = sel.astype(BF16)

    m_s[...] = jnp.full(m_s.shape, NEG_INF, F32)
    l_s[...] = jnp.zeros(l_s.shape, F32)
    acc_s[...] = jnp.zeros(acc_s.shape, F32)
    ekey = lax.broadcasted_iota(jnp.int32, (LANES, LANES), 0)
    eblk = lax.broadcasted_iota(jnp.int32, (LANES, LANES), 1)

    def online(slot, k_b, v_b, bias, valid):
        s_all = lax.dot_general(k_b, qb, _NT, preferred_element_type=F32) + bias
        ps, alphas = [], []
        for g in range(G):
            s = jnp.where(valid, s_all[:, slab(g)], NEG_INF)
            m_old = m_s[slot, :, slab(g)]
            m_new = jnp.maximum(m_old, jnp.max(s, axis=0, keepdims=True))
            alpha = jnp.exp(m_old - m_new)
            p = jnp.where(valid, jnp.exp(s - m_new), 0.0)
            l_s[slot, :, slab(g)] = alpha * l_s[slot, :, slab(g)] + jnp.sum(p, axis=0, keepdims=True)
            m_s[slot, :, slab(g)] = m_new
            ps.append(p.astype(BF16))
            alphas.append(alpha)
        pv = lax.dot_general(v_b, jnp.concatenate(ps, axis=1), _TN, preferred_element_type=F32)
        acc_s[slot] = jnp.concatenate(alphas, axis=1) * acc_s[slot] + pv

    def chunk(c, carry):
        k0 = pl.multiple_of(c * LANES, LANES)
        delta = i - c
        dist = qpos - (k0 + row)
        causal = dist >= 0
        expand = (eblk == 2 * (c * (LANES // SEL_BLOCK) + (ekey // SEL_BLOCK))).astype(BF16)
        chosen = jnp.dot(expand, sel_b, preferred_element_type=F32) > 0.5
        bias = tb_ref[0, delta]
        online(0, ks_ref[pl.ds(k0, LANES), :].astype(BF16), vs_ref[pl.ds(k0, LANES), :].astype(BF16), bias,
               chosen & causal)

        @pl.when(delta < n_win_chunks)
        def _():
            online(1, kw_ref[pl.ds(k0, LANES), :].astype(BF16), vw_ref[pl.ds(k0, LANES), :].astype(BF16), bias,
                   causal & (dist < WINDOW))

        return carry

    lax.fori_loop(0, i + 1, chunk, 0)

    gates = jax.nn.sigmoid(gate_ref[...]).T
    for g in range(G):
        o_sel = acc_s[0, :, slab(g)] / l_s[0, :, slab(g)]
        o_win = acc_s[1, :, slab(g)] / l_s[1, :, slab(g)]
        o = (gates[3 * g:3 * g + 1] * oc_s[:, slab(g)] + gates[3 * g + 1:3 * g + 2] * o_sel
             + gates[3 * g + 2:3 * g + 3] * o_win)
        o_ref[:, g * hd:(g + 1) * hd] = (o.T * jax.nn.silu(path_ref[:, g * hd:(g + 1) * hd])).astype(o_ref.dtype)


def nsa_prompt(main, tail, kcmp, vcmp, table, B, T, cols):
    q_col, kv_col, path_col, gate_col, hd = cols
    G = NSA_G
    tq = NSA_TQ
    n_cb = kcmp.shape[1]
    assert n_cb <= LANES and T % tq == 0 and hd == LANES
    nd = T // LANES
    n_win_chunks = WINDOW // LANES + 1
    tb = bias_tiles(table, nd, T, 1, 0)
    cb = bias_tiles(table, T // tq, T, CMP_BLOCK, CMP_BLOCK - 1)
    pad = ((0, 0), (0, LANES - n_cb), (0, 0))
    kcmp = jnp.pad(kcmp, pad)
    vcmp = jnp.pad(vcmp, pad)
    nblk = T // tq
    gw = G * hd
    row_map = lambda b, h, i: (b * nblk + i, 0)
    kv_spec = lambda j: pl.BlockSpec((T, hd), lambda b, h, i: (b, kv_col // hd + 2 * j + h))
    return pl.pallas_call(
        functools.partial(_nsa_prompt_kernel, hd=hd, n_win_chunks=n_win_chunks,
                          n_pick=min(N_SELECT, -(-T // SEL_BLOCK))),
        grid=(B, NSA_KVH, nblk),
        in_specs=[
            pl.BlockSpec((tq, gw), lambda b, h, i: (b * nblk + i, q_col // gw + h)),
            pl.BlockSpec((1, LANES, hd), lambda b, h, i: (b, 0, h)),
            pl.BlockSpec((1, LANES, hd), lambda b, h, i: (b, 0, h)),
            kv_spec(2), kv_spec(3), kv_spec(4), kv_spec(5),
            pl.BlockSpec((1, 1, LANES, gw), lambda b, h, i: (h, i, 0, 0)),
            pl.BlockSpec((1, nd, LANES, gw), lambda b, h, i: (h, 0, 0, 0)),
            pl.BlockSpec((tq, LANES), lambda b, h, i: (b * nblk + i, gate_col // LANES + h)),
            pl.BlockSpec((tq, gw), lambda b, h, i: (b * nblk + i, path_col // gw + h)),
        ],
        out_specs=pl.BlockSpec((tq, gw), lambda b, h, i: (b * nblk + i, h)),
        out_shape=jax.ShapeDtypeStruct((B * T, NSA_KVH * gw), BF16),
        scratch_shapes=[pltpu.VMEM((G * tq, hd), BF16), pltpu.VMEM((hd, G * tq), F32),
                        pltpu.VMEM((2, 1, G * tq), F32), pltpu.VMEM((2, 1, G * tq), F32),
                        pltpu.VMEM((2, hd, G * tq), F32)],
        compiler_params=_params("parallel", "parallel", "arbitrary"),
        name="nsa_prompt",
    )(main, kcmp, vcmp, main, main, main, main, cb, tb, tail, tail)


_TN = (((0,), (0,)), ((), ()))


def _cumsum_rows(x, n):
    row = lax.broadcasted_iota(jnp.int32, x.shape, 0)
    sh = 1
    while sh < n:
        x = x + jnp.where(row >= sh, pltpu.roll(x, sh, axis=0), 0.0)
        sh *= 2
    return x


def _hgrn_kernel(q_ref, f_ref, v_ref, gate_ref, lb_ref, gain_ref, s0_ref, y_ref, s_out_ref, st_s,
                 *, C, SB, c_eff, dk, HB):
    c = pl.program_id(2)

    @pl.when(c == 0)
    def _():
        for hh in range(HB):
            st_s[hh] = s0_ref[0, hh].T

    for hh in range(HB):
        _hgrn_head(hh, slice(hh * dk, (hh + 1) * dk), q_ref, f_ref, v_ref, gate_ref, lb_ref, gain_ref,
                   y_ref, st_s, C=C, SB=SB, c_eff=c_eff, dk=dk)

    @pl.when(c == pl.num_programs(2) - 1)
    def _():
        for hh in range(HB):
            s_out_ref[0, hh] = st_s[hh].T


def _hgrn_head(hh, sl, q_ref, f_ref, v_ref, gate_ref, lb_ref, gain_ref, y_ref, st_s, *, C, SB, c_eff, dk):
    lb = lb_ref[:, sl]
    f = lb + (1.0 - lb) * jax.nn.sigmoid(f_ref[:, sl])
    logf = jnp.log(f)
    k = 1.0 - f
    if c_eff < C:
        real = lax.broadcasted_iota(jnp.int32, (C, dk), 0) < c_eff
        logf = jnp.where(real, logf, 0.0)
        k = jnp.where(real, k, 0.0)
    q = q_ref[:, sl] * dk ** -0.5
    v_b = v_ref[:, sl].astype(BF16)
    b = _cumsum_rows(logf, C)
    st = st_s[hh]
    o = lax.dot_general((q * jnp.exp(b)).astype(BF16), st.astype(BF16), _NT, preferred_element_type=F32)

    lane_c = lax.broadcasted_iota(jnp.int32, (SB, C), 1)
    row_c = lax.broadcasted_iota(jnp.int32, (SB, C), 0)
    outs = []
    for I in range(C // SB):
        r0 = I * SB
        q_i = q[r0:r0 + SB]
        b_i = b[r0:r0 + SB]
        if I > 0:
            b_r = b[r0 - 1:r0]
            qq = (q_i * jnp.exp(b_i - b_r)).astype(BF16)
            kk = (k * jnp.exp(jnp.minimum(b_r - b, 0.0))).astype(BF16)
            a = lax.dot_general(qq, kk, _NT, preferred_element_type=F32)
            a = jnp.where(lane_c < r0, a, 0.0)
        else:
            a = jnp.zeros((SB, C), F32)
        for s in range(SB):
            z = q_i * k[r0 + s:r0 + s + 1] * jnp.exp(jnp.minimum(b_i - b[r0 + s:r0 + s + 1], 0.0))
            col = jnp.sum(z, axis=-1, keepdims=True)
            a = jnp.where((lane_c == r0 + s) & (row_c >= s), col, a)
        outs.append(jnp.dot(a.astype(BF16), v_b, preferred_element_type=F32))
    o = o + jnp.concatenate(outs, axis=0)

    b_last = b[C - 1:C]
    kd = (k * jnp.exp(b_last - b)).astype(BF16)
    st_new = jnp.exp(b_last) * st + lax.dot_general(v_b, kd, _TN, preferred_element_type=F32)
    st_s[hh] = st_new

    y = o * lax.rsqrt(jnp.mean(o * o, axis=-1, keepdims=True) + EPS) * gain_ref[:, sl]
    y_ref[:, sl] = (y * jax.nn.silu(gate_ref[:, sl])).astype(y_ref.dtype)


HGRN_HEADS_PER_STEP = 8


def hgrn_mix(main, lb, gain, s0, B, T, cols, C, SB, c_eff):
    q_col, f_col, v_col, g_col = cols
    _, H, dk, dv = s0.shape
    HB = HGRN_HEADS_PER_STEP
    W = HB * LANES
    assert dk == LANES and dv == LANES and T % C == 0 and C % SB == 0 and H % HB == 0
    assert all(off % W == 0 for off in cols)
    nc = T // C
    col = lambda off: pl.BlockSpec((C, W), lambda b, h, c: (b * nc + c, off // W + h))
    vec = pl.BlockSpec((1, W), lambda b, h, c: (0, h))
    st = pl.BlockSpec((1, HB, dk, dv), lambda b, h, c: (b, h, 0, 0))
    return pl.pallas_call(
        functools.partial(_hgrn_kernel, C=C, SB=SB, c_eff=c_eff, dk=dk, HB=HB),
        grid=(B, H // HB, nc),
        in_specs=[col(q_col), col(f_col), col(v_col), col(g_col), vec, vec, st],
        out_specs=[pl.BlockSpec((C, W), lambda b, h, c: (b * nc + c, h)), st],
        out_shape=[jax.ShapeDtypeStruct((B * T, H * dv), BF16), jax.ShapeDtypeStruct(s0.shape, F32)],
        scratch_shapes=[pltpu.VMEM((HB, dv, dk), F32)],
        compiler_params=_params("parallel", "parallel", "arbitrary"),
        name="hgrn_mix",
    )(main, main, main, main, lb.reshape(1, -1), gain.reshape(1, -1), s0)


def _retention_kernel(q_ref, k_ref, v_ref, g_ref, cos_ref, sin_ref, lg_ref, gain_ref, s0_ref, y_ref, s_out_ref, s_s,
                      *, C, c_eff, dk, dv, HB):
    c = pl.program_id(2)

    @pl.when(c == 0)
    def _():
        s_s[...] = s0_ref[0]

    for hh in range(HB):
        _retention_head(hh, q_ref, k_ref, v_ref, g_ref, cos_ref, sin_ref, lg_ref, gain_ref, y_ref, s_s,
                        C=C, c_eff=c_eff, dk=dk, dv=dv)

    @pl.when(c == pl.num_programs(2) - 1)
    def _():
        s_out_ref[0] = s_s[...]


def _retention_head(hh, q_ref, k_ref, v_ref, g_ref, cos_ref, sin_ref, lg_ref, gain_ref, y_ref, s_s,
                    *, C, c_eff, dk, dv):
    half = dk // 2
    vsl = slice(hh * dv, (hh + 1) * dv)
    cos = cos_ref[...]
    sin = sin_ref[...]
    lg_w = lg_ref[hh]
    lg = lg_w[:, :LANES]
    row = lax.broadcasted_iota(jnp.int32, (C, LANES), 0).astype(F32)

    def rot(ref, w):
        x1 = ref[:, hh * dk:hh * dk + half]
        x2 = ref[:, hh * dk + half:(hh + 1) * dk]
        return jnp.concatenate([(x1 * cos - x2 * sin) * w, (x1 * sin + x2 * cos) * w], axis=1)

    q = rot(q_ref, dk ** -0.5)
    k = rot(k_ref, 1.0)
    v_b = v_ref[:, vsl].astype(BF16)
    s_old = s_s[hh]

    a = lax.dot_general(q.astype(BF16), k.astype(BF16), _NT, preferred_element_type=F32)
    ti = lax.broadcasted_iota(jnp.int32, (C, C), 0)
    si = lax.broadcasted_iota(jnp.int32, (C, C), 1)
    diff = (ti - si).astype(F32)
    a = a * jnp.where(diff >= 0, jnp.exp(diff * lg_w[:, :C]), 0.0)
    inner = jnp.dot(a.astype(BF16), v_b, preferred_element_type=F32)
    q_w = jnp.exp((row + 1.0) * lg)
    q_dec = q * jnp.concatenate([q_w] * (dk // LANES), axis=1)
    cross = jnp.dot(q_dec.astype(BF16), s_old.astype(BF16), preferred_element_type=F32)
    k_w = jnp.where(row < c_eff, jnp.exp((c_eff - 1.0 - row) * lg), 0.0)
    k_dec = k * jnp.concatenate([k_w] * (dk // LANES), axis=1)
    s_new = jnp.exp(c_eff * lg_w) * s_old + lax.dot_general(k_dec.astype(BF16), v_b, _TN, preferred_element_type=F32)
    s_s[hh] = s_new

    o = inner + cross
    cen = o - jnp.mean(o, axis=-1, keepdims=True)
    y = cen * lax.rsqrt(jnp.mean(cen * cen, axis=-1, keepdims=True) + EPS) * gain_ref[:, vsl]
    y_ref[:, vsl] = (y * jax.nn.silu(g_ref[:, vsl])).astype(y_ref.dtype)


RETENTION_HEADS_PER_STEP = 4


def retention_mix(proj, pos, gain, s0, B, T, C, c_eff):
    _, H, dk, dv = s0.shape
    HB = RETENTION_HEADS_PER_STEP
    assert T % C == 0 and C <= LANES and dk % LANES == 0 and H % HB == 0
    nc = T // C
    nh = H // HB
    half = dk // 2
    inv = ROPE_BASE ** (-jnp.arange(half, dtype=F32) / half)
    ang = pos.astype(F32)[:, None] * inv[None, :]
    log_gamma = jnp.log1p(-jnp.exp2(-5.0 - jnp.arange(H, dtype=F32)))
    lg = jnp.broadcast_to(log_gamma[:, None, None], (H, 1, dv))
    qk = lambda j: pl.BlockSpec((C, HB * dk), lambda b, h, c: (b * nc + c, j * nh + h))
    vg = lambda j: pl.BlockSpec((C, HB * dv), lambda b, h, c: (b * nc + c, (2 * H * dk) // (HB * dv) + j * nh + h))
    tab = pl.BlockSpec((C, half), lambda b, h, c: (c, 0))
    st = pl.BlockSpec((1, HB, dk, dv), lambda b, h, c: (b, h, 0, 0))
    return pl.pallas_call(
        functools.partial(_retention_kernel, C=C, c_eff=c_eff, dk=dk, dv=dv, HB=HB),
        grid=(B, nh, nc),
        in_specs=[qk(0), qk(1), vg(0), vg(1), tab, tab,
                  pl.BlockSpec((HB, 1, dv), lambda b, h, c: (h, 0, 0)),
                  pl.BlockSpec((1, HB * dv), lambda b, h, c: (0, h)), st],
        out_specs=[pl.BlockSpec((C, HB * dv), lambda b, h, c: (b * nc + c, h)), st],
        out_shape=[jax.ShapeDtypeStruct((B * T, H * dv), BF16), jax.ShapeDtypeStruct(s0.shape, F32)],
        scratch_shapes=[pltpu.VMEM((HB, dk, dv), F32)],
        compiler_params=_params("parallel", "parallel", "arbitrary"),
        name="retention_mix",
    )(proj, proj, proj, proj, jnp.cos(ang), jnp.sin(ang), lg, gain.reshape(1, -1), s0)


def _mem_kernel(qg_ref, mk_ref, mv_ref, o_ref, *, heads, hd):
    for h in range(heads):
        sl = slice(h * hd, (h + 1) * hd)
        q = (qg_ref[:, sl] * hd ** -0.5).astype(BF16)
        s = lax.dot_general(q, mk_ref[0, :, sl].astype(BF16), _NT, preferred_element_type=F32)
        e = jnp.exp(s - jnp.max(s, axis=-1, keepdims=True))
        p = e / jnp.sum(e, axis=-1, keepdims=True)
        o = jnp.dot(p.astype(BF16), mv_ref[0, :, sl].astype(BF16), preferred_element_type=F32)
        gate = qg_ref[:, heads * hd + h * hd:heads * hd + (h + 1) * hd]
        o_ref[:, sl] = (o * jax.nn.silu(gate)).astype(o_ref.dtype)


def mem_mix(qg, mk, mv, B, T):
    W = mk.shape[2]
    tq = min(T, 256)
    nb = T // tq
    kv = pl.BlockSpec((1, mk.shape[1], W), lambda b, i: (b, 0, 0))
    return pl.pallas_call(
        functools.partial(_mem_kernel, heads=MEM_HEADS, hd=W // MEM_HEADS),
        grid=(B, nb),
        in_specs=[pl.BlockSpec((tq, 2 * W), lambda b, i: (b * nb + i, 0)), kv, kv],
        out_specs=pl.BlockSpec((tq, W), lambda b, i: (b * nb + i, 0)),
        out_shape=jax.ShapeDtypeStruct((B * T, W), BF16),
        compiler_params=_params("parallel", "arbitrary"),
        name="mem_mix",
    )(qg, mk, mv)


def _compress_rows(x_ref, n_blocks, pe_ref, w1_ref, w2_ref, kvh):
    stride = CMP_BLOCK * kvh
    outs = []
    for h in range(kvh):
        acc = jnp.zeros((n_blocks, w1_ref.shape[2]), F32)
        for j in range(0, CMP_BLOCK, 2):
            xa = x_ref[pl.ds(j * kvh + h, n_blocks, stride=stride), :] + pe_ref[j:j + 1]
            xb = x_ref[pl.ds((j + 1) * kvh + h, n_blocks, stride=stride), :] + pe_ref[j + 1:j + 2]
            x2 = jnp.concatenate([xa, xb], axis=1).astype(BF16)
            acc = acc + jnp.dot(x2, w1_ref[j // 2], preferred_element_type=F32)
        outs.append(jnp.dot(jax.nn.silu(acc).astype(BF16), w2_ref[...], preferred_element_type=F32))
    return jnp.concatenate(outs, axis=1)


def _compress_kernel(x_ref, pe_ref, w1_ref, w2_ref, o_ref, *, kvh):
    o_ref[...] = _compress_rows(x_ref, o_ref.shape[0], pe_ref, w1_ref, w2_ref, kvh)


def _compress_paged_kernel(pt_ref, pool_ref, pe_ref, w1_ref, w2_ref, o_ref, x_s, sem, *, n_pages, rows_per_page,
                           kvh):
    b = pl.program_id(0)

    def page_copy(p):
        return pltpu.make_async_copy(pool_ref.at[pt_ref[b * n_pages + p]],
                                     x_s.at[pl.ds(p * rows_per_page, rows_per_page)], sem.at[0])

    def start(p, carry):
        page_copy(p).start()
        return carry

    def wait(p, carry):
        page_copy(p).wait()
        return carry

    lax.fori_loop(0, n_pages, start, 0)
    lax.fori_loop(0, n_pages, wait, 0)
    o_ref[0] = _compress_rows(x_s, o_ref.shape[1], pe_ref, w1_ref, w2_ref, kvh)


def _compress_weights(pe, w1, w2, hd):
    return pe.astype(F32), w1.reshape(CMP_BLOCK // 2, 2 * hd, w1.shape[1]).astype(BF16), w2.astype(BF16)


def compress_blocks(k, pe, w1, w2):
    B, L, kvh, hd = k.shape
    n = B * (L // CMP_BLOCK)
    assert hd == LANES and L % CMP_BLOCK == 0
    peb, w1b, w2b = _compress_weights(pe, w1, w2, hd)
    full = lambda a: pl.BlockSpec(a.shape, lambda i: (0,) * a.ndim)
    x = k.reshape(B * L * kvh, hd)
    out = pl.pallas_call(
        functools.partial(_compress_kernel, kvh=kvh),
        grid=(1,),
        in_specs=[full(x), full(peb), full(w1b), full(w2b)],
        out_specs=pl.BlockSpec((n, kvh * hd), lambda i: (0, 0)),
        out_shape=jax.ShapeDtypeStruct((n, kvh * hd), F32),
        compiler_params=_params("arbitrary"),
        name="compress_blocks",
    )(x, peb, w1b, w2b)
    return out.reshape(B, L // CMP_BLOCK, kvh * hd)


def compress_paged(pool, page_table, pe, w1, w2):
    n_pool, page, kvh, hd = pool.shape
    B, NP = page_table.shape
    rpp = page * kvh
    n_blocks = NP * page // CMP_BLOCK
    assert hd == LANES and page % CMP_BLOCK == 0
    peb, w1b, w2b = _compress_weights(pe, w1, w2, hd)
    full = lambda a: pl.BlockSpec(a.shape, lambda b, pt: (0,) * a.ndim)
    return pl.pallas_call(
        functools.partial(_compress_paged_kernel, n_pages=NP, rows_per_page=rpp, kvh=kvh),
        grid_spec=pltpu.PrefetchScalarGridSpec(
            num_scalar_prefetch=1,
            grid=(B,),
            in_specs=[pl.BlockSpec(memory_space=pl.ANY), full(peb), full(w1b), full(w2b)],
            out_specs=pl.BlockSpec((1, n_blocks, kvh * hd), lambda b, pt: (b, 0, 0)),
            scratch_shapes=[pltpu.VMEM((NP * rpp, hd), F32), pltpu.SemaphoreType.DMA((1,))],
        ),
        out_shape=jax.ShapeDtypeStruct((B, n_blocks, kvh * hd), F32),
        compiler_params=_params("arbitrary"),
        name="compress_paged",
    )(page_table.reshape(-1), pool.reshape(n_pool, rpp, hd), peb, w1b, w2b)


def _bias_rows(dist, thr_ref, tab):
    bias = jnp.broadcast_to(tab[:, 0:1], dist.shape)
    for k in range(1, NUM_BUCKETS):
        bias = jnp.where(dist >= thr_ref[k], tab[:, k:k + 1], bias)
    return bias


def _sample_cmp_kernel(thr_ref, q_ref, kc_ref, vc_ref, tab_ref, oc_ref, idx_ref, *, qpos, n_pick, hd):
    n_cb = kc_ref.shape[1]
    qg = (q_ref[0, 0] * hd ** -0.5).astype(BF16)
    lane = lax.broadcasted_iota(jnp.int32, (NSA_G, n_cb), 1)
    dist = qpos - ((lane + 1) * CMP_BLOCK - 1)
    valid = dist >= 0
    s = lax.dot_general(qg, kc_ref[0].astype(BF16), _NT, preferred_element_type=F32)
    s = jnp.where(valid, s + _bias_rows(dist, thr_ref, tab_ref[0]), NEG_INF)
    e = jnp.exp(s - jnp.max(s, axis=-1, keepdims=True))
    p = e / jnp.sum(e, axis=-1, keepdims=True) * valid.astype(F32)
    oc_ref[0, 0] = jnp.dot(p.astype(BF16), vc_ref[0].astype(BF16), preferred_element_type=F32)

    imp = jnp.sum(p, axis=0, keepdims=True)
    imp2 = imp + pltpu.roll(imp, n_cb - 1, axis=1)
    lane1 = lax.broadcasted_iota(jnp.int32, (1, n_cb), 1)
    blk = lane1 >> 1
    cur = qpos // SEL_BLOCK
    forced = (blk == 0) | (blk == cur) | (blk == cur - 1)
    usable = ((lane1 & 1) == 0) & (blk * SEL_BLOCK <= qpos)
    work = jnp.where(usable, jnp.where(forced, jnp.inf, imp2), -jnp.inf)
    out_lane = lax.broadcasted_iota(jnp.int32, (1, LANES), 1)
    picks = jnp.zeros((1, LANES), jnp.int32)
    for it in range(n_pick):
        mx = jnp.max(work, axis=-1, keepdims=True)
        first = jnp.min(jnp.where(work == mx, lane1, n_cb), axis=-1, keepdims=True)
        picks = jnp.where(out_lane == it, first >> 1, picks)
        work = jnp.where(lane1 == first, -jnp.inf, work)
    idx_ref[0, 0] = picks


def _sample_attn_kernel(idx_ref, pt_ref, thr_ref, q_ref, ksel_ref, vsel_ref, ksn_ref, vsn_ref, kwin_ref, vwin_ref,
                        kwn_ref, vwn_ref, oc_ref, gate_ref, path_ref, tab_ref, o_ref, m_s, l_s, acc_s,
                        *, qpos, hd, n_cached):
    b, h, j = pl.program_id(0), pl.program_id(1), pl.program_id(2)
    G = NSA_G
    qg = (q_ref[0, 0] * hd ** -0.5).astype(BF16)
    qf = qg.astype(F32)
    tab = tab_ref[0]
    zero_d = jnp.zeros((G, 1), jnp.int32)
    bias_new = _bias_rows(zero_d, thr_ref, tab)

    def new_token_score(k_ref):
        return jnp.sum(qf * k_ref[0, 0].astype(BF16).astype(F32), axis=-1, keepdims=True) + bias_new

    @pl.when(j == 0)
    def _():
        m_s[...] = new_token_score(ksn_ref)
        l_s[...] = jnp.ones(l_s.shape, F32)
        acc_s[...] = jnp.broadcast_to(vsn_ref[0, 0].astype(BF16).astype(F32), acc_s.shape)

    blk = idx_ref[(b * NSA_KVH + h) * LANES + j]
    lane = lax.broadcasted_iota(jnp.int32, (G, SEL_BLOCK), 1)
    dist = qpos - (blk * SEL_BLOCK + lane)
    valid = dist >= 0
    s = lax.dot_general(qg, ksel_ref[0].astype(BF16), _NT, preferred_element_type=F32)
    s = jnp.where(valid, s + _bias_rows(dist, thr_ref, tab), NEG_INF)
    m_old = m_s[...]
    m_new = jnp.maximum(m_old, jnp.max(s, axis=-1, keepdims=True))
    alpha = jnp.exp(m_old - m_new)
    p = jnp.where(valid, jnp.exp(s - m_new), 0.0)
    l_s[...] = alpha * l_s[...] + jnp.sum(p, axis=-1, keepdims=True)
    acc_s[...] = alpha * acc_s[...] + jnp.dot(p.astype(BF16), vsel_ref[0].astype(BF16), preferred_element_type=F32)
    m_s[...] = m_new

    @pl.when(j == pl.num_programs(2) - 1)
    def _():
        o_sel = acc_s[...] / l_s[...]
        wlane = lax.broadcasted_iota(jnp.int32, (G, n_cached), 1)
        wdist = n_cached - wlane
        wvalid = (wdist < WINDOW) & (qpos - wdist >= 0)
        sw = lax.dot_general(qg, kwin_ref[0].astype(BF16), _NT, preferred_element_type=F32)
        sw = jnp.where(wvalid, sw + _bias_rows(wdist, thr_ref, tab), NEG_INF)
        sn = new_token_score(kwn_ref)
        mw = jnp.maximum(jnp.max(sw, axis=-1, keepdims=True), sn)
        pw = jnp.where(wvalid, jnp.exp(sw - mw), 0.0)
        pn = jnp.exp(sn - mw)
        lw = jnp.sum(pw, axis=-1, keepdims=True) + pn
        o_win = (jnp.dot(pw.astype(BF16), vwin_ref[0].astype(BF16), preferred_element_type=F32)
                 + pn.astype(BF16).astype(F32) * vwn_ref[0, 0].astype(BF16).astype(F32)) / lw
        gates = jax.nn.sigmoid(gate_ref[0, 0])
        o = gates[:, 0:1] * oc_ref[0, 0] + gates[:, 1:2] * o_sel + gates[:, 2:3] * o_win
        o_ref[0, 0] = o * jax.nn.silu(path_ref[0, 0])


def nsa_sample(q, kcmp, vcmp, pool_sk, pool_sv, page_table, new_kv, win_k, win_v, gate, path, table, past_len):
    B, KVH, G, hd = q.shape
    n_cb = kcmp.shape[1]
    n_pool, page = pool_sk.shape[:2]
    NP = page_table.shape[1]
    n_cached = win_k.shape[1]
    qpos = past_len
    n_sel = -(-(past_len + 1) // SEL_BLOCK)
    n_pick = min(N_SELECT, n_sel) - 1
    assert past_len % SEL_BLOCK == 0 and n_cb * CMP_BLOCK == past_len and 1 <= n_pick <= LANES
    bucket = rel_bucket(jnp.arange(past_len + 1, dtype=jnp.int32))
    thr = jnp.sum(bucket[None, :] < jnp.arange(NUM_BUCKETS, dtype=jnp.int32)[:, None], axis=1).astype(jnp.int32)
    tab = table.astype(F32).T.reshape(KVH, G, NUM_BUCKETS)
    head = lambda shape: pl.BlockSpec((1, 1) + shape, lambda b, h, *_: (b, h, 0, 0))
    cmp_spec = pl.BlockSpec((1, n_cb, hd), lambda b, h, *_: (b, 0, h))
    tab_spec = pl.BlockSpec((1, G, NUM_BUCKETS), lambda b, h, *_: (h, 0, 0))
    oc, idx = pl.pallas_call(
        functools.partial(_sample_cmp_kernel, qpos=qpos, n_pick=n_pick, hd=hd),
        grid_spec=pltpu.PrefetchScalarGridSpec(
            num_scalar_prefetch=1,
            grid=(B, KVH),
            in_specs=[head((G, hd)), cmp_spec, cmp_spec, tab_spec],
            out_specs=[head((G, hd)), head((1, LANES))],
        ),
        out_shape=[jax.ShapeDtypeStruct((B, KVH, G, hd), F32), jax.ShapeDtypeStruct((B, KVH, 1, LANES), jnp.int32)],
        compiler_params=_params("parallel", "parallel"),
        name="nsa_sample_cmp",
    )(thr, q, kcmp, vcmp, tab)

    halves = page // SEL_BLOCK

    def sel_map(b, h, j, idx_ref, pt_ref, thr_ref):
        blk = idx_ref[(b * KVH + h) * LANES + j]
        return (pt_ref[b * NP + blk // halves] * halves + blk % halves, 0, h)

    sel_spec = pl.BlockSpec((1, SEL_BLOCK, hd), sel_map)
    win_spec = pl.BlockSpec((1, n_cached, hd), lambda b, h, j, *_: (b, 0, h))
    new_spec = pl.BlockSpec((1, 1, 1, hd), lambda b, h, j, *_: (b, h, 0, 0))
    head3 = lambda shape: pl.BlockSpec((1, 1) + shape, lambda b, h, j, *_: (b, h, 0, 0))
    ks_n, vs_n, kw_n, vw_n = new_kv
    return pl.pallas_call(
        functools.partial(_sample_attn_kernel, qpos=qpos, hd=hd, n_cached=n_cached),
        grid_spec=pltpu.PrefetchScalarGridSpec(
            num_scalar_prefetch=3,
            grid=(B, KVH, n_pick),
            in_specs=[head3((G, hd)), sel_spec, sel_spec, new_spec, new_spec, win_spec, win_spec, new_spec, new_spec,
                      head3((G, hd)), head3((G, 3)), head3((G, hd)),
                      pl.BlockSpec((1, G, NUM_BUCKETS), lambda b, h, j, *_: (h, 0, 0))],
            out_specs=head3((G, hd)),
            scratch_shapes=[pltpu.VMEM((G, 1), F32), pltpu.VMEM((G, 1), F32), pltpu.VMEM((G, hd), F32)],
        ),
        out_shape=jax.ShapeDtypeStruct((B, KVH, G, hd), F32),
        compiler_params=_params("parallel", "parallel", "arbitrary"),
        name="nsa_sample_attn",
    )(idx.reshape(-1), page_table.reshape(-1), thr, q, pool_sk.reshape(n_pool * halves, SEL_BLOCK, KVH * hd),
      pool_sv.reshape(n_pool * halves, SEL_BLOCK, KVH * hd), ks_n, vs_n, win_k.reshape(B, n_cached, KVH * hd),
      win_v.reshape(B, n_cached, KVH * hd), kw_n, vw_n, oc, gate, path, tab)


def split_cols(x, sizes):
    return jnp.split(x, [int(c) for c in np.cumsum(sizes)[:-1]], axis=-1)


def head_rmsnorm(o, g):
    B, T, H, D = o.shape
    y = o * lax.rsqrt(jnp.mean(o * o, axis=-1, keepdims=True) + EPS)
    return y.reshape(B, T, H * D) * g.astype(F32)


def head_groupnorm(o, g):
    B, T, H, D = o.shape
    c = o - jnp.mean(o, axis=-1, keepdims=True)
    y = c * lax.rsqrt(jnp.mean(c * c, axis=-1, keepdims=True) + EPS)
    return y.reshape(B, T, H * D) * g.astype(F32)


def masked_softmax(s, valid):
    p = jax.nn.softmax(jnp.where(valid, s, NEG_INF), axis=-1)
    return p * valid.astype(F32)


def rel_bucket(dist):
    d = jnp.maximum(dist, 0)
    me = NUM_BUCKETS // 2
    logd = jnp.log(jnp.maximum(d, 1).astype(F32) / me)
    large = me + (logd / math.log(MAX_DISTANCE / me) * (NUM_BUCKETS - me)).astype(jnp.int32)
    return jnp.where(d < me, d, jnp.minimum(large, NUM_BUCKETS - 1))


def heads_bias(table, bucket):
    b = table.astype(F32)[bucket]
    return b.reshape(bucket.shape + (NSA_KVH, NSA_G)).transpose(2, 3, 0, 1)


def rotary(x, pos):
    half = x.shape[-1] // 2
    inv = ROPE_BASE ** (-jnp.arange(half, dtype=F32) / half)
    ang = pos.astype(F32)[:, None] * inv[None, :]
    cos = jnp.cos(ang)[None, :, None, :]
    sin = jnp.sin(ang)[None, :, None, :]
    x1, x2 = x[..., :half], x[..., half:]
    return jnp.concatenate([x1 * cos - x2 * sin, x1 * sin + x2 * cos], axis=-1)


def chunked_scan(step, S0, xs, chunk):
    B, T = xs[0].shape[:2]
    nc = T // chunk
    xs_c = tuple(jnp.moveaxis(a.reshape((B, nc, chunk) + a.shape[2:]), 1, 0) for a in xs)
    S, o = lax.scan(lambda St, c: step(St, *c), S0, xs_c)
    o = jnp.moveaxis(o, 0, 1)
    return o.reshape((B, T) + o.shape[3:]), S


def hgrn_prep(hq, hf, hv, lb, dv):
    B, T, _ = hq.shape
    f = lb[None, None, :] + (1.0 - lb[None, None, :]) * jax.nn.sigmoid(hf.astype(F32))
    q = hq.astype(F32).reshape(B, T, HG_HEADS, HG_DK) * HG_DK ** -0.5
    logf = jnp.log(f).reshape(B, T, HG_HEADS, HG_DK)
    k = (1.0 - f).reshape(B, T, HG_HEADS, HG_DK)
    v = hv.astype(F32).reshape(B, T, HG_HEADS, dv)
    return q, logf, k, v


def hgrn_chunk(S0, q, logf, k, v):
    C = q.shape[1]
    b = jnp.cumsum(logf, axis=1)
    o_inter = jnp.einsum('bthd,bhde->bthe', q * jnp.exp(b), S0)
    causal = jnp.tril(jnp.ones((C, C), dtype=bool))
    diff = b[:, :, None] - b[:, None, :]
    decay = jnp.exp(jnp.where(causal[None, :, :, None, None], diff, -jnp.inf))
    a = jnp.einsum('bthd,bshd,btshd->bhts', q, k, decay)
    o_intra = jnp.einsum('bhts,bshe->bthe', a, v)
    b_last = b[:, -1]
    S_new = jnp.exp(b_last)[..., None] * S0 + jnp.einsum('bshd,bshe->bhde', k * jnp.exp(b_last[:, None] - b), v)
    return S_new, o_inter + o_intra


def hgrn_out(o, gain, gate):
    return head_rmsnorm(o, gain) * jax.nn.silu(gate.astype(F32))


def nsa_prep(nq, kvs, hd):
    B, T, _ = nq.shape
    q = nq.reshape(B, T, NSA_KVH, NSA_G, hd) * hd ** -0.5
    kv = [a.reshape(B, T, NSA_KVH, hd) for a in split_cols(kvs, (NSA_KVH * hd,) * 6)]
    return q, kv


def compress(k, pe, w1, w2):
    B, L, KVH, HD = k.shape
    n_cb = L // CMP_BLOCK
    kb = k[:, : n_cb * CMP_BLOCK].reshape(B, n_cb, CMP_BLOCK, KVH, HD) + pe[None, None, :, None, :]
    kb = kb.transpose(0, 1, 3, 2, 4).reshape(B, n_cb, KVH, CMP_BLOCK * HD)
    return jax.nn.silu(kb @ w1) @ w2


def nsa_compressed(q, qpos, kc, vc, table):
    n_cb = kc.shape[1]
    end = (jnp.arange(n_cb, dtype=jnp.int32) + 1) * CMP_BLOCK - 1
    dist = qpos[:, None] - end[None, :]
    s = jnp.einsum('bqkgd,bnkd->bkgqn', q, kc).astype(F32) + heads_bias(table, rel_bucket(dist))
    p = masked_softmax(s, dist >= 0)
    o = jnp.einsum('bkgqn,bnkd->bqkgd', p.astype(vc.dtype), vc)
    return o, p


def select_blocks(p, qpos, L):
    B, KVH, G, Tq, n_cb = p.shape
    ratio = SEL_BLOCK // CMP_BLOCK
    n_sel = -(-L // SEL_BLOCK)
    imp = jnp.pad(p.sum(2), ((0, 0), (0, 0), (0, 0), (0, n_sel * ratio - n_cb)))
    imp = imp.reshape(B, KVH, Tq, n_sel, ratio).sum(-1)
    blk = jnp.arange(n_sel, dtype=jnp.int32)[None, :]
    cur = (qpos // SEL_BLOCK)[:, None]
    forced = (blk == 0) | (blk == cur) | (blk == cur - 1)
    valid = blk * SEL_BLOCK <= qpos[:, None]
    score = jnp.where(valid, jnp.where(forced, jnp.inf, imp), -jnp.inf)
    _, idx = lax.top_k(score, min(N_SELECT, n_sel))
    return idx


def pad_blocks(k):
    L = k.shape[1]
    n_sel = -(-L // SEL_BLOCK)
    return jnp.pad(k, ((0, 0), (0, n_sel * SEL_BLOCK - L), (0, 0), (0, 0)))


def nsa_selected(q, qpos, idx, ks, vs, table):
    B, Tq = q.shape[:2]
    n = idx.shape[-1] * SEL_BLOCK
    tok = (idx[..., None] * SEL_BLOCK + jnp.arange(SEL_BLOCK, dtype=jnp.int32)).reshape(B, NSA_KVH, Tq, n)
    bi = jnp.arange(B)[:, None, None, None]
    hi = jnp.arange(NSA_KVH)[None, :, None, None]
    kg = ks.transpose(0, 2, 1, 3)[bi, hi, tok]
    vg = vs.transpose(0, 2, 1, 3)[bi, hi, tok]
    dist = qpos[None, None, :, None] - tok
    bias = table.astype(F32).reshape(NUM_BUCKETS, NSA_KVH, NSA_G)[rel_bucket(dist), hi]
    s = jnp.einsum('bqkgd,bkqsd->bkgqs', q, kg).astype(F32) + jnp.moveaxis(bias, 4, 2)
    p = masked_softmax(s, (dist >= 0)[:, :, None])
    return jnp.einsum('bkgqs,bkqsd->bqkgd', p.astype(vg.dtype), vg)


def window_banded(q, kw, vw, table):
    B, T = q.shape[:2]
    hd = q.shape[-1]
    nb = T // WIN_QBLOCK
    span = WIN_QBLOCK + WINDOW
    pad = ((0, 0), (WINDOW, 0), (0, 0), (0, 0))
    gidx = jnp.arange(nb, dtype=jnp.int32)[:, None] * WIN_QBLOCK + jnp.arange(span, dtype=jnp.int32)[None, :]
    kb = jnp.pad(kw, pad)[:, gidx]
    vb = jnp.pad(vw, pad)[:, gidx]
    qb = q.reshape(B, nb, WIN_QBLOCK, NSA_KVH, NSA_G, hd)
    dist = jnp.arange(WIN_QBLOCK, dtype=jnp.int32)[:, None] + WINDOW - jnp.arange(span, dtype=jnp.int32)[None, :]
    kpos = gidx - WINDOW
    valid = (dist >= 0) & (dist < WINDOW) & (kpos[:, None, :] >= 0)
    s = jnp.einsum('bnqkgd,bnskd->bnkgqs', qb, kb).astype(F32) + heads_bias(table, rel_bucket(dist))
    p = masked_softmax(s, valid[None, :, None, None])
    o = jnp.einsum('bnkgqs,bnskd->bnqkgd', p.astype(vb.dtype), vb)
    return o.reshape(B, T, NSA_KVH, NSA_G, hd)


def window_dense(q, qpos, kw, vw, kpos, table):
    dist = qpos[:, None] - kpos[None, :]
    valid = (dist >= 0) & (dist < WINDOW) & (kpos[None, :] >= 0)
    s = jnp.einsum('bqkgd,bskd->bkgqs', q, kw).astype(F32) + heads_bias(table, rel_bucket(dist))
    p = masked_softmax(s, valid)
    return jnp.einsum('bkgqs,bskd->bqkgd', p.astype(vw.dtype), vw)


def nsa_combine(o_c, o_s, o_w, gate_logits, path):
    B, T = o_c.shape[:2]
    g = jax.nn.sigmoid(gate_logits.astype(F32)).reshape(B, T, NSA_KVH, NSA_G, 3)
    o = g[..., 0:1] * o_c.astype(F32) + g[..., 1:2] * o_s.astype(F32) + g[..., 2:3] * o_w.astype(F32)
    return o.reshape(B, T, -1) * jax.nn.silu(path.astype(F32))


def gather_pages(pool, page_table):
    B, NP = page_table.shape
    g = pool[page_table]
    return g.reshape((B, NP * PAGE_SIZE) + pool.shape[2:])


def retention_chunk(S0, q, k, v, log_gamma):
    C = q.shape[1]
    i = jnp.arange(C, dtype=F32)
    diff = i[:, None] - i[None, :]
    decay = jnp.where(diff[None] >= 0, jnp.exp(diff[None] * log_gamma[:, None, None]), 0.0)
    a = jnp.einsum('bthd,bshd->bhts', q, k) * decay[None]
    inner = jnp.einsum('bhts,bshe->bthe', a, v)
    q_dec = q * jnp.exp((i + 1.0)[:, None] * log_gamma[None, :])[None, :, :, None]
    cross = jnp.einsum('bthd,bhde->bthe', q_dec, S0)
    k_dec = k * jnp.exp((C - 1.0 - i)[:, None] * log_gamma[None, :])[None, :, :, None]
    S_new = jnp.exp(C * log_gamma)[None, :, None, None] * S0 + jnp.einsum('bshd,bshe->bhde', k_dec, v)
    return S_new, inner + cross


SAMPLE_ROWS = 16
MATMUL_TN = 512


class EvenCols:
    def __init__(self, hg_qk, hg_w, nsa_w, nsa_kvw):
        self.hd = nsa_w // NSA_HEADS
        self.kvw = nsa_kvw
        self.nsa_w = nsa_w
        self.q, self.f, self.v, self.g = 0, hg_qk, 2 * hg_qk, 2 * hg_qk + hg_w
        self.nq = 2 * hg_qk + 2 * hg_w
        self.kv = self.nq + nsa_w
        self.main_w = self.kv + 6 * nsa_kvw
        self.gate_w = 3 * NSA_G
        self.tail_path, self.tail_gate = 0, nsa_w
        used = nsa_w + NSA_KVH * LANES
        self.tail_w = -(-used // MATMUL_TN) * MATMUL_TN
        assert self.main_w % MATMUL_TN == 0

    def tail_weight(self, w_in):
        gate0 = self.main_w
        path0 = gate0 + NSA_KVH * self.gate_w
        parts = [w_in[:, path0:path0 + self.nsa_w]]
        for h in range(NSA_KVH):
            parts.append(jnp.pad(w_in[:, gate0 + h * self.gate_w:gate0 + (h + 1) * self.gate_w],
                                 ((0, 0), (0, LANES - self.gate_w))))
        w = jnp.concatenate(parts, axis=1)
        return jnp.pad(w, ((0, 0), (0, self.tail_w - w.shape[1])))


def even_prompt(h2, B, T, w_in, w_tail, w_out, lb, hg_gain, pe, w1k, w2k, w1v, w2v, table, ec):
    hd = ec.hd
    main = matmul(h2, w_in, n_cols=ec.main_w)
    tail = matmul(h2, w_tail)
    S0 = jnp.zeros((B, HG_HEADS, HG_DK, hg_gain.shape[0] // HG_HEADS), F32)
    y_hg, S = hgrn_mix(main, lb, hg_gain, S0, B, T, (ec.q, ec.f, ec.v, ec.g), HG_CHUNK, HGRN_SUB, HG_CHUNK)
    kvs = main[:, ec.kv:ec.main_w].reshape(B, T, 6, NSA_KVH, hd)
    kc, vc, ks, vs, kw, vw = [kvs[:, :, j] for j in range(6)]
    kcmp = compress_blocks(kc, pe, w1k, w2k)
    vcmp = compress_blocks(vc, pe, w1v, w2v)
    y_nsa = nsa_prompt(main, tail, kcmp, vcmp, table, B, T, (ec.nq, ec.kv, ec.tail_path, ec.tail_gate, hd))
    y = matmul(jnp.concatenate([y_hg, y_nsa], axis=-1), w_out)
    wb = min(WINDOW, T)
    return y, S, kc, vc, ks, vs, kw[:, T - wb:], vw[:, T - wb:]


def even_sample(h2, B, T, pos, past_len, S0, page_table, pool_ck, pool_cv, pool_sk, pool_sv, win_k, win_v,
                w_in, w_tail, w_out, lb, hg_gain, pe, w1k, w2k, w1v, w2v, table, ec):
    hd = ec.hd
    R = SAMPLE_ROWS
    main = matmul(h2, w_in, n_cols=ec.main_w)
    tail = matmul(h2, w_tail)
    y_hg, S = hgrn_mix(main, lb, hg_gain, S0.astype(F32), B, R, (ec.q, ec.f, ec.v, ec.g), R, R, T)
    m3 = main.reshape(B, R, -1)[:, :T]
    t3 = tail.reshape(B, R, -1)[:, :T]
    assert T == 1
    heads = lambda a, w: a.reshape(B, NSA_KVH, NSA_G, w)
    q = heads(m3[..., ec.nq:ec.kv], hd)
    path = heads(t3[..., ec.tail_path:ec.tail_path + ec.nsa_w], hd)
    gate = jnp.stack([t3[:, 0, ec.tail_gate + h * LANES:ec.tail_gate + h * LANES + ec.gate_w].reshape(B, NSA_G, 3)
                      for h in range(NSA_KVH)], axis=1)
    kvs = m3[..., ec.kv:ec.main_w].reshape(B, T, 6, NSA_KVH, hd)
    kc, vc, ks, vs, kw, vw = [kvs[:, :, j] for j in range(6)]
    new_kv = tuple(jnp.swapaxes(a, 1, 2) for a in (ks, vs, kw, vw))
    kcmp = compress_paged(pool_ck, page_table, pe, w1k, w2k)
    vcmp = compress_paged(pool_cv, page_table, pe, w1v, w2v)
    y_nsa = nsa_sample(q, kcmp, vcmp, pool_sk, pool_sv, page_table, new_kv, win_k, win_v, gate, path, table,
                       past_len)
    y_nsa = jnp.pad(y_nsa.reshape(B, T, -1).astype(BF16), ((0, 0), (0, R - T), (0, 0))).reshape(B * R, -1)
    y = matmul(jnp.concatenate([y_hg, y_nsa], axis=-1), w_out)
    kw_all = jnp.concatenate([win_k, kw], axis=1)
    vw_all = jnp.concatenate([win_v, vw], axis=1)
    return y, S, kc, vc, ks, vs, kw_all[:, T:], vw_all[:, T:]


def retention_layer(h2, B, T, pos, S0, w_in, w_out, gain, C, c_eff):
    y, S = retention_mix(matmul(h2, w_in), pos, gain, S0, B, T, C, c_eff)
    return matmul(y, w_out), S


def mem_attend(h2, B, T, mk, mv, w_q, w_o):
    slots = mk.shape[1]
    o = mem_mix(matmul(h2, w_q), mk.reshape(B, slots, -1), mv.reshape(B, slots, -1), B, T)
    return matmul(o, w_o)


def kernel(x_prompt, x_sample, mem_prompt, state_hgrn, cache_cmp_k, cache_cmp_v, cache_sel_k, cache_sel_v,
           cache_win_k, cache_win_v, state_ret, cache_mem_k, cache_mem_v, page_table, rel_table,
           norm_mix_pre, norm_mix_post, norm_mem_pre, norm_mem_post, ev_w_in, ev_w_out, hgrn_lb, hgrn_norm,
           cmp_pe, cmp_w1_k, cmp_w2_k, cmp_w1_v, cmp_w2_v, od_w_in, od_w_out, ret_norm,
           mem_w_q, mem_w_k, mem_w_v, mem_w_o):
    B, T, D = x_prompt.shape
    Bs, Ts, _ = x_sample.shape
    depth = norm_mix_pre.shape[0]
    past_len = page_table.shape[1] * PAGE_SIZE
    pos_p = jnp.arange(T, dtype=jnp.int32)
    pos_s = past_len + jnp.arange(Ts, dtype=jnp.int32)
    lb_all = jnp.cumsum(jax.nn.softmax(hgrn_lb.astype(F32), axis=0), axis=0)
    hg_qk = hgrn_lb.shape[1]
    hg_w = hgrn_norm.shape[1]
    nsa_w = ev_w_out.shape[1] - hg_w
    nsa_kvw = NSA_KVH * (nsa_w // NSA_HEADS)
    ec = EvenCols(hg_qk, hg_w, nsa_w, nsa_kvw)
    R = SAMPLE_ROWS
    assert Ts <= R

    even_p, even_s, ret_p, ret_s, memk_p, memv_p = [], [], [], [], [], []
    xp = x_prompt.reshape(B * T, D)
    xs = jnp.pad(x_sample, ((0, 0), (0, R - Ts), (0, 0))).reshape(Bs * R, D)
    pos_sr = past_len + jnp.arange(R, dtype=jnp.int32)
    mem2 = mem_prompt.reshape(-1, D).astype(BF16)
    ML = mem_prompt.shape[1]
    for l in range(depth):
        hp = rmsnorm_bf16(xp, norm_mix_pre[l])
        hs = rmsnorm_bf16(xs, norm_mix_pre[l])
        if l % 2 == 0:
            e = l // 2
            w = (ev_w_in[e], ec.tail_weight(ev_w_in[e]), ev_w_out[e], lb_all[e], hgrn_norm[e], cmp_pe[e],
                 cmp_w1_k[e], cmp_w2_k[e], cmp_w1_v[e], cmp_w2_v[e], rel_table)
            yp, *sp = even_prompt(hp, B, T, *w, ec)
            ys, *ss = even_sample(hs, Bs, Ts, pos_s, past_len, state_hgrn[e], page_table, cache_cmp_k[e],
                                  cache_cmp_v[e], cache_sel_k[e], cache_sel_v[e], cache_win_k[e], cache_win_v[e],
                                  *w, ec)
            even_p.append(sp)
            even_s.append(ss)
        else:
            o = l // 2
            S0p = jnp.zeros((B,) + state_ret.shape[2:], F32)
            yp, sp = retention_layer(hp, B, T, pos_p, S0p, od_w_in[o], od_w_out[o], ret_norm[o],
                                     RET_CHUNK, RET_CHUNK)
            ys, ss = retention_layer(hs, Bs, R, pos_sr, state_ret[o].astype(F32), od_w_in[o], od_w_out[o],
                                     ret_norm[o], R, Ts)
            ret_p.append(sp)
            ret_s.append(ss)
        xp = residual_post(xp, yp, norm_mix_post[l])
        xs = residual_post(xs, ys, norm_mix_post[l])
        mk_p = matmul(mem2, mem_w_k[l]).reshape(B, ML, MEM_HEADS, MEM_HD)
        mv_p = matmul(mem2, mem_w_v[l]).reshape(B, ML, MEM_HEADS, MEM_HD)
        memk_p.append(mk_p)
        memv_p.append(mv_p)
        hp = rmsnorm_bf16(xp, norm_mem_pre[l])
        hs = rmsnorm_bf16(xs, norm_mem_pre[l])
        yp = mem_attend(hp, B, T, mk_p, mv_p, mem_w_q[l], mem_w_o[l])
        ys = mem_attend(hs, Bs, R, cache_mem_k[l], cache_mem_v[l], mem_w_q[l], mem_w_o[l])
        xp = residual_post(xp, yp, norm_mem_post[l])
        xs = residual_post(xs, ys, norm_mem_post[l])
    p_hgrn, p_cmp_k, p_cmp_v, p_sel_k, p_sel_v, p_win_k, p_win_v = [jnp.stack(a) for a in zip(*even_p)]
    s_hgrn, s_cmp_k, s_cmp_v, s_sel_k, s_sel_v, s_win_k, s_win_v = [jnp.stack(a) for a in zip(*even_s)]
    p_ret = jnp.stack(ret_p)
    s_ret = jnp.stack(ret_s)
    p_mem_k = jnp.stack(memk_p)
    p_mem_v = jnp.stack(memv_p)
    return (xp.reshape(B, T, D), xs.reshape(Bs, R, D)[:, :Ts], p_hgrn, p_cmp_k, p_cmp_v, p_sel_k, p_sel_v,
            p_win_k, p_win_v, p_ret, p_mem_k, p_mem_v,
            s_hgrn, s_cmp_k, s_cmp_v, s_sel_k, s_sel_v, s_win_k, s_win_v, s_ret)
```

```python
import functools
import math

import jax
import jax.numpy as jnp
import numpy as np
from jax import lax
from jax.experimental import pallas as pl
from jax.experimental.pallas import tpu as pltpu

F32 = jnp.float32
BF16 = jnp.bfloat16
EPS = 1e-6
NEG_INF = -1e30

PAGE_SIZE = 128
HG_HEADS = 16
HG_DK = 128
HG_CHUNK = 64
HGRN_SUB = 16
NSA_HEADS = 16
NSA_KVH = 2
NSA_G = NSA_HEADS // NSA_KVH
CMP_BLOCK = 32
SEL_BLOCK = 64
N_SELECT = 16
WINDOW = 512
SEL_QBLOCK = 64
WIN_QBLOCK = 128
NUM_BUCKETS = 32
MAX_DISTANCE = 1024
RET_HEADS = 16
RET_CHUNK = 128
ROPE_BASE = 10000.0
MEM_HEADS = 4
MEM_HD = 128
MEM_W = MEM_HEADS * MEM_HD

VMEM_LIMIT_BYTES = 56 * 1024 * 1024


def _params(*sem):
    return pltpu.CompilerParams(dimension_semantics=sem, vmem_limit_bytes=VMEM_LIMIT_BYTES)


def _rmsnorm_kernel(x_ref, g_ref, o_ref):
    x = x_ref[...]
    y = x * lax.rsqrt(jnp.mean(x * x, axis=-1, keepdims=True) + EPS)
    o_ref[...] = (y * g_ref[...]).astype(o_ref.dtype)


def rmsnorm_bf16(x, g):
    M, D = x.shape
    tm = min(M, 512)
    return pl.pallas_call(
        _rmsnorm_kernel,
        grid=(M // tm,),
        in_specs=[pl.BlockSpec((tm, D), lambda i: (i, 0)), pl.BlockSpec((1, D), lambda i: (0, 0))],
        out_specs=pl.BlockSpec((tm, D), lambda i: (i, 0)),
        out_shape=jax.ShapeDtypeStruct((M, D), BF16),
        compiler_params=_params("parallel"),
        name="rmsnorm_bf16",
    )(x, g.reshape(1, D))


def _post_kernel(x_ref, y_ref, g_ref, o_ref):
    y = y_ref[...]
    n = y * lax.rsqrt(jnp.mean(y * y, axis=-1, keepdims=True) + EPS)
    o_ref[...] = x_ref[...] + n * g_ref[...]


def residual_post(x, y, g):
    M, D = x.shape
    tm = min(M, 512)
    return pl.pallas_call(
        _post_kernel,
        grid=(M // tm,),
        in_specs=[pl.BlockSpec((tm, D), lambda i: (i, 0)), pl.BlockSpec((tm, D), lambda i: (i, 0)),
                  pl.BlockSpec((1, D), lambda i: (0, 0))],
        out_specs=pl.BlockSpec((tm, D), lambda i: (i, 0)),
        out_shape=jax.ShapeDtypeStruct((M, D), F32),
        compiler_params=_params("parallel"),
        name="residual_post",
    )(x, y, g.reshape(1, D))


def _post_pre_kernel(x_ref, y_ref, g_ref, gn_ref, o_ref, h_ref):
    y = y_ref[...]
    x = x_ref[...] + y * lax.rsqrt(jnp.mean(y * y, axis=-1, keepdims=True) + EPS) * g_ref[...]
    o_ref[...] = x
    h_ref[...] = (x * lax.rsqrt(jnp.mean(x * x, axis=-1, keepdims=True) + EPS) * gn_ref[...]).astype(h_ref.dtype)


def residual_post_pre(x, y, g, g_next):
    M, D = x.shape
    tm = min(M, 256)
    row = pl.BlockSpec((tm, D), lambda i: (i, 0))
    vec = pl.BlockSpec((1, D), lambda i: (0, 0))
    return pl.pallas_call(
        _post_pre_kernel,
        grid=(M // tm,),
        in_specs=[row, row, vec, vec],
        out_specs=[row, row],
        out_shape=[jax.ShapeDtypeStruct((M, D), F32), jax.ShapeDtypeStruct((M, D), BF16)],
        compiler_params=_params("parallel"),
        name="residual_post_pre",
    )(x, y, g.reshape(1, D), g_next.reshape(1, D))


def _matmul_kernel(x_ref, w_ref, o_ref, wb_ref, *, transposed):
    @pl.when(pl.program_id(1) == 0)
    def _():
        w = w_ref[...]
        wb_ref[...] = (w.T if transposed else w).astype(BF16)

    o_ref[...] = jnp.dot(x_ref[...], wb_ref[...], preferred_element_type=F32)


def matmul(x, w, layer=None, n_cols=None, transposed=False):
    M, K = x.shape
    N = (w.shape[-2] if transposed else w.shape[-1]) if n_cols is None else n_cols
    tn = 512 if K <= 4096 else 256
    tn = min(tn, N)
    tm = min(M, 1024 if K <= 4096 else 512)
    assert N % tn == 0 and M % tm == 0, (M, K, N)
    blk = (tn, K) if transposed else (K, tn)
    pick = (lambda n: (n, 0)) if transposed else (lambda n: (0, n))
    if w.ndim == 3:
        w_spec = pl.BlockSpec((None,) + blk, lambda n, m: (layer,) + pick(n))
    else:
        w_spec = pl.BlockSpec(blk, lambda n, m: pick(n))
    return pl.pallas_call(
        functools.partial(_matmul_kernel, transposed=transposed),
        grid=(N // tn, M // tm),
        in_specs=[pl.BlockSpec((tm, K), lambda n, m: (m, 0)), w_spec],
        out_specs=pl.BlockSpec((tm, tn), lambda n, m: (m, n)),
        out_shape=jax.ShapeDtypeStruct((M, N), F32),
        scratch_shapes=[pltpu.VMEM((K, tn), BF16)],
        compiler_params=_params("parallel", "arbitrary"),
        name="matmul",
    )(x, w)


LANES = 128
NSA_TQ = 128
_NT = (((1,), (1,)), ((), ()))


_TN = (((0,), (0,)), ((), ()))


def _bias_tile_kernel(thr_ref, table_ref, o_ref, *, row_stride, row_offset):
    kvh = pl.program_id(0)
    step = pl.program_id(1)
    row = lax.broadcasted_iota(jnp.int32, (LANES, LANES), 0)
    lane = lax.broadcasted_iota(jnp.int32, (LANES, LANES), 1)
    dist = LANES * step + lane - row_stride * row - row_offset
    for g in range(NSA_G):
        h = kvh * NSA_G + g
        bias = jnp.full((LANES, LANES), table_ref[0, h], F32)
        for k in range(1, NUM_BUCKETS):
            bias = jnp.where(dist >= thr_ref[k], table_ref[k, h], bias)
        o_ref[0, 0, :, g * LANES:(g + 1) * LANES] = bias


def bias_tiles(table, n_steps, max_dist, row_stride, row_offset):
    bucket = rel_bucket(jnp.arange(max_dist, dtype=jnp.int32))
    thr = jnp.sum(bucket[None, :] < jnp.arange(NUM_BUCKETS, dtype=jnp.int32)[:, None], axis=1).astype(jnp.int32)
    smem = pl.BlockSpec(memory_space=pltpu.SMEM)
    return pl.pallas_call(
        functools.partial(_bias_tile_kernel, row_stride=row_stride, row_offset=row_offset),
        grid=(NSA_KVH, n_steps),
        in_specs=[smem, smem],
        out_specs=pl.BlockSpec((1, 1, LANES, NSA_G * LANES), lambda h, s: (h, s, 0, 0)),
        out_shape=jax.ShapeDtypeStruct((NSA_KVH, n_steps, LANES, NSA_G * LANES), F32),
        compiler_params=_params("parallel", "parallel"),
        name="bias_tiles",
    )(thr, table.astype(F32))


def _nsa_prompt_kernel(q_ref, kc_ref, vc_ref, ks_ref, vs_ref, kw_ref, vw_ref, cb_ref, tb_ref, gate_ref, path_ref,
                       o_ref, qb_s, oc_s, m_s, l_s, acc_s, *, hd, n_win_chunks, n_pick):
    i = pl.program_id(2)
    tq = NSA_TQ
    G = NSA_G
    scale = hd ** -0.5
    row = lax.broadcasted_iota(jnp.int32, (LANES, tq), 0)
    qpos = i * tq + lax.broadcasted_iota(jnp.int32, (LANES, tq), 1)
    slab = lambda g: slice(g * tq, (g + 1) * tq)

    for g in range(G):
        qb_s[slab(g), :] = (q_ref[:, g * hd:(g + 1) * hd] * scale).astype(BF16)
    qb = qb_s[...]

    s_all = lax.dot_general(kc_ref[0].astype(BF16), qb, _NT, preferred_element_type=F32) + cb_ref[0, 0]
    valid_c = qpos >= (row + 1) * CMP_BLOCK - 1
    valid_cf = valid_c.astype(F32)
    imp = jnp.zeros((LANES, tq), F32)
    ps = []
    for g in range(G):
        s = jnp.where(valid_c, s_all[:, slab(g)], NEG_INF)
        e = jnp.exp(s - jnp.max(s, axis=0, keepdims=True))
        p = e / jnp.sum(e, axis=0, keepdims=True) * valid_cf
        imp = imp + p
        ps.append(p.astype(BF16))
    oc_s[...] = lax.dot_general(vc_ref[0].astype(BF16), jnp.concatenate(ps, axis=1), _TN,
                                preferred_element_type=F32)

    imp2 = imp + pltpu.roll(imp, LANES - 1, axis=0)
    blk = row >> 1
    cur = qpos // SEL_BLOCK
    forced = (blk == 0) | (blk == cur) | (blk == cur - 1)
    usable = ((row & 1) == 0) & (blk * SEL_BLOCK <= qpos)
    work = jnp.where(usable, jnp.where(forced, jnp.inf, imp2), -jnp.inf)
    sel = jnp.zeros((LANES, tq), F32)
    for _ in range(n_pick):
        mx = jnp.max(work, axis=0, keepdims=True)
        first = jnp.min(jnp.where(work == mx, row, LANES), axis=0, keepdims=True)
        pick = row == first
        sel = jnp.where(pick, 1.0, sel)
        work = jnp.where(pick, -jnp.inf, work)
    sel_b = sel.astype(BF16)

    m_s[...] = jnp.full(m_s.shape, NEG_INF, F32)
    l_s[...] = jnp.zeros(l_s.shape, F32)
    acc_s[...] = jnp.zeros(acc_s.shape, F32)
    ekey = lax.broadcasted_iota(jnp.int32, (LANES, LANES), 0)
    eblk = lax.broadcasted_iota(jnp.int32, (LANES, LANES), 1)

    def online(slot, k_b, v_b, bias, valid):
        s_all = lax.dot_general(k_b, qb, _NT, preferred_element_type=F32) + bias
        ps, alphas = [], []
        for g in range(G):
            s = jnp.where(valid, s_all[:, slab(g)], NEG_INF)
            m_old = m_s[slot, :, slab(g)]
            m_new = jnp.maximum(m_old, jnp.max(s, axis=0, keepdims=True))
            alpha = jnp.exp(m_old - m_new)
            p = jnp.where(valid, jnp.exp(s - m_new), 0.0)
            l_s[slot, :, slab(g)] = alpha * l_s[slot, :, slab(g)] + jnp.sum(p, axis=0, keepdims=True)
            m_s[slot, :, slab(g)] = m_new
            ps.append(p.astype(BF16))
            alphas.append(alpha)
        pv = lax.dot_general(v_b, jnp.concatenate(ps, axis=1), _TN, preferred_element_type=F32)
        acc_s[slot] = jnp.concatenate(alphas, axis=1) * acc_s[slot] + pv

    def chunk(c, carry):
        k0 = pl.multiple_of(c * LANES, LANES)
        delta = i - c
        dist = qpos - (k0 + row)
        causal = dist >= 0
        expand = (eblk == 2 * (c * (LANES // SEL_BLOCK) + (ekey // SEL_BLOCK))).astype(BF16)
        chosen = jnp.dot(expand, sel_b, preferred_element_type=F32) > 0.5
        bias = tb_ref[0, delta]
        online(0, ks_ref[pl.ds(k0, LANES), :].astype(BF16), vs_ref[pl.ds(k0, LANES), :].astype(BF16), bias,
               chosen & causal)

        @pl.when(delta < n_win_chunks)
        def _():
            online(1, kw_ref[pl.ds(k0, LANES), :].astype(BF16), vw_ref[pl.ds(k0, LANES), :].astype(BF16), bias,
                   causal & (dist < WINDOW))

        return carry

    lax.fori_loop(0, i + 1, chunk, 0)

    gates = jax.nn.sigmoid(gate_ref[...]).T
    for g in range(G):
        o_sel = acc_s[0, :, slab(g)] / l_s[0, :, slab(g)]
        o_win = acc_s[1, :, slab(g)] / l_s[1, :, slab(g)]
        o = (gates[3 * g:3 * g + 1] * oc_s[:, slab(g)] + gates[3 * g + 1:3 * g + 2] * o_sel
             + gates[3 * g + 2:3 * g + 3] * o_win)
        o_ref[:, g * hd:(g + 1) * hd] = (o.T * jax.nn.silu(path_ref[:, g * hd:(g + 1) * hd])).astype(o_ref.dtype)


def nsa_prompt(main, tail, kcmp, vcmp, table, B, T, cols):
    q_col, kv_col, path_col, gate_col, hd = cols
    G = NSA_G
    tq = NSA_TQ
    n_cb = kcmp.shape[1]
    assert n_cb <= LANES and T % tq == 0 and hd == LANES
    nd = T // LANES
    n_win_chunks = WINDOW // LANES + 1
    tb = bias_tiles(table, nd, T, 1, 0)
    cb = bias_tiles(table, T // tq, T, CMP_BLOCK, CMP_BLOCK - 1)
    pad = ((0, 0), (0, LANES - n_cb), (0, 0))
    kcmp = jnp.pad(kcmp, pad)
    vcmp = jnp.pad(vcmp, pad)
    nblk = T // tq
    gw = G * hd
    row_map = lambda b, h, i: (b * nblk + i, 0)
    kv_spec = lambda j: pl.BlockSpec((T, hd), lambda b, h, i: (b, kv_col // hd + 2 * j + h))
    return pl.pallas_call(
        functools.partial(_nsa_prompt_kernel, hd=hd, n_win_chunks=n_win_chunks,
                          n_pick=min(N_SELECT, -(-T // SEL_BLOCK))),
        grid=(B, NSA_KVH, nblk),
        in_specs=[
            pl.BlockSpec((tq, gw), lambda b, h, i: (b * nblk + i, q_col // gw + h)),
            pl.BlockSpec((1, LANES, hd), lambda b, h, i: (b, 0, h)),
            pl.BlockSpec((1, LANES, hd), lambda b, h, i: (b, 0, h)),
            kv_spec(2), kv_spec(3), kv_spec(4), kv_spec(5),
            pl.BlockSpec((1, 1, LANES, gw), lambda b, h, i: (h, i, 0, 0)),
            pl.BlockSpec((1, nd, LANES, gw), lambda b, h, i: (h, 0, 0, 0)),
            pl.BlockSpec((tq, LANES), lambda b, h, i: (b * nblk + i, gate_col // LANES + h)),
            pl.BlockSpec((tq, gw), lambda b, h, i: (b * nblk + i, path_col // gw + h)),
        ],
        out_specs=pl.BlockSpec((tq, gw), lambda b, h, i: (b * nblk + i, h)),
        out_shape=jax.ShapeDtypeStruct((B * T, NSA_KVH * gw), BF16),
        scratch_shapes=[pltpu.VMEM((G * tq, hd), BF16), pltpu.VMEM((hd, G * tq), F32),
                        pltpu.VMEM((2, 1, G * tq), F32), pltpu.VMEM((2, 1, G * tq), F32),
                        pltpu.VMEM((2, hd, G * tq), F32)],
        compiler_params=_params("parallel", "parallel", "arbitrary"),
        name="nsa_prompt",
    )(main, kcmp, vcmp, main, main, main, main, cb, tb, tail, tail)


_TN = (((0,), (0,)), ((), ()))


def _cumsum_rows(x, n):
    row = lax.broadcasted_iota(jnp.int32, x.shape, 0)
    sh = 1
    while sh < n:
        x = x + jnp.where(row >= sh, pltpu.roll(x, sh, axis=0), 0.0)
        sh *= 2
    return x


def _hgrn_kernel(q_ref, f_ref, v_ref, gate_ref, lb_ref, gain_ref, s0_ref, y_ref, s_out_ref, st_s,
                 *, C, SB, c_eff, dk, HB):
    c = pl.program_id(2)

    @pl.when(c == 0)
    def _():
        for hh in range(HB):
            st_s[hh] = s0_ref[0, hh].T

    for hh in range(HB):
        _hgrn_head(hh, slice(hh * dk, (hh + 1) * dk), q_ref, f_ref, v_ref, gate_ref, lb_ref, gain_ref,
                   y_ref, st_s, C=C, SB=SB, c_eff=c_eff, dk=dk)

    @pl.when(c == pl.num_programs(2) - 1)
    def _():
        for hh in range(HB):
            s_out_ref[0, hh] = st_s[hh].T


def _hgrn_head(hh, sl, q_ref, f_ref, v_ref, gate_ref, lb_ref, gain_ref, y_ref, st_s, *, C, SB, c_eff, dk):
    lb = lb_ref[:, sl]
    f = lb + (1.0 - lb) * jax.nn.sigmoid(f_ref[:, sl])
    logf = jnp.log(f)
    k = 1.0 - f
    if c_eff < C:
        real = lax.broadcasted_iota(jnp.int32, (C, dk), 0) < c_eff
        logf = jnp.where(real, logf, 0.0)
        k = jnp.where(real, k, 0.0)
    q = q_ref[:, sl] * dk ** -0.5
    v_b = v_ref[:, sl].astype(BF16)
    b = _cumsum_rows(logf, C)
    st = st_s[hh]
    o = lax.dot_general((q * jnp.exp(b)).astype(BF16), st.astype(BF16), _NT, preferred_element_type=F32)

    lane_c = lax.broadcasted_iota(jnp.int32, (SB, C), 1)
    row_c = lax.broadcasted_iota(jnp.int32, (SB, C), 0)
    outs = []
    for I in range(C // SB):
        r0 = I * SB
        q_i = q[r0:r0 + SB]
        b_i = b[r0:r0 + SB]
        if I > 0:
            b_r = b[r0 - 1:r0]
            qq = (q_i * jnp.exp(b_i - b_r)).astype(BF16)
            kk = (k * jnp.exp(jnp.minimum(b_r - b, 0.0))).astype(BF16)
            a = lax.dot_general(qq, kk, _NT, preferred_element_type=F32)
            a = jnp.where(lane_c < r0, a, 0.0)
        else:
            a = jnp.zeros((SB, C), F32)
        for s in range(SB):
            z = q_i * k[r0 + s:r0 + s + 1] * jnp.exp(jnp.minimum(b_i - b[r0 + s:r0 + s + 1], 0.0))
            col = jnp.sum(z, axis=-1, keepdims=True)
            a = jnp.where((lane_c == r0 + s) & (row_c >= s), col, a)
        outs.append(jnp.dot(a.astype(BF16), v_b, preferred_element_type=F32))
    o = o + jnp.concatenate(outs, axis=0)

    b_last = b[C - 1:C]
    kd = (k * jnp.exp(b_last - b)).astype(BF16)
    st_new = jnp.exp(b_last) * st + lax.dot_general(v_b, kd, _TN, preferred_element_type=F32)
    st_s[hh] = st_new

    y = o * lax.rsqrt(jnp.mean(o * o, axis=-1, keepdims=True) + EPS) * gain_ref[:, sl]
    y_ref[:, sl] = (y * jax.nn.silu(gate_ref[:, sl])).astype(y_ref.dtype)


HGRN_HEADS_PER_STEP = 8


def hgrn_mix(main, lb, gain, s0, B, T, cols, C, SB, c_eff):
    q_col, f_col, v_col, g_col = cols
    _, H, dk, dv = s0.shape
    HB = HGRN_HEADS_PER_STEP
    W = HB * LANES
    assert dk == LANES and dv == LANES and T % C == 0 and C % SB == 0 and H % HB == 0
    assert all(off % W == 0 for off in cols)
    nc = T // C
    col = lambda off: pl.BlockSpec((C, W), lambda b, h, c: (b * nc + c, off // W + h))
    vec = pl.BlockSpec((1, W), lambda b, h, c: (0, h))
    st = pl.BlockSpec((1, HB, dk, dv), lambda b, h, c: (b, h, 0, 0))
    return pl.pallas_call(
        functools.partial(_hgrn_kernel, C=C, SB=SB, c_eff=c_eff, dk=dk, HB=HB),
        grid=(B, H // HB, nc),
        in_specs=[col(q_col), col(f_col), col(v_col), col(g_col), vec, vec, st],
        out_specs=[pl.BlockSpec((C, W), lambda b, h, c: (b * nc + c, h)), st],
        out_shape=[jax.ShapeDtypeStruct((B * T, H * dv), BF16), jax.ShapeDtypeStruct(s0.shape, F32)],
        scratch_shapes=[pltpu.VMEM((HB, dv, dk), F32)],
        compiler_params=_params("parallel", "parallel", "arbitrary"),
        name="hgrn_mix",
    )(main, main, main, main, lb.reshape(1, -1), gain.reshape(1, -1), s0)


def _retention_kernel(q_ref, k_ref, v_ref, g_ref, cos_ref, sin_ref, lg_ref, gain_ref, s0_ref, y_ref, s_out_ref, s_s,
                      *, C, c_eff, dk, dv, HB):
    c = pl.program_id(2)

    @pl.when(c == 0)
    def _():
        s_s[...] = s0_ref[0]

    for hh in range(HB):
        _retention_head(hh, q_ref, k_ref, v_ref, g_ref, cos_ref, sin_ref, lg_ref, gain_ref, y_ref, s_s,
                        C=C, c_eff=c_eff, dk=dk, dv=dv)

    @pl.when(c == pl.num_programs(2) - 1)
    def _():
        s_out_ref[0] = s_s[...]


def _retention_head(hh, q_ref, k_ref, v_ref, g_ref, cos_ref, sin_ref, lg_ref, gain_ref, y_ref, s_s,
                    *, C, c_eff, dk, dv):
    half = dk // 2
    vsl = slice(hh * dv, (hh + 1) * dv)
    cos = cos_ref[...]
    sin = sin_ref[...]
    lg_w = lg_ref[hh]
    lg = lg_w[:, :LANES]
    row = lax.broadcasted_iota(jnp.int32, (C, LANES), 0).astype(F32)

    def rot(ref, w):
        x1 = ref[:, hh * dk:hh * dk + half]
        x2 = ref[:, hh * dk + half:(hh + 1) * dk]
        return jnp.concatenate([(x1 * cos - x2 * sin) * w, (x1 * sin + x2 * cos) * w], axis=1)

    q = rot(q_ref, dk ** -0.5)
    k = rot(k_ref, 1.0)
    v_b = v_ref[:, vsl].astype(BF16)
    s_old = s_s[hh]

    a = lax.dot_general(q.astype(BF16), k.astype(BF16), _NT, preferred_element_type=F32)
    ti = lax.broadcasted_iota(jnp.int32, (C, C), 0)
    si = lax.broadcasted_iota(jnp.int32, (C, C), 1)
    diff = (ti - si).astype(F32)
    a = a * jnp.where(diff >= 0, jnp.exp(diff * lg_w[:, :C]), 0.0)
    inner = jnp.dot(a.astype(BF16), v_b, preferred_element_type=F32)
    q_w = jnp.exp((row + 1.0) * lg)
    q_dec = q * jnp.concatenate([q_w] * (dk // LANES), axis=1)
    cross = jnp.dot(q_dec.astype(BF16), s_old.astype(BF16), preferred_element_type=F32)
    k_w = jnp.where(row < c_eff, jnp.exp((c_eff - 1.0 - row) * lg), 0.0)
    k_dec = k * jnp.concatenate([k_w] * (dk // LANES), axis=1)
    s_new = jnp.exp(c_eff * lg_w) * s_old + lax.dot_general(k_dec.astype(BF16), v_b, _TN, preferred_element_type=F32)
    s_s[hh] = s_new

    o = inner + cross
    cen = o - jnp.mean(o, axis=-1, keepdims=True)
    y = cen * lax.rsqrt(jnp.mean(cen * cen, axis=-1, keepdims=True) + EPS) * gain_ref[:, vsl]
    y_ref[:, vsl] = (y * jax.nn.silu(g_ref[:, vsl])).astype(y_ref.dtype)


RETENTION_HEADS_PER_STEP = 4


def retention_mix(proj, pos, gain, s0, B, T, C, c_eff):
    _, H, dk, dv = s0.shape
    HB = RETENTION_HEADS_PER_STEP
    assert T % C == 0 and C <= LANES and dk % LANES == 0 and H % HB == 0
    nc = T // C
    nh = H // HB
    half = dk // 2
    inv = ROPE_BASE ** (-jnp.arange(half, dtype=F32) / half)
    ang = pos.astype(F32)[:, None] * inv[None, :]
    log_gamma = jnp.log1p(-jnp.exp2(-5.0 - jnp.arange(H, dtype=F32)))
    lg = jnp.broadcast_to(log_gamma[:, None, None], (H, 1, dv))
    qk = lambda j: pl.BlockSpec((C, HB * dk), lambda b, h, c: (b * nc + c, j * nh + h))
    vg = lambda j: pl.BlockSpec((C, HB * dv), lambda b, h, c: (b * nc + c, (2 * H * dk) // (HB * dv) + j * nh + h))
    tab = pl.BlockSpec((C, half), lambda b, h, c: (c, 0))
    st = pl.BlockSpec((1, HB, dk, dv), lambda b, h, c: (b, h, 0, 0))
    return pl.pallas_call(
        functools.partial(_retention_kernel, C=C, c_eff=c_eff, dk=dk, dv=dv, HB=HB),
        grid=(B, nh, nc),
        in_specs=[qk(0), qk(1), vg(0), vg(1), tab, tab,
                  pl.BlockSpec((HB, 1, dv), lambda b, h, c: (h, 0, 0)),
                  pl.BlockSpec((1, HB * dv), lambda b, h, c: (0, h)), st],
        out_specs=[pl.BlockSpec((C, HB * dv), lambda b, h, c: (b * nc + c, h)), st],
        out_shape=[jax.ShapeDtypeStruct((B * T, H * dv), BF16), jax.ShapeDtypeStruct(s0.shape, F32)],
        scratch_shapes=[pltpu.VMEM((HB, dk, dv), F32)],
        compiler_params=_params("parallel", "parallel", "arbitrary"),
        name="retention_mix",
    )(proj, proj, proj, proj, jnp.cos(ang), jnp.sin(ang), lg, gain.reshape(1, -1), s0)


def _mem_kernel(qg_ref, mk_ref, mv_ref, o_ref, *, heads, hd):
    for h in range(heads):
        sl = slice(h * hd, (h + 1) * hd)
        q = (qg_ref[:, sl] * hd ** -0.5).astype(BF16)
        s = lax.dot_general(q, mk_ref[0, :, sl].astype(BF16), _NT, preferred_element_type=F32)
        e = jnp.exp(s - jnp.max(s, axis=-1, keepdims=True))
        p = e / jnp.sum(e, axis=-1, keepdims=True)
        o = jnp.dot(p.astype(BF16), mv_ref[0, :, sl].astype(BF16), preferred_element_type=F32)
        gate = qg_ref[:, heads * hd + h * hd:heads * hd + (h + 1) * hd]
        o_ref[:, sl] = (o * jax.nn.silu(gate)).astype(o_ref.dtype)


def mem_mix(qg, mk, mv, B, T):
    W = mk.shape[2]
    tq = min(T, 256)
    nb = T // tq
    kv = pl.BlockSpec((1, mk.shape[1], W), lambda b, i: (b, 0, 0))
    return pl.pallas_call(
        functools.partial(_mem_kernel, heads=MEM_HEADS, hd=W // MEM_HEADS),
        grid=(B, nb),
        in_specs=[pl.BlockSpec((tq, 2 * W), lambda b, i: (b * nb + i, 0)), kv, kv],
        out_specs=pl.BlockSpec((tq, W), lambda b, i: (b * nb + i, 0)),
        out_shape=jax.ShapeDtypeStruct((B * T, W), BF16),
        compiler_params=_params("parallel", "arbitrary"),
        name="mem_mix",
    )(qg, mk, mv)


def _compress_rows(x_ref, n_blocks, pe_ref, w1_ref, w2_ref, kvh):
    stride = CMP_BLOCK * kvh
    outs = []
    for h in range(kvh):
        acc = jnp.zeros((n_blocks, w1_ref.shape[2]), F32)
        for j in range(0, CMP_BLOCK, 2):
            xa = x_ref[pl.ds(j * kvh + h, n_blocks, stride=stride), :] + pe_ref[j:j + 1]
            xb = x_ref[pl.ds((j + 1) * kvh + h, n_blocks, stride=stride), :] + pe_ref[j + 1:j + 2]
            x2 = jnp.concatenate([xa, xb], axis=1).astype(BF16)
            acc = acc + jnp.dot(x2, w1_ref[j // 2], preferred_element_type=F32)
        outs.append(jnp.dot(jax.nn.silu(acc).astype(BF16), w2_ref[...], preferred_element_type=F32))
    return jnp.concatenate(outs, axis=1)


def _compress_kernel(x_ref, pe_ref, w1_ref, w2_ref, o_ref, *, kvh):
    o_ref[...] = _compress_rows(x_ref, o_ref.shape[0], pe_ref, w1_ref, w2_ref, kvh)


def _compress_paged_kernel(pt_ref, pool_ref, pe_ref, w1_ref, w2_ref, o_ref, x_s, sem, *, n_pages, rows_per_page,
                           kvh):
    b = pl.program_id(0)

    def page_copy(p):
        return pltpu.make_async_copy(pool_ref.at[pt_ref[b * n_pages + p]],
                                     x_s.at[pl.ds(p * rows_per_page, rows_per_page)], sem.at[0])

    def start(p, carry):
        page_copy(p).start()
        return carry

    def wait(p, carry):
        page_copy(p).wait()
        return carry

    lax.fori_loop(0, n_pages, start, 0)
    lax.fori_loop(0, n_pages, wait, 0)
    o_ref[0] = _compress_rows(x_s, o_ref.shape[1], pe_ref, w1_ref, w2_ref, kvh)


def _compress_weights(pe, w1, w2, hd):
    return pe.astype(F32), w1.reshape(CMP_BLOCK // 2, 2 * hd, w1.shape[1]).astype(BF16), w2.astype(BF16)


def _kv_split_kernel(*refs, n, kvh, hd):
    for x_ref, o_ref in zip(refs[:n], refs[n:]):
        for h in range(kvh):
            o_ref[pl.ds(h, x_ref.shape[0], stride=kvh), :] = x_ref[:, h * hd:(h + 1) * hd]


def kv_split(main, col, n, kvh, hd):
    M = main.shape[0]
    tm = min(M, 512)
    w = kvh * hd
    assert col % w == 0 and M % tm == 0
    return pl.pallas_call(
        functools.partial(_kv_split_kernel, n=n, kvh=kvh, hd=hd),
        grid=(M // tm,),
        in_specs=[pl.BlockSpec((tm, w), functools.partial(lambda i, j: (i, col // w + j), j=j)) for j in range(n)],
        out_specs=[pl.BlockSpec((tm * kvh, hd), lambda i: (i, 0))] * n,
        out_shape=[jax.ShapeDtypeStruct((M * kvh, hd), F32)] * n,
        compiler_params=_params("parallel"),
        name="kv_split",
    )(*([main] * n))


def compress_blocks(x, B, L, kvh, pe, w1, w2):
    hd = x.shape[1]
    n = B * (L // CMP_BLOCK)
    assert hd == LANES and L % CMP_BLOCK == 0
    peb, w1b, w2b = _compress_weights(pe, w1, w2, hd)
    full = lambda a: pl.BlockSpec(a.shape, lambda i: (0,) * a.ndim)
    out = pl.pallas_call(
        functools.partial(_compress_kernel, kvh=kvh),
        grid=(1,),
        in_specs=[full(x), full(peb), full(w1b), full(w2b)],
        out_specs=pl.BlockSpec((n, kvh * hd), lambda i: (0, 0)),
        out_shape=jax.ShapeDtypeStruct((n, kvh * hd), F32),
        compiler_params=_params("arbitrary"),
        name="compress_blocks",
    )(x, peb, w1b, w2b)
    return out.reshape(B, L // CMP_BLOCK, kvh * hd)


def compress_paged(pool, page_table, pe, w1, w2):
    n_pool, page, kvh, hd = pool.shape
    B, NP = page_table.shape
    rpp = page * kvh
    n_blocks = NP * page // CMP_BLOCK
    assert hd == LANES and page % CMP_BLOCK == 0
    peb, w1b, w2b = _compress_weights(pe, w1, w2, hd)
    full = lambda a: pl.BlockSpec(a.shape, lambda b, pt: (0,) * a.ndim)
    return pl.pallas_call(
        functools.partial(_compress_paged_kernel, n_pages=NP, rows_per_page=rpp, kvh=kvh),
        grid_spec=pltpu.PrefetchScalarGridSpec(
            num_scalar_prefetch=1,
            grid=(B,),
            in_specs=[pl.BlockSpec(memory_space=pl.ANY), full(peb), full(w1b), full(w2b)],
            out_specs=pl.BlockSpec((1, n_blocks, kvh * hd), lambda b, pt: (b, 0, 0)),
            scratch_shapes=[pltpu.VMEM((NP * rpp, hd), F32), pltpu.SemaphoreType.DMA((1,))],
        ),
        out_shape=jax.ShapeDtypeStruct((B, n_blocks, kvh * hd), F32),
        compiler_params=_params("arbitrary"),
        name="compress_paged",
    )(page_table.reshape(-1), pool.reshape(n_pool, rpp, hd), peb, w1b, w2b)


def _bias_rows(dist, thr_ref, tab):
    bias = jnp.broadcast_to(tab[:, 0:1], dist.shape)
    for k in range(1, NUM_BUCKETS):
        bias = jnp.where(dist >= thr_ref[k], tab[:, k:k + 1], bias)
    return bias


def _sample_cmp_kernel(thr_ref, q_ref, kc_ref, vc_ref, tab_ref, oc_ref, idx_ref, *, qpos, n_pick, hd):
    n_cb = kc_ref.shape[1]
    qg = (q_ref[0, 0] * hd ** -0.5).astype(BF16)
    lane = lax.broadcasted_iota(jnp.int32, (NSA_G, n_cb), 1)
    dist = qpos - ((lane + 1) * CMP_BLOCK - 1)
    valid = dist >= 0
    s = lax.dot_general(qg, kc_ref[0].astype(BF16), _NT, preferred_element_type=F32)
    s = jnp.where(valid, s + _bias_rows(dist, thr_ref, tab_ref[0]), NEG_INF)
    e = jnp.exp(s - jnp.max(s, axis=-1, keepdims=True))
    p = e / jnp.sum(e, axis=-1, keepdims=True) * valid.astype(F32)
    oc_ref[0, 0] = jnp.dot(p.astype(BF16), vc_ref[0].astype(BF16), preferred_element_type=F32)

    imp = jnp.sum(p, axis=0, keepdims=True)
    imp2 = imp + pltpu.roll(imp, n_cb - 1, axis=1)
    lane1 = lax.broadcasted_iota(jnp.int32, (1, n_cb), 1)
    blk = lane1 >> 1
    cur = qpos // SEL_BLOCK
    forced = (blk == 0) | (blk == cur) | (blk == cur - 1)
    usable = ((lane1 & 1) == 0) & (blk * SEL_BLOCK <= qpos)
    work = jnp.where(usable, jnp.where(forced, jnp.inf, imp2), -jnp.inf)
    out_lane = lax.broadcasted_iota(jnp.int32, (1, LANES), 1)
    picks = jnp.zeros((1, LANES), jnp.int32)
    for it in range(n_pick):
        mx = jnp.max(work, axis=-1, keepdims=True)
        first = jnp.min(jnp.where(work == mx, lane1, n_cb), axis=-1, keepdims=True)
        picks = jnp.where(out_lane == it, first >> 1, picks)
        work = jnp.where(lane1 == first, -jnp.inf, work)
    idx_ref[0, 0] = picks


def _sample_attn_kernel(idx_ref, pt_ref, thr_ref, q_ref, ksn_ref, vsn_ref, kwin_ref, vwin_ref, kwn_ref, vwn_ref,
                        oc_ref, gate_ref, path_ref, tab_ref, *rest, qpos, hd, n_cached, n_pick, kvh):
    ksel_refs, vsel_refs, o_ref = rest[:n_pick], rest[n_pick:2 * n_pick], rest[2 * n_pick]
    b, h = pl.program_id(0), pl.program_id(1)
    G = NSA_G
    qg = (q_ref[0, 0] * hd ** -0.5).astype(BF16)
    qf = qg.astype(F32)
    tab = tab_ref[0]
    bias_new = _bias_rows(jnp.zeros((G, 1), jnp.int32), thr_ref, tab)

    def new_token_score(k_ref):
        return jnp.sum(qf * k_ref[0, 0].astype(BF16).astype(F32), axis=-1, keepdims=True) + bias_new

    def attend(k_tiles, v_tiles, dists, k_new_ref, v_new_ref, extra_valid):
        scores, valids = [], []
        for k_t, dist in zip(k_tiles, dists):
            row = lax.broadcasted_iota(jnp.int32, dist.shape, 1)
            valid = (row % kvh == h) & (dist >= 0) & extra_valid(dist)
            s = lax.dot_general(qg, k_t.astype(BF16), _NT, preferred_element_type=F32)
            scores.append(jnp.where(valid, s + _bias_rows(dist, thr_ref, tab), NEG_INF))
            valids.append(valid)
        s_new = new_token_score(k_new_ref)
        m = s_new
        for s in scores:
            m = jnp.maximum(m, jnp.max(s, axis=-1, keepdims=True))
        p_new = jnp.exp(s_new - m)
        l = p_new
        acc = p_new.astype(BF16).astype(F32) * v_new_ref[0, 0].astype(BF16).astype(F32)
        for s, valid, v_t in zip(scores, valids, v_tiles):
            p = jnp.where(valid, jnp.exp(s - m), 0.0)
            l = l + jnp.sum(p, axis=-1, keepdims=True)
            acc = acc + jnp.dot(p.astype(BF16), v_t.astype(BF16), preferred_element_type=F32)
        return acc / l

    rows = SEL_BLOCK * kvh
    tok = lax.broadcasted_iota(jnp.int32, (G, rows), 1) // kvh
    sel_dists = [qpos - (idx_ref[(b * kvh + h) * LANES + j] * SEL_BLOCK + tok) for j in range(n_pick)]
    o_sel = attend([r[...] for r in ksel_refs], [r[...] for r in vsel_refs], sel_dists, ksn_ref, vsn_ref,
                   lambda dist: dist >= 0)
    slot = lax.broadcasted_iota(jnp.int32, (G, n_cached * kvh), 1) // kvh
    wdist = n_cached - slot
    o_win = attend([kwin_ref[0]], [vwin_ref[0]], [wdist], kwn_ref, vwn_ref,
                   lambda dist: (dist < WINDOW) & (qpos - dist >= 0))
    gates = jax.nn.sigmoid(gate_ref[0, 0])
    o = gates[:, 0:1] * oc_ref[0, 0] + gates[:, 1:2] * o_sel + gates[:, 2:3] * o_win
    o_ref[0, 0] = o * jax.nn.silu(path_ref[0, 0])


def nsa_sample(q, kcmp, vcmp, pool_sk, pool_sv, page_table, new_kv, win_k, win_v, gate, path, table, past_len):
    B, KVH, G, hd = q.shape
    n_cb = kcmp.shape[1]
    n_pool, page = pool_sk.shape[:2]
    NP = page_table.shape[1]
    n_cached = win_k.shape[1]
    qpos = past_len
    n_sel = -(-(past_len + 1) // SEL_BLOCK)
    n_pick = min(N_SELECT, n_sel) - 1
    assert past_len % SEL_BLOCK == 0 and n_cb * CMP_BLOCK == past_len and 1 <= n_pick <= LANES
    bucket = rel_bucket(jnp.arange(past_len + 1, dtype=jnp.int32))
    thr = jnp.sum(bucket[None, :] < jnp.arange(NUM_BUCKETS, dtype=jnp.int32)[:, None], axis=1).astype(jnp.int32)
    tab = table.astype(F32).T.reshape(KVH, G, NUM_BUCKETS)
    head = lambda shape: pl.BlockSpec((1, 1) + shape, lambda b, h, *_: (b, h, 0, 0))
    cmp_spec = pl.BlockSpec((1, n_cb, hd), lambda b, h, *_: (b, 0, h))
    tab_spec = pl.BlockSpec((1, G, NUM_BUCKETS), lambda b, h, *_: (h, 0, 0))
    oc, idx = pl.pallas_call(
        functools.partial(_sample_cmp_kernel, qpos=qpos, n_pick=n_pick, hd=hd),
        grid_spec=pltpu.PrefetchScalarGridSpec(
            num_scalar_prefetch=1,
            grid=(B, KVH),
            in_specs=[head((G, hd)), cmp_spec, cmp_spec, tab_spec],
            out_specs=[head((G, hd)), head((1, LANES))],
        ),
        out_shape=[jax.ShapeDtypeStruct((B, KVH, G, hd), F32), jax.ShapeDtypeStruct((B, KVH, 1, LANES), jnp.int32)],
        compiler_params=_params("parallel", "parallel"),
        name="nsa_sample_cmp",
    )(thr, q, kcmp, vcmp, tab)

    halves = page // SEL_BLOCK
    rows = SEL_BLOCK * KVH

    def sel_spec(j):
        def index(b, h, idx_ref, pt_ref, thr_ref):
            blk = idx_ref[(b * KVH + h) * LANES + j]
            return (pt_ref[b * NP + blk // halves] * halves + blk % halves, 0)
        return pl.BlockSpec((rows, hd), index)

    win_spec = pl.BlockSpec((1, n_cached * KVH, hd), lambda b, h, *_: (b, 0, 0))
    new_spec = pl.BlockSpec((1, 1, 1, hd), lambda b, h, *_: (b, h, 0, 0))
    ks_n, vs_n, kw_n, vw_n = new_kv
    sel_k = pool_sk.reshape(n_pool * page * KVH, hd)
    sel_v = pool_sv.reshape(n_pool * page * KVH, hd)
    return pl.pallas_call(
        functools.partial(_sample_attn_kernel, qpos=qpos, hd=hd, n_cached=n_cached, n_pick=n_pick, kvh=KVH),
        grid_spec=pltpu.PrefetchScalarGridSpec(
            num_scalar_prefetch=3,
            grid=(B, KVH),
            in_specs=[head((G, hd)), new_spec, new_spec, win_spec, win_spec, new_spec, new_spec,
                      head((G, hd)), head((G, 3)), head((G, hd)), tab_spec]
                     + [sel_spec(j) for j in range(n_pick)] * 2,
            out_specs=head((G, hd)),
        ),
        out_shape=jax.ShapeDtypeStruct((B, KVH, G, hd), F32),
        compiler_params=_params("parallel", "parallel"),
        name="nsa_sample_attn",
    )(idx.reshape(-1), page_table.reshape(-1), thr, q, ks_n, vs_n, win_k.reshape(B, n_cached * KVH, hd),
      win_v.reshape(B, n_cached * KVH, hd), kw_n, vw_n, oc, gate, path, tab,
      *([sel_k] * n_pick), *([sel_v] * n_pick))


def split_cols(x, sizes):
    return jnp.split(x, [int(c) for c in np.cumsum(sizes)[:-1]], axis=-1)


def head_rmsnorm(o, g):
    B, T, H, D = o.shape
    y = o * lax.rsqrt(jnp.mean(o * o, axis=-1, keepdims=True) + EPS)
    return y.reshape(B, T, H * D) * g.astype(F32)


def head_groupnorm(o, g):
    B, T, H, D = o.shape
    c = o - jnp.mean(o, axis=-1, keepdims=True)
    y = c * lax.rsqrt(jnp.mean(c * c, axis=-1, keepdims=True) + EPS)
    return y.reshape(B, T, H * D) * g.astype(F32)


def masked_softmax(s, valid):
    p = jax.nn.softmax(jnp.where(valid, s, NEG_INF), axis=-1)
    return p * valid.astype(F32)


def rel_bucket(dist):
    d = jnp.maximum(dist, 0)
    me = NUM_BUCKETS // 2
    logd = jnp.log(jnp.maximum(d, 1).astype(F32) / me)
    large = me + (logd / math.log(MAX_DISTANCE / me) * (NUM_BUCKETS - me)).astype(jnp.int32)
    return jnp.where(d < me, d, jnp.minimum(large, NUM_BUCKETS - 1))


def heads_bias(table, bucket):
    b = table.astype(F32)[bucket]
    return b.reshape(bucket.shape + (NSA_KVH, NSA_G)).transpose(2, 3, 0, 1)


def rotary(x, pos):
    half = x.shape[-1] // 2
    inv = ROPE_BASE ** (-jnp.arange(half, dtype=F32) / half)
    ang = pos.astype(F32)[:, None] * inv[None, :]
    cos = jnp.cos(ang)[None, :, None, :]
    sin = jnp.sin(ang)[None, :, None, :]
    x1, x2 = x[..., :half], x[..., half:]
    return jnp.concatenate([x1 * cos - x2 * sin, x1 * sin + x2 * cos], axis=-1)


def chunked_scan(step, S0, xs, chunk):
    B, T = xs[0].shape[:2]
    nc = T // chunk
    xs_c = tuple(jnp.moveaxis(a.reshape((B, nc, chunk) + a.shape[2:]), 1, 0) for a in xs)
    S, o = lax.scan(lambda St, c: step(St, *c), S0, xs_c)
    o = jnp.moveaxis(o, 0, 1)
    return o.reshape((B, T) + o.shape[3:]), S


def hgrn_prep(hq, hf, hv, lb, dv):
    B, T, _ = hq.shape
    f = lb[None, None, :] + (1.0 - lb[None, None, :]) * jax.nn.sigmoid(hf.astype(F32))
    q = hq.astype(F32).reshape(B, T, HG_HEADS, HG_DK) * HG_DK ** -0.5
    logf = jnp.log(f).reshape(B, T, HG_HEADS, HG_DK)
    k = (1.0 - f).reshape(B, T, HG_HEADS, HG_DK)
    v = hv.astype(F32).reshape(B, T, HG_HEADS, dv)
    return q, logf, k, v


def hgrn_chunk(S0, q, logf, k, v):
    C = q.shape[1]
    b = jnp.cumsum(logf, axis=1)
    o_inter = jnp.einsum('bthd,bhde->bthe', q * jnp.exp(b), S0)
    causal = jnp.tril(jnp.ones((C, C), dtype=bool))
    diff = b[:, :, None] - b[:, None, :]
    decay = jnp.exp(jnp.where(causal[None, :, :, None, None], diff, -jnp.inf))
    a = jnp.einsum('bthd,bshd,btshd->bhts', q, k, decay)
    o_intra = jnp.einsum('bhts,bshe->bthe', a, v)
    b_last = b[:, -1]
    S_new = jnp.exp(b_last)[..., None] * S0 + jnp.einsum('bshd,bshe->bhde', k * jnp.exp(b_last[:, None] - b), v)
    return S_new, o_inter + o_intra


def hgrn_out(o, gain, gate):
    return head_rmsnorm(o, gain) * jax.nn.silu(gate.astype(F32))


def nsa_prep(nq, kvs, hd):
    B, T, _ = nq.shape
    q = nq.reshape(B, T, NSA_KVH, NSA_G, hd) * hd ** -0.5
    kv = [a.reshape(B, T, NSA_KVH, hd) for a in split_cols(kvs, (NSA_KVH * hd,) * 6)]
    return q, kv


def compress(k, pe, w1, w2):
    B, L, KVH, HD = k.shape
    n_cb = L // CMP_BLOCK
    kb = k[:, : n_cb * CMP_BLOCK].reshape(B, n_cb, CMP_BLOCK, KVH, HD) + pe[None, None, :, None, :]
    kb = kb.transpose(0, 1, 3, 2, 4).reshape(B, n_cb, KVH, CMP_BLOCK * HD)
    return jax.nn.silu(kb @ w1) @ w2


def nsa_compressed(q, qpos, kc, vc, table):
    n_cb = kc.shape[1]
    end = (jnp.arange(n_cb, dtype=jnp.int32) + 1) * CMP_BLOCK - 1
    dist = qpos[:, None] - end[None, :]
    s = jnp.einsum('bqkgd,bnkd->bkgqn', q, kc).astype(F32) + heads_bias(table, rel_bucket(dist))
    p = masked_softmax(s, dist >= 0)
    o = jnp.einsum('bkgqn,bnkd->bqkgd', p.astype(vc.dtype), vc)
    return o, p


def select_blocks(p, qpos, L):
    B, KVH, G, Tq, n_cb = p.shape
    ratio = SEL_BLOCK // CMP_BLOCK
    n_sel = -(-L // SEL_BLOCK)
    imp = jnp.pad(p.sum(2), ((0, 0), (0, 0), (0, 0), (0, n_sel * ratio - n_cb)))
    imp = imp.reshape(B, KVH, Tq, n_sel, ratio).sum(-1)
    blk = jnp.arange(n_sel, dtype=jnp.int32)[None, :]
    cur = (qpos // SEL_BLOCK)[:, None]
    forced = (blk == 0) | (blk == cur) | (blk == cur - 1)
    valid = blk * SEL_BLOCK <= qpos[:, None]
    score = jnp.where(valid, jnp.where(forced, jnp.inf, imp), -jnp.inf)
    _, idx = lax.top_k(score, min(N_SELECT, n_sel))
    return idx


def pad_blocks(k):
    L = k.shape[1]
    n_sel = -(-L // SEL_BLOCK)
    return jnp.pad(k, ((0, 0), (0, n_sel * SEL_BLOCK - L), (0, 0), (0, 0)))


def nsa_selected(q, qpos, idx, ks, vs, table):
    B, Tq = q.shape[:2]
    n = idx.shape[-1] * SEL_BLOCK
    tok = (idx[..., None] * SEL_BLOCK + jnp.arange(SEL_BLOCK, dtype=jnp.int32)).reshape(B, NSA_KVH, Tq, n)
    bi = jnp.arange(B)[:, None, None, None]
    hi = jnp.arange(NSA_KVH)[None, :, None, None]
    kg = ks.transpose(0, 2, 1, 3)[bi, hi, tok]
    vg = vs.transpose(0, 2, 1, 3)[bi, hi, tok]
    dist = qpos[None, None, :, None] - tok
    bias = table.astype(F32).reshape(NUM_BUCKETS, NSA_KVH, NSA_G)[rel_bucket(dist), hi]
    s = jnp.einsum('bqkgd,bkqsd->bkgqs', q, kg).astype(F32) + jnp.moveaxis(bias, 4, 2)
    p = masked_softmax(s, (dist >= 0)[:, :, None])
    return jnp.einsum('bkgqs,bkqsd->bqkgd', p.astype(vg.dtype), vg)


def window_banded(q, kw, vw, table):
    B, T = q.shape[:2]
    hd = q.shape[-1]
    nb = T // WIN_QBLOCK
    span = WIN_QBLOCK + WINDOW
    pad = ((0, 0), (WINDOW, 0), (0, 0), (0, 0))
    gidx = jnp.arange(nb, dtype=jnp.int32)[:, None] * WIN_QBLOCK + jnp.arange(span, dtype=jnp.int32)[None, :]
    kb = jnp.pad(kw, pad)[:, gidx]
    vb = jnp.pad(vw, pad)[:, gidx]
    qb = q.reshape(B, nb, WIN_QBLOCK, NSA_KVH, NSA_G, hd)
    dist = jnp.arange(WIN_QBLOCK, dtype=jnp.int32)[:, None] + WINDOW - jnp.arange(span, dtype=jnp.int32)[None, :]
    kpos = gidx - WINDOW
    valid = (dist >= 0) & (dist < WINDOW) & (kpos[:, None, :] >= 0)
    s = jnp.einsum('bnqkgd,bnskd->bnkgqs', qb, kb).astype(F32) + heads_bias(table, rel_bucket(dist))
    p = masked_softmax(s, valid[None, :, None, None])
    o = jnp.einsum('bnkgqs,bnskd->bnqkgd', p.astype(vb.dtype), vb)
    return o.reshape(B, T, NSA_KVH, NSA_G, hd)


def window_dense(q, qpos, kw, vw, kpos, table):
    dist = qpos[:, None] - kpos[None, :]
    valid = (dist >= 0) & (dist < WINDOW) & (kpos[None, :] >= 0)
    s = jnp.einsum('bqkgd,bskd->bkgqs', q, kw).astype(F32) + heads_bias(table, rel_bucket(dist))
    p = masked_softmax(s, valid)
    return jnp.einsum('bkgqs,bskd->bqkgd', p.astype(vw.dtype), vw)


def nsa_combine(o_c, o_s, o_w, gate_logits, path):
    B, T = o_c.shape[:2]
    g = jax.nn.sigmoid(gate_logits.astype(F32)).reshape(B, T, NSA_KVH, NSA_G, 3)
    o = g[..., 0:1] * o_c.astype(F32) + g[..., 1:2] * o_s.astype(F32) + g[..., 2:3] * o_w.astype(F32)
    return o.reshape(B, T, -1) * jax.nn.silu(path.astype(F32))


def gather_pages(pool, page_table):
    B, NP = page_table.shape
    g = pool[page_table]
    return g.reshape((B, NP * PAGE_SIZE) + pool.shape[2:])


def retention_chunk(S0, q, k, v, log_gamma):
    C = q.shape[1]
    i = jnp.arange(C, dtype=F32)
    diff = i[:, None] - i[None, :]
    decay = jnp.where(diff[None] >= 0, jnp.exp(diff[None] * log_gamma[:, None, None]), 0.0)
    a = jnp.einsum('bthd,bshd->bhts', q, k) * decay[None]
    inner = jnp.einsum('bhts,bshe->bthe', a, v)
    q_dec = q * jnp.exp((i + 1.0)[:, None] * log_gamma[None, :])[None, :, :, None]
    cross = jnp.einsum('bthd,bhde->bthe', q_dec, S0)
    k_dec = k * jnp.exp((C - 1.0 - i)[:, None] * log_gamma[None, :])[None, :, :, None]
    S_new = jnp.exp(C * log_gamma)[None, :, None, None] * S0 + jnp.einsum('bshd,bshe->bhde', k_dec, v)
    return S_new, inner + cross


SAMPLE_ROWS = 16
MATMUL_TN = 512


class EvenCols:
    def __init__(self, hg_qk, hg_w, nsa_w, nsa_kvw):
        self.hd = nsa_w // NSA_HEADS
        self.kvw = nsa_kvw
        self.nsa_w = nsa_w
        self.q, self.f, self.v, self.g = 0, hg_qk, 2 * hg_qk, 2 * hg_qk + hg_w
        self.nq = 2 * hg_qk + 2 * hg_w
        self.kv = self.nq + nsa_w
        self.main_w = self.kv + 6 * nsa_kvw
        self.gate_w = 3 * NSA_G
        self.tail_path, self.tail_gate = 0, nsa_w
        used = nsa_w + NSA_KVH * LANES
        self.tail_w = -(-used // MATMUL_TN) * MATMUL_TN
        assert self.main_w % MATMUL_TN == 0

    def tail_weight(self, w_in_t):
        gate0 = self.main_w
        path0 = gate0 + NSA_KVH * self.gate_w
        parts = [w_in_t[path0:path0 + self.nsa_w]]
        for h in range(NSA_KVH):
            parts.append(jnp.pad(w_in_t[gate0 + h * self.gate_w:gate0 + (h + 1) * self.gate_w],
                                 ((0, LANES - self.gate_w), (0, 0))))
        w = jnp.concatenate(parts, axis=0)
        return jnp.pad(w, ((0, self.tail_w - w.shape[0]), (0, 0)))


def even_prompt(h2, B, T, w_in, w_tail, w_out, lb, hg_gain, pe, w1k, w2k, w1v, w2v, table, ec):
    hd = ec.hd
    main = matmul(h2, *w_in, n_cols=ec.main_w, transposed=True)
    tail = matmul(h2, w_tail, transposed=True)
    S0 = jnp.zeros((B, HG_HEADS, HG_DK, hg_gain.shape[0] // HG_HEADS), F32)
    y_hg, S = hgrn_mix(main, lb, hg_gain, S0, B, T, (ec.q, ec.f, ec.v, ec.g), HG_CHUNK, HGRN_SUB, HG_CHUNK)
    rows = kv_split(main, ec.kv, 6, NSA_KVH, hd)
    kcmp = compress_blocks(rows[0], B, T, NSA_KVH, pe, w1k, w2k)
    vcmp = compress_blocks(rows[1], B, T, NSA_KVH, pe, w1v, w2v)
    kc, vc, ks, vs, kw, vw = [r.reshape(B, T, NSA_KVH, hd) for r in rows]
    y_nsa = nsa_prompt(main, tail, kcmp, vcmp, table, B, T, (ec.nq, ec.kv, ec.tail_path, ec.tail_gate, hd))
    y = matmul(jnp.concatenate([y_hg, y_nsa], axis=-1), *w_out)
    wb = min(WINDOW, T)
    return y, S, kc, vc, ks, vs, kw[:, T - wb:], vw[:, T - wb:]


def even_sample(h2, B, T, pos, past_len, S0, page_table, pool_ck, pool_cv, pool_sk, pool_sv, win_k, win_v,
                w_in, w_tail, w_out, lb, hg_gain, pe, w1k, w2k, w1v, w2v, table, ec):
    hd = ec.hd
    R = SAMPLE_ROWS
    main = matmul(h2, *w_in, n_cols=ec.main_w, transposed=True)
    tail = matmul(h2, w_tail, transposed=True)
    y_hg, S = hgrn_mix(main, lb, hg_gain, S0.astype(F32), B, R, (ec.q, ec.f, ec.v, ec.g), R, R, T)
    m3 = main.reshape(B, R, -1)[:, :T]
    t3 = tail.reshape(B, R, -1)[:, :T]
    assert T == 1
    heads = lambda a, w: a.reshape(B, NSA_KVH, NSA_G, w)
    q = heads(m3[..., ec.nq:ec.kv], hd)
    path = heads(t3[..., ec.tail_path:ec.tail_path + ec.nsa_w], hd)
    gate = jnp.stack([t3[:, 0, ec.tail_gate + h * LANES:ec.tail_gate + h * LANES + ec.gate_w].reshape(B, NSA_G, 3)
                      for h in range(NSA_KVH)], axis=1)
    kvs = m3[..., ec.kv:ec.main_w].reshape(B, T, 6, NSA_KVH, hd)
    kc, vc, ks, vs, kw, vw = [kvs[:, :, j] for j in range(6)]
    new_kv = tuple(jnp.swapaxes(a, 1, 2) for a in (ks, vs, kw, vw))
    kcmp = compress_paged(pool_ck, page_table, pe, w1k, w2k)
    vcmp = compress_paged(pool_cv, page_table, pe, w1v, w2v)
    y_nsa = nsa_sample(q, kcmp, vcmp, pool_sk, pool_sv, page_table, new_kv, win_k, win_v, gate, path, table,
                       past_len)
    y_nsa = jnp.pad(y_nsa.reshape(B, T, -1).astype(BF16), ((0, 0), (0, R - T), (0, 0))).reshape(B * R, -1)
    y = matmul(jnp.concatenate([y_hg, y_nsa], axis=-1), *w_out)
    kw_all = jnp.concatenate([win_k, kw], axis=1)
    vw_all = jnp.concatenate([win_v, vw], axis=1)
    return y, S, kc, vc, ks, vs, kw_all[:, T:], vw_all[:, T:]


def retention_layer(h2, B, T, pos, S0, w_in, w_out, gain, C, c_eff):
    y, S = retention_mix(matmul(h2, *w_in), pos, gain, S0, B, T, C, c_eff)
    return matmul(y, *w_out), S


def mem_attend(h2, B, T, mk, mv, w_q, w_o):
    slots = mk.shape[1]
    o = mem_mix(matmul(h2, *w_q), mk.reshape(B, slots, -1), mv.reshape(B, slots, -1), B, T)
    return matmul(o, *w_o)


def kernel(x_prompt, x_sample, mem_prompt, state_hgrn, cache_cmp_k, cache_cmp_v, cache_sel_k, cache_sel_v,
           cache_win_k, cache_win_v, state_ret, cache_mem_k, cache_mem_v, page_table, rel_table,
           norm_mix_pre, norm_mix_post, norm_mem_pre, norm_mem_post, ev_w_in, ev_w_out, hgrn_lb, hgrn_norm,
           cmp_pe, cmp_w1_k, cmp_w2_k, cmp_w1_v, cmp_w2_v, od_w_in, od_w_out, ret_norm,
           mem_w_q, mem_w_k, mem_w_v, mem_w_o):
    B, T, D = x_prompt.shape
    Bs, Ts, _ = x_sample.shape
    depth = norm_mix_pre.shape[0]
    past_len = page_table.shape[1] * PAGE_SIZE
    pos_p = jnp.arange(T, dtype=jnp.int32)
    pos_s = past_len + jnp.arange(Ts, dtype=jnp.int32)
    lb_all = jnp.cumsum(jax.nn.softmax(hgrn_lb.astype(F32), axis=0), axis=0)
    hg_qk = hgrn_lb.shape[1]
    hg_w = hgrn_norm.shape[1]
    nsa_w = ev_w_out.shape[1] - hg_w
    nsa_kvw = NSA_KVH * (nsa_w // NSA_HEADS)
    ec = EvenCols(hg_qk, hg_w, nsa_w, nsa_kvw)
    ev_w_in_t = jnp.swapaxes(ev_w_in, 1, 2)
    R = SAMPLE_ROWS
    assert Ts <= R

    even_p, even_s, ret_p, ret_s, memk_p, memv_p = [], [], [], [], [], []
    xp = x_prompt.reshape(B * T, D)
    xs = jnp.pad(x_sample, ((0, 0), (0, R - Ts), (0, 0))).reshape(Bs * R, D)
    pos_sr = past_len + jnp.arange(R, dtype=jnp.int32)
    mem2 = mem_prompt.reshape(-1, D).astype(BF16)
    ML = mem_prompt.shape[1]
    hp = rmsnorm_bf16(xp, norm_mix_pre[0])
    hs = rmsnorm_bf16(xs, norm_mix_pre[0])
    for l in range(depth):
        if l % 2 == 0:
            e = l // 2
            w = ((ev_w_in_t, e), ec.tail_weight(ev_w_in_t[e]), (ev_w_out, e), lb_all[e], hgrn_norm[e], cmp_pe[e],
                 cmp_w1_k[e], cmp_w2_k[e], cmp_w1_v[e], cmp_w2_v[e], rel_table)
            yp, *sp = even_prompt(hp, B, T, *w, ec)
            ys, *ss = even_sample(hs, Bs, Ts, pos_s, past_len, state_hgrn[e], page_table, cache_cmp_k[e],
                                  cache_cmp_v[e], cache_sel_k[e], cache_sel_v[e], cache_win_k[e], cache_win_v[e],
                                  *w, ec)
            even_p.append(sp)
            even_s.append(ss)
        else:
            o = l // 2
            S0p = jnp.zeros((B,) + state_ret.shape[2:], F32)
            yp, sp = retention_layer(hp, B, T, pos_p, S0p, (od_w_in, o), (od_w_out, o), ret_norm[o],
                                     RET_CHUNK, RET_CHUNK)
            ys, ss = retention_layer(hs, Bs, R, pos_sr, state_ret[o].astype(F32), (od_w_in, o), (od_w_out, o),
                                     ret_norm[o], R, Ts)
            ret_p.append(sp)
            ret_s.append(ss)
        xp, hp = residual_post_pre(xp, yp, norm_mix_post[l], norm_mem_pre[l])
        xs, hs = residual_post_pre(xs, ys, norm_mix_post[l], norm_mem_pre[l])
        mk_p = matmul(mem2, mem_w_k, l).reshape(B, ML, MEM_HEADS, MEM_HD)
        mv_p = matmul(mem2, mem_w_v, l).reshape(B, ML, MEM_HEADS, MEM_HD)
        memk_p.append(mk_p)
        memv_p.append(mv_p)
        yp = mem_attend(hp, B, T, mk_p, mv_p, (mem_w_q, l), (mem_w_o, l))
        ys = mem_attend(hs, Bs, R, cache_mem_k[l], cache_mem_v[l], (mem_w_q, l), (mem_w_o, l))
        if l + 1 < depth:
            xp, hp = residual_post_pre(xp, yp, norm_mem_post[l], norm_mix_pre[l + 1])
            xs, hs = residual_post_pre(xs, ys, norm_mem_post[l], norm_mix_pre[l + 1])
        else:
            xp = residual_post(xp, yp, norm_mem_post[l])
            xs = residual_post(xs, ys, norm_mem_post[l])
    p_hgrn, p_cmp_k, p_cmp_v, p_sel_k, p_sel_v, p_win_k, p_win_v = [jnp.stack(a) for a in zip(*even_p)]
    s_hgrn, s_cmp_k, s_cmp_v, s_sel_k, s_sel_v, s_win_k, s_win_v = [jnp.stack(a) for a in zip(*even_s)]
    p_ret = jnp.stack(ret_p)
    s_ret = jnp.stack(ret_s)
    p_mem_k = jnp.stack(memk_p)
    p_mem_v = jnp.stack(memv_p)
    return (xp.reshape(B, T, D), xs.reshape(Bs, R, D)[:, :Ts], p_hgrn, p_cmp_k, p_cmp_v, p_sel_k, p_sel_v,
            p_win_k, p_win_v, p_ret, p_mem_k, p_mem_v,
            s_hgrn, s_cmp_k, s_cmp_v, s_sel_k, s_sel_v, s_win_k, s_win_v, s_ret)
```

```python
import functools
import math

import jax
import jax.numpy as jnp
import numpy as np
from jax import lax
from jax.experimental import pallas as pl
from jax.experimental.pallas import tpu as pltpu

F32 = jnp.float32
BF16 = jnp.bfloat16
EPS = 1e-6
NEG_INF = -1e30

PAGE_SIZE = 128
HG_HEADS = 16
HG_DK = 128
HG_CHUNK = 64
HGRN_SUB = 16
NSA_HEADS = 16
NSA_KVH = 2
NSA_G = NSA_HEADS // NSA_KVH
CMP_BLOCK = 32
SEL_BLOCK = 64
N_SELECT = 16
WINDOW = 512
SEL_QBLOCK = 64
WIN_QBLOCK = 128
NUM_BUCKETS = 32
MAX_DISTANCE = 1024
RET_HEADS = 16
RET_CHUNK = 128
ROPE_BASE = 10000.0
MEM_HEADS = 4
MEM_HD = 128
MEM_W = MEM_HEADS * MEM_HD

VMEM_LIMIT_BYTES = 56 * 1024 * 1024


def _params(*sem):
    return pltpu.CompilerParams(dimension_semantics=sem, vmem_limit_bytes=VMEM_LIMIT_BYTES)


def _rmsnorm_kernel(x_ref, g_ref, o_ref):
    x = x_ref[...]
    y = x * lax.rsqrt(jnp.mean(x * x, axis=-1, keepdims=True) + EPS)
    o_ref[...] = (y * g_ref[...]).astype(o_ref.dtype)


def rmsnorm_bf16(x, g):
    M, D = x.shape
    tm = min(M, 512)
    return pl.pallas_call(
        _rmsnorm_kernel,
        grid=(M // tm,),
        in_specs=[pl.BlockSpec((tm, D), lambda i: (i, 0)), pl.BlockSpec((1, D), lambda i: (0, 0))],
        out_specs=pl.BlockSpec((tm, D), lambda i: (i, 0)),
        out_shape=jax.ShapeDtypeStruct((M, D), BF16),
        compiler_params=_params("parallel"),
        name="rmsnorm_bf16",
    )(x, g.reshape(1, D))


def _post_kernel(x_ref, y_ref, g_ref, o_ref):
    y = y_ref[...]
    n = y * lax.rsqrt(jnp.mean(y * y, axis=-1, keepdims=True) + EPS)
    o_ref[...] = x_ref[...] + n * g_ref[...]


def residual_post(x, y, g):
    M, D = x.shape
    tm = min(M, 512)
    return pl.pallas_call(
        _post_kernel,
        grid=(M // tm,),
        in_specs=[pl.BlockSpec((tm, D), lambda i: (i, 0)), pl.BlockSpec((tm, D), lambda i: (i, 0)),
                  pl.BlockSpec((1, D), lambda i: (0, 0))],
        out_specs=pl.BlockSpec((tm, D), lambda i: (i, 0)),
        out_shape=jax.ShapeDtypeStruct((M, D), F32),
        compiler_params=_params("parallel"),
        name="residual_post",
    )(x, y, g.reshape(1, D))


def _post_pre_kernel(x_ref, y_ref, g_ref, gn_ref, o_ref, h_ref):
    y = y_ref[...]
    x = x_ref[...] + y * lax.rsqrt(jnp.mean(y * y, axis=-1, keepdims=True) + EPS) * g_ref[...]
    o_ref[...] = x
    h_ref[...] = (x * lax.rsqrt(jnp.mean(x * x, axis=-1, keepdims=True) + EPS) * gn_ref[...]).astype(h_ref.dtype)


def residual_post_pre(x, y, g, g_next):
    M, D = x.shape
    tm = min(M, 256)
    row = pl.BlockSpec((tm, D), lambda i: (i, 0))
    vec = pl.BlockSpec((1, D), lambda i: (0, 0))
    return pl.pallas_call(
        _post_pre_kernel,
        grid=(M // tm,),
        in_specs=[row, row, vec, vec],
        out_specs=[row, row],
        out_shape=[jax.ShapeDtypeStruct((M, D), F32), jax.ShapeDtypeStruct((M, D), BF16)],
        compiler_params=_params("parallel"),
        name="residual_post_pre",
    )(x, y, g.reshape(1, D), g_next.reshape(1, D))


def _matmul_kernel(*refs, transposed, paired):
    if paired:
        x_ref, xs_ref, w_ref, o_ref, os_ref, wb_ref = refs
    else:
        x_ref, w_ref, o_ref, wb_ref = refs

    @pl.when(pl.program_id(1) == 0)
    def _():
        w = w_ref[...]
        wb_ref[...] = (w.T if transposed else w).astype(BF16)
        if paired:
            os_ref[...] = jnp.dot(xs_ref[...], wb_ref[...], preferred_element_type=F32)

    o_ref[...] = jnp.dot(x_ref[...], wb_ref[...], preferred_element_type=F32)


def matmul(x, w, layer=None, n_cols=None, transposed=False, xs=None):
    M, K = x.shape
    N = (w.shape[-2] if transposed else w.shape[-1]) if n_cols is None else n_cols
    tn = 512 if K <= 4096 else 256
    tn = min(tn, N)
    tm = min(M, 1024 if K <= 4096 else 512)
    assert N % tn == 0 and M % tm == 0, (M, K, N)
    blk = (tn, K) if transposed else (K, tn)
    pick = (lambda n: (n, 0)) if transposed else (lambda n: (0, n))
    if w.ndim == 3:
        w_spec = pl.BlockSpec((None,) + blk, lambda n, m: (layer,) + pick(n))
    else:
        w_spec = pl.BlockSpec(blk, lambda n, m: pick(n))
    in_specs = [pl.BlockSpec((tm, K), lambda n, m: (m, 0))]
    out_specs = [pl.BlockSpec((tm, tn), lambda n, m: (m, n))]
    out_shape = [jax.ShapeDtypeStruct((M, N), F32)]
    args = [x]
    if xs is not None:
        Ms = xs.shape[0]
        in_specs.append(pl.BlockSpec((Ms, K), lambda n, m: (0, 0)))
        out_specs.append(pl.BlockSpec((Ms, tn), lambda n, m: (0, n)))
        out_shape.append(jax.ShapeDtypeStruct((Ms, N), F32))
        args.append(xs)
    out = pl.pallas_call(
        functools.partial(_matmul_kernel, transposed=transposed, paired=xs is not None),
        grid=(N // tn, M // tm),
        in_specs=in_specs + [w_spec],
        out_specs=out_specs,
        out_shape=out_shape,
        scratch_shapes=[pltpu.VMEM((K, tn), BF16)],
        compiler_params=_params("parallel", "arbitrary"),
        name="matmul",
    )(*args, w)
    return out if xs is not None else out[0]


LANES = 128
NSA_TQ = 128
_NT = (((1,), (1,)), ((), ()))


_TN = (((0,), (0,)), ((), ()))


def _bias_tile_kernel(thr_ref, table_ref, o_ref, *, row_stride, row_offset):
    kvh = pl.program_id(0)
    step = pl.program_id(1)
    row = lax.broadcasted_iota(jnp.int32, (LANES, LANES), 0)
    lane = lax.broadcasted_iota(jnp.int32, (LANES, LANES), 1)
    dist = LANES * step + lane - row_stride * row - row_offset
    for g in range(NSA_G):
        h = kvh * NSA_G + g
        bias = jnp.full((LANES, LANES), table_ref[0, h], F32)
        for k in range(1, NUM_BUCKETS):
            bias = jnp.where(dist >= thr_ref[k], table_ref[k, h], bias)
        o_ref[0, 0, :, g * LANES:(g + 1) * LANES] = bias


def bias_tiles(table, n_steps, max_dist, row_stride, row_offset):
    bucket = rel_bucket(jnp.arange(max_dist, dtype=jnp.int32))
    thr = jnp.sum(bucket[None, :] < jnp.arange(NUM_BUCKETS, dtype=jnp.int32)[:, None], axis=1).astype(jnp.int32)
    smem = pl.BlockSpec(memory_space=pltpu.SMEM)
    return pl.pallas_call(
        functools.partial(_bias_tile_kernel, row_stride=row_stride, row_offset=row_offset),
        grid=(NSA_KVH, n_steps),
        in_specs=[smem, smem],
        out_specs=pl.BlockSpec((1, 1, LANES, NSA_G * LANES), lambda h, s: (h, s, 0, 0)),
        out_shape=jax.ShapeDtypeStruct((NSA_KVH, n_steps, LANES, NSA_G * LANES), F32),
        compiler_params=_params("parallel", "parallel"),
        name="bias_tiles",
    )(thr, table.astype(F32))


def _nsa_prompt_kernel(q_ref, kc_ref, vc_ref, ks_ref, vs_ref, kw_ref, vw_ref, cb_ref, tb_ref, gate_ref, path_ref,
                       o_ref, qb_s, oc_s, m_s, l_s, acc_s, *, hd, n_win_chunks, n_pick):
    i = pl.program_id(2)
    tq = NSA_TQ
    G = NSA_G
    scale = hd ** -0.5
    row = lax.broadcasted_iota(jnp.int32, (LANES, tq), 0)
    qpos = i * tq + lax.broadcasted_iota(jnp.int32, (LANES, tq), 1)
    slab = lambda g: slice(g * tq, (g + 1) * tq)

    for g in range(G):
        qb_s[slab(g), :] = (q_ref[:, g * hd:(g + 1) * hd] * scale).astype(BF16)
    qb = qb_s[...]

    s_all = lax.dot_general(kc_ref[0].astype(BF16), qb, _NT, preferred_element_type=F32) + cb_ref[0, 0]
    valid_c = qpos >= (row + 1) * CMP_BLOCK - 1
    valid_cf = valid_c.astype(F32)
    imp = jnp.zeros((LANES, tq), F32)
    ps = []
    for g in range(G):
        s = jnp.where(valid_c, s_all[:, slab(g)], NEG_INF)
        e = jnp.exp(s - jnp.max(s, axis=0, keepdims=True))
        p = e / jnp.sum(e, axis=0, keepdims=True) * valid_cf
        imp = imp + p
        ps.append(p.astype(BF16))
    oc_s[...] = lax.dot_general(vc_ref[0].astype(BF16), jnp.concatenate(ps, axis=1), _TN,
                                preferred_element_type=F32)

    imp2 = imp + pltpu.roll(imp, LANES - 1, axis=0)
    blk = row >> 1
    cur = qpos // SEL_BLOCK
    forced = (blk == 0) | (blk == cur) | (blk == cur - 1)
    usable = ((row & 1) == 0) & (blk * SEL_BLOCK <= qpos)
    work = jnp.where(usable, jnp.where(forced, jnp.inf, imp2), -jnp.inf)
    sel = jnp.zeros((LANES, tq), F32)
    for _ in range(n_pick):
        mx = jnp.max(work, axis=0, keepdims=True)
        first = jnp.min(jnp.where(work == mx, row, LANES), axis=0, keepdims=True)
        pick = row == first
        sel = jnp.where(pick, 1.0, sel)
        work = jnp.where(pick, -jnp.inf, work)
    sel_b = sel.astype(BF16)

    m_s[...] = jnp.full(m_s.shape, NEG_INF, F32)
    l_s[...] = jnp.zeros(l_s.shape, F32)
    acc_s[...] = jnp.zeros(acc_s.shape, F32)
    ekey = lax.broadcasted_iota(jnp.int32, (LANES, LANES), 0)
    eblk = lax.broadcasted_iota(jnp.int32, (LANES, LANES), 1)

    def online(slot, k_b, v_b, bias, valid):
        s_all = lax.dot_general(k_b, qb, _NT, preferred_element_type=F32) + bias
        ps, alphas = [], []
        for g in range(G):
            s = jnp.where(valid, s_all[:, slab(g)], NEG_INF)
            m_old = m_s[slot, :, slab(g)]
            m_new = jnp.maximum(m_old, jnp.max(s, axis=0, keepdims=True))
            alpha = jnp.exp(m_old - m_new)
            p = jnp.where(valid, jnp.exp(s - m_new), 0.0)
            l_s[slot, :, slab(g)] = alpha * l_s[slot, :, slab(g)] + jnp.sum(p, axis=0, keepdims=True)
            m_s[slot, :, slab(g)] = m_new
            ps.append(p.astype(BF16))
            alphas.append(alpha)
        pv = lax.dot_general(v_b, jnp.concatenate(ps, axis=1), _TN, preferred_element_type=F32)
        acc_s[slot] = jnp.concatenate(alphas, axis=1) * acc_s[slot] + pv

    def chunk(c, carry):
        k0 = pl.multiple_of(c * LANES, LANES)
        delta = i - c
        dist = qpos - (k0 + row)
        causal = dist >= 0
        expand = (eblk == 2 * (c * (LANES // SEL_BLOCK) + (ekey // SEL_BLOCK))).astype(BF16)
        chosen = jnp.dot(expand, sel_b, preferred_element_type=F32) > 0.5
        bias = tb_ref[0, delta]
        online(0, ks_ref[pl.ds(k0, LANES), :].astype(BF16), vs_ref[pl.ds(k0, LANES), :].astype(BF16), bias,
               chosen & causal)

        @pl.when(delta < n_win_chunks)
        def _():
            online(1, kw_ref[pl.ds(k0, LANES), :].astype(BF16), vw_ref[pl.ds(k0, LANES), :].astype(BF16), bias,
                   causal & (dist < WINDOW))

        return carry

    lax.fori_loop(0, i + 1, chunk, 0)

    gates = jax.nn.sigmoid(gate_ref[...]).T
    for g in range(G):
        o_sel = acc_s[0, :, slab(g)] / l_s[0, :, slab(g)]
        o_win = acc_s[1, :, slab(g)] / l_s[1, :, slab(g)]
        o = (gates[3 * g:3 * g + 1] * oc_s[:, slab(g)] + gates[3 * g + 1:3 * g + 2] * o_sel
             + gates[3 * g + 2:3 * g + 3] * o_win)
        o_ref[:, g * hd:(g + 1) * hd] = (o.T * jax.nn.silu(path_ref[:, g * hd:(g + 1) * hd])).astype(o_ref.dtype)


def nsa_prompt(main, tail, kcmp, vcmp, table, B, T, cols):
    q_col, kv_col, path_col, gate_col, hd = cols
    G = NSA_G
    tq = NSA_TQ
    n_cb = kcmp.shape[1]
    assert n_cb <= LANES and T % tq == 0 and hd == LANES
    nd = T // LANES
    n_win_chunks = WINDOW // LANES + 1
    tb = bias_tiles(table, nd, T, 1, 0)
    cb = bias_tiles(table, T // tq, T, CMP_BLOCK, CMP_BLOCK - 1)
    pad = ((0, 0), (0, LANES - n_cb), (0, 0))
    kcmp = jnp.pad(kcmp, pad)
    vcmp = jnp.pad(vcmp, pad)
    nblk = T // tq
    gw = G * hd
    row_map = lambda b, h, i: (b * nblk + i, 0)
    kv_spec = lambda j: pl.BlockSpec((T, hd), lambda b, h, i: (b, kv_col // hd + 2 * j + h))
    return pl.pallas_call(
        functools.partial(_nsa_prompt_kernel, hd=hd, n_win_chunks=n_win_chunks,
                          n_pick=min(N_SELECT, -(-T // SEL_BLOCK))),
        grid=(B, NSA_KVH, nblk),
        in_specs=[
            pl.BlockSpec((tq, gw), lambda b, h, i: (b * nblk + i, q_col // gw + h)),
            pl.BlockSpec((1, LANES, hd), lambda b, h, i: (b, 0, h)),
            pl.BlockSpec((1, LANES, hd), lambda b, h, i: (b, 0, h)),
            kv_spec(2), kv_spec(3), kv_spec(4), kv_spec(5),
            pl.BlockSpec((1, 1, LANES, gw), lambda b, h, i: (h, i, 0, 0)),
            pl.BlockSpec((1, nd, LANES, gw), lambda b, h, i: (h, 0, 0, 0)),
            pl.BlockSpec((tq, LANES), lambda b, h, i: (b * nblk + i, gate_col // LANES + h)),
            pl.BlockSpec((tq, gw), lambda b, h, i: (b * nblk + i, path_col // gw + h)),
        ],
        out_specs=pl.BlockSpec((tq, gw), lambda b, h, i: (b * nblk + i, h)),
        out_shape=jax.ShapeDtypeStruct((B * T, NSA_KVH * gw), BF16),
        scratch_shapes=[pltpu.VMEM((G * tq, hd), BF16), pltpu.VMEM((hd, G * tq), F32),
                        pltpu.VMEM((2, 1, G * tq), F32), pltpu.VMEM((2, 1, G * tq), F32),
                        pltpu.VMEM((2, hd, G * tq), F32)],
        compiler_params=_params("parallel", "parallel", "arbitrary"),
        name="nsa_prompt",
    )(main, kcmp, vcmp, main, main, main, main, cb, tb, tail, tail)


_TN = (((0,), (0,)), ((), ()))


def _cumsum_rows(x, n):
    row = lax.broadcasted_iota(jnp.int32, x.shape, 0)
    sh = 1
    while sh < n:
        x = x + jnp.where(row >= sh, pltpu.roll(x, sh, axis=0), 0.0)
        sh *= 2
    return x


def _hgrn_kernel(q_ref, f_ref, v_ref, gate_ref, lb_ref, gain_ref, s0_ref, y_ref, s_out_ref, st_s,
                 *, C, SB, c_eff, dk, HB):
    c = pl.program_id(2)

    @pl.when(c == 0)
    def _():
        for hh in range(HB):
            st_s[hh] = s0_ref[0, hh].T

    for hh in range(HB):
        _hgrn_head(hh, slice(hh * dk, (hh + 1) * dk), q_ref, f_ref, v_ref, gate_ref, lb_ref, gain_ref,
                   y_ref, st_s, C=C, SB=SB, c_eff=c_eff, dk=dk)

    @pl.when(c == pl.num_programs(2) - 1)
    def _():
        for hh in range(HB):
            s_out_ref[0, hh] = st_s[hh].T


def _hgrn_head(hh, sl, q_ref, f_ref, v_ref, gate_ref, lb_ref, gain_ref, y_ref, st_s, *, C, SB, c_eff, dk):
    lb = lb_ref[:, sl]
    f = lb + (1.0 - lb) * jax.nn.sigmoid(f_ref[:, sl])
    logf = jnp.log(f)
    k = 1.0 - f
    if c_eff < C:
        real = lax.broadcasted_iota(jnp.int32, (C, dk), 0) < c_eff
        logf = jnp.where(real, logf, 0.0)
        k = jnp.where(real, k, 0.0)
    q = q_ref[:, sl] * dk ** -0.5
    v_b = v_ref[:, sl].astype(BF16)
    b = _cumsum_rows(logf, C)
    st = st_s[hh]
    o = lax.dot_general((q * jnp.exp(b)).astype(BF16), st.astype(BF16), _NT, preferred_element_type=F32)

    lane_c = lax.broadcasted_iota(jnp.int32, (SB, C), 1)
    row_c = lax.broadcasted_iota(jnp.int32, (SB, C), 0)
    outs = []
    for I in range(C // SB):
        r0 = I * SB
        q_i = q[r0:r0 + SB]
        b_i = b[r0:r0 + SB]
        if I > 0:
            b_r = b[r0 - 1:r0]
            qq = (q_i * jnp.exp(b_i - b_r)).astype(BF16)
            kk = (k * jnp.exp(jnp.minimum(b_r - b, 0.0))).astype(BF16)
            a = lax.dot_general(qq, kk, _NT, preferred_element_type=F32)
            a = jnp.where(lane_c < r0, a, 0.0)
        else:
            a = jnp.zeros((SB, C), F32)
        for s in range(SB):
            z = q_i * k[r0 + s:r0 + s + 1] * jnp.exp(jnp.minimum(b_i - b[r0 + s:r0 + s + 1], 0.0))
            col = jnp.sum(z, axis=-1, keepdims=True)
            a = jnp.where((lane_c == r0 + s) & (row_c >= s), col, a)
        outs.append(jnp.dot(a.astype(BF16), v_b, preferred_element_type=F32))
    o = o + jnp.concatenate(outs, axis=0)

    b_last = b[C - 1:C]
    kd = (k * jnp.exp(b_last - b)).astype(BF16)
    st_new = jnp.exp(b_last) * st + lax.dot_general(v_b, kd, _TN, preferred_element_type=F32)
    st_s[hh] = st_new

    y = o * lax.rsqrt(jnp.mean(o * o, axis=-1, keepdims=True) + EPS) * gain_ref[:, sl]
    y_ref[:, sl] = (y * jax.nn.silu(gate_ref[:, sl])).astype(y_ref.dtype)


HGRN_HEADS_PER_STEP = 8


def hgrn_mix(main, lb, gain, s0, B, T, cols, C, SB, c_eff):
    q_col, f_col, v_col, g_col = cols
    _, H, dk, dv = s0.shape
    HB = HGRN_HEADS_PER_STEP
    W = HB * LANES
    assert dk == LANES and dv == LANES and T % C == 0 and C % SB == 0 and H % HB == 0
    assert all(off % W == 0 for off in cols)
    nc = T // C
    col = lambda off: pl.BlockSpec((C, W), lambda b, h, c: (b * nc + c, off // W + h))
    vec = pl.BlockSpec((1, W), lambda b, h, c: (0, h))
    st = pl.BlockSpec((1, HB, dk, dv), lambda b, h, c: (b, h, 0, 0))
    return pl.pallas_call(
        functools.partial(_hgrn_kernel, C=C, SB=SB, c_eff=c_eff, dk=dk, HB=HB),
        grid=(B, H // HB, nc),
        in_specs=[col(q_col), col(f_col), col(v_col), col(g_col), vec, vec, st],
        out_specs=[pl.BlockSpec((C, W), lambda b, h, c: (b * nc + c, h)), st],
        out_shape=[jax.ShapeDtypeStruct((B * T, H * dv), BF16), jax.ShapeDtypeStruct(s0.shape, F32)],
        scratch_shapes=[pltpu.VMEM((HB, dv, dk), F32)],
        compiler_params=_params("parallel", "parallel", "arbitrary"),
        name="hgrn_mix",
    )(main, main, main, main, lb.reshape(1, -1), gain.reshape(1, -1), s0)


def _retention_kernel(q_ref, k_ref, v_ref, g_ref, cos_ref, sin_ref, lg_ref, gain_ref, s0_ref, y_ref, s_out_ref, s_s,
                      *, C, c_eff, dk, dv, HB):
    c = pl.program_id(2)

    @pl.when(c == 0)
    def _():
        s_s[...] = s0_ref[0]

    for hh in range(HB):
        _retention_head(hh, q_ref, k_ref, v_ref, g_ref, cos_ref, sin_ref, lg_ref, gain_ref, y_ref, s_s,
                        C=C, c_eff=c_eff, dk=dk, dv=dv)

    @pl.when(c == pl.num_programs(2) - 1)
    def _():
        s_out_ref[0] = s_s[...]


def _retention_head(hh, q_ref, k_ref, v_ref, g_ref, cos_ref, sin_ref, lg_ref, gain_ref, y_ref, s_s,
                    *, C, c_eff, dk, dv):
    half = dk // 2
    vsl = slice(hh * dv, (hh + 1) * dv)
    cos = cos_ref[...]
    sin = sin_ref[...]
    lg_w = lg_ref[hh]
    lg = lg_w[:, :LANES]
    row = lax.broadcasted_iota(jnp.int32, (C, LANES), 0).astype(F32)

    def rot(ref, w):
        x1 = ref[:, hh * dk:hh * dk + half]
        x2 = ref[:, hh * dk + half:(hh + 1) * dk]
        return jnp.concatenate([(x1 * cos - x2 * sin) * w, (x1 * sin + x2 * cos) * w], axis=1)

    q = rot(q_ref, dk ** -0.5)
    k = rot(k_ref, 1.0)
    v_b = v_ref[:, vsl].astype(BF16)
    s_old = s_s[hh]

    a = lax.dot_general(q.astype(BF16), k.astype(BF16), _NT, preferred_element_type=F32)
    ti = lax.broadcasted_iota(jnp.int32, (C, C), 0)
    si = lax.broadcasted_iota(jnp.int32, (C, C), 1)
    diff = (ti - si).astype(F32)
    a = a * jnp.where(diff >= 0, jnp.exp(diff * lg_w[:, :C]), 0.0)
    inner = jnp.dot(a.astype(BF16), v_b, preferred_element_type=F32)
    q_w = jnp.exp((row + 1.0) * lg)
    q_dec = q * jnp.concatenate([q_w] * (dk // LANES), axis=1)
    cross = jnp.dot(q_dec.astype(BF16), s_old.astype(BF16), preferred_element_type=F32)
    k_w = jnp.where(row < c_eff, jnp.exp((c_eff - 1.0 - row) * lg), 0.0)
    k_dec = k * jnp.concatenate([k_w] * (dk // LANES), axis=1)
    s_new = jnp.exp(c_eff * lg_w) * s_old + lax.dot_general(k_dec.astype(BF16), v_b, _TN, preferred_element_type=F32)
    s_s[hh] = s_new

    o = inner + cross
    cen = o - jnp.mean(o, axis=-1, keepdims=True)
    y = cen * lax.rsqrt(jnp.mean(cen * cen, axis=-1, keepdims=True) + EPS) * gain_ref[:, vsl]
    y_ref[:, vsl] = (y * jax.nn.silu(g_ref[:, vsl])).astype(y_ref.dtype)


RETENTION_HEADS_PER_STEP = 4


def retention_mix(proj, pos, gain, s0, B, T, C, c_eff):
    _, H, dk, dv = s0.shape
    HB = RETENTION_HEADS_PER_STEP
    assert T % C == 0 and C <= LANES and dk % LANES == 0 and H % HB == 0
    nc = T // C
    nh = H // HB
    half = dk // 2
    inv = ROPE_BASE ** (-jnp.arange(half, dtype=F32) / half)
    ang = pos.astype(F32)[:, None] * inv[None, :]
    log_gamma = jnp.log1p(-jnp.exp2(-5.0 - jnp.arange(H, dtype=F32)))
    lg = jnp.broadcast_to(log_gamma[:, None, None], (H, 1, dv))
    qk = lambda j: pl.BlockSpec((C, HB * dk), lambda b, h, c: (b * nc + c, j * nh + h))
    vg = lambda j: pl.BlockSpec((C, HB * dv), lambda b, h, c: (b * nc + c, (2 * H * dk) // (HB * dv) + j * nh + h))
    tab = pl.BlockSpec((C, half), lambda b, h, c: (c, 0))
    st = pl.BlockSpec((1, HB, dk, dv), lambda b, h, c: (b, h, 0, 0))
    return pl.pallas_call(
        functools.partial(_retention_kernel, C=C, c_eff=c_eff, dk=dk, dv=dv, HB=HB),
        grid=(B, nh, nc),
        in_specs=[qk(0), qk(1), vg(0), vg(1), tab, tab,
                  pl.BlockSpec((HB, 1, dv), lambda b, h, c: (h, 0, 0)),
                  pl.BlockSpec((1, HB * dv), lambda b, h, c: (0, h)), st],
        out_specs=[pl.BlockSpec((C, HB * dv), lambda b, h, c: (b * nc + c, h)), st],
        out_shape=[jax.ShapeDtypeStruct((B * T, H * dv), BF16), jax.ShapeDtypeStruct(s0.shape, F32)],
        scratch_shapes=[pltpu.VMEM((HB, dk, dv), F32)],
        compiler_params=_params("parallel", "parallel", "arbitrary"),
        name="retention_mix",
    )(proj, proj, proj, proj, jnp.cos(ang), jnp.sin(ang), lg, gain.reshape(1, -1), s0)


def _mem_kernel(qg_ref, mk_ref, mv_ref, o_ref, *, heads, hd):
    for h in range(heads):
        sl = slice(h * hd, (h + 1) * hd)
        q = (qg_ref[:, sl] * hd ** -0.5).astype(BF16)
        s = lax.dot_general(q, mk_ref[0, :, sl].astype(BF16), _NT, preferred_element_type=F32)
        e = jnp.exp(s - jnp.max(s, axis=-1, keepdims=True))
        p = e / jnp.sum(e, axis=-1, keepdims=True)
        o = jnp.dot(p.astype(BF16), mv_ref[0, :, sl].astype(BF16), preferred_element_type=F32)
        gate = qg_ref[:, heads * hd + h * hd:heads * hd + (h + 1) * hd]
        o_ref[:, sl] = (o * jax.nn.silu(gate)).astype(o_ref.dtype)


def mem_mix(qg, mk, mv, B, T):
    W = mk.shape[2]
    tq = min(T, 256)
    nb = T // tq
    kv = pl.BlockSpec((1, mk.shape[1], W), lambda b, i: (b, 0, 0))
    return pl.pallas_call(
        functools.partial(_mem_kernel, heads=MEM_HEADS, hd=W // MEM_HEADS),
        grid=(B, nb),
        in_specs=[pl.BlockSpec((tq, 2 * W), lambda b, i: (b * nb + i, 0)), kv, kv],
        out_specs=pl.BlockSpec((tq, W), lambda b, i: (b * nb + i, 0)),
        out_shape=jax.ShapeDtypeStruct((B * T, W), BF16),
        compiler_params=_params("parallel", "arbitrary"),
        name="mem_mix",
    )(qg, mk, mv)


def _compress_rows(x_ref, n_blocks, pe_ref, w1_ref, w2_ref, kvh):
    stride = CMP_BLOCK * kvh
    outs = []
    for h in range(kvh):
        acc = jnp.zeros((n_blocks, w1_ref.shape[2]), F32)
        for j in range(0, CMP_BLOCK, 2):
            xa = x_ref[pl.ds(j * kvh + h, n_blocks, stride=stride), :] + pe_ref[j:j + 1]
            xb = x_ref[pl.ds((j + 1) * kvh + h, n_blocks, stride=stride), :] + pe_ref[j + 1:j + 2]
            x2 = jnp.concatenate([xa, xb], axis=1).astype(BF16)
            acc = acc + jnp.dot(x2, w1_ref[j // 2], preferred_element_type=F32)
        outs.append(jnp.dot(jax.nn.silu(acc).astype(BF16), w2_ref[...], preferred_element_type=F32))
    return jnp.concatenate(outs, axis=1)


def _compress_kernel(x_ref, pe_ref, w1_ref, w2_ref, o_ref, *, kvh):
    o_ref[...] = _compress_rows(x_ref, o_ref.shape[0], pe_ref, w1_ref, w2_ref, kvh)


def _compress_paged_kernel(pt_ref, pool_ref, pe_ref, w1_ref, w2_ref, o_ref, x_s, sem, *, n_pages, rows_per_page,
                           kvh):
    b = pl.program_id(0)

    def page_copy(p):
        return pltpu.make_async_copy(pool_ref.at[pt_ref[b * n_pages + p]],
                                     x_s.at[pl.ds(p * rows_per_page, rows_per_page)], sem.at[0])

    def start(p, carry):
        page_copy(p).start()
        return carry

    def wait(p, carry):
        page_copy(p).wait()
        return carry

    lax.fori_loop(0, n_pages, start, 0)
    lax.fori_loop(0, n_pages, wait, 0)
    o_ref[0] = _compress_rows(x_s, o_ref.shape[1], pe_ref, w1_ref, w2_ref, kvh)


def _compress_weights(pe, w1, w2, hd):
    return pe.astype(F32), w1.reshape(CMP_BLOCK // 2, 2 * hd, w1.shape[1]).astype(BF16), w2.astype(BF16)


def _kv_split_kernel(*refs, n, kvh, hd):
    for x_ref, o_ref in zip(refs[:n], refs[n:]):
        for h in range(kvh):
            o_ref[pl.ds(h, x_ref.shape[0], stride=kvh), :] = x_ref[:, h * hd:(h + 1) * hd]


def kv_split(main, col, n, kvh, hd):
    M = main.shape[0]
    tm = min(M, 512)
    w = kvh * hd
    assert col % w == 0 and M % tm == 0
    return pl.pallas_call(
        functools.partial(_kv_split_kernel, n=n, kvh=kvh, hd=hd),
        grid=(M // tm,),
        in_specs=[pl.BlockSpec((tm, w), functools.partial(lambda i, j: (i, col // w + j), j=j)) for j in range(n)],
        out_specs=[pl.BlockSpec((tm * kvh, hd), lambda i: (i, 0))] * n,
        out_shape=[jax.ShapeDtypeStruct((M * kvh, hd), F32)] * n,
        compiler_params=_params("parallel"),
        name="kv_split",
    )(*([main] * n))


def compress_blocks(x, B, L, kvh, pe, w1, w2):
    hd = x.shape[1]
    n = B * (L // CMP_BLOCK)
    assert hd == LANES and L % CMP_BLOCK == 0
    peb, w1b, w2b = _compress_weights(pe, w1, w2, hd)
    full = lambda a: pl.BlockSpec(a.shape, lambda i: (0,) * a.ndim)
    out = pl.pallas_call(
        functools.partial(_compress_kernel, kvh=kvh),
        grid=(1,),
        in_specs=[full(x), full(peb), full(w1b), full(w2b)],
        out_specs=pl.BlockSpec((n, kvh * hd), lambda i: (0, 0)),
        out_shape=jax.ShapeDtypeStruct((n, kvh * hd), F32),
        compiler_params=_params("arbitrary"),
        name="compress_blocks",
    )(x, peb, w1b, w2b)
    return out.reshape(B, L // CMP_BLOCK, kvh * hd)


def compress_paged(pool, page_table, pe, w1, w2):
    n_pool, page, kvh, hd = pool.shape
    B, NP = page_table.shape
    rpp = page * kvh
    n_blocks = NP * page // CMP_BLOCK
    assert hd == LANES and page % CMP_BLOCK == 0
    peb, w1b, w2b = _compress_weights(pe, w1, w2, hd)
    full = lambda a: pl.BlockSpec(a.shape, lambda b, pt: (0,) * a.ndim)
    return pl.pallas_call(
        functools.partial(_compress_paged_kernel, n_pages=NP, rows_per_page=rpp, kvh=kvh),
        grid_spec=pltpu.PrefetchScalarGridSpec(
            num_scalar_prefetch=1,
            grid=(B,),
            in_specs=[pl.BlockSpec(memory_space=pl.ANY), full(peb), full(w1b), full(w2b)],
            out_specs=pl.BlockSpec((1, n_blocks, kvh * hd), lambda b, pt: (b, 0, 0)),
            scratch_shapes=[pltpu.VMEM((NP * rpp, hd), F32), pltpu.SemaphoreType.DMA((1,))],
        ),
        out_shape=jax.ShapeDtypeStruct((B, n_blocks, kvh * hd), F32),
        compiler_params=_params("arbitrary"),
        name="compress_paged",
    )(page_table.reshape(-1), pool.reshape(n_pool, rpp, hd), peb, w1b, w2b)


def _bias_rows(dist, thr_ref, tab):
    bias = jnp.broadcast_to(tab[:, 0:1], dist.shape)
    for k in range(1, NUM_BUCKETS):
        bias = jnp.where(dist >= thr_ref[k], tab[:, k:k + 1], bias)
    return bias


def _sample_cmp_kernel(thr_ref, q_ref, kc_ref, vc_ref, tab_ref, oc_ref, idx_ref, *, qpos, n_pick, hd):
    n_cb = kc_ref.shape[1]
    qg = (q_ref[0, 0] * hd ** -0.5).astype(BF16)
    lane = lax.broadcasted_iota(jnp.int32, (NSA_G, n_cb), 1)
    dist = qpos - ((lane + 1) * CMP_BLOCK - 1)
    valid = dist >= 0
    s = lax.dot_general(qg, kc_ref[0].astype(BF16), _NT, preferred_element_type=F32)
    s = jnp.where(valid, s + _bias_rows(dist, thr_ref, tab_ref[0]), NEG_INF)
    e = jnp.exp(s - jnp.max(s, axis=-1, keepdims=True))
    p = e / jnp.sum(e, axis=-1, keepdims=True) * valid.astype(F32)
    oc_ref[0, 0] = jnp.dot(p.astype(BF16), vc_ref[0].astype(BF16), preferred_element_type=F32)

    imp = jnp.sum(p, axis=0, keepdims=True)
    imp2 = imp + pltpu.roll(imp, n_cb - 1, axis=1)
    lane1 = lax.broadcasted_iota(jnp.int32, (1, n_cb), 1)
    blk = lane1 >> 1
    cur = qpos // SEL_BLOCK
    forced = (blk == 0) | (blk == cur) | (blk == cur - 1)
    usable = ((lane1 & 1) == 0) & (blk * SEL_BLOCK <= qpos)
    work = jnp.where(usable, jnp.where(forced, jnp.inf, imp2), -jnp.inf)
    out_lane = lax.broadcasted_iota(jnp.int32, (1, LANES), 1)
    picks = jnp.zeros((1, LANES), jnp.int32)
    for it in range(n_pick):
        mx = jnp.max(work, axis=-1, keepdims=True)
        first = jnp.min(jnp.where(work == mx, lane1, n_cb), axis=-1, keepdims=True)
        picks = jnp.where(out_lane == it, first >> 1, picks)
        work = jnp.where(lane1 == first, -jnp.inf, work)
    idx_ref[0, 0] = picks


def _sample_attn_kernel(idx_ref, pt_ref, thr_ref, q_ref, ksn_ref, vsn_ref, kwin_ref, vwin_ref, kwn_ref, vwn_ref,
                        oc_ref, gate_ref, path_ref, tab_ref, *rest, qpos, hd, n_cached, n_pick, kvh):
    ksel_refs, vsel_refs, o_ref = rest[:n_pick], rest[n_pick:2 * n_pick], rest[2 * n_pick]
    b, h = pl.program_id(0), pl.program_id(1)
    G = NSA_G
    qg = (q_ref[0, 0] * hd ** -0.5).astype(BF16)
    qf = qg.astype(F32)
    tab = tab_ref[0]
    bias_new = _bias_rows(jnp.zeros((G, 1), jnp.int32), thr_ref, tab)

    def new_token_score(k_ref):
        return jnp.sum(qf * k_ref[0, 0].astype(BF16).astype(F32), axis=-1, keepdims=True) + bias_new

    def attend(k_tiles, v_tiles, dists, k_new_ref, v_new_ref, extra_valid):
        scores, valids = [], []
        for k_t, dist in zip(k_tiles, dists):
            row = lax.broadcasted_iota(jnp.int32, dist.shape, 1)
            valid = (row % kvh == h) & (dist >= 0) & extra_valid(dist)
            s = lax.dot_general(qg, k_t.astype(BF16), _NT, preferred_element_type=F32)
            scores.append(jnp.where(valid, s + _bias_rows(dist, thr_ref, tab), NEG_INF))
            valids.append(valid)
        s_new = new_token_score(k_new_ref)
        m = s_new
        for s in scores:
            m = jnp.maximum(m, jnp.max(s, axis=-1, keepdims=True))
        p_new = jnp.exp(s_new - m)
        l = p_new
        acc = p_new.astype(BF16).astype(F32) * v_new_ref[0, 0].astype(BF16).astype(F32)
        for s, valid, v_t in zip(scores, valids, v_tiles):
            p = jnp.where(valid, jnp.exp(s - m), 0.0)
            l = l + jnp.sum(p, axis=-1, keepdims=True)
            acc = acc + jnp.dot(p.astype(BF16), v_t.astype(BF16), preferred_element_type=F32)
        return acc / l

    rows = SEL_BLOCK * kvh
    tok = lax.broadcasted_iota(jnp.int32, (G, rows), 1) // kvh
    sel_dists = [qpos - (idx_ref[(b * kvh + h) * LANES + j] * SEL_BLOCK + tok) for j in range(n_pick)]
    o_sel = attend([r[...] for r in ksel_refs], [r[...] for r in vsel_refs], sel_dists, ksn_ref, vsn_ref,
                   lambda dist: dist >= 0)
    slot = lax.broadcasted_iota(jnp.int32, (G, n_cached * kvh), 1) // kvh
    wdist = n_cached - slot
    o_win = attend([kwin_ref[0]], [vwin_ref[0]], [wdist], kwn_ref, vwn_ref,
                   lambda dist: (dist < WINDOW) & (qpos - dist >= 0))
    gates = jax.nn.sigmoid(gate_ref[0, 0])
    o = gates[:, 0:1] * oc_ref[0, 0] + gates[:, 1:2] * o_sel + gates[:, 2:3] * o_win
    o_ref[0, 0] = o * jax.nn.silu(path_ref[0, 0])


def nsa_sample(q, kcmp, vcmp, pool_sk, pool_sv, page_table, new_kv, win_k, win_v, gate, path, table, past_len):
    B, KVH, G, hd = q.shape
    n_cb = kcmp.shape[1]
    n_pool, page = pool_sk.shape[:2]
    NP = page_table.shape[1]
    n_cached = win_k.shape[1]
    qpos = past_len
    n_sel = -(-(past_len + 1) // SEL_BLOCK)
    n_pick = min(N_SELECT, n_sel) - 1
    assert past_len % SEL_BLOCK == 0 and n_cb * CMP_BLOCK == past_len and 1 <= n_pick <= LANES
    bucket = rel_bucket(jnp.arange(past_len + 1, dtype=jnp.int32))
    thr = jnp.sum(bucket[None, :] < jnp.arange(NUM_BUCKETS, dtype=jnp.int32)[:, None], axis=1).astype(jnp.int32)
    tab = table.astype(F32).T.reshape(KVH, G, NUM_BUCKETS)
    head = lambda shape: pl.BlockSpec((1, 1) + shape, lambda b, h, *_: (b, h, 0, 0))
    cmp_spec = pl.BlockSpec((1, n_cb, hd), lambda b, h, *_: (b, 0, h))
    tab_spec = pl.BlockSpec((1, G, NUM_BUCKETS), lambda b, h, *_: (h, 0, 0))
    oc, idx = pl.pallas_call(
        functools.partial(_sample_cmp_kernel, qpos=qpos, n_pick=n_pick, hd=hd),
        grid_spec=pltpu.PrefetchScalarGridSpec(
            num_scalar_prefetch=1,
            grid=(B, KVH),
            in_specs=[head((G, hd)), cmp_spec, cmp_spec, tab_spec],
            out_specs=[head((G, hd)), head((1, LANES))],
        ),
        out_shape=[jax.ShapeDtypeStruct((B, KVH, G, hd), F32), jax.ShapeDtypeStruct((B, KVH, 1, LANES), jnp.int32)],
        compiler_params=_params("parallel", "parallel"),
        name="nsa_sample_cmp",
    )(thr, q, kcmp, vcmp, tab)

    halves = page // SEL_BLOCK
    rows = SEL_BLOCK * KVH

    def sel_spec(j):
        def index(b, h, idx_ref, pt_ref, thr_ref):
            blk = idx_ref[(b * KVH + h) * LANES + j]
            return (pt_ref[b * NP + blk // halves] * halves + blk % halves, 0)
        return pl.BlockSpec((rows, hd), index)

    win_spec = pl.BlockSpec((1, n_cached * KVH, hd), lambda b, h, *_: (b, 0, 0))
    new_spec = pl.BlockSpec((1, 1, 1, hd), lambda b, h, *_: (b, h, 0, 0))
    ks_n, vs_n, kw_n, vw_n = new_kv
    sel_k = pool_sk.reshape(n_pool * page * KVH, hd)
    sel_v = pool_sv.reshape(n_pool * page * KVH, hd)
    return pl.pallas_call(
        functools.partial(_sample_attn_kernel, qpos=qpos, hd=hd, n_cached=n_cached, n_pick=n_pick, kvh=KVH),
        grid_spec=pltpu.PrefetchScalarGridSpec(
            num_scalar_prefetch=3,
            grid=(B, KVH),
            in_specs=[head((G, hd)), new_spec, new_spec, win_spec, win_spec, new_spec, new_spec,
                      head((G, hd)), head((G, 3)), head((G, hd)), tab_spec]
                     + [sel_spec(j) for j in range(n_pick)] * 2,
            out_specs=head((G, hd)),
        ),
        out_shape=jax.ShapeDtypeStruct((B, KVH, G, hd), F32),
        compiler_params=_params("parallel", "parallel"),
        name="nsa_sample_attn",
    )(idx.reshape(-1), page_table.reshape(-1), thr, q, ks_n, vs_n, win_k.reshape(B, n_cached * KVH, hd),
      win_v.reshape(B, n_cached * KVH, hd), kw_n, vw_n, oc, gate, path, tab,
      *([sel_k] * n_pick), *([sel_v] * n_pick))


def rel_bucket(dist):
    d = jnp.maximum(dist, 0)
    me = NUM_BUCKETS // 2
    logd = jnp.log(jnp.maximum(d, 1).astype(F32) / me)
    large = me + (logd / math.log(MAX_DISTANCE / me) * (NUM_BUCKETS - me)).astype(jnp.int32)
    return jnp.where(d < me, d, jnp.minimum(large, NUM_BUCKETS - 1))


SAMPLE_ROWS = 16
MATMUL_TN = 512


class EvenCols:
    def __init__(self, hg_qk, hg_w, nsa_w, nsa_kvw):
        self.hd = nsa_w // NSA_HEADS
        self.kvw = nsa_kvw
        self.nsa_w = nsa_w
        self.q, self.f, self.v, self.g = 0, hg_qk, 2 * hg_qk, 2 * hg_qk + hg_w
        self.nq = 2 * hg_qk + 2 * hg_w
        self.kv = self.nq + nsa_w
        self.main_w = self.kv + 6 * nsa_kvw
        self.gate_w = 3 * NSA_G
        self.tail_path, self.tail_gate = 0, nsa_w
        used = nsa_w + NSA_KVH * LANES
        self.tail_w = -(-used // MATMUL_TN) * MATMUL_TN
        assert self.main_w % MATMUL_TN == 0

    def tail_weight(self, w_in_t):
        gate0 = self.main_w
        path0 = gate0 + NSA_KVH * self.gate_w
        parts = [w_in_t[path0:path0 + self.nsa_w]]
        for h in range(NSA_KVH):
            parts.append(jnp.pad(w_in_t[gate0 + h * self.gate_w:gate0 + (h + 1) * self.gate_w],
                                 ((0, LANES - self.gate_w), (0, 0))))
        w = jnp.concatenate(parts, axis=0)
        return jnp.pad(w, ((0, self.tail_w - w.shape[0]), (0, 0)))


def even_prompt(main, tail, B, T, lb, hg_gain, pe, w1k, w2k, w1v, w2v, table, ec):
    hd = ec.hd
    S0 = jnp.zeros((B, HG_HEADS, HG_DK, hg_gain.shape[0] // HG_HEADS), F32)
    y_hg, S = hgrn_mix(main, lb, hg_gain, S0, B, T, (ec.q, ec.f, ec.v, ec.g), HG_CHUNK, HGRN_SUB, HG_CHUNK)
    rows = kv_split(main, ec.kv, 6, NSA_KVH, hd)
    kcmp = compress_blocks(rows[0], B, T, NSA_KVH, pe, w1k, w2k)
    vcmp = compress_blocks(rows[1], B, T, NSA_KVH, pe, w1v, w2v)
    kc, vc, ks, vs, kw, vw = [r.reshape(B, T, NSA_KVH, hd) for r in rows]
    y_nsa = nsa_prompt(main, tail, kcmp, vcmp, table, B, T, (ec.nq, ec.kv, ec.tail_path, ec.tail_gate, hd))
    wb = min(WINDOW, T)
    return jnp.concatenate([y_hg, y_nsa], axis=-1), S, kc, vc, ks, vs, kw[:, T - wb:], vw[:, T - wb:]


def even_sample(main, tail, B, T, past_len, S0, page_table, pool_ck, pool_cv, pool_sk, pool_sv, win_k, win_v,
                lb, hg_gain, pe, w1k, w2k, w1v, w2v, table, ec):
    hd = ec.hd
    R = SAMPLE_ROWS
    y_hg, S = hgrn_mix(main, lb, hg_gain, S0.astype(F32), B, R, (ec.q, ec.f, ec.v, ec.g), R, R, T)
    m3 = main.reshape(B, R, -1)[:, :T]
    t3 = tail.reshape(B, R, -1)[:, :T]
    assert T == 1
    heads = lambda a, w: a.reshape(B, NSA_KVH, NSA_G, w)
    q = heads(m3[..., ec.nq:ec.kv], hd)
    path = heads(t3[..., ec.tail_path:ec.tail_path + ec.nsa_w], hd)
    gate = jnp.stack([t3[:, 0, ec.tail_gate + h * LANES:ec.tail_gate + h * LANES + ec.gate_w].reshape(B, NSA_G, 3)
                      for h in range(NSA_KVH)], axis=1)
    kvs = m3[..., ec.kv:ec.main_w].reshape(B, T, 6, NSA_KVH, hd)
    kc, vc, ks, vs, kw, vw = [kvs[:, :, j] for j in range(6)]
    new_kv = tuple(jnp.swapaxes(a, 1, 2) for a in (ks, vs, kw, vw))
    kcmp = compress_paged(pool_ck, page_table, pe, w1k, w2k)
    vcmp = compress_paged(pool_cv, page_table, pe, w1v, w2v)
    y_nsa = nsa_sample(q, kcmp, vcmp, pool_sk, pool_sv, page_table, new_kv, win_k, win_v, gate, path, table,
                       past_len)
    y_nsa = jnp.pad(y_nsa.reshape(B, T, -1).astype(BF16), ((0, 0), (0, R - T), (0, 0))).reshape(B * R, -1)
    kw_all = jnp.concatenate([win_k, kw], axis=1)
    vw_all = jnp.concatenate([win_v, vw], axis=1)
    return jnp.concatenate([y_hg, y_nsa], axis=-1), S, kc, vc, ks, vs, kw_all[:, T:], vw_all[:, T:]


def mem_attend(qg, B, T, mk, mv):
    slots = mk.shape[1]
    return mem_mix(qg, mk.reshape(B, slots, -1), mv.reshape(B, slots, -1), B, T)


def kernel(x_prompt, x_sample, mem_prompt, state_hgrn, cache_cmp_k, cache_cmp_v, cache_sel_k, cache_sel_v,
           cache_win_k, cache_win_v, state_ret, cache_mem_k, cache_mem_v, page_table, rel_table,
           norm_mix_pre, norm_mix_post, norm_mem_pre, norm_mem_post, ev_w_in, ev_w_out, hgrn_lb, hgrn_norm,
           cmp_pe, cmp_w1_k, cmp_w2_k, cmp_w1_v, cmp_w2_v, od_w_in, od_w_out, ret_norm,
           mem_w_q, mem_w_k, mem_w_v, mem_w_o):
    B, T, D = x_prompt.shape
    Bs, Ts, _ = x_sample.shape
    depth = norm_mix_pre.shape[0]
    past_len = page_table.shape[1] * PAGE_SIZE
    pos_p = jnp.arange(T, dtype=jnp.int32)
    pos_s = past_len + jnp.arange(Ts, dtype=jnp.int32)
    lb_all = jnp.cumsum(jax.nn.softmax(hgrn_lb.astype(F32), axis=0), axis=0)
    hg_qk = hgrn_lb.shape[1]
    hg_w = hgrn_norm.shape[1]
    nsa_w = ev_w_out.shape[1] - hg_w
    nsa_kvw = NSA_KVH * (nsa_w // NSA_HEADS)
    ec = EvenCols(hg_qk, hg_w, nsa_w, nsa_kvw)
    ev_w_in_t = jnp.swapaxes(ev_w_in, 1, 2)
    R = SAMPLE_ROWS
    assert Ts <= R

    even_p, even_s, ret_p, ret_s, memk_p, memv_p = [], [], [], [], [], []
    xp = x_prompt.reshape(B * T, D)
    xs = jnp.pad(x_sample, ((0, 0), (0, R - Ts), (0, 0))).reshape(Bs * R, D)
    pos_sr = past_len + jnp.arange(R, dtype=jnp.int32)
    mem2 = mem_prompt.reshape(-1, D).astype(BF16)
    ML = mem_prompt.shape[1]
    hp = rmsnorm_bf16(xp, norm_mix_pre[0])
    hs = rmsnorm_bf16(xs, norm_mix_pre[0])
    for l in range(depth):
        if l % 2 == 0:
            e = l // 2
            w = (lb_all[e], hgrn_norm[e], cmp_pe[e], cmp_w1_k[e], cmp_w2_k[e], cmp_w1_v[e], cmp_w2_v[e], rel_table)
            main_p, main_s = matmul(hp, ev_w_in_t, e, n_cols=ec.main_w, transposed=True, xs=hs)
            tail_p, tail_s = matmul(hp, ec.tail_weight(ev_w_in_t[e]), transposed=True, xs=hs)
            ap, *sp = even_prompt(main_p, tail_p, B, T, *w, ec)
            as_, *ss = even_sample(main_s, tail_s, Bs, Ts, past_len, state_hgrn[e], page_table, cache_cmp_k[e],
                                   cache_cmp_v[e], cache_sel_k[e], cache_sel_v[e], cache_win_k[e], cache_win_v[e],
                                   *w, ec)
            yp, ys = matmul(ap, ev_w_out, e, xs=as_)
            even_p.append(sp)
            even_s.append(ss)
        else:
            o = l // 2
            S0p = jnp.zeros((B,) + state_ret.shape[2:], F32)
            proj_p, proj_s = matmul(hp, od_w_in, o, xs=hs)
            ap, sp = retention_mix(proj_p, pos_p, ret_norm[o], S0p, B, T, RET_CHUNK, RET_CHUNK)
            as_, ss = retention_mix(proj_s, pos_sr, ret_norm[o], state_ret[o].astype(F32), Bs, R, R, Ts)
            yp, ys = matmul(ap, od_w_out, o, xs=as_)
            ret_p.append(sp)
            ret_s.append(ss)
        xp, hp = residual_post_pre(xp, yp, norm_mix_post[l], norm_mem_pre[l])
        xs, hs = residual_post_pre(xs, ys, norm_mix_post[l], norm_mem_pre[l])
        mk_p = matmul(mem2, mem_w_k, l).reshape(B, ML, MEM_HEADS, MEM_HD)
        mv_p = matmul(mem2, mem_w_v, l).reshape(B, ML, MEM_HEADS, MEM_HD)
        memk_p.append(mk_p)
        memv_p.append(mv_p)
        qg_p, qg_s = matmul(hp, mem_w_q, l, xs=hs)
        yp, ys = matmul(mem_attend(qg_p, B, T, mk_p, mv_p), mem_w_o, l,
                        xs=mem_attend(qg_s, Bs, R, cache_mem_k[l], cache_mem_v[l]))
        if l + 1 < depth:
            xp, hp = residual_post_pre(xp, yp, norm_mem_post[l], norm_mix_pre[l + 1])
            xs, hs = residual_post_pre(xs, ys, norm_mem_post[l], norm_mix_pre[l + 1])
        else:
            xp = residual_post(xp, yp, norm_mem_post[l])
            xs = residual_post(xs, ys, norm_mem_post[l])
    p_hgrn, p_cmp_k, p_cmp_v, p_sel_k, p_sel_v, p_win_k, p_win_v = [jnp.stack(a) for a in zip(*even_p)]
    s_hgrn, s_cmp_k, s_cmp_v, s_sel_k, s_sel_v, s_win_k, s_win_v = [jnp.stack(a) for a in zip(*even_s)]
    p_ret = jnp.stack(ret_p)
    s_ret = jnp.stack(ret_s)
    p_mem_k = jnp.stack(memk_p)
    p_mem_v = jnp.stack(memv_p)
    return (xp.reshape(B, T, D), xs.reshape(Bs, R, D)[:, :Ts], p_hgrn, p_cmp_k, p_cmp_v, p_sel_k, p_sel_v,
            p_win_k, p_win_v, p_ret, p_mem_k, p_mem_v,
            s_hgrn, s_cmp_k, s_cmp_v, s_sel_k, s_sel_v, s_win_k, s_win_v, s_ret)
```

```python
import functools
import math

import jax
import jax.numpy as jnp
import numpy as np
from jax import lax
from jax.experimental import pallas as pl
from jax.experimental.pallas import tpu as pltpu

F32 = jnp.float32
BF16 = jnp.bfloat16
EPS = 1e-6
NEG_INF = -1e30

PAGE_SIZE = 128
HG_HEADS = 16
HG_DK = 128
HG_CHUNK = 64
HGRN_SUB = 16
NSA_HEADS = 16
NSA_KVH = 2
NSA_G = NSA_HEADS // NSA_KVH
CMP_BLOCK = 32
SEL_BLOCK = 64
N_SELECT = 16
WINDOW = 512
SEL_QBLOCK = 64
WIN_QBLOCK = 128
NUM_BUCKETS = 32
MAX_DISTANCE = 1024
RET_HEADS = 16
RET_CHUNK = 128
ROPE_BASE = 10000.0
MEM_HEADS = 4
MEM_HD = 128
MEM_W = MEM_HEADS * MEM_HD

VMEM_LIMIT_BYTES = 56 * 1024 * 1024


def _params(*sem):
    return pltpu.CompilerParams(dimension_semantics=sem, vmem_limit_bytes=VMEM_LIMIT_BYTES)


def _rmsnorm_kernel(x_ref, g_ref, o_ref):
    x = x_ref[...]
    y = x * lax.rsqrt(jnp.mean(x * x, axis=-1, keepdims=True) + EPS)
    o_ref[...] = (y * g_ref[...]).astype(o_ref.dtype)


def rmsnorm_bf16(x, g):
    M, D = x.shape
    tm = min(M, 512)
    return pl.pallas_call(
        _rmsnorm_kernel,
        grid=(M // tm,),
        in_specs=[pl.BlockSpec((tm, D), lambda i: (i, 0)), pl.BlockSpec((1, D), lambda i: (0, 0))],
        out_specs=pl.BlockSpec((tm, D), lambda i: (i, 0)),
        out_shape=jax.ShapeDtypeStruct((M, D), BF16),
        compiler_params=_params("parallel"),
        name="rmsnorm_bf16",
    )(x, g.reshape(1, D))


def _post_pre_kernel(x_ref, y_ref, g_ref, gn_ref, o_ref, h_ref):
    y = y_ref[...]
    x = x_ref[...] + y * lax.rsqrt(jnp.mean(y * y, axis=-1, keepdims=True) + EPS) * g_ref[...]
    o_ref[...] = x
    h_ref[...] = (x * lax.rsqrt(jnp.mean(x * x, axis=-1, keepdims=True) + EPS) * gn_ref[...]).astype(h_ref.dtype)


def residual_post_pre(x, y, g, g_next):
    M, D = x.shape
    tm = min(M, 256)
    row = pl.BlockSpec((tm, D), lambda i: (i, 0))
    vec = pl.BlockSpec((1, D), lambda i: (0, 0))
    return pl.pallas_call(
        _post_pre_kernel,
        grid=(M // tm,),
        in_specs=[row, row, vec, vec],
        out_specs=[row, row],
        out_shape=[jax.ShapeDtypeStruct((M, D), F32), jax.ShapeDtypeStruct((M, D), BF16)],
        compiler_params=_params("parallel"),
        name="residual_post_pre",
    )(x, y, g.reshape(1, D), g_next.reshape(1, D))


def _matmul_post_kernel(*refs, has_next):
    if has_next:
        a_ref, w_ref, x_ref, g_ref, gn_ref, o_ref, h_ref, wb_ref = refs
    else:
        a_ref, w_ref, x_ref, g_ref, o_ref, wb_ref = refs

    @pl.when(pl.program_id(0) == 0)
    def _():
        wb_ref[...] = w_ref[...].astype(BF16)

    y = jnp.dot(a_ref[...], wb_ref[...], preferred_element_type=F32)
    x = x_ref[...] + y * lax.rsqrt(jnp.mean(y * y, axis=-1, keepdims=True) + EPS) * g_ref[...]
    o_ref[...] = x
    if has_next:
        h_ref[...] = (x * lax.rsqrt(jnp.mean(x * x, axis=-1, keepdims=True) + EPS) * gn_ref[...]).astype(h_ref.dtype)


def matmul_residual(a, w, layer, x, g, g_next=None):
    M, K = a.shape
    D = w.shape[-1]
    tm = min(M, 256)
    has_next = g_next is not None
    row = lambda width: pl.BlockSpec((tm, width), lambda i: (i, 0))
    vec = pl.BlockSpec((1, D), lambda i: (0, 0))
    in_specs = [row(K), pl.BlockSpec((None, K, D), lambda i: (layer, 0, 0)), row(D), vec]
    args = [a, w, x, g.reshape(1, D)]
    out_specs = [row(D)]
    out_shape = [jax.ShapeDtypeStruct((M, D), F32)]
    if has_next:
        in_specs.append(vec)
        args.append(g_next.reshape(1, D))
        out_specs.append(row(D))
        out_shape.append(jax.ShapeDtypeStruct((M, D), BF16))
    out = pl.pallas_call(
        functools.partial(_matmul_post_kernel, has_next=has_next),
        grid=(M // tm,),
        in_specs=in_specs,
        out_specs=out_specs,
        out_shape=out_shape,
        scratch_shapes=[pltpu.VMEM((K, D), BF16)],
        compiler_params=_params("arbitrary"),
        name="matmul_residual",
    )(*args)
    return out if has_next else (out[0], None)


def _matmul_kernel(*refs, transposed, paired):
    if paired:
        x_ref, xs_ref, w_ref, o_ref, os_ref, wb_ref = refs
    else:
        x_ref, w_ref, o_ref, wb_ref = refs

    @pl.when(pl.program_id(1) == 0)
    def _():
        w = w_ref[...]
        wb_ref[...] = (w.T if transposed else w).astype(BF16)
        if paired:
            os_ref[...] = jnp.dot(xs_ref[...], wb_ref[...], preferred_element_type=F32)

    o_ref[...] = jnp.dot(x_ref[...], wb_ref[...], preferred_element_type=F32)


def matmul(x, w, layer=None, n_cols=None, transposed=False, xs=None):
    M, K = x.shape
    N = (w.shape[-2] if transposed else w.shape[-1]) if n_cols is None else n_cols
    tn = 512 if K <= 4096 else 256
    tn = min(tn, N)
    tm = min(M, 1024 if K <= 4096 else 512)
    assert N % tn == 0 and M % tm == 0, (M, K, N)
    blk = (tn, K) if transposed else (K, tn)
    pick = (lambda n: (n, 0)) if transposed else (lambda n: (0, n))
    if w.ndim == 3:
        w_spec = pl.BlockSpec((None,) + blk, lambda n, m: (layer,) + pick(n))
    else:
        w_spec = pl.BlockSpec(blk, lambda n, m: pick(n))
    in_specs = [pl.BlockSpec((tm, K), lambda n, m: (m, 0))]
    out_specs = [pl.BlockSpec((tm, tn), lambda n, m: (m, n))]
    out_shape = [jax.ShapeDtypeStruct((M, N), F32)]
    args = [x]
    if xs is not None:
        Ms = xs.shape[0]
        in_specs.append(pl.BlockSpec((Ms, K), lambda n, m: (0, 0)))
        out_specs.append(pl.BlockSpec((Ms, tn), lambda n, m: (0, n)))
        out_shape.append(jax.ShapeDtypeStruct((Ms, N), F32))
        args.append(xs)
    out = pl.pallas_call(
        functools.partial(_matmul_kernel, transposed=transposed, paired=xs is not None),
        grid=(N // tn, M // tm),
        in_specs=in_specs + [w_spec],
        out_specs=out_specs,
        out_shape=out_shape,
        scratch_shapes=[pltpu.VMEM((K, tn), BF16)],
        compiler_params=_params("parallel", "arbitrary"),
        name="matmul",
    )(*args, w)
    return out if xs is not None else out[0]


LANES = 128
NSA_TQ = 128
_NT = (((1,), (1,)), ((), ()))


_TN = (((0,), (0,)), ((), ()))


def _bias_tile_kernel(thr_ref, table_ref, o_ref, *, row_stride, row_offset):
    kvh = pl.program_id(0)
    step = pl.program_id(1)
    row = lax.broadcasted_iota(jnp.int32, (LANES, LANES), 0)
    lane = lax.broadcasted_iota(jnp.int32, (LANES, LANES), 1)
    dist = LANES * step + lane - row_stride * row - row_offset
    for g in range(NSA_G):
        h = kvh * NSA_G + g
        bias = jnp.full((LANES, LANES), table_ref[0, h], F32)
        for k in range(1, NUM_BUCKETS):
            bias = jnp.where(dist >= thr_ref[k], table_ref[k, h], bias)
        o_ref[0, 0, :, g * LANES:(g + 1) * LANES] = bias


def bias_tiles(table, n_steps, max_dist, row_stride, row_offset):
    bucket = rel_bucket(jnp.arange(max_dist, dtype=jnp.int32))
    thr = jnp.sum(bucket[None, :] < jnp.arange(NUM_BUCKETS, dtype=jnp.int32)[:, None], axis=1).astype(jnp.int32)
    smem = pl.BlockSpec(memory_space=pltpu.SMEM)
    return pl.pallas_call(
        functools.partial(_bias_tile_kernel, row_stride=row_stride, row_offset=row_offset),
        grid=(NSA_KVH, n_steps),
        in_specs=[smem, smem],
        out_specs=pl.BlockSpec((1, 1, LANES, NSA_G * LANES), lambda h, s: (h, s, 0, 0)),
        out_shape=jax.ShapeDtypeStruct((NSA_KVH, n_steps, LANES, NSA_G * LANES), F32),
        compiler_params=_params("parallel", "parallel"),
        name="bias_tiles",
    )(thr, table.astype(F32))


def _nsa_prompt_kernel(q_ref, kc_ref, vc_ref, ks_ref, vs_ref, kw_ref, vw_ref, cb_ref, tb_ref, gate_ref, path_ref,
                       o_ref, qb_s, oc_s, m_s, l_s, acc_s, *, hd, n_win_chunks, n_pick):
    i = pl.program_id(2)
    tq = NSA_TQ
    G = NSA_G
    scale = hd ** -0.5
    row = lax.broadcasted_iota(jnp.int32, (LANES, tq), 0)
    qpos = i * tq + lax.broadcasted_iota(jnp.int32, (LANES, tq), 1)
    slab = lambda g: slice(g * tq, (g + 1) * tq)

    for g in range(G):
        qb_s[slab(g), :] = (q_ref[:, g * hd:(g + 1) * hd] * scale).astype(BF16)
    qb = qb_s[...]

    s_all = lax.dot_general(kc_ref[0].astype(BF16), qb, _NT, preferred_element_type=F32) + cb_ref[0, 0]
    valid_c = qpos >= (row + 1) * CMP_BLOCK - 1
    valid_cf = valid_c.astype(F32)
    imp = jnp.zeros((LANES, tq), F32)
    ps = []
    for g in range(G):
        s = jnp.where(valid_c, s_all[:, slab(g)], NEG_INF)
        e = jnp.exp(s - jnp.max(s, axis=0, keepdims=True))
        p = e * (1.0 / jnp.sum(e, axis=0, keepdims=True)) * valid_cf
        imp = imp + p
        ps.append(p.astype(BF16))
    oc_s[...] = lax.dot_general(vc_ref[0].astype(BF16), jnp.concatenate(ps, axis=1), _TN,
                                preferred_element_type=F32)

    imp2 = imp + pltpu.roll(imp, LANES - 1, axis=0)
    blk = row >> 1
    cur = qpos // SEL_BLOCK
    forced = (blk == 0) | (blk == cur) | (blk == cur - 1)
    usable = ((row & 1) == 0) & (blk * SEL_BLOCK <= qpos)
    work = jnp.where(usable, jnp.where(forced, jnp.inf, imp2), -jnp.inf)
    sel = jnp.zeros((LANES, tq), F32)
    for _ in range(n_pick):
        mx = jnp.max(work, axis=0, keepdims=True)
        first = jnp.min(jnp.where(work == mx, row, LANES), axis=0, keepdims=True)
        pick = row == first
        sel = jnp.where(pick, 1.0, sel)
        work = jnp.where(pick, -jnp.inf, work)
    sel_b = sel.astype(BF16)

    m_s[...] = jnp.full(m_s.shape, NEG_INF, F32)
    l_s[...] = jnp.zeros(l_s.shape, F32)
    acc_s[...] = jnp.zeros(acc_s.shape, F32)
    ekey = lax.broadcasted_iota(jnp.int32, (LANES, LANES), 0)
    eblk = lax.broadcasted_iota(jnp.int32, (LANES, LANES), 1)

    def online(slot, k_b, v_b, bias, valid):
        s_all = lax.dot_general(k_b, qb, _NT, preferred_element_type=F32) + bias
        ps, alphas = [], []
        for g in range(G):
            s = jnp.where(valid, s_all[:, slab(g)], NEG_INF)
            m_old = m_s[slot, :, slab(g)]
            m_new = jnp.maximum(m_old, jnp.max(s, axis=0, keepdims=True))
            alpha = jnp.exp(m_old - m_new)
            p = jnp.where(valid, jnp.exp(s - m_new), 0.0)
            l_s[slot, :, slab(g)] = alpha * l_s[slot, :, slab(g)] + jnp.sum(p, axis=0, keepdims=True)
            m_s[slot, :, slab(g)] = m_new
            ps.append(p.astype(BF16))
            alphas.append(alpha)
        pv = lax.dot_general(v_b, jnp.concatenate(ps, axis=1), _TN, preferred_element_type=F32)
        acc_s[slot] = jnp.concatenate(alphas, axis=1) * acc_s[slot] + pv

    def chunk(c, carry):
        k0 = pl.multiple_of(c * LANES, LANES)
        delta = i - c
        dist = qpos - (k0 + row)
        causal = dist >= 0
        expand = (eblk == 2 * (c * (LANES // SEL_BLOCK) + (ekey // SEL_BLOCK))).astype(BF16)
        chosen = jnp.dot(expand, sel_b, preferred_element_type=F32) > 0.5
        bias = tb_ref[0, delta]
        online(0, ks_ref[pl.ds(k0, LANES), :].astype(BF16), vs_ref[pl.ds(k0, LANES), :].astype(BF16), bias,
               chosen & causal)

        @pl.when(delta < n_win_chunks)
        def _():
            online(1, kw_ref[pl.ds(k0, LANES), :].astype(BF16), vw_ref[pl.ds(k0, LANES), :].astype(BF16), bias,
                   causal & (dist < WINDOW))

        return carry

    lax.fori_loop(0, i + 1, chunk, 0)

    gates = jax.nn.sigmoid(gate_ref[...]).T
    for g in range(G):
        o_sel = acc_s[0, :, slab(g)] * (1.0 / l_s[0, :, slab(g)])
        o_win = acc_s[1, :, slab(g)] * (1.0 / l_s[1, :, slab(g)])
        o = (gates[3 * g:3 * g + 1] * oc_s[:, slab(g)] + gates[3 * g + 1:3 * g + 2] * o_sel
             + gates[3 * g + 2:3 * g + 3] * o_win)
        o_ref[:, g * hd:(g + 1) * hd] = (o.T * jax.nn.silu(path_ref[:, g * hd:(g + 1) * hd])).astype(o_ref.dtype)


def nsa_prompt(main, tail, kcmp, vcmp, table, B, T, cols):
    q_col, kv_col, path_col, gate_col, hd = cols
    G = NSA_G
    tq = NSA_TQ
    n_cb = kcmp.shape[1]
    assert n_cb <= LANES and T % tq == 0 and hd == LANES
    nd = T // LANES
    n_win_chunks = WINDOW // LANES + 1
    tb = bias_tiles(table, nd, T, 1, 0)
    cb = bias_tiles(table, T // tq, T, CMP_BLOCK, CMP_BLOCK - 1)
    pad = ((0, 0), (0, LANES - n_cb), (0, 0))
    kcmp = jnp.pad(kcmp, pad)
    vcmp = jnp.pad(vcmp, pad)
    nblk = T // tq
    gw = G * hd
    row_map = lambda b, h, i: (b * nblk + i, 0)
    kv_spec = lambda j: pl.BlockSpec((T, hd), lambda b, h, i: (b, kv_col // hd + 2 * j + h))
    return pl.pallas_call(
        functools.partial(_nsa_prompt_kernel, hd=hd, n_win_chunks=n_win_chunks,
                          n_pick=min(N_SELECT, -(-T // SEL_BLOCK))),
        grid=(B, NSA_KVH, nblk),
        in_specs=[
            pl.BlockSpec((tq, gw), lambda b, h, i: (b * nblk + i, q_col // gw + h)),
            pl.BlockSpec((1, LANES, hd), lambda b, h, i: (b, 0, h)),
            pl.BlockSpec((1, LANES, hd), lambda b, h, i: (b, 0, h)),
            kv_spec(2), kv_spec(3), kv_spec(4), kv_spec(5),
            pl.BlockSpec((1, 1, LANES, gw), lambda b, h, i: (h, i, 0, 0)),
            pl.BlockSpec((1, nd, LANES, gw), lambda b, h, i: (h, 0, 0, 0)),
            pl.BlockSpec((tq, LANES), lambda b, h, i: (b * nblk + i, gate_col // LANES + h)),
            pl.BlockSpec((tq, gw), lambda b, h, i: (b * nblk + i, path_col // gw + h)),
        ],
        out_specs=pl.BlockSpec((tq, gw), lambda b, h, i: (b * nblk + i, h)),
        out_shape=jax.ShapeDtypeStruct((B * T, NSA_KVH * gw), BF16),
        scratch_shapes=[pltpu.VMEM((G * tq, hd), BF16), pltpu.VMEM((hd, G * tq), F32),
                        pltpu.VMEM((2, 1, G * tq), F32), pltpu.VMEM((2, 1, G * tq), F32),
                        pltpu.VMEM((2, hd, G * tq), F32)],
        compiler_params=_params("parallel", "parallel", "arbitrary"),
        name="nsa_prompt",
    )(main, kcmp, vcmp, main, main, main, main, cb, tb, tail, tail)


_TN = (((0,), (0,)), ((), ()))


def _cumsum_rows(x, n):
    row = lax.broadcasted_iota(jnp.int32, x.shape, 0)
    sh = 1
    while sh < n:
        x = x + jnp.where(row >= sh, pltpu.roll(x, sh, axis=0), 0.0)
        sh *= 2
    return x


def _hgrn_kernel(q_ref, f_ref, v_ref, gate_ref, lb_ref, gain_ref, s0_ref, y_ref, s_out_ref, st_s,
                 *, C, SB, c_eff, dk, HB):
    c = pl.program_id(2)

    @pl.when(c == 0)
    def _():
        for hh in range(HB):
            st_s[hh] = s0_ref[0, hh].T

    states = [st_s[hh] for hh in range(HB)]
    results = [_hgrn_head(slice(hh * dk, (hh + 1) * dk), states[hh], q_ref, f_ref, v_ref, gate_ref, lb_ref,
                          gain_ref, C=C, SB=SB, c_eff=c_eff, dk=dk) for hh in range(HB)]
    y_ref[...] = jnp.concatenate([y for y, _ in results], axis=1)
    for hh in range(HB):
        st_s[hh] = results[hh][1]

    @pl.when(c == pl.num_programs(2) - 1)
    def _():
        for hh in range(HB):
            s_out_ref[0, hh] = st_s[hh].T


def _hgrn_head(sl, st, q_ref, f_ref, v_ref, gate_ref, lb_ref, gain_ref, *, C, SB, c_eff, dk):
    lb = lb_ref[:, sl]
    f = lb + (1.0 - lb) * jax.nn.sigmoid(f_ref[:, sl])
    logf = jnp.log(f)
    k = 1.0 - f
    if c_eff < C:
        real = lax.broadcasted_iota(jnp.int32, (C, dk), 0) < c_eff
        logf = jnp.where(real, logf, 0.0)
        k = jnp.where(real, k, 0.0)
    q = q_ref[:, sl] * dk ** -0.5
    v_b = v_ref[:, sl].astype(BF16)
    b = _cumsum_rows(logf, C)
    o = lax.dot_general((q * jnp.exp(b)).astype(BF16), st.astype(BF16), _NT, preferred_element_type=F32)

    lane_c = lax.broadcasted_iota(jnp.int32, (SB, C), 1)
    row_c = lax.broadcasted_iota(jnp.int32, (SB, C), 0)
    outs = []
    for I in range(C // SB):
        r0 = I * SB
        q_i = q[r0:r0 + SB]
        b_i = b[r0:r0 + SB]
        if I > 0:
            b_r = b[r0 - 1:r0]
            qq = (q_i * jnp.exp(b_i - b_r)).astype(BF16)
            kk = (k * jnp.exp(jnp.minimum(b_r - b, 0.0))).astype(BF16)
            a = lax.dot_general(qq, kk, _NT, preferred_element_type=F32)
            a = jnp.where(lane_c < r0, a, 0.0)
        else:
            a = jnp.zeros((SB, C), F32)
        for s in range(SB):
            z = q_i * k[r0 + s:r0 + s + 1] * jnp.exp(jnp.minimum(b_i - b[r0 + s:r0 + s + 1], 0.0))
            col = jnp.sum(z, axis=-1, keepdims=True)
            a = jnp.where((lane_c == r0 + s) & (row_c >= s), col, a)
        outs.append(jnp.dot(a.astype(BF16), v_b, preferred_element_type=F32))
    o = o + jnp.concatenate(outs, axis=0)

    b_last = b[C - 1:C]
    kd = (k * jnp.exp(b_last - b)).astype(BF16)
    st_new = jnp.exp(b_last) * st + lax.dot_general(v_b, kd, _TN, preferred_element_type=F32)
    y = o * lax.rsqrt(jnp.mean(o * o, axis=-1, keepdims=True) + EPS) * gain_ref[:, sl]
    return (y * jax.nn.silu(gate_ref[:, sl])).astype(BF16), st_new


HGRN_HEADS_PER_STEP = 8


def hgrn_mix(main, lb, gain, s0, B, T, cols, C, SB, c_eff):
    q_col, f_col, v_col, g_col = cols
    _, H, dk, dv = s0.shape
    HB = HGRN_HEADS_PER_STEP
    W = HB * LANES
    assert dk == LANES and dv == LANES and T % C == 0 and C % SB == 0 and H % HB == 0
    assert all(off % W == 0 for off in cols)
    nc = T // C
    col = lambda off: pl.BlockSpec((C, W), lambda b, h, c: (b * nc + c, off // W + h))
    vec = pl.BlockSpec((1, W), lambda b, h, c: (0, h))
    st = pl.BlockSpec((1, HB, dk, dv), lambda b, h, c: (b, h, 0, 0))
    return pl.pallas_call(
        functools.partial(_hgrn_kernel, C=C, SB=SB, c_eff=c_eff, dk=dk, HB=HB),
        grid=(B, H // HB, nc),
        in_specs=[col(q_col), col(f_col), col(v_col), col(g_col), vec, vec, st],
        out_specs=[pl.BlockSpec((C, W), lambda b, h, c: (b * nc + c, h)), st],
        out_shape=[jax.ShapeDtypeStruct((B * T, H * dv), BF16), jax.ShapeDtypeStruct(s0.shape, F32)],
        scratch_shapes=[pltpu.VMEM((HB, dv, dk), F32)],
        compiler_params=_params("parallel", "parallel", "arbitrary"),
        name="hgrn_mix",
    )(main, main, main, main, lb.reshape(1, -1), gain.reshape(1, -1), s0)


def _retention_kernel(q_ref, k_ref, v_ref, g_ref, cos_ref, sin_ref, lg_ref, gain_ref, s0_ref, y_ref, s_out_ref, s_s,
                      *, C, c_eff, dk, dv, HB):
    c = pl.program_id(2)

    @pl.when(c == 0)
    def _():
        s_s[...] = s0_ref[0]

    for hh in range(HB):
        _retention_head(hh, q_ref, k_ref, v_ref, g_ref, cos_ref, sin_ref, lg_ref, gain_ref, y_ref, s_s,
                        C=C, c_eff=c_eff, dk=dk, dv=dv)

    @pl.when(c == pl.num_programs(2) - 1)
    def _():
        s_out_ref[0] = s_s[...]


def _retention_head(hh, q_ref, k_ref, v_ref, g_ref, cos_ref, sin_ref, lg_ref, gain_ref, y_ref, s_s,
                    *, C, c_eff, dk, dv):
    half = dk // 2
    vsl = slice(hh * dv, (hh + 1) * dv)
    cos = cos_ref[...]
    sin = sin_ref[...]
    lg_w = lg_ref[hh]
    lg = lg_w[:, :LANES]
    row = lax.broadcasted_iota(jnp.int32, (C, LANES), 0).astype(F32)

    def rot(ref, w):
        x1 = ref[:, hh * dk:hh * dk + half]
        x2 = ref[:, hh * dk + half:(hh + 1) * dk]
        return jnp.concatenate([(x1 * cos - x2 * sin) * w, (x1 * sin + x2 * cos) * w], axis=1)

    q = rot(q_ref, dk ** -0.5)
    k = rot(k_ref, 1.0)
    v_b = v_ref[:, vsl].astype(BF16)
    s_old = s_s[hh]

    a = lax.dot_general(q.astype(BF16), k.astype(BF16), _NT, preferred_element_type=F32)
    ti = lax.broadcasted_iota(jnp.int32, (C, C), 0)
    si = lax.broadcasted_iota(jnp.int32, (C, C), 1)
    diff = (ti - si).astype(F32)
    a = a * jnp.where(diff >= 0, jnp.exp(diff * lg_w[:, :C]), 0.0)
    inner = jnp.dot(a.astype(BF16), v_b, preferred_element_type=F32)
    q_w = jnp.exp((row + 1.0) * lg)
    q_dec = q * jnp.concatenate([q_w] * (dk // LANES), axis=1)
    cross = jnp.dot(q_dec.astype(BF16), s_old.astype(BF16), preferred_element_type=F32)
    k_w = jnp.where(row < c_eff, jnp.exp((c_eff - 1.0 - row) * lg), 0.0)
    k_dec = k * jnp.concatenate([k_w] * (dk // LANES), axis=1)
    s_new = jnp.exp(c_eff * lg_w) * s_old + lax.dot_general(k_dec.astype(BF16), v_b, _TN, preferred_element_type=F32)
    s_s[hh] = s_new

    o = inner + cross
    cen = o - jnp.mean(o, axis=-1, keepdims=True)
    y = cen * lax.rsqrt(jnp.mean(cen * cen, axis=-1, keepdims=True) + EPS) * gain_ref[:, vsl]
    y_ref[:, vsl] = (y * jax.nn.silu(g_ref[:, vsl])).astype(y_ref.dtype)


RETENTION_HEADS_PER_STEP = 4


def retention_mix(proj, pos, gain, s0, B, T, C, c_eff):
    _, H, dk, dv = s0.shape
    HB = RETENTION_HEADS_PER_STEP
    assert T % C == 0 and C <= LANES and dk % LANES == 0 and H % HB == 0
    nc = T // C
    nh = H // HB
    half = dk // 2
    inv = ROPE_BASE ** (-jnp.arange(half, dtype=F32) / half)
    ang = pos.astype(F32)[:, None] * inv[None, :]
    log_gamma = jnp.log1p(-jnp.exp2(-5.0 - jnp.arange(H, dtype=F32)))
    lg = jnp.broadcast_to(log_gamma[:, None, None], (H, 1, dv))
    qk = lambda j: pl.BlockSpec((C, HB * dk), lambda b, h, c: (b * nc + c, j * nh + h))
    vg = lambda j: pl.BlockSpec((C, HB * dv), lambda b, h, c: (b * nc + c, (2 * H * dk) // (HB * dv) + j * nh + h))
    tab = pl.BlockSpec((C, half), lambda b, h, c: (c, 0))
    st = pl.BlockSpec((1, HB, dk, dv), lambda b, h, c: (b, h, 0, 0))
    return pl.pallas_call(
        functools.partial(_retention_kernel, C=C, c_eff=c_eff, dk=dk, dv=dv, HB=HB),
        grid=(B, nh, nc),
        in_specs=[qk(0), qk(1), vg(0), vg(1), tab, tab,
                  pl.BlockSpec((HB, 1, dv), lambda b, h, c: (h, 0, 0)),
                  pl.BlockSpec((1, HB * dv), lambda b, h, c: (0, h)), st],
        out_specs=[pl.BlockSpec((C, HB * dv), lambda b, h, c: (b * nc + c, h)), st],
        out_shape=[jax.ShapeDtypeStruct((B * T, H * dv), BF16), jax.ShapeDtypeStruct(s0.shape, F32)],
        scratch_shapes=[pltpu.VMEM((HB, dk, dv), F32)],
        compiler_params=_params("parallel", "parallel", "arbitrary"),
        name="retention_mix",
    )(proj, proj, proj, proj, jnp.cos(ang), jnp.sin(ang), lg, gain.reshape(1, -1), s0)


def _mem_kernel(qg_ref, mk_ref, mv_ref, o_ref, *, heads, hd):
    for h in range(heads):
        sl = slice(h * hd, (h + 1) * hd)
        q = (qg_ref[:, sl] * hd ** -0.5).astype(BF16)
        s = lax.dot_general(q, mk_ref[0, :, sl].astype(BF16), _NT, preferred_element_type=F32)
        e = jnp.exp(s - jnp.max(s, axis=-1, keepdims=True))
        p = e * (1.0 / jnp.sum(e, axis=-1, keepdims=True))
        o = jnp.dot(p.astype(BF16), mv_ref[0, :, sl].astype(BF16), preferred_element_type=F32)
        gate = qg_ref[:, heads * hd + h * hd:heads * hd + (h + 1) * hd]
        o_ref[:, sl] = (o * jax.nn.silu(gate)).astype(o_ref.dtype)


def mem_mix(qg, mk, mv, B, T):
    W = mk.shape[2]
    tq = min(T, 256)
    nb = T // tq
    kv = pl.BlockSpec((1, mk.shape[1], W), lambda b, i: (b, 0, 0))
    return pl.pallas_call(
        functools.partial(_mem_kernel, heads=MEM_HEADS, hd=W // MEM_HEADS),
        grid=(B, nb),
        in_specs=[pl.BlockSpec((tq, 2 * W), lambda b, i: (b * nb + i, 0)), kv, kv],
        out_specs=pl.BlockSpec((tq, W), lambda b, i: (b * nb + i, 0)),
        out_shape=jax.ShapeDtypeStruct((B * T, W), BF16),
        compiler_params=_params("parallel", "arbitrary"),
        name="mem_mix",
    )(qg, mk, mv)


def _compress_rows(x_ref, n_blocks, pe_ref, w1_ref, w2_ref, kvh):
    stride = CMP_BLOCK * kvh
    outs = []
    for h in range(kvh):
        acc = jnp.zeros((n_blocks, w1_ref.shape[2]), F32)
        for j in range(0, CMP_BLOCK, 2):
            xa = x_ref[pl.ds(j * kvh + h, n_blocks, stride=stride), :] + pe_ref[j:j + 1]
            xb = x_ref[pl.ds((j + 1) * kvh + h, n_blocks, stride=stride), :] + pe_ref[j + 1:j + 2]
            x2 = jnp.concatenate([xa, xb], axis=1).astype(BF16)
            acc = acc + jnp.dot(x2, w1_ref[j // 2], preferred_element_type=F32)
        outs.append(jnp.dot(jax.nn.silu(acc).astype(BF16), w2_ref[...], preferred_element_type=F32))
    return jnp.concatenate(outs, axis=1)


def _compress_kernel(x_ref, pe_ref, w1_ref, w2_ref, o_ref, *, kvh):
    o_ref[...] = _compress_rows(x_ref, o_ref.shape[0], pe_ref, w1_ref, w2_ref, kvh)


def _compress_paged_kernel(pt_ref, pool_ref, pe_ref, w1_ref, w2_ref, o_ref, x_s, sem, *, n_pages, rows_per_page,
                           kvh):
    b = pl.program_id(0)

    def page_copy(p):
        return pltpu.make_async_copy(pool_ref.at[pt_ref[b * n_pages + p]],
                                     x_s.at[pl.ds(p * rows_per_page, rows_per_page)], sem.at[0])

    def start(p, carry):
        page_copy(p).start()
        return carry

    def wait(p, carry):
        page_copy(p).wait()
        return carry

    lax.fori_loop(0, n_pages, start, 0)
    lax.fori_loop(0, n_pages, wait, 0)
    o_ref[0] = _compress_rows(x_s, o_ref.shape[1], pe_ref, w1_ref, w2_ref, kvh)


def _compress_weights(pe, w1, w2, hd):
    return pe.astype(F32), w1.reshape(CMP_BLOCK // 2, 2 * hd, w1.shape[1]).astype(BF16), w2.astype(BF16)


def _kv_split_kernel(*refs, n, kvh, hd):
    for x_ref, o_ref in zip(refs[:n], refs[n:]):
        for h in range(kvh):
            o_ref[pl.ds(h, x_ref.shape[0], stride=kvh), :] = x_ref[:, h * hd:(h + 1) * hd]


def kv_split(main, col, n, kvh, hd):
    M = main.shape[0]
    tm = min(M, 512)
    w = kvh * hd
    assert col % w == 0 and M % tm == 0
    return pl.pallas_call(
        functools.partial(_kv_split_kernel, n=n, kvh=kvh, hd=hd),
        grid=(M // tm,),
        in_specs=[pl.BlockSpec((tm, w), functools.partial(lambda i, j: (i, col // w + j), j=j)) for j in range(n)],
        out_specs=[pl.BlockSpec((tm * kvh, hd), lambda i: (i, 0))] * n,
        out_shape=[jax.ShapeDtypeStruct((M * kvh, hd), F32)] * n,
        compiler_params=_params("parallel"),
        name="kv_split",
    )(*([main] * n))


def compress_blocks(x, B, L, kvh, pe, w1, w2):
    hd = x.shape[1]
    n = B * (L // CMP_BLOCK)
    assert hd == LANES and L % CMP_BLOCK == 0
    peb, w1b, w2b = _compress_weights(pe, w1, w2, hd)
    full = lambda a: pl.BlockSpec(a.shape, lambda i: (0,) * a.ndim)
    out = pl.pallas_call(
        functools.partial(_compress_kernel, kvh=kvh),
        grid=(1,),
        in_specs=[full(x), full(peb), full(w1b), full(w2b)],
        out_specs=pl.BlockSpec((n, kvh * hd), lambda i: (0, 0)),
        out_shape=jax.ShapeDtypeStruct((n, kvh * hd), F32),
        compiler_params=_params("arbitrary"),
        name="compress_blocks",
    )(x, peb, w1b, w2b)
    return out.reshape(B, L // CMP_BLOCK, kvh * hd)


def compress_paged(pool, page_table, pe, w1, w2):
    n_pool, page, kvh, hd = pool.shape
    B, NP = page_table.shape
    rpp = page * kvh
    n_blocks = NP * page // CMP_BLOCK
    assert hd == LANES and page % CMP_BLOCK == 0
    peb, w1b, w2b = _compress_weights(pe, w1, w2, hd)
    full = lambda a: pl.BlockSpec(a.shape, lambda b, pt: (0,) * a.ndim)
    return pl.pallas_call(
        functools.partial(_compress_paged_kernel, n_pages=NP, rows_per_page=rpp, kvh=kvh),
        grid_spec=pltpu.PrefetchScalarGridSpec(
            num_scalar_prefetch=1,
            grid=(B,),
            in_specs=[pl.BlockSpec(memory_space=pl.ANY), full(peb), full(w1b), full(w2b)],
            out_specs=pl.BlockSpec((1, n_blocks, kvh * hd), lambda b, pt: (b, 0, 0)),
            scratch_shapes=[pltpu.VMEM((NP * rpp, hd), F32), pltpu.SemaphoreType.DMA((1,))],
        ),
        out_shape=jax.ShapeDtypeStruct((B, n_blocks, kvh * hd), F32),
        compiler_params=_params("arbitrary"),
        name="compress_paged",
    )(page_table.reshape(-1), pool.reshape(n_pool, rpp, hd), peb, w1b, w2b)


def _bias_rows(dist, thr_ref, tab):
    bias = jnp.broadcast_to(tab[:, 0:1], dist.shape)
    for k in range(1, NUM_BUCKETS):
        bias = jnp.where(dist >= thr_ref[k], tab[:, k:k + 1], bias)
    return bias


def _sample_cmp_kernel(thr_ref, q_ref, kc_ref, vc_ref, tab_ref, oc_ref, idx_ref, *, qpos, n_pick, hd):
    n_cb = kc_ref.shape[1]
    qg = (q_ref[0, 0] * hd ** -0.5).astype(BF16)
    lane = lax.broadcasted_iota(jnp.int32, (NSA_G, n_cb), 1)
    dist = qpos - ((lane + 1) * CMP_BLOCK - 1)
    valid = dist >= 0
    s = lax.dot_general(qg, kc_ref[0].astype(BF16), _NT, preferred_element_type=F32)
    s = jnp.where(valid, s + _bias_rows(dist, thr_ref, tab_ref[0]), NEG_INF)
    e = jnp.exp(s - jnp.max(s, axis=-1, keepdims=True))
    p = e / jnp.sum(e, axis=-1, keepdims=True) * valid.astype(F32)
    oc_ref[0, 0] = jnp.dot(p.astype(BF16), vc_ref[0].astype(BF16), preferred_element_type=F32)

    imp = jnp.sum(p, axis=0, keepdims=True)
    imp2 = imp + pltpu.roll(imp, n_cb - 1, axis=1)
    lane1 = lax.broadcasted_iota(jnp.int32, (1, n_cb), 1)
    blk = lane1 >> 1
    cur = qpos // SEL_BLOCK
    forced = (blk == 0) | (blk == cur) | (blk == cur - 1)
    usable = ((lane1 & 1) == 0) & (blk * SEL_BLOCK <= qpos)
    work = jnp.where(usable, jnp.where(forced, jnp.inf, imp2), -jnp.inf)
    out_lane = lax.broadcasted_iota(jnp.int32, (1, LANES), 1)
    picks = jnp.zeros((1, LANES), jnp.int32)
    for it in range(n_pick):
        mx = jnp.max(work, axis=-1, keepdims=True)
        first = jnp.min(jnp.where(work == mx, lane1, n_cb), axis=-1, keepdims=True)
        picks = jnp.where(out_lane == it, first >> 1, picks)
        work = jnp.where(lane1 == first, -jnp.inf, work)
    idx_ref[0, 0] = picks


def _sample_attn_kernel(idx_ref, pt_ref, thr_ref, q_ref, ksn_ref, vsn_ref, kwin_ref, vwin_ref, kwn_ref, vwn_ref,
                        oc_ref, gate_ref, path_ref, tab_ref, *rest, qpos, hd, n_cached, n_pick, kvh):
    ksel_refs, vsel_refs, o_ref = rest[:n_pick], rest[n_pick:2 * n_pick], rest[2 * n_pick]
    b, h = pl.program_id(0), pl.program_id(1)
    G = NSA_G
    qg = (q_ref[0, 0] * hd ** -0.5).astype(BF16)
    qf = qg.astype(F32)
    tab = tab_ref[0]
    bias_new = _bias_rows(jnp.zeros((G, 1), jnp.int32), thr_ref, tab)

    def new_token_score(k_ref):
        return jnp.sum(qf * k_ref[0, 0].astype(BF16).astype(F32), axis=-1, keepdims=True) + bias_new

    def attend(k_tiles, v_tiles, dists, k_new_ref, v_new_ref, extra_valid):
        scores, valids = [], []
        for k_t, dist in zip(k_tiles, dists):
            row = lax.broadcasted_iota(jnp.int32, dist.shape, 1)
            valid = (row % kvh == h) & (dist >= 0) & extra_valid(dist)
            s = lax.dot_general(qg, k_t.astype(BF16), _NT, preferred_element_type=F32)
            scores.append(jnp.where(valid, s + _bias_rows(dist, thr_ref, tab), NEG_INF))
            valids.append(valid)
        s_new = new_token_score(k_new_ref)
        m = s_new
        for s in scores:
            m = jnp.maximum(m, jnp.max(s, axis=-1, keepdims=True))
        p_new = jnp.exp(s_new - m)
        l = p_new
        acc = p_new.astype(BF16).astype(F32) * v_new_ref[0, 0].astype(BF16).astype(F32)
        for s, valid, v_t in zip(scores, valids, v_tiles):
            p = jnp.where(valid, jnp.exp(s - m), 0.0)
            l = l + jnp.sum(p, axis=-1, keepdims=True)
            acc = acc + jnp.dot(p.astype(BF16), v_t.astype(BF16), preferred_element_type=F32)
        return acc / l

    rows = SEL_BLOCK * kvh
    tok = lax.broadcasted_iota(jnp.int32, (G, rows), 1) // kvh
    sel_dists = [qpos - (idx_ref[(b * kvh + h) * LANES + j] * SEL_BLOCK + tok) for j in range(n_pick)]
    o_sel = attend([r[...] for r in ksel_refs], [r[...] for r in vsel_refs], sel_dists, ksn_ref, vsn_ref,
                   lambda dist: dist >= 0)
    slot = lax.broadcasted_iota(jnp.int32, (G, n_cached * kvh), 1) // kvh
    wdist = n_cached - slot
    o_win = attend([kwin_ref[0]], [vwin_ref[0]], [wdist], kwn_ref, vwn_ref,
                   lambda dist: (dist < WINDOW) & (qpos - dist >= 0))
    gates = jax.nn.sigmoid(gate_ref[0, 0])
    o = gates[:, 0:1] * oc_ref[0, 0] + gates[:, 1:2] * o_sel + gates[:, 2:3] * o_win
    o_ref[0, 0] = o * jax.nn.silu(path_ref[0, 0])


def nsa_sample(q, kcmp, vcmp, pool_sk, pool_sv, page_table, new_kv, win_k, win_v, gate, path, table, past_len):
    B, KVH, G, hd = q.shape
    n_cb = kcmp.shape[1]
    n_pool, page = pool_sk.shape[:2]
    NP = page_table.shape[1]
    n_cached = win_k.shape[1]
    qpos = past_len
    n_sel = -(-(past_len + 1) // SEL_BLOCK)
    n_pick = min(N_SELECT, n_sel) - 1
    assert past_len % SEL_BLOCK == 0 and n_cb * CMP_BLOCK == past_len and 1 <= n_pick <= LANES
    bucket = rel_bucket(jnp.arange(past_len + 1, dtype=jnp.int32))
    thr = jnp.sum(bucket[None, :] < jnp.arange(NUM_BUCKETS, dtype=jnp.int32)[:, None], axis=1).astype(jnp.int32)
    tab = table.astype(F32).T.reshape(KVH, G, NUM_BUCKETS)
    head = lambda shape: pl.BlockSpec((1, 1) + shape, lambda b, h, *_: (b, h, 0, 0))
    cmp_spec = pl.BlockSpec((1, n_cb, hd), lambda b, h, *_: (b, 0, h))
    tab_spec = pl.BlockSpec((1, G, NUM_BUCKETS), lambda b, h, *_: (h, 0, 0))
    oc, idx = pl.pallas_call(
        functools.partial(_sample_cmp_kernel, qpos=qpos, n_pick=n_pick, hd=hd),
        grid_spec=pltpu.PrefetchScalarGridSpec(
            num_scalar_prefetch=1,
            grid=(B, KVH),
            in_specs=[head((G, hd)), cmp_spec, cmp_spec, tab_spec],
            out_specs=[head((G, hd)), head((1, LANES))],
        ),
        out_shape=[jax.ShapeDtypeStruct((B, KVH, G, hd), F32), jax.ShapeDtypeStruct((B, KVH, 1, LANES), jnp.int32)],
        compiler_params=_params("parallel", "parallel"),
        name="nsa_sample_cmp",
    )(thr, q, kcmp, vcmp, tab)

    halves = page // SEL_BLOCK
    rows = SEL_BLOCK * KVH

    def sel_spec(j):
        def index(b, h, idx_ref, pt_ref, thr_ref):
            blk = idx_ref[(b * KVH + h) * LANES + j]
            return (pt_ref[b * NP + blk // halves] * halves + blk % halves, 0)
        return pl.BlockSpec((rows, hd), index)

    win_spec = pl.BlockSpec((1, n_cached * KVH, hd), lambda b, h, *_: (b, 0, 0))
    new_spec = pl.BlockSpec((1, 1, 1, hd), lambda b, h, *_: (b, h, 0, 0))
    ks_n, vs_n, kw_n, vw_n = new_kv
    sel_k = pool_sk.reshape(n_pool * page * KVH, hd)
    sel_v = pool_sv.reshape(n_pool * page * KVH, hd)
    return pl.pallas_call(
        functools.partial(_sample_attn_kernel, qpos=qpos, hd=hd, n_cached=n_cached, n_pick=n_pick, kvh=KVH),
        grid_spec=pltpu.PrefetchScalarGridSpec(
            num_scalar_prefetch=3,
            grid=(B, KVH),
            in_specs=[head((G, hd)), new_spec, new_spec, win_spec, win_spec, new_spec, new_spec,
                      head((G, hd)), head((G, 3)), head((G, hd)), tab_spec]
                     + [sel_spec(j) for j in range(n_pick)] * 2,
            out_specs=head((G, hd)),
        ),
        out_shape=jax.ShapeDtypeStruct((B, KVH, G, hd), F32),
        compiler_params=_params("parallel", "parallel"),
        name="nsa_sample_attn",
    )(idx.reshape(-1), page_table.reshape(-1), thr, q, ks_n, vs_n, win_k.reshape(B, n_cached * KVH, hd),
      win_v.reshape(B, n_cached * KVH, hd), kw_n, vw_n, oc, gate, path, tab,
      *([sel_k] * n_pick), *([sel_v] * n_pick))


def rel_bucket(dist):
    d = jnp.maximum(dist, 0)
    me = NUM_BUCKETS // 2
    logd = jnp.log(jnp.maximum(d, 1).astype(F32) / me)
    large = me + (logd / math.log(MAX_DISTANCE / me) * (NUM_BUCKETS - me)).astype(jnp.int32)
    return jnp.where(d < me, d, jnp.minimum(large, NUM_BUCKETS - 1))


SAMPLE_ROWS = 16
MATMUL_TN = 512


class EvenCols:
    def __init__(self, hg_qk, hg_w, nsa_w, nsa_kvw):
        self.hd = nsa_w // NSA_HEADS
        self.kvw = nsa_kvw
        self.nsa_w = nsa_w
        self.q, self.f, self.v, self.g = 0, hg_qk, 2 * hg_qk, 2 * hg_qk + hg_w
        self.nq = 2 * hg_qk + 2 * hg_w
        self.kv = self.nq + nsa_w
        self.main_w = self.kv + 6 * nsa_kvw
        self.gate_w = 3 * NSA_G
        self.tail_path, self.tail_gate = 0, nsa_w
        used = nsa_w + NSA_KVH * LANES
        self.tail_w = -(-used // MATMUL_TN) * MATMUL_TN
        assert self.main_w % MATMUL_TN == 0

    def tail_weight(self, w_in_t):
        gate0 = self.main_w
        path0 = gate0 + NSA_KVH * self.gate_w
        parts = [w_in_t[path0:path0 + self.nsa_w]]
        for h in range(NSA_KVH):
            parts.append(jnp.pad(w_in_t[gate0 + h * self.gate_w:gate0 + (h + 1) * self.gate_w],
                                 ((0, LANES - self.gate_w), (0, 0))))
        w = jnp.concatenate(parts, axis=0)
        return jnp.pad(w, ((0, self.tail_w - w.shape[0]), (0, 0)))


def even_prompt(main, tail, B, T, lb, hg_gain, pe, w1k, w2k, w1v, w2v, table, ec):
    hd = ec.hd
    S0 = jnp.zeros((B, HG_HEADS, HG_DK, hg_gain.shape[0] // HG_HEADS), F32)
    y_hg, S = hgrn_mix(main, lb, hg_gain, S0, B, T, (ec.q, ec.f, ec.v, ec.g), HG_CHUNK, HGRN_SUB, HG_CHUNK)
    rows = kv_split(main, ec.kv, 6, NSA_KVH, hd)
    kcmp = compress_blocks(rows[0], B, T, NSA_KVH, pe, w1k, w2k)
    vcmp = compress_blocks(rows[1], B, T, NSA_KVH, pe, w1v, w2v)
    kc, vc, ks, vs, kw, vw = [r.reshape(B, T, NSA_KVH, hd) for r in rows]
    y_nsa = nsa_prompt(main, tail, kcmp, vcmp, table, B, T, (ec.nq, ec.kv, ec.tail_path, ec.tail_gate, hd))
    wb = min(WINDOW, T)
    return jnp.concatenate([y_hg, y_nsa], axis=-1), S, kc, vc, ks, vs, kw[:, T - wb:], vw[:, T - wb:]


def even_sample(main, tail, B, T, past_len, S0, page_table, pool_ck, pool_cv, pool_sk, pool_sv, win_k, win_v,
                lb, hg_gain, pe, w1k, w2k, w1v, w2v, table, ec):
    hd = ec.hd
    R = SAMPLE_ROWS
    y_hg, S = hgrn_mix(main, lb, hg_gain, S0.astype(F32), B, R, (ec.q, ec.f, ec.v, ec.g), R, R, T)
    m3 = main.reshape(B, R, -1)[:, :T]
    t3 = tail.reshape(B, R, -1)[:, :T]
    assert T == 1
    heads = lambda a, w: a.reshape(B, NSA_KVH, NSA_G, w)
    q = heads(m3[..., ec.nq:ec.kv], hd)
    path = heads(t3[..., ec.tail_path:ec.tail_path + ec.nsa_w], hd)
    gate = jnp.stack([t3[:, 0, ec.tail_gate + h * LANES:ec.tail_gate + h * LANES + ec.gate_w].reshape(B, NSA_G, 3)
                      for h in range(NSA_KVH)], axis=1)
    kvs = m3[..., ec.kv:ec.main_w].reshape(B, T, 6, NSA_KVH, hd)
    kc, vc, ks, vs, kw, vw = [kvs[:, :, j] for j in range(6)]
    new_kv = tuple(jnp.swapaxes(a, 1, 2) for a in (ks, vs, kw, vw))
    kcmp = compress_paged(pool_ck, page_table, pe, w1k, w2k)
    vcmp = compress_paged(pool_cv, page_table, pe, w1v, w2v)
    y_nsa = nsa_sample(q, kcmp, vcmp, pool_sk, pool_sv, page_table, new_kv, win_k, win_v, gate, path, table,
                       past_len)
    y_nsa = jnp.pad(y_nsa.reshape(B, T, -1).astype(BF16), ((0, 0), (0, R - T), (0, 0))).reshape(B * R, -1)
    kw_all = jnp.concatenate([win_k, kw], axis=1)
    vw_all = jnp.concatenate([win_v, vw], axis=1)
    return jnp.concatenate([y_hg, y_nsa], axis=-1), S, kc, vc, ks, vs, kw_all[:, T:], vw_all[:, T:]


def mem_attend(qg, B, T, mk, mv):
    slots = mk.shape[1]
    return mem_mix(qg, mk.reshape(B, slots, -1), mv.reshape(B, slots, -1), B, T)


def kernel(x_prompt, x_sample, mem_prompt, state_hgrn, cache_cmp_k, cache_cmp_v, cache_sel_k, cache_sel_v,
           cache_win_k, cache_win_v, state_ret, cache_mem_k, cache_mem_v, page_table, rel_table,
           norm_mix_pre, norm_mix_post, norm_mem_pre, norm_mem_post, ev_w_in, ev_w_out, hgrn_lb, hgrn_norm,
           cmp_pe, cmp_w1_k, cmp_w2_k, cmp_w1_v, cmp_w2_v, od_w_in, od_w_out, ret_norm,
           mem_w_q, mem_w_k, mem_w_v, mem_w_o):
    B, T, D = x_prompt.shape
    Bs, Ts, _ = x_sample.shape
    depth = norm_mix_pre.shape[0]
    past_len = page_table.shape[1] * PAGE_SIZE
    pos_p = jnp.arange(T, dtype=jnp.int32)
    pos_s = past_len + jnp.arange(Ts, dtype=jnp.int32)
    lb_all = jnp.cumsum(jax.nn.softmax(hgrn_lb.astype(F32), axis=0), axis=0)
    hg_qk = hgrn_lb.shape[1]
    hg_w = hgrn_norm.shape[1]
    nsa_w = ev_w_out.shape[1] - hg_w
    nsa_kvw = NSA_KVH * (nsa_w // NSA_HEADS)
    ec = EvenCols(hg_qk, hg_w, nsa_w, nsa_kvw)
    ev_w_in_t = jnp.swapaxes(ev_w_in, 1, 2)
    R = SAMPLE_ROWS
    assert Ts <= R

    even_p, even_s, ret_p, ret_s, memk_p, memv_p = [], [], [], [], [], []
    xp = x_prompt.reshape(B * T, D)
    xs = jnp.pad(x_sample, ((0, 0), (0, R - Ts), (0, 0))).reshape(Bs * R, D)
    pos_sr = past_len + jnp.arange(R, dtype=jnp.int32)
    mem2 = mem_prompt.reshape(-1, D).astype(BF16)
    ML = mem_prompt.shape[1]
    hp = rmsnorm_bf16(xp, norm_mix_pre[0])
    hs = rmsnorm_bf16(xs, norm_mix_pre[0])
    for l in range(depth):
        if l % 2 == 0:
            e = l // 2
            w = (lb_all[e], hgrn_norm[e], cmp_pe[e], cmp_w1_k[e], cmp_w2_k[e], cmp_w1_v[e], cmp_w2_v[e], rel_table)
            main_p, main_s = matmul(hp, ev_w_in_t, e, n_cols=ec.main_w, transposed=True, xs=hs)
            tail_p, tail_s = matmul(hp, ec.tail_weight(ev_w_in_t[e]), transposed=True, xs=hs)
            ap, *sp = even_prompt(main_p, tail_p, B, T, *w, ec)
            as_, *ss = even_sample(main_s, tail_s, Bs, Ts, past_len, state_hgrn[e], page_table, cache_cmp_k[e],
                                   cache_cmp_v[e], cache_sel_k[e], cache_sel_v[e], cache_win_k[e], cache_win_v[e],
                                   *w, ec)
            yp, ys = matmul(ap, ev_w_out, e, xs=as_)
            even_p.append(sp)
            even_s.append(ss)
        else:
            o = l // 2
            S0p = jnp.zeros((B,) + state_ret.shape[2:], F32)
            proj_p, proj_s = matmul(hp, od_w_in, o, xs=hs)
            ap, sp = retention_mix(proj_p, pos_p, ret_norm[o], S0p, B, T, RET_CHUNK, RET_CHUNK)
            as_, ss = retention_mix(proj_s, pos_sr, ret_norm[o], state_ret[o].astype(F32), Bs, R, R, Ts)
            yp, ys = matmul(ap, od_w_out, o, xs=as_)
            ret_p.append(sp)
            ret_s.append(ss)
        xp, hp = residual_post_pre(xp, yp, norm_mix_post[l], norm_mem_pre[l])
        xs, hs = residual_post_pre(xs, ys, norm_mix_post[l], norm_mem_pre[l])
        mk_p = matmul(mem2, mem_w_k, l).reshape(B, ML, MEM_HEADS, MEM_HD)
        mv_p = matmul(mem2, mem_w_v, l).reshape(B, ML, MEM_HEADS, MEM_HD)
        memk_p.append(mk_p)
        memv_p.append(mv_p)
        qg_p, qg_s = matmul(hp, mem_w_q, l, xs=hs)
        g_next = norm_mix_pre[l + 1] if l + 1 < depth else None
        xp, hp = matmul_residual(mem_attend(qg_p, B, T, mk_p, mv_p), mem_w_o, l, xp, norm_mem_post[l], g_next)
        xs, hs = matmul_residual(mem_attend(qg_s, Bs, R, cache_mem_k[l], cache_mem_v[l]), mem_w_o, l, xs,
                                 norm_mem_post[l], g_next)
    p_hgrn, p_cmp_k, p_cmp_v, p_sel_k, p_sel_v, p_win_k, p_win_v = [jnp.stack(a) for a in zip(*even_p)]
    s_hgrn, s_cmp_k, s_cmp_v, s_sel_k, s_sel_v, s_win_k, s_win_v = [jnp.stack(a) for a in zip(*even_s)]
    p_ret = jnp.stack(ret_p)
    s_ret = jnp.stack(ret_s)
    p_mem_k = jnp.stack(memk_p)
    p_mem_v = jnp.stack(memv_p)
    return (xp.reshape(B, T, D), xs.reshape(Bs, R, D)[:, :Ts], p_hgrn, p_cmp_k, p_cmp_v, p_sel_k, p_sel_v,
            p_win_k, p_win_v, p_ret, p_mem_k, p_mem_v,
            s_hgrn, s_cmp_k, s_cmp_v, s_sel_k, s_sel_v, s_win_k, s_win_v, s_ret)
```

```python
import functools
import math

import jax
import jax.numpy as jnp
import numpy as np
from jax import lax
from jax.experimental import pallas as pl
from jax.experimental.pallas import tpu as pltpu

F32 = jnp.float32
BF16 = jnp.bfloat16
EPS = 1e-6
NEG_INF = -1e30
LOG2E = math.log2(math.e)

PAGE_SIZE = 128
HG_HEADS = 16
HG_DK = 128
HG_CHUNK = 64
HGRN_SUB = 16
NSA_HEADS = 16
NSA_KVH = 2
NSA_G = NSA_HEADS // NSA_KVH
CMP_BLOCK = 32
SEL_BLOCK = 64
N_SELECT = 16
WINDOW = 512
SEL_QBLOCK = 64
WIN_QBLOCK = 128
NUM_BUCKETS = 32
MAX_DISTANCE = 1024
RET_HEADS = 16
RET_CHUNK = 128
ROPE_BASE = 10000.0
MEM_HEADS = 4
MEM_HD = 128
MEM_W = MEM_HEADS * MEM_HD

VMEM_LIMIT_BYTES = 56 * 1024 * 1024


def _params(*sem):
    return pltpu.CompilerParams(dimension_semantics=sem, vmem_limit_bytes=VMEM_LIMIT_BYTES)


def _rmsnorm_kernel(x_ref, g_ref, o_ref):
    x = x_ref[...]
    y = x * lax.rsqrt(jnp.mean(x * x, axis=-1, keepdims=True) + EPS)
    o_ref[...] = (y * g_ref[...]).astype(o_ref.dtype)


def rmsnorm_bf16(x, g):
    M, D = x.shape
    tm = min(M, 512)
    return pl.pallas_call(
        _rmsnorm_kernel,
        grid=(M // tm,),
        in_specs=[pl.BlockSpec((tm, D), lambda i: (i, 0)), pl.BlockSpec((1, D), lambda i: (0, 0))],
        out_specs=pl.BlockSpec((tm, D), lambda i: (i, 0)),
        out_shape=jax.ShapeDtypeStruct((M, D), BF16),
        compiler_params=_params("parallel"),
        name="rmsnorm_bf16",
    )(x, g.reshape(1, D))


def _post_pre_kernel(x_ref, y_ref, g_ref, gn_ref, o_ref, h_ref):
    y = y_ref[...]
    x = x_ref[...] + y * lax.rsqrt(jnp.mean(y * y, axis=-1, keepdims=True) + EPS) * g_ref[...]
    o_ref[...] = x
    h_ref[...] = (x * lax.rsqrt(jnp.mean(x * x, axis=-1, keepdims=True) + EPS) * gn_ref[...]).astype(h_ref.dtype)


def residual_post_pre(x, y, g, g_next):
    M, D = x.shape
    tm = min(M, 256)
    row = pl.BlockSpec((tm, D), lambda i: (i, 0))
    vec = pl.BlockSpec((1, D), lambda i: (0, 0))
    return pl.pallas_call(
        _post_pre_kernel,
        grid=(M // tm,),
        in_specs=[row, row, vec, vec],
        out_specs=[row, row],
        out_shape=[jax.ShapeDtypeStruct((M, D), F32), jax.ShapeDtypeStruct((M, D), BF16)],
        compiler_params=_params("parallel"),
        name="residual_post_pre",
    )(x, y, g.reshape(1, D), g_next.reshape(1, D))


def _matmul_post_kernel(*refs, has_next):
    if has_next:
        a_ref, w_ref, x_ref, g_ref, gn_ref, o_ref, h_ref, wb_ref = refs
    else:
        a_ref, w_ref, x_ref, g_ref, o_ref, wb_ref = refs

    @pl.when(pl.program_id(0) == 0)
    def _():
        wb_ref[...] = w_ref[...].astype(BF16)

    y = jnp.dot(a_ref[...], wb_ref[...], preferred_element_type=F32)
    x = x_ref[...] + y * lax.rsqrt(jnp.mean(y * y, axis=-1, keepdims=True) + EPS) * g_ref[...]
    o_ref[...] = x
    if has_next:
        h_ref[...] = (x * lax.rsqrt(jnp.mean(x * x, axis=-1, keepdims=True) + EPS) * gn_ref[...]).astype(h_ref.dtype)


def matmul_residual(a, w, layer, x, g, g_next=None):
    M, K = a.shape
    D = w.shape[-1]
    tm = min(M, 256)
    has_next = g_next is not None
    row = lambda width: pl.BlockSpec((tm, width), lambda i: (i, 0))
    vec = pl.BlockSpec((1, D), lambda i: (0, 0))
    in_specs = [row(K), pl.BlockSpec((None, K, D), lambda i: (layer, 0, 0)), row(D), vec]
    args = [a, w, x, g.reshape(1, D)]
    out_specs = [row(D)]
    out_shape = [jax.ShapeDtypeStruct((M, D), F32)]
    if has_next:
        in_specs.append(vec)
        args.append(g_next.reshape(1, D))
        out_specs.append(row(D))
        out_shape.append(jax.ShapeDtypeStruct((M, D), BF16))
    out = pl.pallas_call(
        functools.partial(_matmul_post_kernel, has_next=has_next),
        grid=(M // tm,),
        in_specs=in_specs,
        out_specs=out_specs,
        out_shape=out_shape,
        scratch_shapes=[pltpu.VMEM((K, D), BF16)],
        compiler_params=_params("arbitrary"),
        name="matmul_residual",
    )(*args)
    return out if has_next else (out[0], None)


def _matmul_kernel(*refs, transposed, paired):
    if paired:
        x_ref, xs_ref, w_ref, o_ref, os_ref, wb_ref = refs
    else:
        x_ref, w_ref, o_ref, wb_ref = refs

    @pl.when(pl.program_id(1) == 0)
    def _():
        w = w_ref[...]
        wb_ref[...] = (w.T if transposed else w).astype(BF16)
        if paired:
            os_ref[...] = jnp.dot(xs_ref[...], wb_ref[...], preferred_element_type=F32)

    o_ref[...] = jnp.dot(x_ref[...], wb_ref[...], preferred_element_type=F32)


def matmul(x, w, layer=None, n_cols=None, transposed=False, xs=None):
    M, K = x.shape
    N = (w.shape[-2] if transposed else w.shape[-1]) if n_cols is None else n_cols
    tn = 512 if K <= 4096 else 256
    tn = min(tn, N)
    tm = min(M, 1024 if K <= 4096 else 512)
    assert N % tn == 0 and M % tm == 0, (M, K, N)
    blk = (tn, K) if transposed else (K, tn)
    pick = (lambda n: (n, 0)) if transposed else (lambda n: (0, n))
    if w.ndim == 3:
        w_spec = pl.BlockSpec((None,) + blk, lambda n, m: (layer,) + pick(n))
    else:
        w_spec = pl.BlockSpec(blk, lambda n, m: pick(n))
    in_specs = [pl.BlockSpec((tm, K), lambda n, m: (m, 0))]
    out_specs = [pl.BlockSpec((tm, tn), lambda n, m: (m, n))]
    out_shape = [jax.ShapeDtypeStruct((M, N), F32)]
    args = [x]
    if xs is not None:
        Ms = xs.shape[0]
        in_specs.append(pl.BlockSpec((Ms, K), lambda n, m: (0, 0)))
        out_specs.append(pl.BlockSpec((Ms, tn), lambda n, m: (0, n)))
        out_shape.append(jax.ShapeDtypeStruct((Ms, N), F32))
        args.append(xs)
    out = pl.pallas_call(
        functools.partial(_matmul_kernel, transposed=transposed, paired=xs is not None),
        grid=(N // tn, M // tm),
        in_specs=in_specs + [w_spec],
        out_specs=out_specs,
        out_shape=out_shape,
        scratch_shapes=[pltpu.VMEM((K, tn), BF16)],
        compiler_params=_params("parallel", "arbitrary"),
        name="matmul",
    )(*args, w)
    return out if xs is not None else out[0]


LANES = 128
NSA_TQ = 128
NSA_KC = 256
_NT = (((1,), (1,)), ((), ()))


_TN = (((0,), (0,)), ((), ()))


def _bias_tile_kernel(thr_ref, table_ref, o_ref, *, row_stride, row_offset):
    kvh = pl.program_id(0)
    step = pl.program_id(1)
    rows = o_ref.shape[2]
    row = lax.broadcasted_iota(jnp.int32, (rows, LANES), 0)
    lane = lax.broadcasted_iota(jnp.int32, (rows, LANES), 1)
    dist = LANES * step + lane - row_stride * row - row_offset
    for g in range(NSA_G):
        h = kvh * NSA_G + g
        bias = jnp.full((rows, LANES), table_ref[0, h], F32)
        for k in range(1, NUM_BUCKETS):
            bias = jnp.where(dist >= thr_ref[k], table_ref[k, h], bias)
        o_ref[0, 0, :, g * LANES:(g + 1) * LANES] = bias


def bias_tiles(table, n_steps, rows, max_dist, row_stride, row_offset):
    bucket = rel_bucket(jnp.arange(max_dist, dtype=jnp.int32))
    thr = jnp.sum(bucket[None, :] < jnp.arange(NUM_BUCKETS, dtype=jnp.int32)[:, None], axis=1).astype(jnp.int32)
    smem = pl.BlockSpec(memory_space=pltpu.SMEM)
    return pl.pallas_call(
        functools.partial(_bias_tile_kernel, row_stride=row_stride, row_offset=row_offset),
        grid=(NSA_KVH, n_steps),
        in_specs=[smem, smem],
        out_specs=pl.BlockSpec((1, 1, rows, NSA_G * LANES), lambda h, s: (h, s, 0, 0)),
        out_shape=jax.ShapeDtypeStruct((NSA_KVH, n_steps, rows, NSA_G * LANES), F32),
        compiler_params=_params("parallel", "parallel"),
        name="bias_tiles",
    )(thr, table)


def _nsa_prompt_kernel(q_ref, kc_ref, vc_ref, ks_ref, vs_ref, kw_ref, vw_ref, cb_ref, tb_ref, gate_ref, path_ref,
                       o_ref, qb_s, oc_s, m_s, l_s, acc_s, *, hd, n_win_chunks, n_pick):
    i = pl.program_id(2)
    tq = NSA_TQ
    G = NSA_G
    scale = hd ** -0.5 * LOG2E
    n_cb = kc_ref.shape[1]
    row_c = lax.broadcasted_iota(jnp.int32, (n_cb, tq), 0)
    qpos_c = i * tq + lax.broadcasted_iota(jnp.int32, (n_cb, tq), 1)
    row = lax.broadcasted_iota(jnp.int32, (NSA_KC, tq), 0)
    qpos = i * tq + lax.broadcasted_iota(jnp.int32, (NSA_KC, tq), 1)
    slab = lambda g: slice(g * tq, (g + 1) * tq)

    for g in range(G):
        qb_s[slab(g), :] = (q_ref[:, g * hd:(g + 1) * hd] * scale).astype(BF16)
    qb = qb_s[...]

    s_all = lax.dot_general(kc_ref[0].astype(BF16), qb, _NT, preferred_element_type=F32) + cb_ref[0, 0]
    valid_c = qpos_c >= (row_c + 1) * CMP_BLOCK - 1
    valid_cf = valid_c.astype(F32)
    imp = jnp.zeros((n_cb, tq), F32)
    ps = []
    for g in range(G):
        s = jnp.where(valid_c, s_all[:, slab(g)], NEG_INF)
        e = jnp.exp2(s - jnp.max(s, axis=0, keepdims=True))
        p = e * (1.0 / jnp.sum(e, axis=0, keepdims=True)) * valid_cf
        imp = imp + p
        ps.append(p.astype(BF16))
    oc_s[...] = lax.dot_general(vc_ref[0].astype(BF16), jnp.concatenate(ps, axis=1), _TN,
                                preferred_element_type=F32)

    imp2 = imp + pltpu.roll(imp, n_cb - 1, axis=0)
    blk = row_c >> 1
    cur = qpos_c // SEL_BLOCK
    forced = (blk == 0) | (blk == cur) | (blk == cur - 1)
    usable = ((row_c & 1) == 0) & (blk * SEL_BLOCK <= qpos_c)
    work = jnp.where(usable, jnp.where(forced, jnp.inf, imp2), -jnp.inf)
    sel = jnp.zeros((n_cb, tq), F32)
    for _ in range(n_pick):
        mx = jnp.max(work, axis=0, keepdims=True)
        first = jnp.min(jnp.where(work == mx, row_c, n_cb), axis=0, keepdims=True)
        pick = row_c == first
        sel = jnp.where(pick, 1.0, sel)
        work = jnp.where(pick, -jnp.inf, work)
    sel_b = sel.astype(BF16)

    m_s[...] = jnp.full(m_s.shape, NEG_INF, F32)
    l_s[...] = jnp.zeros(l_s.shape, F32)
    acc_s[...] = jnp.zeros(acc_s.shape, F32)
    ekey = lax.broadcasted_iota(jnp.int32, (NSA_KC, n_cb), 0)
    eblk = lax.broadcasted_iota(jnp.int32, (NSA_KC, n_cb), 1)
    tiles_per_chunk = NSA_KC // LANES

    def online(slot, k_b, v_b, bias, valid):
        s_all = lax.dot_general(k_b, qb, _NT, preferred_element_type=F32) + bias
        ps, alphas = [], []
        for g in range(G):
            s = jnp.where(valid, s_all[:, slab(g)], NEG_INF)
            m_old = m_s[slot, :, slab(g)]
            m_new = jnp.maximum(m_old, jnp.max(s, axis=0, keepdims=True))
            alpha = jnp.exp2(m_old - m_new)
            p = jnp.exp2(s - m_new)
            l_s[slot, :, slab(g)] = alpha * l_s[slot, :, slab(g)] + jnp.sum(p, axis=0, keepdims=True)
            m_s[slot, :, slab(g)] = m_new
            ps.append(p.astype(BF16))
            alphas.append(alpha)
        pv = lax.dot_general(v_b, jnp.concatenate(ps, axis=1), _TN, preferred_element_type=F32)
        acc_s[slot] = jnp.concatenate(alphas, axis=1) * acc_s[slot] + pv

    def chunk(c, carry):
        k0 = pl.multiple_of(c * NSA_KC, NSA_KC)
        delta = i - tiles_per_chunk * c
        dist = qpos - (k0 + row)
        causal = dist >= 0
        expand = (eblk == 2 * (c * (NSA_KC // SEL_BLOCK) + (ekey // SEL_BLOCK))).astype(BF16)
        chosen = jnp.dot(expand, sel_b, preferred_element_type=F32) > 0.5
        bias = jnp.concatenate([tb_ref[0, jnp.maximum(delta - t, 0)] for t in range(tiles_per_chunk)], axis=0)
        online(0, ks_ref[pl.ds(k0, NSA_KC), :].astype(BF16), vs_ref[pl.ds(k0, NSA_KC), :].astype(BF16), bias,
               chosen & causal)

        @pl.when(delta < n_win_chunks + tiles_per_chunk - 1)
        def _():
            online(1, kw_ref[pl.ds(k0, NSA_KC), :].astype(BF16), vw_ref[pl.ds(k0, NSA_KC), :].astype(BF16), bias,
                   causal & (dist < WINDOW))

        return carry

    lax.fori_loop(0, i // tiles_per_chunk + 1, chunk, 0)

    gates = jax.nn.sigmoid(gate_ref[...]).T
    for g in range(G):
        o_sel = acc_s[0, :, slab(g)] * (1.0 / l_s[0, :, slab(g)])
        o_win = acc_s[1, :, slab(g)] * (1.0 / l_s[1, :, slab(g)])
        o = (gates[3 * g:3 * g + 1] * oc_s[:, slab(g)] + gates[3 * g + 1:3 * g + 2] * o_sel
             + gates[3 * g + 2:3 * g + 3] * o_win)
        o_ref[:, g * hd:(g + 1) * hd] = (o.T * jax.nn.silu(path_ref[:, g * hd:(g + 1) * hd])).astype(o_ref.dtype)


def nsa_prompt(main, tail, kcmp, vcmp, table, B, T, cols):
    q_col, kv_col, path_col, gate_col, hd = cols
    G = NSA_G
    tq = NSA_TQ
    n_cb = kcmp.shape[1]
    assert n_cb % 8 == 0 and tq == LANES and T % NSA_KC == 0 and NSA_KC % LANES == 0 and hd == LANES
    nd = T // LANES
    n_win_chunks = WINDOW // LANES + 1
    table2 = table.astype(F32) * LOG2E
    tb = bias_tiles(table2, nd, LANES, T, 1, 0)
    cb = bias_tiles(table2, T // tq, n_cb, T, CMP_BLOCK, CMP_BLOCK - 1)
    nblk = T // tq
    gw = G * hd
    kv_spec = lambda j: pl.BlockSpec((T, hd), lambda b, h, i: (b, kv_col // hd + 2 * j + h))
    return pl.pallas_call(
        functools.partial(_nsa_prompt_kernel, hd=hd, n_win_chunks=n_win_chunks,
                          n_pick=min(N_SELECT, -(-T // SEL_BLOCK))),
        grid=(B, NSA_KVH, nblk),
        in_specs=[
            pl.BlockSpec((tq, gw), lambda b, h, i: (b * nblk + i, q_col // gw + h)),
            pl.BlockSpec((1, n_cb, hd), lambda b, h, i: (b, 0, h)),
            pl.BlockSpec((1, n_cb, hd), lambda b, h, i: (b, 0, h)),
            kv_spec(2), kv_spec(3), kv_spec(4), kv_spec(5),
            pl.BlockSpec((1, 1, n_cb, gw), lambda b, h, i: (h, i, 0, 0)),
            pl.BlockSpec((1, nd, LANES, gw), lambda b, h, i: (h, 0, 0, 0)),
            pl.BlockSpec((tq, LANES), lambda b, h, i: (b * nblk + i, gate_col // LANES + h)),
            pl.BlockSpec((tq, gw), lambda b, h, i: (b * nblk + i, path_col // gw + h)),
        ],
        out_specs=pl.BlockSpec((tq, gw), lambda b, h, i: (b * nblk + i, h)),
        out_shape=jax.ShapeDtypeStruct((B * T, NSA_KVH * gw), BF16),
        scratch_shapes=[pltpu.VMEM((G * tq, hd), BF16), pltpu.VMEM((hd, G * tq), F32),
                        pltpu.VMEM((2, 1, G * tq), F32), pltpu.VMEM((2, 1, G * tq), F32),
                        pltpu.VMEM((2, hd, G * tq), F32)],
        compiler_params=_params("parallel", "parallel", "arbitrary"),
        name="nsa_prompt",
    )(main, kcmp, vcmp, main, main, main, main, cb, tb, tail, tail)


_TN = (((0,), (0,)), ((), ()))


def _cumsum_rows(x, n):
    row = lax.broadcasted_iota(jnp.int32, x.shape, 0)
    sh = 1
    while sh < n:
        x = x + jnp.where(row >= sh, pltpu.roll(x, sh, axis=0), 0.0)
        sh *= 2
    return x


def _hgrn_kernel(q_ref, f_ref, v_ref, gate_ref, lb_ref, gain_ref, s0_ref, y_ref, s_out_ref, st_s,
                 *, C, SB, c_eff, dk, HB):
    c = pl.program_id(2)

    @pl.when(c == 0)
    def _():
        for hh in range(HB):
            st_s[hh] = s0_ref[0, hh].T

    states = [st_s[hh] for hh in range(HB)]
    results = [_hgrn_head(slice(hh * dk, (hh + 1) * dk), states[hh], q_ref, f_ref, v_ref, gate_ref, lb_ref,
                          gain_ref, C=C, SB=SB, c_eff=c_eff, dk=dk) for hh in range(HB)]
    y_ref[...] = jnp.concatenate([y for y, _ in results], axis=1)
    for hh in range(HB):
        st_s[hh] = results[hh][1]

    @pl.when(c == pl.num_programs(2) - 1)
    def _():
        for hh in range(HB):
            s_out_ref[0, hh] = st_s[hh].T


def _hgrn_head(sl, st, q_ref, f_ref, v_ref, gate_ref, lb_ref, gain_ref, *, C, SB, c_eff, dk):
    lb = lb_ref[:, sl]
    f = lb + (1.0 - lb) * jax.nn.sigmoid(f_ref[:, sl])
    logf = jnp.log(f)
    k = 1.0 - f
    if c_eff < C:
        real = lax.broadcasted_iota(jnp.int32, (C, dk), 0) < c_eff
        logf = jnp.where(real, logf, 0.0)
        k = jnp.where(real, k, 0.0)
    q = q_ref[:, sl] * dk ** -0.5
    v_b = v_ref[:, sl].astype(BF16)
    b = _cumsum_rows(logf, C)
    o = lax.dot_general((q * jnp.exp(b)).astype(BF16), st.astype(BF16), _NT, preferred_element_type=F32)

    lane_c = lax.broadcasted_iota(jnp.int32, (SB, C), 1)
    row_c = lax.broadcasted_iota(jnp.int32, (SB, C), 0)
    outs = []
    for I in range(C // SB):
        r0 = I * SB
        q_i = q[r0:r0 + SB]
        b_i = b[r0:r0 + SB]
        if I > 0:
            b_r = b[r0 - 1:r0]
            qq = (q_i * jnp.exp(b_i - b_r)).astype(BF16)
            kk = (k * jnp.exp(jnp.minimum(b_r - b, 0.0))).astype(BF16)
            a = lax.dot_general(qq, kk, _NT, preferred_element_type=F32)
            a = jnp.where(lane_c < r0, a, 0.0)
        else:
            a = jnp.zeros((SB, C), F32)
        for s in range(SB):
            z = q_i * k[r0 + s:r0 + s + 1] * jnp.exp(jnp.minimum(b_i - b[r0 + s:r0 + s + 1], 0.0))
            col = jnp.sum(z, axis=-1, keepdims=True)
            a = jnp.where((lane_c == r0 + s) & (row_c >= s), col, a)
        outs.append(jnp.dot(a.astype(BF16), v_b, preferred_element_type=F32))
    o = o + jnp.concatenate(outs, axis=0)

    b_last = b[C - 1:C]
    kd = (k * jnp.exp(b_last - b)).astype(BF16)
    st_new = jnp.exp(b_last) * st + lax.dot_general(v_b, kd, _TN, preferred_element_type=F32)
    y = o * lax.rsqrt(jnp.mean(o * o, axis=-1, keepdims=True) + EPS) * gain_ref[:, sl]
    return (y * jax.nn.silu(gate_ref[:, sl])).astype(BF16), st_new


HGRN_HEADS_PER_STEP = 8


def hgrn_mix(main, lb, gain, s0, B, T, cols, C, SB, c_eff):
    q_col, f_col, v_col, g_col = cols
    _, H, dk, dv = s0.shape
    HB = HGRN_HEADS_PER_STEP
    W = HB * LANES
    assert dk == LANES and dv == LANES and T % C == 0 and C % SB == 0 and H % HB == 0
    assert all(off % W == 0 for off in cols)
    nc = T // C
    col = lambda off: pl.BlockSpec((C, W), lambda b, h, c: (b * nc + c, off // W + h))
    vec = pl.BlockSpec((1, W), lambda b, h, c: (0, h))
    st = pl.BlockSpec((1, HB, dk, dv), lambda b, h, c: (b, h, 0, 0))
    return pl.pallas_call(
        functools.partial(_hgrn_kernel, C=C, SB=SB, c_eff=c_eff, dk=dk, HB=HB),
        grid=(B, H // HB, nc),
        in_specs=[col(q_col), col(f_col), col(v_col), col(g_col), vec, vec, st],
        out_specs=[pl.BlockSpec((C, W), lambda b, h, c: (b * nc + c, h)), st],
        out_shape=[jax.ShapeDtypeStruct((B * T, H * dv), BF16), jax.ShapeDtypeStruct(s0.shape, F32)],
        scratch_shapes=[pltpu.VMEM((HB, dv, dk), F32)],
        compiler_params=_params("parallel", "parallel", "arbitrary"),
        name="hgrn_mix",
    )(main, main, main, main, lb.reshape(1, -1), gain.reshape(1, -1), s0)


def _retention_kernel(q_ref, k_ref, v_ref, g_ref, cos_ref, sin_ref, lg_ref, gain_ref, s0_ref, y_ref, s_out_ref, s_s,
                      *, C, c_eff, dk, dv, HB):
    c = pl.program_id(2)

    @pl.when(c == 0)
    def _():
        s_s[...] = s0_ref[0]

    for hh in range(HB):
        _retention_head(hh, q_ref, k_ref, v_ref, g_ref, cos_ref, sin_ref, lg_ref, gain_ref, y_ref, s_s,
                        C=C, c_eff=c_eff, dk=dk, dv=dv)

    @pl.when(c == pl.num_programs(2) - 1)
    def _():
        s_out_ref[0] = s_s[...]


def _retention_head(hh, q_ref, k_ref, v_ref, g_ref, cos_ref, sin_ref, lg_ref, gain_ref, y_ref, s_s,
                    *, C, c_eff, dk, dv):
    half = dk // 2
    vsl = slice(hh * dv, (hh + 1) * dv)
    cos = cos_ref[...]
    sin = sin_ref[...]
    lg_w = lg_ref[hh]
    lg = lg_w[:, :LANES]
    row = lax.broadcasted_iota(jnp.int32, (C, LANES), 0).astype(F32)

    def rot(ref, w):
        x1 = ref[:, hh * dk:hh * dk + half]
        x2 = ref[:, hh * dk + half:(hh + 1) * dk]
        return jnp.concatenate([(x1 * cos - x2 * sin) * w, (x1 * sin + x2 * cos) * w], axis=1)

    q = rot(q_ref, dk ** -0.5)
    k = rot(k_ref, 1.0)
    v_b = v_ref[:, vsl].astype(BF16)
    s_old = s_s[hh]

    a = lax.dot_general(q.astype(BF16), k.astype(BF16), _NT, preferred_element_type=F32)
    ti = lax.broadcasted_iota(jnp.int32, (C, C), 0)
    si = lax.broadcasted_iota(jnp.int32, (C, C), 1)
    diff = (ti - si).astype(F32)
    a = a * jnp.where(diff >= 0, jnp.exp(diff * lg_w[:, :C]), 0.0)
    inner = jnp.dot(a.astype(BF16), v_b, preferred_element_type=F32)
    q_w = jnp.exp((row + 1.0) * lg)
    q_dec = q * jnp.concatenate([q_w] * (dk // LANES), axis=1)
    cross = jnp.dot(q_dec.astype(BF16), s_old.astype(BF16), preferred_element_type=F32)
    k_w = jnp.where(row < c_eff, jnp.exp((c_eff - 1.0 - row) * lg), 0.0)
    k_dec = k * jnp.concatenate([k_w] * (dk // LANES), axis=1)
    s_new = jnp.exp(c_eff * lg_w) * s_old + lax.dot_general(k_dec.astype(BF16), v_b, _TN, preferred_element_type=F32)
    s_s[hh] = s_new

    o = inner + cross
    cen = o - jnp.mean(o, axis=-1, keepdims=True)
    y = cen * lax.rsqrt(jnp.mean(cen * cen, axis=-1, keepdims=True) + EPS) * gain_ref[:, vsl]
    y_ref[:, vsl] = (y * jax.nn.silu(g_ref[:, vsl])).astype(y_ref.dtype)


RETENTION_HEADS_PER_STEP = 4


def retention_mix(proj, pos, gain, s0, B, T, C, c_eff):
    _, H, dk, dv = s0.shape
    HB = RETENTION_HEADS_PER_STEP
    assert T % C == 0 and C <= LANES and dk % LANES == 0 and H % HB == 0
    nc = T // C
    nh = H // HB
    half = dk // 2
    inv = ROPE_BASE ** (-jnp.arange(half, dtype=F32) / half)
    ang = pos.astype(F32)[:, None] * inv[None, :]
    log_gamma = jnp.log1p(-jnp.exp2(-5.0 - jnp.arange(H, dtype=F32)))
    lg = jnp.broadcast_to(log_gamma[:, None, None], (H, 1, dv))
    qk = lambda j: pl.BlockSpec((C, HB * dk), lambda b, h, c: (b * nc + c, j * nh + h))
    vg = lambda j: pl.BlockSpec((C, HB * dv), lambda b, h, c: (b * nc + c, (2 * H * dk) // (HB * dv) + j * nh + h))
    tab = pl.BlockSpec((C, half), lambda b, h, c: (c, 0))
    st = pl.BlockSpec((1, HB, dk, dv), lambda b, h, c: (b, h, 0, 0))
    return pl.pallas_call(
        functools.partial(_retention_kernel, C=C, c_eff=c_eff, dk=dk, dv=dv, HB=HB),
        grid=(B, nh, nc),
        in_specs=[qk(0), qk(1), vg(0), vg(1), tab, tab,
                  pl.BlockSpec((HB, 1, dv), lambda b, h, c: (h, 0, 0)),
                  pl.BlockSpec((1, HB * dv), lambda b, h, c: (0, h)), st],
        out_specs=[pl.BlockSpec((C, HB * dv), lambda b, h, c: (b * nc + c, h)), st],
        out_shape=[jax.ShapeDtypeStruct((B * T, H * dv), BF16), jax.ShapeDtypeStruct(s0.shape, F32)],
        scratch_shapes=[pltpu.VMEM((HB, dk, dv), F32)],
        compiler_params=_params("parallel", "parallel", "arbitrary"),
        name="retention_mix",
    )(proj, proj, proj, proj, jnp.cos(ang), jnp.sin(ang), lg, gain.reshape(1, -1), s0)


def _mem_kernel(qg_ref, mk_ref, mv_ref, o_ref, *, heads, hd):
    for h in range(heads):
        sl = slice(h * hd, (h + 1) * hd)
        q = (qg_ref[:, sl] * hd ** -0.5).astype(BF16)
        s = lax.dot_general(q, mk_ref[0, :, sl].astype(BF16), _NT, preferred_element_type=F32)
        e = jnp.exp(s - jnp.max(s, axis=-1, keepdims=True))
        p = e * (1.0 / jnp.sum(e, axis=-1, keepdims=True))
        o = jnp.dot(p.astype(BF16), mv_ref[0, :, sl].astype(BF16), preferred_element_type=F32)
        gate = qg_ref[:, heads * hd + h * hd:heads * hd + (h + 1) * hd]
        o_ref[:, sl] = (o * jax.nn.silu(gate)).astype(o_ref.dtype)


def mem_mix(qg, mk, mv, B, T):
    W = mk.shape[2]
    tq = min(T, 256)
    nb = T // tq
    kv = pl.BlockSpec((1, mk.shape[1], W), lambda b, i: (b, 0, 0))
    return pl.pallas_call(
        functools.partial(_mem_kernel, heads=MEM_HEADS, hd=W // MEM_HEADS),
        grid=(B, nb),
        in_specs=[pl.BlockSpec((tq, 2 * W), lambda b, i: (b * nb + i, 0)), kv, kv],
        out_specs=pl.BlockSpec((tq, W), lambda b, i: (b * nb + i, 0)),
        out_shape=jax.ShapeDtypeStruct((B * T, W), BF16),
        compiler_params=_params("parallel", "arbitrary"),
        name="mem_mix",
    )(qg, mk, mv)


def _compress_rows(x_ref, n_blocks, pe_ref, w1_ref, w2_ref, kvh):
    stride = CMP_BLOCK * kvh
    outs = []
    for h in range(kvh):
        acc = jnp.zeros((n_blocks, w1_ref.shape[2]), F32)
        for j in range(0, CMP_BLOCK, 2):
            xa = x_ref[pl.ds(j * kvh + h, n_blocks, stride=stride), :] + pe_ref[j:j + 1]
            xb = x_ref[pl.ds((j + 1) * kvh + h, n_blocks, stride=stride), :] + pe_ref[j + 1:j + 2]
            x2 = jnp.concatenate([xa, xb], axis=1).astype(BF16)
            acc = acc + jnp.dot(x2, w1_ref[j // 2], preferred_element_type=F32)
        outs.append(jnp.dot(jax.nn.silu(acc).astype(BF16), w2_ref[...], preferred_element_type=F32))
    return jnp.concatenate(outs, axis=1)


def _compress_kernel(x_ref, pe_ref, w1_ref, w2_ref, o_ref, *, kvh):
    o_ref[...] = _compress_rows(x_ref, o_ref.shape[0], pe_ref, w1_ref, w2_ref, kvh)


def _compress_paged_kernel(pt_ref, pool_ref, pe_ref, w1_ref, w2_ref, o_ref, x_s, sem, *, n_pages, rows_per_page,
                           kvh):
    b = pl.program_id(0)

    def page_copy(p):
        return pltpu.make_async_copy(pool_ref.at[pt_ref[b * n_pages + p]],
                                     x_s.at[pl.ds(p * rows_per_page, rows_per_page)], sem.at[0])

    def start(p, carry):
        page_copy(p).start()
        return carry

    def wait(p, carry):
        page_copy(p).wait()
        return carry

    lax.fori_loop(0, n_pages, start, 0)
    lax.fori_loop(0, n_pages, wait, 0)
    o_ref[0] = _compress_rows(x_s, o_ref.shape[1], pe_ref, w1_ref, w2_ref, kvh)


def _compress_weights(pe, w1, w2, hd):
    return pe.astype(F32), w1.reshape(CMP_BLOCK // 2, 2 * hd, w1.shape[1]).astype(BF16), w2.astype(BF16)


def _kv_split_kernel(*refs, n, kvh, hd):
    for x_ref, o_ref in zip(refs[:n], refs[n:]):
        for h in range(kvh):
            o_ref[pl.ds(h, x_ref.shape[0], stride=kvh), :] = x_ref[:, h * hd:(h + 1) * hd]


def kv_split(main, col, n, kvh, hd):
    M = main.shape[0]
    tm = min(M, 512)
    w = kvh * hd
    assert col % w == 0 and M % tm == 0
    return pl.pallas_call(
        functools.partial(_kv_split_kernel, n=n, kvh=kvh, hd=hd),
        grid=(M // tm,),
        in_specs=[pl.BlockSpec((tm, w), functools.partial(lambda i, j: (i, col // w + j), j=j)) for j in range(n)],
        out_specs=[pl.BlockSpec((tm * kvh, hd), lambda i: (i, 0))] * n,
        out_shape=[jax.ShapeDtypeStruct((M * kvh, hd), F32)] * n,
        compiler_params=_params("parallel"),
        name="kv_split",
    )(*([main] * n))


def compress_blocks(x, B, L, kvh, pe, w1, w2):
    hd = x.shape[1]
    n = B * (L // CMP_BLOCK)
    assert hd == LANES and L % CMP_BLOCK == 0
    peb, w1b, w2b = _compress_weights(pe, w1, w2, hd)
    full = lambda a: pl.BlockSpec(a.shape, lambda i: (0,) * a.ndim)
    out = pl.pallas_call(
        functools.partial(_compress_kernel, kvh=kvh),
        grid=(1,),
        in_specs=[full(x), full(peb), full(w1b), full(w2b)],
        out_specs=pl.BlockSpec((n, kvh * hd), lambda i: (0, 0)),
        out_shape=jax.ShapeDtypeStruct((n, kvh * hd), F32),
        compiler_params=_params("arbitrary"),
        name="compress_blocks",
    )(x, peb, w1b, w2b)
    return out.reshape(B, L // CMP_BLOCK, kvh * hd)


def compress_paged(pool, page_table, pe, w1, w2):
    n_pool, page, kvh, hd = pool.shape
    B, NP = page_table.shape
    rpp = page * kvh
    n_blocks = NP * page // CMP_BLOCK
    assert hd == LANES and page % CMP_BLOCK == 0
    peb, w1b, w2b = _compress_weights(pe, w1, w2, hd)
    full = lambda a: pl.BlockSpec(a.shape, lambda b, pt: (0,) * a.ndim)
    return pl.pallas_call(
        functools.partial(_compress_paged_kernel, n_pages=NP, rows_per_page=rpp, kvh=kvh),
        grid_spec=pltpu.PrefetchScalarGridSpec(
            num_scalar_prefetch=1,
            grid=(B,),
            in_specs=[pl.BlockSpec(memory_space=pl.ANY), full(peb), full(w1b), full(w2b)],
            out_specs=pl.BlockSpec((1, n_blocks, kvh * hd), lambda b, pt: (b, 0, 0)),
            scratch_shapes=[pltpu.VMEM((NP * rpp, hd), F32), pltpu.SemaphoreType.DMA((1,))],
        ),
        out_shape=jax.ShapeDtypeStruct((B, n_blocks, kvh * hd), F32),
        compiler_params=_params("arbitrary"),
        name="compress_paged",
    )(page_table.reshape(-1), pool.reshape(n_pool, rpp, hd), peb, w1b, w2b)


def _bias_rows(dist, thr_ref, tab):
    bias = jnp.broadcast_to(tab[:, 0:1], dist.shape)
    for k in range(1, NUM_BUCKETS):
        bias = jnp.where(dist >= thr_ref[k], tab[:, k:k + 1], bias)
    return bias


def _sample_cmp_kernel(thr_ref, q_ref, kc_ref, vc_ref, tab_ref, oc_ref, idx_ref, *, qpos, n_pick, hd):
    n_cb = kc_ref.shape[1]
    qg = (q_ref[0, 0] * hd ** -0.5).astype(BF16)
    lane = lax.broadcasted_iota(jnp.int32, (NSA_G, n_cb), 1)
    dist = qpos - ((lane + 1) * CMP_BLOCK - 1)
    valid = dist >= 0
    s = lax.dot_general(qg, kc_ref[0].astype(BF16), _NT, preferred_element_type=F32)
    s = jnp.where(valid, s + _bias_rows(dist, thr_ref, tab_ref[0]), NEG_INF)
    e = jnp.exp(s - jnp.max(s, axis=-1, keepdims=True))
    p = e / jnp.sum(e, axis=-1, keepdims=True) * valid.astype(F32)
    oc_ref[0, 0] = jnp.dot(p.astype(BF16), vc_ref[0].astype(BF16), preferred_element_type=F32)

    imp = jnp.sum(p, axis=0, keepdims=True)
    imp2 = imp + pltpu.roll(imp, n_cb - 1, axis=1)
    lane1 = lax.broadcasted_iota(jnp.int32, (1, n_cb), 1)
    blk = lane1 >> 1
    cur = qpos // SEL_BLOCK
    forced = (blk == 0) | (blk == cur) | (blk == cur - 1)
    usable = ((lane1 & 1) == 0) & (blk * SEL_BLOCK <= qpos)
    work = jnp.where(usable, jnp.where(forced, jnp.inf, imp2), -jnp.inf)
    out_lane = lax.broadcasted_iota(jnp.int32, (1, LANES), 1)
    picks = jnp.zeros((1, LANES), jnp.int32)
    for it in range(n_pick):
        mx = jnp.max(work, axis=-1, keepdims=True)
        first = jnp.min(jnp.where(work == mx, lane1, n_cb), axis=-1, keepdims=True)
        picks = jnp.where(out_lane == it, first >> 1, picks)
        work = jnp.where(lane1 == first, -jnp.inf, work)
    idx_ref[0, 0] = picks


def _sample_attn_kernel(idx_ref, pt_ref, thr_ref, q_ref, ksn_ref, vsn_ref, kwin_ref, vwin_ref, kwn_ref, vwn_ref,
                        oc_ref, gate_ref, path_ref, tab_ref, *rest, qpos, hd, n_cached, n_pick, kvh):
    ksel_refs, vsel_refs, o_ref = rest[:n_pick], rest[n_pick:2 * n_pick], rest[2 * n_pick]
    b, h = pl.program_id(0), pl.program_id(1)
    G = NSA_G
    qg = (q_ref[0, 0] * hd ** -0.5).astype(BF16)
    qf = qg.astype(F32)
    tab = tab_ref[0]
    bias_new = _bias_rows(jnp.zeros((G, 1), jnp.int32), thr_ref, tab)

    def new_token_score(k_ref):
        return jnp.sum(qf * k_ref[0, 0].astype(BF16).astype(F32), axis=-1, keepdims=True) + bias_new

    def attend(k_tiles, v_tiles, dists, k_new_ref, v_new_ref, extra_valid):
        scores, valids = [], []
        for k_t, dist in zip(k_tiles, dists):
            row = lax.broadcasted_iota(jnp.int32, dist.shape, 1)
            valid = (row % kvh == h) & (dist >= 0) & extra_valid(dist)
            s = lax.dot_general(qg, k_t.astype(BF16), _NT, preferred_element_type=F32)
            scores.append(jnp.where(valid, s + _bias_rows(dist, thr_ref, tab), NEG_INF))
            valids.append(valid)
        s_new = new_token_score(k_new_ref)
        m = s_new
        for s in scores:
            m = jnp.maximum(m, jnp.max(s, axis=-1, keepdims=True))
        p_new = jnp.exp(s_new - m)
        l = p_new
        acc = p_new.astype(BF16).astype(F32) * v_new_ref[0, 0].astype(BF16).astype(F32)
        for s, valid, v_t in zip(scores, valids, v_tiles):
            p = jnp.where(valid, jnp.exp(s - m), 0.0)
            l = l + jnp.sum(p, axis=-1, keepdims=True)
            acc = acc + jnp.dot(p.astype(BF16), v_t.astype(BF16), preferred_element_type=F32)
        return acc / l

    rows = SEL_BLOCK * kvh
    tok = lax.broadcasted_iota(jnp.int32, (G, rows), 1) // kvh
    sel_dists = [qpos - (idx_ref[(b * kvh + h) * LANES + j] * SEL_BLOCK + tok) for j in range(n_pick)]
    o_sel = attend([r[...] for r in ksel_refs], [r[...] for r in vsel_refs], sel_dists, ksn_ref, vsn_ref,
                   lambda dist: dist >= 0)
    slot = lax.broadcasted_iota(jnp.int32, (G, n_cached * kvh), 1) // kvh
    wdist = n_cached - slot
    o_win = attend([kwin_ref[0]], [vwin_ref[0]], [wdist], kwn_ref, vwn_ref,
                   lambda dist: (dist < WINDOW) & (qpos - dist >= 0))
    gates = jax.nn.sigmoid(gate_ref[0, 0])
    o = gates[:, 0:1] * oc_ref[0, 0] + gates[:, 1:2] * o_sel + gates[:, 2:3] * o_win
    o_ref[0, 0] = o * jax.nn.silu(path_ref[0, 0])


def nsa_sample(q, kcmp, vcmp, pool_sk, pool_sv, page_table, new_kv, win_k, win_v, gate, path, table, past_len):
    B, KVH, G, hd = q.shape
    n_cb = kcmp.shape[1]
    n_pool, page = pool_sk.shape[:2]
    NP = page_table.shape[1]
    n_cached = win_k.shape[1]
    qpos = past_len
    n_sel = -(-(past_len + 1) // SEL_BLOCK)
    n_pick = min(N_SELECT, n_sel) - 1
    assert past_len % SEL_BLOCK == 0 and n_cb * CMP_BLOCK == past_len and 1 <= n_pick <= LANES
    bucket = rel_bucket(jnp.arange(past_len + 1, dtype=jnp.int32))
    thr = jnp.sum(bucket[None, :] < jnp.arange(NUM_BUCKETS, dtype=jnp.int32)[:, None], axis=1).astype(jnp.int32)
    tab = table.astype(F32).T.reshape(KVH, G, NUM_BUCKETS)
    head = lambda shape: pl.BlockSpec((1, 1) + shape, lambda b, h, *_: (b, h, 0, 0))
    cmp_spec = pl.BlockSpec((1, n_cb, hd), lambda b, h, *_: (b, 0, h))
    tab_spec = pl.BlockSpec((1, G, NUM_BUCKETS), lambda b, h, *_: (h, 0, 0))
    oc, idx = pl.pallas_call(
        functools.partial(_sample_cmp_kernel, qpos=qpos, n_pick=n_pick, hd=hd),
        grid_spec=pltpu.PrefetchScalarGridSpec(
            num_scalar_prefetch=1,
            grid=(B, KVH),
            in_specs=[head((G, hd)), cmp_spec, cmp_spec, tab_spec],
            out_specs=[head((G, hd)), head((1, LANES))],
        ),
        out_shape=[jax.ShapeDtypeStruct((B, KVH, G, hd), F32), jax.ShapeDtypeStruct((B, KVH, 1, LANES), jnp.int32)],
        compiler_params=_params("parallel", "parallel"),
        name="nsa_sample_cmp",
    )(thr, q, kcmp, vcmp, tab)

    halves = page // SEL_BLOCK
    rows = SEL_BLOCK * KVH

    def sel_spec(j):
        def index(b, h, idx_ref, pt_ref, thr_ref):
            blk = idx_ref[(b * KVH + h) * LANES + j]
            return (pt_ref[b * NP + blk // halves] * halves + blk % halves, 0)
        return pl.BlockSpec((rows, hd), index)

    win_spec = pl.BlockSpec((1, n_cached * KVH, hd), lambda b, h, *_: (b, 0, 0))
    new_spec = pl.BlockSpec((1, 1, 1, hd), lambda b, h, *_: (b, h, 0, 0))
    ks_n, vs_n, kw_n, vw_n = new_kv
    sel_k = pool_sk.reshape(n_pool * page * KVH, hd)
    sel_v = pool_sv.reshape(n_pool * page * KVH, hd)
    return pl.pallas_call(
        functools.partial(_sample_attn_kernel, qpos=qpos, hd=hd, n_cached=n_cached, n_pick=n_pick, kvh=KVH),
        grid_spec=pltpu.PrefetchScalarGridSpec(
            num_scalar_prefetch=3,
            grid=(B, KVH),
            in_specs=[head((G, hd)), new_spec, new_spec, win_spec, win_spec, new_spec, new_spec,
                      head((G, hd)), head((G, 3)), head((G, hd)), tab_spec]
                     + [sel_spec(j) for j in range(n_pick)] * 2,
            out_specs=head((G, hd)),
        ),
        out_shape=jax.ShapeDtypeStruct((B, KVH, G, hd), F32),
        compiler_params=_params("parallel", "parallel"),
        name="nsa_sample_attn",
    )(idx.reshape(-1), page_table.reshape(-1), thr, q, ks_n, vs_n, win_k.reshape(B, n_cached * KVH, hd),
      win_v.reshape(B, n_cached * KVH, hd), kw_n, vw_n, oc, gate, path, tab,
      *([sel_k] * n_pick), *([sel_v] * n_pick))


def rel_bucket(dist):
    d = jnp.maximum(dist, 0)
    me = NUM_BUCKETS // 2
    logd = jnp.log(jnp.maximum(d, 1).astype(F32) / me)
    large = me + (logd / math.log(MAX_DISTANCE / me) * (NUM_BUCKETS - me)).astype(jnp.int32)
    return jnp.where(d < me, d, jnp.minimum(large, NUM_BUCKETS - 1))


SAMPLE_ROWS = 16
MATMUL_TN = 512


class EvenCols:
    def __init__(self, hg_qk, hg_w, nsa_w, nsa_kvw):
        self.hd = nsa_w // NSA_HEADS
        self.kvw = nsa_kvw
        self.nsa_w = nsa_w
        self.q, self.f, self.v, self.g = 0, hg_qk, 2 * hg_qk, 2 * hg_qk + hg_w
        self.nq = 2 * hg_qk + 2 * hg_w
        self.kv = self.nq + nsa_w
        self.main_w = self.kv + 6 * nsa_kvw
        self.gate_w = 3 * NSA_G
        self.tail_path, self.tail_gate = 0, nsa_w
        used = nsa_w + NSA_KVH * LANES
        self.tail_w = -(-used // MATMUL_TN) * MATMUL_TN
        assert self.main_w % MATMUL_TN == 0

    def tail_weight(self, w_in_t):
        gate0 = self.main_w
        path0 = gate0 + NSA_KVH * self.gate_w
        parts = [w_in_t[path0:path0 + self.nsa_w]]
        for h in range(NSA_KVH):
            parts.append(jnp.pad(w_in_t[gate0 + h * self.gate_w:gate0 + (h + 1) * self.gate_w],
                                 ((0, LANES - self.gate_w), (0, 0))))
        w = jnp.concatenate(parts, axis=0)
        return jnp.pad(w, ((0, self.tail_w - w.shape[0]), (0, 0)))


def even_prompt(main, tail, B, T, lb, hg_gain, pe, w1k, w2k, w1v, w2v, table, ec):
    hd = ec.hd
    S0 = jnp.zeros((B, HG_HEADS, HG_DK, hg_gain.shape[0] // HG_HEADS), F32)
    y_hg, S = hgrn_mix(main, lb, hg_gain, S0, B, T, (ec.q, ec.f, ec.v, ec.g), HG_CHUNK, HGRN_SUB, HG_CHUNK)
    rows = kv_split(main, ec.kv, 6, NSA_KVH, hd)
    kcmp = compress_blocks(rows[0], B, T, NSA_KVH, pe, w1k, w2k)
    vcmp = compress_blocks(rows[1], B, T, NSA_KVH, pe, w1v, w2v)
    kc, vc, ks, vs, kw, vw = [r.reshape(B, T, NSA_KVH, hd) for r in rows]
    y_nsa = nsa_prompt(main, tail, kcmp, vcmp, table, B, T, (ec.nq, ec.kv, ec.tail_path, ec.tail_gate, hd))
    wb = min(WINDOW, T)
    return jnp.concatenate([y_hg, y_nsa], axis=-1), S, kc, vc, ks, vs, kw[:, T - wb:], vw[:, T - wb:]


def even_sample(main, tail, B, T, past_len, S0, page_table, pool_ck, pool_cv, pool_sk, pool_sv, win_k, win_v,
                lb, hg_gain, pe, w1k, w2k, w1v, w2v, table, ec):
    hd = ec.hd
    R = SAMPLE_ROWS
    y_hg, S = hgrn_mix(main, lb, hg_gain, S0.astype(F32), B, R, (ec.q, ec.f, ec.v, ec.g), R, R, T)
    m3 = main.reshape(B, R, -1)[:, :T]
    t3 = tail.reshape(B, R, -1)[:, :T]
    assert T == 1
    heads = lambda a, w: a.reshape(B, NSA_KVH, NSA_G, w)
    q = heads(m3[..., ec.nq:ec.kv], hd)
    path = heads(t3[..., ec.tail_path:ec.tail_path + ec.nsa_w], hd)
    gate = jnp.stack([t3[:, 0, ec.tail_gate + h * LANES:ec.tail_gate + h * LANES + ec.gate_w].reshape(B, NSA_G, 3)
                      for h in range(NSA_KVH)], axis=1)
    kvs = m3[..., ec.kv:ec.main_w].reshape(B, T, 6, NSA_KVH, hd)
    kc, vc, ks, vs, kw, vw = [kvs[:, :, j] for j in range(6)]
    new_kv = tuple(jnp.swapaxes(a, 1, 2) for a in (ks, vs, kw, vw))
    kcmp = compress_paged(pool_ck, page_table, pe, w1k, w2k)
    vcmp = compress_paged(pool_cv, page_table, pe, w1v, w2v)
    y_nsa = nsa_sample(q, kcmp, vcmp, pool_sk, pool_sv, page_table, new_kv, win_k, win_v, gate, path, table,
                       past_len)
    y_nsa = jnp.pad(y_nsa.reshape(B, T, -1).astype(BF16), ((0, 0), (0, R - T), (0, 0))).reshape(B * R, -1)
    kw_all = jnp.concatenate([win_k, kw], axis=1)
    vw_all = jnp.concatenate([win_v, vw], axis=1)
    return jnp.concatenate([y_hg, y_nsa], axis=-1), S, kc, vc, ks, vs, kw_all[:, T:], vw_all[:, T:]


def mem_attend(qg, B, T, mk, mv):
    slots = mk.shape[1]
    return mem_mix(qg, mk.reshape(B, slots, -1), mv.reshape(B, slots, -1), B, T)


def kernel(x_prompt, x_sample, mem_prompt, state_hgrn, cache_cmp_k, cache_cmp_v, cache_sel_k, cache_sel_v,
           cache_win_k, cache_win_v, state_ret, cache_mem_k, cache_mem_v, page_table, rel_table,
           norm_mix_pre, norm_mix_post, norm_mem_pre, norm_mem_post, ev_w_in, ev_w_out, hgrn_lb, hgrn_norm,
           cmp_pe, cmp_w1_k, cmp_w2_k, cmp_w1_v, cmp_w2_v, od_w_in, od_w_out, ret_norm,
           mem_w_q, mem_w_k, mem_w_v, mem_w_o):
    B, T, D = x_prompt.shape
    Bs, Ts, _ = x_sample.shape
    depth = norm_mix_pre.shape[0]
    past_len = page_table.shape[1] * PAGE_SIZE
    pos_p = jnp.arange(T, dtype=jnp.int32)
    pos_s = past_len + jnp.arange(Ts, dtype=jnp.int32)
    lb_all = jnp.cumsum(jax.nn.softmax(hgrn_lb.astype(F32), axis=0), axis=0)
    hg_qk = hgrn_lb.shape[1]
    hg_w = hgrn_norm.shape[1]
    nsa_w = ev_w_out.shape[1] - hg_w
    nsa_kvw = NSA_KVH * (nsa_w // NSA_HEADS)
    ec = EvenCols(hg_qk, hg_w, nsa_w, nsa_kvw)
    ev_w_in_t = jnp.swapaxes(ev_w_in, 1, 2)
    R = SAMPLE_ROWS
    assert Ts <= R

    even_p, even_s, ret_p, ret_s, memk_p, memv_p = [], [], [], [], [], []
    xp = x_prompt.reshape(B * T, D)
    xs = jnp.pad(x_sample, ((0, 0), (0, R - Ts), (0, 0))).reshape(Bs * R, D)
    pos_sr = past_len + jnp.arange(R, dtype=jnp.int32)
    mem2 = mem_prompt.reshape(-1, D).astype(BF16)
    ML = mem_prompt.shape[1]
    hp = rmsnorm_bf16(xp, norm_mix_pre[0])
    hs = rmsnorm_bf16(xs, norm_mix_pre[0])
    for l in range(depth):
        if l % 2 == 0:
            e = l // 2
            w = (lb_all[e], hgrn_norm[e], cmp_pe[e], cmp_w1_k[e], cmp_w2_k[e], cmp_w1_v[e], cmp_w2_v[e], rel_table)
            main_p, main_s = matmul(hp, ev_w_in_t, e, n_cols=ec.main_w, transposed=True, xs=hs)
            tail_p, tail_s = matmul(hp, ec.tail_weight(ev_w_in_t[e]), transposed=True, xs=hs)
            ap, *sp = even_prompt(main_p, tail_p, B, T, *w, ec)
            as_, *ss = even_sample(main_s, tail_s, Bs, Ts, past_len, state_hgrn[e], page_table, cache_cmp_k[e],
                                   cache_cmp_v[e], cache_sel_k[e], cache_sel_v[e], cache_win_k[e], cache_win_v[e],
                                   *w, ec)
            yp, ys = matmul(ap, ev_w_out, e, xs=as_)
            even_p.append(sp)
            even_s.append(ss)
        else:
            o = l // 2
            S0p = jnp.zeros((B,) + state_ret.shape[2:], F32)
            proj_p, proj_s = matmul(hp, od_w_in, o, xs=hs)
            ap, sp = retention_mix(proj_p, pos_p, ret_norm[o], S0p, B, T, RET_CHUNK, RET_CHUNK)
            as_, ss = retention_mix(proj_s, pos_sr, ret_norm[o], state_ret[o].astype(F32), Bs, R, R, Ts)
            yp, ys = matmul(ap, od_w_out, o, xs=as_)
            ret_p.append(sp)
            ret_s.append(ss)
        xp, hp = residual_post_pre(xp, yp, norm_mix_post[l], norm_mem_pre[l])
        xs, hs = residual_post_pre(xs, ys, norm_mix_post[l], norm_mem_pre[l])
        mk_p = matmul(mem2, mem_w_k, l).reshape(B, ML, MEM_HEADS, MEM_HD)
        mv_p = matmul(mem2, mem_w_v, l).reshape(B, ML, MEM_HEADS, MEM_HD)
        memk_p.append(mk_p)
        memv_p.append(mv_p)
        qg_p, qg_s = matmul(hp, mem_w_q, l, xs=hs)
        g_next = norm_mix_pre[l + 1] if l + 1 < depth else None
        xp, hp = matmul_residual(mem_attend(qg_p, B, T, mk_p, mv_p), mem_w_o, l, xp, norm_mem_post[l], g_next)
        xs, hs = matmul_residual(mem_attend(qg_s, Bs, R, cache_mem_k[l], cache_mem_v[l]), mem_w_o, l, xs,
                                 norm_mem_post[l], g_next)
    p_hgrn, p_cmp_k, p_cmp_v, p_sel_k, p_sel_v, p_win_k, p_win_v = [jnp.stack(a) for a in zip(*even_p)]
    s_hgrn, s_cmp_k, s_cmp_v, s_sel_k, s_sel_v, s_win_k, s_win_v = [jnp.stack(a) for a in zip(*even_s)]
    p_ret = jnp.stack(ret_p)
    s_ret = jnp.stack(ret_s)
    p_mem_k = jnp.stack(memk_p)
    p_mem_v = jnp.stack(memv_p)
    return (xp.reshape(B, T, D), xs.reshape(Bs, R, D)[:, :Ts], p_hgrn, p_cmp_k, p_cmp_v, p_sel_k, p_sel_v,
            p_win_k, p_win_v, p_ret, p_mem_k, p_mem_v,
            s_hgrn, s_cmp_k, s_cmp_v, s_sel_k, s_sel_v, s_win_k, s_win_v, s_ret)
```

```python
import functools
import math

import jax
import jax.numpy as jnp
import numpy as np
from jax import lax
from jax.experimental import pallas as pl
from jax.experimental.pallas import tpu as pltpu

F32 = jnp.float32
BF16 = jnp.bfloat16
EPS = 1e-6
NEG_INF = -1e30
LOG2E = math.log2(math.e)

PAGE_SIZE = 128
HG_HEADS = 16
HG_DK = 128
HG_CHUNK = 64
HGRN_SUB = 16
NSA_HEADS = 16
NSA_KVH = 2
NSA_G = NSA_HEADS // NSA_KVH
CMP_BLOCK = 32
SEL_BLOCK = 64
N_SELECT = 16
WINDOW = 512
SEL_QBLOCK = 64
WIN_QBLOCK = 128
NUM_BUCKETS = 32
MAX_DISTANCE = 1024
RET_HEADS = 16
RET_CHUNK = 128
ROPE_BASE = 10000.0
MEM_HEADS = 4
MEM_HD = 128
MEM_W = MEM_HEADS * MEM_HD

VMEM_LIMIT_BYTES = 56 * 1024 * 1024


def _params(*sem):
    return pltpu.CompilerParams(dimension_semantics=sem, vmem_limit_bytes=VMEM_LIMIT_BYTES)


def _rmsnorm_kernel(x_ref, g_ref, o_ref):
    x = x_ref[...]
    y = x * lax.rsqrt(jnp.mean(x * x, axis=-1, keepdims=True) + EPS)
    o_ref[...] = (y * g_ref[...]).astype(o_ref.dtype)


def rmsnorm_bf16(x, g):
    M, D = x.shape
    tm = min(M, 512)
    return pl.pallas_call(
        _rmsnorm_kernel,
        grid=(M // tm,),
        in_specs=[pl.BlockSpec((tm, D), lambda i: (i, 0)), pl.BlockSpec((1, D), lambda i: (0, 0))],
        out_specs=pl.BlockSpec((tm, D), lambda i: (i, 0)),
        out_shape=jax.ShapeDtypeStruct((M, D), BF16),
        compiler_params=_params("parallel"),
        name="rmsnorm_bf16",
    )(x, g.reshape(1, D))


def _post_pre_kernel(x_ref, y_ref, g_ref, gn_ref, o_ref, h_ref):
    y = y_ref[...]
    x = x_ref[...] + y * lax.rsqrt(jnp.mean(y * y, axis=-1, keepdims=True) + EPS) * g_ref[...]
    o_ref[...] = x
    h_ref[...] = (x * lax.rsqrt(jnp.mean(x * x, axis=-1, keepdims=True) + EPS) * gn_ref[...]).astype(h_ref.dtype)


def residual_post_pre(x, y, g, g_next):
    M, D = x.shape
    tm = min(M, 256)
    row = pl.BlockSpec((tm, D), lambda i: (i, 0))
    vec = pl.BlockSpec((1, D), lambda i: (0, 0))
    return pl.pallas_call(
        _post_pre_kernel,
        grid=(M // tm,),
        in_specs=[row, row, vec, vec],
        out_specs=[row, row],
        out_shape=[jax.ShapeDtypeStruct((M, D), F32), jax.ShapeDtypeStruct((M, D), BF16)],
        compiler_params=_params("parallel"),
        name="residual_post_pre",
    )(x, y, g.reshape(1, D), g_next.reshape(1, D))


def _matmul_post_kernel(*refs, has_next):
    if has_next:
        a_ref, w_ref, x_ref, g_ref, gn_ref, o_ref, h_ref, wb_ref = refs
    else:
        a_ref, w_ref, x_ref, g_ref, o_ref, wb_ref = refs

    @pl.when(pl.program_id(0) == 0)
    def _():
        wb_ref[...] = w_ref[...].astype(BF16)

    y = jnp.dot(a_ref[...], wb_ref[...], preferred_element_type=F32)
    x = x_ref[...] + y * lax.rsqrt(jnp.mean(y * y, axis=-1, keepdims=True) + EPS) * g_ref[...]
    o_ref[...] = x
    if has_next:
        h_ref[...] = (x * lax.rsqrt(jnp.mean(x * x, axis=-1, keepdims=True) + EPS) * gn_ref[...]).astype(h_ref.dtype)


def matmul_residual(a, w, layer, x, g, g_next=None):
    M, K = a.shape
    D = w.shape[-1]
    tm = min(M, 256)
    has_next = g_next is not None
    row = lambda width: pl.BlockSpec((tm, width), lambda i: (i, 0))
    vec = pl.BlockSpec((1, D), lambda i: (0, 0))
    in_specs = [row(K), pl.BlockSpec((None, K, D), lambda i: (layer, 0, 0)), row(D), vec]
    args = [a, w, x, g.reshape(1, D)]
    out_specs = [row(D)]
    out_shape = [jax.ShapeDtypeStruct((M, D), F32)]
    if has_next:
        in_specs.append(vec)
        args.append(g_next.reshape(1, D))
        out_specs.append(row(D))
        out_shape.append(jax.ShapeDtypeStruct((M, D), BF16))
    out = pl.pallas_call(
        functools.partial(_matmul_post_kernel, has_next=has_next),
        grid=(M // tm,),
        in_specs=in_specs,
        out_specs=out_specs,
        out_shape=out_shape,
        scratch_shapes=[pltpu.VMEM((K, D), BF16)],
        compiler_params=_params("arbitrary"),
        name="matmul_residual",
    )(*args)
    return out if has_next else (out[0], None)


MATMUL_KB = 4096
MATMUL_TN = 512
MATMUL_TM = 1024


def _matmul_kernel(*refs, transposed, paired, chained):
    n_in = 1 + paired
    xs_refs = refs[:n_in]
    w_ref = refs[n_in]
    acc_refs = refs[n_in + 1:n_in + 1 + n_in] if chained else (None,) * n_in
    o_refs = refs[-1 - n_in:-1]
    wb_ref = refs[-1]

    def emit(x_ref, acc_ref, o_ref):
        y = jnp.dot(x_ref[...], wb_ref[...], preferred_element_type=F32)
        o_ref[...] = y if acc_ref is None else acc_ref[...] + y

    @pl.when(pl.program_id(1) == 0)
    def _():
        w = w_ref[...]
        wb_ref[...] = (w.T if transposed else w).astype(BF16)
        if paired:
            emit(xs_refs[1], acc_refs[1], o_refs[1])

    emit(xs_refs[0], acc_refs[0], o_refs[0])


def matmul(x, w, layer=None, n_cols=None, transposed=False, xs=None):
    M, K = x.shape
    N = (w.shape[-2] if transposed else w.shape[-1]) if n_cols is None else n_cols
    kb = min(K, MATMUL_KB)
    tn, tm = min(MATMUL_TN, N), min(MATMUL_TM, M)
    assert N % tn == 0 and M % tm == 0 and K % kb == 0, (M, K, N)
    operands = [x] if xs is None else [x, xs]
    out = None
    for kk in range(K // kb):
        blk = (tn, kb) if transposed else (kb, tn)
        pick = (lambda n, kk=kk: (n, kk)) if transposed else (lambda n, kk=kk: (kk, n))
        if w.ndim == 3:
            w_spec = pl.BlockSpec((None,) + blk, lambda n, m, pick=pick: (layer,) + pick(n))
        else:
            w_spec = pl.BlockSpec(blk, lambda n, m, pick=pick: pick(n))
        in_specs = [pl.BlockSpec((tm, kb), lambda n, m, kk=kk: (m, kk))]
        out_specs = [pl.BlockSpec((tm, tn), lambda n, m: (m, n))]
        if xs is not None:
            in_specs.append(pl.BlockSpec((xs.shape[0], kb), lambda n, m, kk=kk: (0, kk)))
            out_specs.append(pl.BlockSpec((xs.shape[0], tn), lambda n, m: (0, n)))
        out = pl.pallas_call(
            functools.partial(_matmul_kernel, transposed=transposed, paired=xs is not None, chained=out is not None),
            grid=(N // tn, M // tm),
            in_specs=in_specs + [w_spec] + (out_specs if out is not None else []),
            out_specs=out_specs,
            out_shape=[jax.ShapeDtypeStruct((a.shape[0], N), F32) for a in operands],
            scratch_shapes=[pltpu.VMEM((kb, tn), BF16)],
            compiler_params=_params("parallel", "arbitrary"),
            name="matmul",
        )(*operands, w, *(out if out is not None else []))
    return out if xs is not None else out[0]


LANES = 128
NSA_TQ = 128
NSA_KC = 256
_NT = (((1,), (1,)), ((), ()))


_TN = (((0,), (0,)), ((), ()))


def _bias_tile_kernel(thr_ref, table_ref, o_ref, *, row_stride, row_offset):
    kvh = pl.program_id(0)
    step = pl.program_id(1)
    rows = o_ref.shape[2]
    row = lax.broadcasted_iota(jnp.int32, (rows, LANES), 0)
    lane = lax.broadcasted_iota(jnp.int32, (rows, LANES), 1)
    dist = LANES * step + lane - row_stride * row - row_offset
    for g in range(NSA_G):
        h = kvh * NSA_G + g
        bias = jnp.full((rows, LANES), table_ref[0, h], F32)
        for k in range(1, NUM_BUCKETS):
            bias = jnp.where(dist >= thr_ref[k], table_ref[k, h], bias)
        o_ref[0, 0, :, g * LANES:(g + 1) * LANES] = bias


def bias_tiles(table, n_steps, rows, max_dist, row_stride, row_offset):
    bucket = rel_bucket(jnp.arange(max_dist, dtype=jnp.int32))
    thr = jnp.sum(bucket[None, :] < jnp.arange(NUM_BUCKETS, dtype=jnp.int32)[:, None], axis=1).astype(jnp.int32)
    smem = pl.BlockSpec(memory_space=pltpu.SMEM)
    return pl.pallas_call(
        functools.partial(_bias_tile_kernel, row_stride=row_stride, row_offset=row_offset),
        grid=(NSA_KVH, n_steps),
        in_specs=[smem, smem],
        out_specs=pl.BlockSpec((1, 1, rows, NSA_G * LANES), lambda h, s: (h, s, 0, 0)),
        out_shape=jax.ShapeDtypeStruct((NSA_KVH, n_steps, rows, NSA_G * LANES), F32),
        compiler_params=_params("parallel", "parallel"),
        name="bias_tiles",
    )(thr, table)


def _nsa_prompt_kernel(q_ref, kc_ref, vc_ref, ks_ref, vs_ref, kw_ref, vw_ref, cb_ref, tb_ref, gate_ref, path_ref,
                       o_ref, qb_s, oc_s, m_s, l_s, acc_s, *, hd, n_win_chunks, n_pick):
    i = pl.program_id(2)
    tq = NSA_TQ
    G = NSA_G
    scale = hd ** -0.5 * LOG2E
    n_cb = kc_ref.shape[1]
    row_c = lax.broadcasted_iota(jnp.int32, (n_cb, tq), 0)
    qpos_c = i * tq + lax.broadcasted_iota(jnp.int32, (n_cb, tq), 1)
    row = lax.broadcasted_iota(jnp.int32, (NSA_KC, tq), 0)
    qpos = i * tq + lax.broadcasted_iota(jnp.int32, (NSA_KC, tq), 1)
    slab = lambda g: slice(g * tq, (g + 1) * tq)

    for g in range(G):
        qb_s[slab(g), :] = (q_ref[:, g * hd:(g + 1) * hd] * scale).astype(BF16)
    qb = qb_s[...]

    s_all = lax.dot_general(kc_ref[0].astype(BF16), qb, _NT, preferred_element_type=F32) + cb_ref[0, 0]
    valid_c = qpos_c >= (row_c + 1) * CMP_BLOCK - 1
    valid_cf = valid_c.astype(F32)
    imp = jnp.zeros((n_cb, tq), F32)
    ps = []
    for g in range(G):
        s = jnp.where(valid_c, s_all[:, slab(g)], NEG_INF)
        e = jnp.exp2(s - jnp.max(s, axis=0, keepdims=True))
        p = e * (1.0 / jnp.sum(e, axis=0, keepdims=True)) * valid_cf
        imp = imp + p
        ps.append(p.astype(BF16))
    oc_s[...] = lax.dot_general(vc_ref[0].astype(BF16), jnp.concatenate(ps, axis=1), _TN,
                                preferred_element_type=F32)

    imp2 = imp + pltpu.roll(imp, n_cb - 1, axis=0)
    blk = row_c >> 1
    cur = qpos_c // SEL_BLOCK
    forced = (blk == 0) | (blk == cur) | (blk == cur - 1)
    usable = ((row_c & 1) == 0) & (blk * SEL_BLOCK <= qpos_c)
    work = jnp.where(usable, jnp.where(forced, jnp.inf, imp2), -jnp.inf)
    sel = jnp.zeros((n_cb, tq), F32)
    for _ in range(n_pick):
        mx = jnp.max(work, axis=0, keepdims=True)
        first = jnp.min(jnp.where(work == mx, row_c, n_cb), axis=0, keepdims=True)
        pick = row_c == first
        sel = jnp.where(pick, 1.0, sel)
        work = jnp.where(pick, -jnp.inf, work)
    sel_b = sel.astype(BF16)

    m_s[...] = jnp.full(m_s.shape, NEG_INF, F32)
    l_s[...] = jnp.zeros(l_s.shape, F32)
    acc_s[...] = jnp.zeros(acc_s.shape, F32)
    ekey = lax.broadcasted_iota(jnp.int32, (NSA_KC, n_cb), 0)
    eblk = lax.broadcasted_iota(jnp.int32, (NSA_KC, n_cb), 1)
    tiles_per_chunk = NSA_KC // LANES

    def online(slot, k_b, v_b, bias, valid):
        s_all = lax.dot_general(k_b, qb, _NT, preferred_element_type=F32) + bias
        ps, alphas = [], []
        for g in range(G):
            s = jnp.where(valid, s_all[:, slab(g)], NEG_INF)
            m_old = m_s[slot, :, slab(g)]
            m_new = jnp.maximum(m_old, jnp.max(s, axis=0, keepdims=True))
            alpha = jnp.exp2(m_old - m_new)
            p = jnp.exp2(s - m_new)
            l_s[slot, :, slab(g)] = alpha * l_s[slot, :, slab(g)] + jnp.sum(p, axis=0, keepdims=True)
            m_s[slot, :, slab(g)] = m_new
            ps.append(p.astype(BF16))
            alphas.append(alpha)
        pv = lax.dot_general(v_b, jnp.concatenate(ps, axis=1), _TN, preferred_element_type=F32)
        acc_s[slot] = jnp.concatenate(alphas, axis=1) * acc_s[slot] + pv

    def chunk(c, carry):
        k0 = pl.multiple_of(c * NSA_KC, NSA_KC)
        delta = i - tiles_per_chunk * c
        dist = qpos - (k0 + row)
        causal = dist >= 0
        expand = (eblk == 2 * (c * (NSA_KC // SEL_BLOCK) + (ekey // SEL_BLOCK))).astype(BF16)
        chosen = jnp.dot(expand, sel_b, preferred_element_type=F32) > 0.5
        bias = jnp.concatenate([tb_ref[0, jnp.maximum(delta - t, 0)] for t in range(tiles_per_chunk)], axis=0)
        online(0, ks_ref[pl.ds(k0, NSA_KC), :].astype(BF16), vs_ref[pl.ds(k0, NSA_KC), :].astype(BF16), bias,
               chosen & causal)

        @pl.when(delta < n_win_chunks + tiles_per_chunk - 1)
        def _():
            online(1, kw_ref[pl.ds(k0, NSA_KC), :].astype(BF16), vw_ref[pl.ds(k0, NSA_KC), :].astype(BF16), bias,
                   causal & (dist < WINDOW))

        return carry

    lax.fori_loop(0, i // tiles_per_chunk + 1, chunk, 0)

    gates = jax.nn.sigmoid(gate_ref[...]).T
    for g in range(G):
        o_sel = acc_s[0, :, slab(g)] * (1.0 / l_s[0, :, slab(g)])
        o_win = acc_s[1, :, slab(g)] * (1.0 / l_s[1, :, slab(g)])
        o = (gates[3 * g:3 * g + 1] * oc_s[:, slab(g)] + gates[3 * g + 1:3 * g + 2] * o_sel
             + gates[3 * g + 2:3 * g + 3] * o_win)
        o_ref[:, g * hd:(g + 1) * hd] = (o.T * jax.nn.silu(path_ref[:, g * hd:(g + 1) * hd])).astype(o_ref.dtype)


def nsa_prompt(main, tail, kcmp, vcmp, table, B, T, cols):
    q_col, kv_col, path_col, gate_col, hd = cols
    G = NSA_G
    tq = NSA_TQ
    n_cb = kcmp.shape[1]
    assert n_cb % 8 == 0 and tq == LANES and T % NSA_KC == 0 and NSA_KC % LANES == 0 and hd == LANES
    nd = T // LANES
    n_win_chunks = WINDOW // LANES + 1
    table2 = table.astype(F32) * LOG2E
    tb = bias_tiles(table2, nd, LANES, T, 1, 0)
    cb = bias_tiles(table2, T // tq, n_cb, T, CMP_BLOCK, CMP_BLOCK - 1)
    nblk = T // tq
    gw = G * hd
    kv_spec = lambda j: pl.BlockSpec((T, hd), lambda b, h, i: (b, kv_col // hd + 2 * j + h))
    return pl.pallas_call(
        functools.partial(_nsa_prompt_kernel, hd=hd, n_win_chunks=n_win_chunks,
                          n_pick=min(N_SELECT, -(-T // SEL_BLOCK))),
        grid=(B, NSA_KVH, nblk),
        in_specs=[
            pl.BlockSpec((tq, gw), lambda b, h, i: (b * nblk + i, q_col // gw + h)),
            pl.BlockSpec((1, n_cb, hd), lambda b, h, i: (b, 0, h)),
            pl.BlockSpec((1, n_cb, hd), lambda b, h, i: (b, 0, h)),
            kv_spec(2), kv_spec(3), kv_spec(4), kv_spec(5),
            pl.BlockSpec((1, 1, n_cb, gw), lambda b, h, i: (h, i, 0, 0)),
            pl.BlockSpec((1, nd, LANES, gw), lambda b, h, i: (h, 0, 0, 0)),
            pl.BlockSpec((tq, LANES), lambda b, h, i: (b * nblk + i, gate_col // LANES + h)),
            pl.BlockSpec((tq, gw), lambda b, h, i: (b * nblk + i, path_col // gw + h)),
        ],
        out_specs=pl.BlockSpec((tq, gw), lambda b, h, i: (b * nblk + i, h)),
        out_shape=jax.ShapeDtypeStruct((B * T, NSA_KVH * gw), BF16),
        scratch_shapes=[pltpu.VMEM((G * tq, hd), BF16), pltpu.VMEM((hd, G * tq), F32),
                        pltpu.VMEM((2, 1, G * tq), F32), pltpu.VMEM((2, 1, G * tq), F32),
                        pltpu.VMEM((2, hd, G * tq), F32)],
        compiler_params=_params("parallel", "parallel", "arbitrary"),
        name="nsa_prompt",
    )(main, kcmp, vcmp, main, main, main, main, cb, tb, tail, tail)


_TN = (((0,), (0,)), ((), ()))


def _cumsum_rows(x, n):
    row = lax.broadcasted_iota(jnp.int32, x.shape, 0)
    sh = 1
    while sh < n:
        x = x + jnp.where(row >= sh, pltpu.roll(x, sh, axis=0), 0.0)
        sh *= 2
    return x


def _hgrn_kernel(q_ref, f_ref, v_ref, gate_ref, lb_ref, gain_ref, s0_ref, y_ref, s_out_ref, st_s,
                 *, C, SB, c_eff, dk, HB):
    c = pl.program_id(2)

    @pl.when(c == 0)
    def _():
        for hh in range(HB):
            st_s[hh] = s0_ref[0, hh].T

    states = [st_s[hh] for hh in range(HB)]
    results = [_hgrn_head(slice(hh * dk, (hh + 1) * dk), states[hh], q_ref, f_ref, v_ref, gate_ref, lb_ref,
                          gain_ref, C=C, SB=SB, c_eff=c_eff, dk=dk) for hh in range(HB)]
    y_ref[...] = jnp.concatenate([y for y, _ in results], axis=1)
    for hh in range(HB):
        st_s[hh] = results[hh][1]

    @pl.when(c == pl.num_programs(2) - 1)
    def _():
        for hh in range(HB):
            s_out_ref[0, hh] = st_s[hh].T


def _hgrn_head(sl, st, q_ref, f_ref, v_ref, gate_ref, lb_ref, gain_ref, *, C, SB, c_eff, dk):
    lb = lb_ref[:, sl]
    f = lb + (1.0 - lb) * jax.nn.sigmoid(f_ref[:, sl])
    logf = jnp.log(f)
    k = 1.0 - f
    if c_eff < C:
        real = lax.broadcasted_iota(jnp.int32, (C, dk), 0) < c_eff
        logf = jnp.where(real, logf, 0.0)
        k = jnp.where(real, k, 0.0)
    q = q_ref[:, sl] * dk ** -0.5
    v_b = v_ref[:, sl].astype(BF16)
    b = _cumsum_rows(logf, C)
    o = lax.dot_general((q * jnp.exp(b)).astype(BF16), st.astype(BF16), _NT, preferred_element_type=F32)

    lane_c = lax.broadcasted_iota(jnp.int32, (SB, C), 1)
    row_c = lax.broadcasted_iota(jnp.int32, (SB, C), 0)
    outs = []
    for I in range(C // SB):
        r0 = I * SB
        q_i = q[r0:r0 + SB]
        b_i = b[r0:r0 + SB]
        if I > 0:
            b_r = b[r0 - 1:r0]
            qq = (q_i * jnp.exp(b_i - b_r)).astype(BF16)
            kk = (k * jnp.exp(jnp.minimum(b_r - b, 0.0))).astype(BF16)
            a = lax.dot_general(qq, kk, _NT, preferred_element_type=F32)
            a = jnp.where(lane_c < r0, a, 0.0)
        else:
            a = jnp.zeros((SB, C), F32)
        for s in range(SB):
            z = q_i * k[r0 + s:r0 + s + 1] * jnp.exp(jnp.minimum(b_i - b[r0 + s:r0 + s + 1], 0.0))
            col = jnp.sum(z, axis=-1, keepdims=True)
            a = jnp.where((lane_c == r0 + s) & (row_c >= s), col, a)
        outs.append(jnp.dot(a.astype(BF16), v_b, preferred_element_type=F32))
    o = o + jnp.concatenate(outs, axis=0)

    b_last = b[C - 1:C]
    kd = (k * jnp.exp(b_last - b)).astype(BF16)
    st_new = jnp.exp(b_last) * st + lax.dot_general(v_b, kd, _TN, preferred_element_type=F32)
    y = o * lax.rsqrt(jnp.mean(o * o, axis=-1, keepdims=True) + EPS) * gain_ref[:, sl]
    return (y * jax.nn.silu(gate_ref[:, sl])).astype(BF16), st_new


HGRN_HEADS_PER_STEP = 8


def hgrn_mix(main, lb, gain, s0, B, T, cols, C, SB, c_eff):
    q_col, f_col, v_col, g_col = cols
    _, H, dk, dv = s0.shape
    HB = HGRN_HEADS_PER_STEP
    W = HB * LANES
    assert dk == LANES and dv == LANES and T % C == 0 and C % SB == 0 and H % HB == 0
    assert all(off % W == 0 for off in cols)
    nc = T // C
    col = lambda off: pl.BlockSpec((C, W), lambda b, h, c: (b * nc + c, off // W + h))
    vec = pl.BlockSpec((1, W), lambda b, h, c: (0, h))
    st = pl.BlockSpec((1, HB, dk, dv), lambda b, h, c: (b, h, 0, 0))
    return pl.pallas_call(
        functools.partial(_hgrn_kernel, C=C, SB=SB, c_eff=c_eff, dk=dk, HB=HB),
        grid=(B, H // HB, nc),
        in_specs=[col(q_col), col(f_col), col(v_col), col(g_col), vec, vec, st],
        out_specs=[pl.BlockSpec((C, W), lambda b, h, c: (b * nc + c, h)), st],
        out_shape=[jax.ShapeDtypeStruct((B * T, H * dv), BF16), jax.ShapeDtypeStruct(s0.shape, F32)],
        scratch_shapes=[pltpu.VMEM((HB, dv, dk), F32)],
        compiler_params=_params("parallel", "parallel", "arbitrary"),
        name="hgrn_mix",
    )(main, main, main, main, lb.reshape(1, -1), gain.reshape(1, -1), s0)


def _retention_kernel(q_ref, k_ref, v_ref, g_ref, cos_ref, sin_ref, lg_ref, gain_ref, s0_ref, y_ref, s_out_ref, s_s,
                      *, C, c_eff, dk, dv, HB):
    c = pl.program_id(2)

    @pl.when(c == 0)
    def _():
        s_s[...] = s0_ref[0]

    for hh in range(HB):
        _retention_head(hh, q_ref, k_ref, v_ref, g_ref, cos_ref, sin_ref, lg_ref, gain_ref, y_ref, s_s,
                        C=C, c_eff=c_eff, dk=dk, dv=dv)

    @pl.when(c == pl.num_programs(2) - 1)
    def _():
        s_out_ref[0] = s_s[...]


def _retention_head(hh, q_ref, k_ref, v_ref, g_ref, cos_ref, sin_ref, lg_ref, gain_ref, y_ref, s_s,
                    *, C, c_eff, dk, dv):
    half = dk // 2
    vsl = slice(hh * dv, (hh + 1) * dv)
    cos = cos_ref[...]
    sin = sin_ref[...]
    lg_w = lg_ref[hh]
    lg = lg_w[:, :LANES]
    row = lax.broadcasted_iota(jnp.int32, (C, LANES), 0).astype(F32)

    def rot(ref, w):
        x1 = ref[:, hh * dk:hh * dk + half]
        x2 = ref[:, hh * dk + half:(hh + 1) * dk]
        return jnp.concatenate([(x1 * cos - x2 * sin) * w, (x1 * sin + x2 * cos) * w], axis=1)

    q = rot(q_ref, dk ** -0.5)
    k = rot(k_ref, 1.0)
    v_b = v_ref[:, vsl].astype(BF16)
    s_old = s_s[hh]

    a = lax.dot_general(q.astype(BF16), k.astype(BF16), _NT, preferred_element_type=F32)
    ti = lax.broadcasted_iota(jnp.int32, (C, C), 0)
    si = lax.broadcasted_iota(jnp.int32, (C, C), 1)
    diff = (ti - si).astype(F32)
    a = a * jnp.where(diff >= 0, jnp.exp(diff * lg_w[:, :C]), 0.0)
    inner = jnp.dot(a.astype(BF16), v_b, preferred_element_type=F32)
    q_w = jnp.exp((row + 1.0) * lg)
    q_dec = q * jnp.concatenate([q_w] * (dk // LANES), axis=1)
    cross = jnp.dot(q_dec.astype(BF16), s_old.astype(BF16), preferred_element_type=F32)
    k_w = jnp.where(row < c_eff, jnp.exp((c_eff - 1.0 - row) * lg), 0.0)
    k_dec = k * jnp.concatenate([k_w] * (dk // LANES), axis=1)
    s_new = jnp.exp(c_eff * lg_w) * s_old + lax.dot_general(k_dec.astype(BF16), v_b, _TN, preferred_element_type=F32)
    s_s[hh] = s_new

    o = inner + cross
    cen = o - jnp.mean(o, axis=-1, keepdims=True)
    y = cen * lax.rsqrt(jnp.mean(cen * cen, axis=-1, keepdims=True) + EPS) * gain_ref[:, vsl]
    y_ref[:, vsl] = (y * jax.nn.silu(g_ref[:, vsl])).astype(y_ref.dtype)


RETENTION_HEADS_PER_STEP = 4


def retention_mix(proj, pos, gain, s0, B, T, C, c_eff):
    _, H, dk, dv = s0.shape
    HB = RETENTION_HEADS_PER_STEP
    assert T % C == 0 and C <= LANES and dk % LANES == 0 and H % HB == 0
    nc = T // C
    nh = H // HB
    half = dk // 2
    inv = ROPE_BASE ** (-jnp.arange(half, dtype=F32) / half)
    ang = pos.astype(F32)[:, None] * inv[None, :]
    log_gamma = jnp.log1p(-jnp.exp2(-5.0 - jnp.arange(H, dtype=F32)))
    lg = jnp.broadcast_to(log_gamma[:, None, None], (H, 1, dv))
    qk = lambda j: pl.BlockSpec((C, HB * dk), lambda b, h, c: (b * nc + c, j * nh + h))
    vg = lambda j: pl.BlockSpec((C, HB * dv), lambda b, h, c: (b * nc + c, (2 * H * dk) // (HB * dv) + j * nh + h))
    tab = pl.BlockSpec((C, half), lambda b, h, c: (c, 0))
    st = pl.BlockSpec((1, HB, dk, dv), lambda b, h, c: (b, h, 0, 0))
    return pl.pallas_call(
        functools.partial(_retention_kernel, C=C, c_eff=c_eff, dk=dk, dv=dv, HB=HB),
        grid=(B, nh, nc),
        in_specs=[qk(0), qk(1), vg(0), vg(1), tab, tab,
                  pl.BlockSpec((HB, 1, dv), lambda b, h, c: (h, 0, 0)),
                  pl.BlockSpec((1, HB * dv), lambda b, h, c: (0, h)), st],
        out_specs=[pl.BlockSpec((C, HB * dv), lambda b, h, c: (b * nc + c, h)), st],
        out_shape=[jax.ShapeDtypeStruct((B * T, H * dv), BF16), jax.ShapeDtypeStruct(s0.shape, F32)],
        scratch_shapes=[pltpu.VMEM((HB, dk, dv), F32)],
        compiler_params=_params("parallel", "parallel", "arbitrary"),
        name="retention_mix",
    )(proj, proj, proj, proj, jnp.cos(ang), jnp.sin(ang), lg, gain.reshape(1, -1), s0)


def _mem_kernel(qg_ref, mk_ref, mv_ref, o_ref, *, heads, hd):
    for h in range(heads):
        sl = slice(h * hd, (h + 1) * hd)
        q = (qg_ref[:, sl] * hd ** -0.5).astype(BF16)
        s = lax.dot_general(q, mk_ref[0, :, sl].astype(BF16), _NT, preferred_element_type=F32)
        e = jnp.exp(s - jnp.max(s, axis=-1, keepdims=True))
        p = e * (1.0 / jnp.sum(e, axis=-1, keepdims=True))
        o = jnp.dot(p.astype(BF16), mv_ref[0, :, sl].astype(BF16), preferred_element_type=F32)
        gate = qg_ref[:, heads * hd + h * hd:heads * hd + (h + 1) * hd]
        o_ref[:, sl] = (o * jax.nn.silu(gate)).astype(o_ref.dtype)


def mem_mix(qg, mk, mv, B, T):
    W = mk.shape[2]
    tq = min(T, 256)
    nb = T // tq
    kv = pl.BlockSpec((1, mk.shape[1], W), lambda b, i: (b, 0, 0))
    return pl.pallas_call(
        functools.partial(_mem_kernel, heads=MEM_HEADS, hd=W // MEM_HEADS),
        grid=(B, nb),
        in_specs=[pl.BlockSpec((tq, 2 * W), lambda b, i: (b * nb + i, 0)), kv, kv],
        out_specs=pl.BlockSpec((tq, W), lambda b, i: (b * nb + i, 0)),
        out_shape=jax.ShapeDtypeStruct((B * T, W), BF16),
        compiler_params=_params("parallel", "arbitrary"),
        name="mem_mix",
    )(qg, mk, mv)


def _compress_rows(x_ref, n_blocks, pe_ref, w1_ref, w2_ref, kvh):
    stride = CMP_BLOCK * kvh
    outs = []
    for h in range(kvh):
        acc = jnp.zeros((n_blocks, w1_ref.shape[2]), F32)
        for j in range(0, CMP_BLOCK, 2):
            xa = x_ref[pl.ds(j * kvh + h, n_blocks, stride=stride), :] + pe_ref[j:j + 1]
            xb = x_ref[pl.ds((j + 1) * kvh + h, n_blocks, stride=stride), :] + pe_ref[j + 1:j + 2]
            x2 = jnp.concatenate([xa, xb], axis=1).astype(BF16)
            acc = acc + jnp.dot(x2, w1_ref[j // 2], preferred_element_type=F32)
        outs.append(jnp.dot(jax.nn.silu(acc).astype(BF16), w2_ref[...], preferred_element_type=F32))
    return jnp.concatenate(outs, axis=1)


def _compress_kernel(x_ref, pe_ref, w1_ref, w2_ref, o_ref, *, kvh):
    o_ref[...] = _compress_rows(x_ref, o_ref.shape[0], pe_ref, w1_ref, w2_ref, kvh)


def _compress_paged_kernel(pt_ref, pool_ref, pe_ref, w1_ref, w2_ref, o_ref, x_s, sem, *, n_pages, rows_per_page,
                           kvh):
    b = pl.program_id(0)

    def page_copy(p):
        return pltpu.make_async_copy(pool_ref.at[pt_ref[b * n_pages + p]],
                                     x_s.at[pl.ds(p * rows_per_page, rows_per_page)], sem.at[0])

    def start(p, carry):
        page_copy(p).start()
        return carry

    def wait(p, carry):
        page_copy(p).wait()
        return carry

    lax.fori_loop(0, n_pages, start, 0)
    lax.fori_loop(0, n_pages, wait, 0)
    o_ref[0] = _compress_rows(x_s, o_ref.shape[1], pe_ref, w1_ref, w2_ref, kvh)


def _compress_weights(pe, w1, w2, hd):
    return pe.astype(F32), w1.reshape(CMP_BLOCK // 2, 2 * hd, w1.shape[1]).astype(BF16), w2.astype(BF16)


def _kv_split_kernel(*refs, n, kvh, hd):
    for x_ref, o_ref in zip(refs[:n], refs[n:]):
        for h in range(kvh):
            o_ref[pl.ds(h, x_ref.shape[0], stride=kvh), :] = x_ref[:, h * hd:(h + 1) * hd]


def kv_split(main, col, n, kvh, hd):
    M = main.shape[0]
    tm = min(M, 512)
    w = kvh * hd
    assert col % w == 0 and M % tm == 0
    return pl.pallas_call(
        functools.partial(_kv_split_kernel, n=n, kvh=kvh, hd=hd),
        grid=(M // tm,),
        in_specs=[pl.BlockSpec((tm, w), functools.partial(lambda i, j: (i, col // w + j), j=j)) for j in range(n)],
        out_specs=[pl.BlockSpec((tm * kvh, hd), lambda i: (i, 0))] * n,
        out_shape=[jax.ShapeDtypeStruct((M * kvh, hd), F32)] * n,
        compiler_params=_params("parallel"),
        name="kv_split",
    )(*([main] * n))


def compress_blocks(x, B, L, kvh, pe, w1, w2):
    hd = x.shape[1]
    n = B * (L // CMP_BLOCK)
    assert hd == LANES and L % CMP_BLOCK == 0
    peb, w1b, w2b = _compress_weights(pe, w1, w2, hd)
    full = lambda a: pl.BlockSpec(a.shape, lambda i: (0,) * a.ndim)
    out = pl.pallas_call(
        functools.partial(_compress_kernel, kvh=kvh),
        grid=(1,),
        in_specs=[full(x), full(peb), full(w1b), full(w2b)],
        out_specs=pl.BlockSpec((n, kvh * hd), lambda i: (0, 0)),
        out_shape=jax.ShapeDtypeStruct((n, kvh * hd), F32),
        compiler_params=_params("arbitrary"),
        name="compress_blocks",
    )(x, peb, w1b, w2b)
    return out.reshape(B, L // CMP_BLOCK, kvh * hd)


def compress_paged(pool, page_table, pe, w1, w2):
    n_pool, page, kvh, hd = pool.shape
    B, NP = page_table.shape
    rpp = page * kvh
    n_blocks = NP * page // CMP_BLOCK
    assert hd == LANES and page % CMP_BLOCK == 0
    peb, w1b, w2b = _compress_weights(pe, w1, w2, hd)
    full = lambda a: pl.BlockSpec(a.shape, lambda b, pt: (0,) * a.ndim)
    return pl.pallas_call(
        functools.partial(_compress_paged_kernel, n_pages=NP, rows_per_page=rpp, kvh=kvh),
        grid_spec=pltpu.PrefetchScalarGridSpec(
            num_scalar_prefetch=1,
            grid=(B,),
            in_specs=[pl.BlockSpec(memory_space=pl.ANY), full(peb), full(w1b), full(w2b)],
            out_specs=pl.BlockSpec((1, n_blocks, kvh * hd), lambda b, pt: (b, 0, 0)),
            scratch_shapes=[pltpu.VMEM((NP * rpp, hd), F32), pltpu.SemaphoreType.DMA((1,))],
        ),
        out_shape=jax.ShapeDtypeStruct((B, n_blocks, kvh * hd), F32),
        compiler_params=_params("arbitrary"),
        name="compress_paged",
    )(page_table.reshape(-1), pool.reshape(n_pool, rpp, hd), peb, w1b, w2b)


def _bias_rows(dist, thr_ref, tab):
    bias = jnp.broadcast_to(tab[:, 0:1], dist.shape)
    for k in range(1, NUM_BUCKETS):
        bias = jnp.where(dist >= thr_ref[k], tab[:, k:k + 1], bias)
    return bias


def _sample_cmp_kernel(thr_ref, q_ref, kc_ref, vc_ref, tab_ref, oc_ref, idx_ref, *, qpos, n_pick, hd):
    n_cb = kc_ref.shape[1]
    qg = (q_ref[0, 0] * hd ** -0.5).astype(BF16)
    lane = lax.broadcasted_iota(jnp.int32, (NSA_G, n_cb), 1)
    dist = qpos - ((lane + 1) * CMP_BLOCK - 1)
    valid = dist >= 0
    s = lax.dot_general(qg, kc_ref[0].astype(BF16), _NT, preferred_element_type=F32)
    s = jnp.where(valid, s + _bias_rows(dist, thr_ref, tab_ref[0]), NEG_INF)
    e = jnp.exp(s - jnp.max(s, axis=-1, keepdims=True))
    p = e / jnp.sum(e, axis=-1, keepdims=True) * valid.astype(F32)
    oc_ref[0, 0] = jnp.dot(p.astype(BF16), vc_ref[0].astype(BF16), preferred_element_type=F32)

    imp = jnp.sum(p, axis=0, keepdims=True)
    imp2 = imp + pltpu.roll(imp, n_cb - 1, axis=1)
    lane1 = lax.broadcasted_iota(jnp.int32, (1, n_cb), 1)
    blk = lane1 >> 1
    cur = qpos // SEL_BLOCK
    forced = (blk == 0) | (blk == cur) | (blk == cur - 1)
    usable = ((lane1 & 1) == 0) & (blk * SEL_BLOCK <= qpos)
    work = jnp.where(usable, jnp.where(forced, jnp.inf, imp2), -jnp.inf)
    out_lane = lax.broadcasted_iota(jnp.int32, (1, LANES), 1)
    picks = jnp.zeros((1, LANES), jnp.int32)
    for it in range(n_pick):
        mx = jnp.max(work, axis=-1, keepdims=True)
        first = jnp.min(jnp.where(work == mx, lane1, n_cb), axis=-1, keepdims=True)
        picks = jnp.where(out_lane == it, first >> 1, picks)
        work = jnp.where(lane1 == first, -jnp.inf, work)
    idx_ref[0, 0] = picks


def _sample_attn_kernel(idx_ref, pt_ref, thr_ref, q_ref, ksn_ref, vsn_ref, kwin_ref, vwin_ref, kwn_ref, vwn_ref,
                        oc_ref, gate_ref, path_ref, tab_ref, *rest, qpos, hd, n_cached, n_pick, kvh):
    ksel_refs, vsel_refs, o_ref = rest[:n_pick], rest[n_pick:2 * n_pick], rest[2 * n_pick]
    b, h = pl.program_id(0), pl.program_id(1)
    G = NSA_G
    qg = (q_ref[0, 0] * hd ** -0.5).astype(BF16)
    qf = qg.astype(F32)
    tab = tab_ref[0]
    bias_new = _bias_rows(jnp.zeros((G, 1), jnp.int32), thr_ref, tab)

    def new_token_score(k_ref):
        return jnp.sum(qf * k_ref[0, 0].astype(BF16).astype(F32), axis=-1, keepdims=True) + bias_new

    def attend(k_tiles, v_tiles, dists, k_new_ref, v_new_ref, extra_valid):
        scores, valids = [], []
        for k_t, dist in zip(k_tiles, dists):
            row = lax.broadcasted_iota(jnp.int32, dist.shape, 1)
            valid = (row % kvh == h) & (dist >= 0) & extra_valid(dist)
            s = lax.dot_general(qg, k_t.astype(BF16), _NT, preferred_element_type=F32)
            scores.append(jnp.where(valid, s + _bias_rows(dist, thr_ref, tab), NEG_INF))
            valids.append(valid)
        s_new = new_token_score(k_new_ref)
        m = s_new
        for s in scores:
            m = jnp.maximum(m, jnp.max(s, axis=-1, keepdims=True))
        p_new = jnp.exp(s_new - m)
        l = p_new
        acc = p_new.astype(BF16).astype(F32) * v_new_ref[0, 0].astype(BF16).astype(F32)
        for s, valid, v_t in zip(scores, valids, v_tiles):
            p = jnp.where(valid, jnp.exp(s - m), 0.0)
            l = l + jnp.sum(p, axis=-1, keepdims=True)
            acc = acc + jnp.dot(p.astype(BF16), v_t.astype(BF16), preferred_element_type=F32)
        return acc / l

    rows = SEL_BLOCK * kvh
    tok = lax.broadcasted_iota(jnp.int32, (G, rows), 1) // kvh
    sel_dists = [qpos - (idx_ref[(b * kvh + h) * LANES + j] * SEL_BLOCK + tok) for j in range(n_pick)]
    o_sel = attend([r[...] for r in ksel_refs], [r[...] for r in vsel_refs], sel_dists, ksn_ref, vsn_ref,
                   lambda dist: dist >= 0)
    slot = lax.broadcasted_iota(jnp.int32, (G, n_cached * kvh), 1) // kvh
    wdist = n_cached - slot
    o_win = attend([kwin_ref[0]], [vwin_ref[0]], [wdist], kwn_ref, vwn_ref,
                   lambda dist: (dist < WINDOW) & (qpos - dist >= 0))
    gates = jax.nn.sigmoid(gate_ref[0, 0])
    o = gates[:, 0:1] * oc_ref[0, 0] + gates[:, 1:2] * o_sel + gates[:, 2:3] * o_win
    o_ref[0, 0] = o * jax.nn.silu(path_ref[0, 0])


def nsa_sample(q, kcmp, vcmp, pool_sk, pool_sv, page_table, new_kv, win_k, win_v, gate, path, table, past_len):
    B, KVH, G, hd = q.shape
    n_cb = kcmp.shape[1]
    n_pool, page = pool_sk.shape[:2]
    NP = page_table.shape[1]
    n_cached = win_k.shape[1]
    qpos = past_len
    n_sel = -(-(past_len + 1) // SEL_BLOCK)
    n_pick = min(N_SELECT, n_sel) - 1
    assert past_len % SEL_BLOCK == 0 and n_cb * CMP_BLOCK == past_len and 1 <= n_pick <= LANES
    bucket = rel_bucket(jnp.arange(past_len + 1, dtype=jnp.int32))
    thr = jnp.sum(bucket[None, :] < jnp.arange(NUM_BUCKETS, dtype=jnp.int32)[:, None], axis=1).astype(jnp.int32)
    tab = table.astype(F32).T.reshape(KVH, G, NUM_BUCKETS)
    head = lambda shape: pl.BlockSpec((1, 1) + shape, lambda b, h, *_: (b, h, 0, 0))
    cmp_spec = pl.BlockSpec((1, n_cb, hd), lambda b, h, *_: (b, 0, h))
    tab_spec = pl.BlockSpec((1, G, NUM_BUCKETS), lambda b, h, *_: (h, 0, 0))
    oc, idx = pl.pallas_call(
        functools.partial(_sample_cmp_kernel, qpos=qpos, n_pick=n_pick, hd=hd),
        grid_spec=pltpu.PrefetchScalarGridSpec(
            num_scalar_prefetch=1,
            grid=(B, KVH),
            in_specs=[head((G, hd)), cmp_spec, cmp_spec, tab_spec],
            out_specs=[head((G, hd)), head((1, LANES))],
        ),
        out_shape=[jax.ShapeDtypeStruct((B, KVH, G, hd), F32), jax.ShapeDtypeStruct((B, KVH, 1, LANES), jnp.int32)],
        compiler_params=_params("parallel", "parallel"),
        name="nsa_sample_cmp",
    )(thr, q, kcmp, vcmp, tab)

    halves = page // SEL_BLOCK
    rows = SEL_BLOCK * KVH

    def sel_spec(j):
        def index(b, h, idx_ref, pt_ref, thr_ref):
            blk = idx_ref[(b * KVH + h) * LANES + j]
            return (pt_ref[b * NP + blk // halves] * halves + blk % halves, 0)
        return pl.BlockSpec((rows, hd), index)

    win_spec = pl.BlockSpec((1, n_cached * KVH, hd), lambda b, h, *_: (b, 0, 0))
    new_spec = pl.BlockSpec((1, 1, 1, hd), lambda b, h, *_: (b, h, 0, 0))
    ks_n, vs_n, kw_n, vw_n = new_kv
    sel_k = pool_sk.reshape(n_pool * page * KVH, hd)
    sel_v = pool_sv.reshape(n_pool * page * KVH, hd)
    return pl.pallas_call(
        functools.partial(_sample_attn_kernel, qpos=qpos, hd=hd, n_cached=n_cached, n_pick=n_pick, kvh=KVH),
        grid_spec=pltpu.PrefetchScalarGridSpec(
            num_scalar_prefetch=3,
            grid=(B, KVH),
            in_specs=[head((G, hd)), new_spec, new_spec, win_spec, win_spec, new_spec, new_spec,
                      head((G, hd)), head((G, 3)), head((G, hd)), tab_spec]
                     + [sel_spec(j) for j in range(n_pick)] * 2,
            out_specs=head((G, hd)),
        ),
        out_shape=jax.ShapeDtypeStruct((B, KVH, G, hd), F32),
        compiler_params=_params("parallel", "parallel"),
        name="nsa_sample_attn",
    )(idx.reshape(-1), page_table.reshape(-1), thr, q, ks_n, vs_n, win_k.reshape(B, n_cached * KVH, hd),
      win_v.reshape(B, n_cached * KVH, hd), kw_n, vw_n, oc, gate, path, tab,
      *([sel_k] * n_pick), *([sel_v] * n_pick))


def rel_bucket(dist):
    d = jnp.maximum(dist, 0)
    me = NUM_BUCKETS // 2
    logd = jnp.log(jnp.maximum(d, 1).astype(F32) / me)
    large = me + (logd / math.log(MAX_DISTANCE / me) * (NUM_BUCKETS - me)).astype(jnp.int32)
    return jnp.where(d < me, d, jnp.minimum(large, NUM_BUCKETS - 1))


SAMPLE_ROWS = 16


class EvenCols:
    def __init__(self, hg_qk, hg_w, nsa_w, nsa_kvw):
        self.hd = nsa_w // NSA_HEADS
        self.kvw = nsa_kvw
        self.nsa_w = nsa_w
        self.q, self.f, self.v, self.g = 0, hg_qk, 2 * hg_qk, 2 * hg_qk + hg_w
        self.nq = 2 * hg_qk + 2 * hg_w
        self.kv = self.nq + nsa_w
        self.main_w = self.kv + 6 * nsa_kvw
        self.gate_w = 3 * NSA_G
        self.tail_path, self.tail_gate = 0, nsa_w
        used = nsa_w + NSA_KVH * LANES
        self.tail_w = -(-used // MATMUL_TN) * MATMUL_TN
        assert self.main_w % MATMUL_TN == 0

    def tail_weight(self, w_in_t):
        gate0 = self.main_w
        path0 = gate0 + NSA_KVH * self.gate_w
        parts = [w_in_t[path0:path0 + self.nsa_w]]
        for h in range(NSA_KVH):
            parts.append(jnp.pad(w_in_t[gate0 + h * self.gate_w:gate0 + (h + 1) * self.gate_w],
                                 ((0, LANES - self.gate_w), (0, 0))))
        w = jnp.concatenate(parts, axis=0)
        return jnp.pad(w, ((0, self.tail_w - w.shape[0]), (0, 0)))


def even_prompt(main, tail, B, T, lb, hg_gain, pe, w1k, w2k, w1v, w2v, table, ec):
    hd = ec.hd
    S0 = jnp.zeros((B, HG_HEADS, HG_DK, hg_gain.shape[0] // HG_HEADS), F32)
    y_hg, S = hgrn_mix(main, lb, hg_gain, S0, B, T, (ec.q, ec.f, ec.v, ec.g), HG_CHUNK, HGRN_SUB, HG_CHUNK)
    rows = kv_split(main, ec.kv, 6, NSA_KVH, hd)
    kcmp = compress_blocks(rows[0], B, T, NSA_KVH, pe, w1k, w2k)
    vcmp = compress_blocks(rows[1], B, T, NSA_KVH, pe, w1v, w2v)
    kc, vc, ks, vs, kw, vw = [r.reshape(B, T, NSA_KVH, hd) for r in rows]
    y_nsa = nsa_prompt(main, tail, kcmp, vcmp, table, B, T, (ec.nq, ec.kv, ec.tail_path, ec.tail_gate, hd))
    wb = min(WINDOW, T)
    return jnp.concatenate([y_hg, y_nsa], axis=-1), S, kc, vc, ks, vs, kw[:, T - wb:], vw[:, T - wb:]


def even_sample(main, tail, B, T, past_len, S0, page_table, pool_ck, pool_cv, pool_sk, pool_sv, win_k, win_v,
                lb, hg_gain, pe, w1k, w2k, w1v, w2v, table, ec):
    hd = ec.hd
    R = SAMPLE_ROWS
    y_hg, S = hgrn_mix(main, lb, hg_gain, S0.astype(F32), B, R, (ec.q, ec.f, ec.v, ec.g), R, R, T)
    m3 = main.reshape(B, R, -1)[:, :T]
    t3 = tail.reshape(B, R, -1)[:, :T]
    assert T == 1
    heads = lambda a, w: a.reshape(B, NSA_KVH, NSA_G, w)
    q = heads(m3[..., ec.nq:ec.kv], hd)
    path = heads(t3[..., ec.tail_path:ec.tail_path + ec.nsa_w], hd)
    gate = jnp.stack([t3[:, 0, ec.tail_gate + h * LANES:ec.tail_gate + h * LANES + ec.gate_w].reshape(B, NSA_G, 3)
                      for h in range(NSA_KVH)], axis=1)
    kvs = m3[..., ec.kv:ec.main_w].reshape(B, T, 6, NSA_KVH, hd)
    kc, vc, ks, vs, kw, vw = [kvs[:, :, j] for j in range(6)]
    new_kv = tuple(jnp.swapaxes(a, 1, 2) for a in (ks, vs, kw, vw))
    kcmp = compress_paged(pool_ck, page_table, pe, w1k, w2k)
    vcmp = compress_paged(pool_cv, page_table, pe, w1v, w2v)
    y_nsa = nsa_sample(q, kcmp, vcmp, pool_sk, pool_sv, page_table, new_kv, win_k, win_v, gate, path, table,
                       past_len)
    y_nsa = jnp.pad(y_nsa.reshape(B, T, -1).astype(BF16), ((0, 0), (0, R - T), (0, 0))).reshape(B * R, -1)
    kw_all = jnp.concatenate([win_k, kw], axis=1)
    vw_all = jnp.concatenate([win_v, vw], axis=1)
    return jnp.concatenate([y_hg, y_nsa], axis=-1), S, kc, vc, ks, vs, kw_all[:, T:], vw_all[:, T:]


def mem_attend(qg, B, T, mk, mv):
    slots = mk.shape[1]
    return mem_mix(qg, mk.reshape(B, slots, -1), mv.reshape(B, slots, -1), B, T)


def kernel(x_prompt, x_sample, mem_prompt, state_hgrn, cache_cmp_k, cache_cmp_v, cache_sel_k, cache_sel_v,
           cache_win_k, cache_win_v, state_ret, cache_mem_k, cache_mem_v, page_table, rel_table,
           norm_mix_pre, norm_mix_post, norm_mem_pre, norm_mem_post, ev_w_in, ev_w_out, hgrn_lb, hgrn_norm,
           cmp_pe, cmp_w1_k, cmp_w2_k, cmp_w1_v, cmp_w2_v, od_w_in, od_w_out, ret_norm,
           mem_w_q, mem_w_k, mem_w_v, mem_w_o):
    B, T, D = x_prompt.shape
    Bs, Ts, _ = x_sample.shape
    depth = norm_mix_pre.shape[0]
    past_len = page_table.shape[1] * PAGE_SIZE
    pos_p = jnp.arange(T, dtype=jnp.int32)
    pos_s = past_len + jnp.arange(Ts, dtype=jnp.int32)
    lb_all = jnp.cumsum(jax.nn.softmax(hgrn_lb.astype(F32), axis=0), axis=0)
    hg_qk = hgrn_lb.shape[1]
    hg_w = hgrn_norm.shape[1]
    nsa_w = ev_w_out.shape[1] - hg_w
    nsa_kvw = NSA_KVH * (nsa_w // NSA_HEADS)
    ec = EvenCols(hg_qk, hg_w, nsa_w, nsa_kvw)
    ev_w_in_t = jnp.swapaxes(ev_w_in, 1, 2)
    R = SAMPLE_ROWS
    assert Ts <= R

    even_p, even_s, ret_p, ret_s, memk_p, memv_p = [], [], [], [], [], []
    xp = x_prompt.reshape(B * T, D)
    xs = jnp.pad(x_sample, ((0, 0), (0, R - Ts), (0, 0))).reshape(Bs * R, D)
    pos_sr = past_len + jnp.arange(R, dtype=jnp.int32)
    mem2 = mem_prompt.reshape(-1, D).astype(BF16)
    ML = mem_prompt.shape[1]
    hp = rmsnorm_bf16(xp, norm_mix_pre[0])
    hs = rmsnorm_bf16(xs, norm_mix_pre[0])
    for l in range(depth):
        if l % 2 == 0:
            e = l // 2
            w = (lb_all[e], hgrn_norm[e], cmp_pe[e], cmp_w1_k[e], cmp_w2_k[e], cmp_w1_v[e], cmp_w2_v[e], rel_table)
            main_p, main_s = matmul(hp, ev_w_in_t, e, n_cols=ec.main_w, transposed=True, xs=hs)
            tail_p, tail_s = matmul(hp, ec.tail_weight(ev_w_in_t[e]), transposed=True, xs=hs)
            ap, *sp = even_prompt(main_p, tail_p, B, T, *w, ec)
            as_, *ss = even_sample(main_s, tail_s, Bs, Ts, past_len, state_hgrn[e], page_table, cache_cmp_k[e],
                                   cache_cmp_v[e], cache_sel_k[e], cache_sel_v[e], cache_win_k[e], cache_win_v[e],
                                   *w, ec)
            yp, ys = matmul(ap, ev_w_out, e, xs=as_)
            even_p.append(sp)
            even_s.append(ss)
        else:
            o = l // 2
            S0p = jnp.zeros((B,) + state_ret.shape[2:], F32)
            proj_p, proj_s = matmul(hp, od_w_in, o, xs=hs)
            ap, sp = retention_mix(proj_p, pos_p, ret_norm[o], S0p, B, T, RET_CHUNK, RET_CHUNK)
            as_, ss = retention_mix(proj_s, pos_sr, ret_norm[o], state_ret[o].astype(F32), Bs, R, R, Ts)
            yp, ys = matmul(ap, od_w_out, o, xs=as_)
            ret_p.append(sp)
            ret_s.append(ss)
        xp, hp = residual_post_pre(xp, yp, norm_mix_post[l], norm_mem_pre[l])
        xs, hs = residual_post_pre(xs, ys, norm_mix_post[l], norm_mem_pre[l])
        mk_p = matmul(mem2, mem_w_k, l).reshape(B, ML, MEM_HEADS, MEM_HD)
        mv_p = matmul(mem2, mem_w_v, l).reshape(B, ML, MEM_HEADS, MEM_HD)
        memk_p.append(mk_p)
        memv_p.append(mv_p)
        qg_p, qg_s = matmul(hp, mem_w_q, l, xs=hs)
        g_next = norm_mix_pre[l + 1] if l + 1 < depth else None
        xp, hp = matmul_residual(mem_attend(qg_p, B, T, mk_p, mv_p), mem_w_o, l, xp, norm_mem_post[l], g_next)
        xs, hs = matmul_residual(mem_attend(qg_s, Bs, R, cache_mem_k[l], cache_mem_v[l]), mem_w_o, l, xs,
                                 norm_mem_post[l], g_next)
    p_hgrn, p_cmp_k, p_cmp_v, p_sel_k, p_sel_v, p_win_k, p_win_v = [jnp.stack(a) for a in zip(*even_p)]
    s_hgrn, s_cmp_k, s_cmp_v, s_sel_k, s_sel_v, s_win_k, s_win_v = [jnp.stack(a) for a in zip(*even_s)]
    p_ret = jnp.stack(ret_p)
    s_ret = jnp.stack(ret_s)
    p_mem_k = jnp.stack(memk_p)
    p_mem_v = jnp.stack(memv_p)
    return (xp.reshape(B, T, D), xs.reshape(Bs, R, D)[:, :Ts], p_hgrn, p_cmp_k, p_cmp_v, p_sel_k, p_sel_v,
            p_win_k, p_win_v, p_ret, p_mem_k, p_mem_v,
            s_hgrn, s_cmp_k, s_cmp_v, s_sel_k, s_sel_v, s_win_k, s_win_v, s_ret)
```

```python
import functools
import math

import jax
import jax.numpy as jnp
import numpy as np
from jax import lax
from jax.experimental import pallas as pl
from jax.experimental.pallas import tpu as pltpu

F32 = jnp.float32
BF16 = jnp.bfloat16
EPS = 1e-6
NEG_INF = -1e30
LOG2E = math.log2(math.e)

PAGE_SIZE = 128
HG_HEADS = 16
HG_DK = 128
HG_CHUNK = 64
HGRN_SUB = 16
NSA_HEADS = 16
NSA_KVH = 2
NSA_G = NSA_HEADS // NSA_KVH
CMP_BLOCK = 32
SEL_BLOCK = 64
N_SELECT = 16
WINDOW = 512
SEL_QBLOCK = 64
WIN_QBLOCK = 128
NUM_BUCKETS = 32
MAX_DISTANCE = 1024
RET_HEADS = 16
RET_CHUNK = 128
ROPE_BASE = 10000.0
MEM_HEADS = 4
MEM_HD = 128
MEM_W = MEM_HEADS * MEM_HD

VMEM_LIMIT_BYTES = 56 * 1024 * 1024


def _params(*sem):
    return pltpu.CompilerParams(dimension_semantics=sem, vmem_limit_bytes=VMEM_LIMIT_BYTES)


def _rmsnorm_kernel(x_ref, g_ref, o_ref):
    x = x_ref[...]
    y = x * lax.rsqrt(jnp.mean(x * x, axis=-1, keepdims=True) + EPS)
    o_ref[...] = (y * g_ref[...]).astype(o_ref.dtype)


def rmsnorm_bf16(x, g):
    M, D = x.shape
    tm = min(M, 512)
    return pl.pallas_call(
        _rmsnorm_kernel,
        grid=(M // tm,),
        in_specs=[pl.BlockSpec((tm, D), lambda i: (i, 0)), pl.BlockSpec((1, D), lambda i: (0, 0))],
        out_specs=pl.BlockSpec((tm, D), lambda i: (i, 0)),
        out_shape=jax.ShapeDtypeStruct((M, D), BF16),
        compiler_params=_params("parallel"),
        name="rmsnorm_bf16",
    )(x, g.reshape(1, D))


def _post_pre_kernel(x_ref, y_ref, g_ref, gn_ref, o_ref, h_ref):
    y = y_ref[...]
    x = x_ref[...] + y * lax.rsqrt(jnp.mean(y * y, axis=-1, keepdims=True) + EPS) * g_ref[...]
    o_ref[...] = x
    h_ref[...] = (x * lax.rsqrt(jnp.mean(x * x, axis=-1, keepdims=True) + EPS) * gn_ref[...]).astype(h_ref.dtype)


def residual_post_pre(x, y, g, g_next):
    M, D = x.shape
    tm = min(M, 256)
    row = pl.BlockSpec((tm, D), lambda i: (i, 0))
    vec = pl.BlockSpec((1, D), lambda i: (0, 0))
    return pl.pallas_call(
        _post_pre_kernel,
        grid=(M // tm,),
        in_specs=[row, row, vec, vec],
        out_specs=[row, row],
        out_shape=[jax.ShapeDtypeStruct((M, D), F32), jax.ShapeDtypeStruct((M, D), BF16)],
        compiler_params=_params("parallel"),
        name="residual_post_pre",
    )(x, y, g.reshape(1, D), g_next.reshape(1, D))


def _matmul_post_kernel(*refs, has_next):
    if has_next:
        a_ref, w_ref, x_ref, g_ref, gn_ref, o_ref, h_ref, wb_ref = refs
    else:
        a_ref, w_ref, x_ref, g_ref, o_ref, wb_ref = refs

    @pl.when(pl.program_id(0) == 0)
    def _():
        wb_ref[...] = w_ref[...].astype(BF16)

    y = jnp.dot(a_ref[...], wb_ref[...], preferred_element_type=F32)
    x = x_ref[...] + y * lax.rsqrt(jnp.mean(y * y, axis=-1, keepdims=True) + EPS) * g_ref[...]
    o_ref[...] = x
    if has_next:
        h_ref[...] = (x * lax.rsqrt(jnp.mean(x * x, axis=-1, keepdims=True) + EPS) * gn_ref[...]).astype(h_ref.dtype)


def matmul_residual(a, w, layer, x, g, g_next=None):
    M, K = a.shape
    D = w.shape[-1]
    tm = min(M, 256)
    has_next = g_next is not None
    row = lambda width: pl.BlockSpec((tm, width), lambda i: (i, 0))
    vec = pl.BlockSpec((1, D), lambda i: (0, 0))
    in_specs = [row(K), pl.BlockSpec((None, K, D), lambda i: (layer, 0, 0)), row(D), vec]
    args = [a, w, x, g.reshape(1, D)]
    out_specs = [row(D)]
    out_shape = [jax.ShapeDtypeStruct((M, D), F32)]
    if has_next:
        in_specs.append(vec)
        args.append(g_next.reshape(1, D))
        out_specs.append(row(D))
        out_shape.append(jax.ShapeDtypeStruct((M, D), BF16))
    out = pl.pallas_call(
        functools.partial(_matmul_post_kernel, has_next=has_next),
        grid=(M // tm,),
        in_specs=in_specs,
        out_specs=out_specs,
        out_shape=out_shape,
        scratch_shapes=[pltpu.VMEM((K, D), BF16)],
        compiler_params=_params("arbitrary"),
        name="matmul_residual",
    )(*args)
    return out if has_next else (out[0], None)


MATMUL_KB = 4096
MATMUL_TN = 512
MATMUL_TM = 1024


def _matmul_kernel(*refs, transposed, paired, chained):
    n_in = 1 + paired
    xs_refs = refs[:n_in]
    w_ref = refs[n_in]
    acc_refs = refs[n_in + 1:n_in + 1 + n_in] if chained else (None,) * n_in
    o_refs = refs[-1 - n_in:-1]
    wb_ref = refs[-1]

    def emit(x_ref, acc_ref, o_ref):
        y = jnp.dot(x_ref[...], wb_ref[...], preferred_element_type=F32)
        o_ref[...] = y if acc_ref is None else acc_ref[...] + y

    @pl.when(pl.program_id(1) == 0)
    def _():
        w = w_ref[...]
        wb_ref[...] = (w.T if transposed else w).astype(BF16)
        if paired:
            emit(xs_refs[1], acc_refs[1], o_refs[1])

    emit(xs_refs[0], acc_refs[0], o_refs[0])


def matmul(x, w, layer=None, n_cols=None, transposed=False, xs=None):
    M, K = x.shape
    N = (w.shape[-2] if transposed else w.shape[-1]) if n_cols is None else n_cols
    kb = min(K, MATMUL_KB)
    tn, tm = min(MATMUL_TN, N), min(MATMUL_TM, M)
    assert N % tn == 0 and M % tm == 0 and K % kb == 0, (M, K, N)
    operands = [x] if xs is None else [x, xs]
    out = None
    for kk in range(K // kb):
        blk = (tn, kb) if transposed else (kb, tn)
        pick = (lambda n, kk=kk: (n, kk)) if transposed else (lambda n, kk=kk: (kk, n))
        if w.ndim == 3:
            w_spec = pl.BlockSpec((None,) + blk, lambda n, m, pick=pick: (layer,) + pick(n))
        else:
            w_spec = pl.BlockSpec(blk, lambda n, m, pick=pick: pick(n))
        in_specs = [pl.BlockSpec((tm, kb), lambda n, m, kk=kk: (m, kk))]
        out_specs = [pl.BlockSpec((tm, tn), lambda n, m: (m, n))]
        if xs is not None:
            in_specs.append(pl.BlockSpec((xs.shape[0], kb), lambda n, m, kk=kk: (0, kk)))
            out_specs.append(pl.BlockSpec((xs.shape[0], tn), lambda n, m: (0, n)))
        out = pl.pallas_call(
            functools.partial(_matmul_kernel, transposed=transposed, paired=xs is not None, chained=out is not None),
            grid=(N // tn, M // tm),
            in_specs=in_specs + [w_spec] + (out_specs if out is not None else []),
            out_specs=out_specs,
            out_shape=[jax.ShapeDtypeStruct((a.shape[0], N), F32) for a in operands],
            scratch_shapes=[pltpu.VMEM((kb, tn), BF16)],
            compiler_params=_params("parallel", "arbitrary"),
            name="matmul",
        )(*operands, w, *(out if out is not None else []))
    return out if xs is not None else out[0]


LANES = 128
NSA_TQ = 128
NSA_KC = 256
_NT = (((1,), (1,)), ((), ()))


_TN = (((0,), (0,)), ((), ()))


def _bias_tile_kernel(thr_ref, table_ref, o_ref, *, row_stride, row_offset):
    kvh = pl.program_id(0)
    step = pl.program_id(1)
    rows = o_ref.shape[2]
    row = lax.broadcasted_iota(jnp.int32, (rows, LANES), 0)
    lane = lax.broadcasted_iota(jnp.int32, (rows, LANES), 1)
    dist = LANES * step + lane - row_stride * row - row_offset
    for g in range(NSA_G):
        h = kvh * NSA_G + g
        bias = jnp.full((rows, LANES), table_ref[0, h], F32)
        for k in range(1, NUM_BUCKETS):
            bias = jnp.where(dist >= thr_ref[k], table_ref[k, h], bias)
        o_ref[0, 0, :, g * LANES:(g + 1) * LANES] = bias


def bias_tiles(table, n_steps, rows, max_dist, row_stride, row_offset):
    bucket = rel_bucket(jnp.arange(max_dist, dtype=jnp.int32))
    thr = jnp.sum(bucket[None, :] < jnp.arange(NUM_BUCKETS, dtype=jnp.int32)[:, None], axis=1).astype(jnp.int32)
    smem = pl.BlockSpec(memory_space=pltpu.SMEM)
    return pl.pallas_call(
        functools.partial(_bias_tile_kernel, row_stride=row_stride, row_offset=row_offset),
        grid=(NSA_KVH, n_steps),
        in_specs=[smem, smem],
        out_specs=pl.BlockSpec((1, 1, rows, NSA_G * LANES), lambda h, s: (h, s, 0, 0)),
        out_shape=jax.ShapeDtypeStruct((NSA_KVH, n_steps, rows, NSA_G * LANES), F32),
        compiler_params=_params("parallel", "parallel"),
        name="bias_tiles",
    )(thr, table)


def _nsa_prompt_kernel(q_ref, kc_ref, vc_ref, ks_ref, vs_ref, kw_ref, vw_ref, cb_ref, tb_ref, gate_ref, path_ref,
                       o_ref, qb_s, oc_s, m_s, l_s, acc_s, *, hd, n_win_chunks, n_pick):
    i = pl.program_id(2)
    tq = NSA_TQ
    G = NSA_G
    scale = hd ** -0.5 * LOG2E
    n_cb = kc_ref.shape[1]
    row_c = lax.broadcasted_iota(jnp.int32, (n_cb, tq), 0)
    qpos_c = i * tq + lax.broadcasted_iota(jnp.int32, (n_cb, tq), 1)
    row = lax.broadcasted_iota(jnp.int32, (NSA_KC, tq), 0)
    qpos = i * tq + lax.broadcasted_iota(jnp.int32, (NSA_KC, tq), 1)
    slab = lambda g: slice(g * tq, (g + 1) * tq)

    for g in range(G):
        qb_s[slab(g), :] = (q_ref[:, g * hd:(g + 1) * hd] * scale).astype(BF16)
    qb = qb_s[...]

    s_all = lax.dot_general(kc_ref[0].astype(BF16), qb, _NT, preferred_element_type=F32) + cb_ref[0, 0]
    valid_c = qpos_c >= (row_c + 1) * CMP_BLOCK - 1
    valid_cf = valid_c.astype(F32)
    imp = jnp.zeros((n_cb, tq), F32)
    ps = []
    for g in range(G):
        s = jnp.where(valid_c, s_all[:, slab(g)], NEG_INF)
        e = jnp.exp2(s - jnp.max(s, axis=0, keepdims=True))
        p = e * (1.0 / jnp.sum(e, axis=0, keepdims=True)) * valid_cf
        imp = imp + p
        ps.append(p.astype(BF16))
    oc_s[...] = lax.dot_general(vc_ref[0].astype(BF16), jnp.concatenate(ps, axis=1), _TN,
                                preferred_element_type=F32)

    imp2 = imp + pltpu.roll(imp, n_cb - 1, axis=0)
    blk = row_c >> 1
    cur = qpos_c // SEL_BLOCK
    forced = (blk == 0) | (blk == cur) | (blk == cur - 1)
    usable = ((row_c & 1) == 0) & (blk * SEL_BLOCK <= qpos_c)
    work = jnp.where(usable, jnp.where(forced, jnp.inf, imp2), -jnp.inf)
    sel = jnp.zeros((n_cb, tq), F32)
    for _ in range(n_pick):
        mx = jnp.max(work, axis=0, keepdims=True)
        first = jnp.min(jnp.where(work == mx, row_c, n_cb), axis=0, keepdims=True)
        pick = row_c == first
        sel = jnp.where(pick, 1.0, sel)
        work = jnp.where(pick, -jnp.inf, work)
    sel_b = sel.astype(BF16)

    m_s[...] = jnp.full(m_s.shape, NEG_INF, F32)
    l_s[...] = jnp.zeros(l_s.shape, F32)
    acc_s[...] = jnp.zeros(acc_s.shape, F32)
    ekey = lax.broadcasted_iota(jnp.int32, (NSA_KC, n_cb), 0)
    eblk = lax.broadcasted_iota(jnp.int32, (NSA_KC, n_cb), 1)
    tiles_per_chunk = NSA_KC // LANES

    def online(slot, k_b, v_b, bias, valid):
        s_all = lax.dot_general(k_b, qb, _NT, preferred_element_type=F32) + bias
        ps, alphas = [], []
        for g in range(G):
            s = jnp.where(valid, s_all[:, slab(g)], NEG_INF)
            m_old = m_s[slot, :, slab(g)]
            m_new = jnp.maximum(m_old, jnp.max(s, axis=0, keepdims=True))
            alpha = jnp.exp2(m_old - m_new)
            p = jnp.exp2(s - m_new)
            l_s[slot, :, slab(g)] = alpha * l_s[slot, :, slab(g)] + jnp.sum(p, axis=0, keepdims=True)
            m_s[slot, :, slab(g)] = m_new
            ps.append(p.astype(BF16))
            alphas.append(alpha)
        pv = lax.dot_general(v_b, jnp.concatenate(ps, axis=1), _TN, preferred_element_type=F32)
        acc_s[slot] = jnp.concatenate(alphas, axis=1) * acc_s[slot] + pv

    def chunk(c, carry):
        k0 = pl.multiple_of(c * NSA_KC, NSA_KC)
        delta = i - tiles_per_chunk * c
        dist = qpos - (k0 + row)
        causal = dist >= 0
        expand = (eblk == 2 * (c * (NSA_KC // SEL_BLOCK) + (ekey // SEL_BLOCK))).astype(BF16)
        chosen = jnp.dot(expand, sel_b, preferred_element_type=F32) > 0.5
        bias = jnp.concatenate([tb_ref[0, jnp.maximum(delta - t, 0)] for t in range(tiles_per_chunk)], axis=0)
        online(0, ks_ref[pl.ds(k0, NSA_KC), :].astype(BF16), vs_ref[pl.ds(k0, NSA_KC), :].astype(BF16), bias,
               chosen & causal)

        @pl.when(delta < n_win_chunks + tiles_per_chunk - 1)
        def _():
            online(1, kw_ref[pl.ds(k0, NSA_KC), :].astype(BF16), vw_ref[pl.ds(k0, NSA_KC), :].astype(BF16), bias,
                   causal & (dist < WINDOW))

        return carry

    lax.fori_loop(0, i // tiles_per_chunk + 1, chunk, 0)

    gates = jax.nn.sigmoid(gate_ref[...]).T
    for g in range(G):
        o_sel = acc_s[0, :, slab(g)] * (1.0 / l_s[0, :, slab(g)])
        o_win = acc_s[1, :, slab(g)] * (1.0 / l_s[1, :, slab(g)])
        o = (gates[3 * g:3 * g + 1] * oc_s[:, slab(g)] + gates[3 * g + 1:3 * g + 2] * o_sel
             + gates[3 * g + 2:3 * g + 3] * o_win)
        o_ref[:, g * hd:(g + 1) * hd] = (o.T * jax.nn.silu(path_ref[:, g * hd:(g + 1) * hd])).astype(o_ref.dtype)


def nsa_prompt(main, tail, kcmp, vcmp, table, B, T, cols):
    q_col, kv_col, path_col, gate_col, hd = cols
    G = NSA_G
    tq = NSA_TQ
    n_cb = kcmp.shape[1]
    assert n_cb % 8 == 0 and tq == LANES and T % NSA_KC == 0 and NSA_KC % LANES == 0 and hd == LANES
    nd = T // LANES
    n_win_chunks = WINDOW // LANES + 1
    table2 = table.astype(F32) * LOG2E
    tb = bias_tiles(table2, nd, LANES, T, 1, 0)
    cb = bias_tiles(table2, T // tq, n_cb, T, CMP_BLOCK, CMP_BLOCK - 1)
    nblk = T // tq
    gw = G * hd
    kv_spec = lambda j: pl.BlockSpec((T, hd), lambda b, h, i: (b, kv_col // hd + 2 * j + h))
    return pl.pallas_call(
        functools.partial(_nsa_prompt_kernel, hd=hd, n_win_chunks=n_win_chunks,
                          n_pick=min(N_SELECT, -(-T // SEL_BLOCK))),
        grid=(B, NSA_KVH, nblk),
        in_specs=[
            pl.BlockSpec((tq, gw), lambda b, h, i: (b * nblk + i, q_col // gw + h)),
            pl.BlockSpec((1, n_cb, hd), lambda b, h, i: (b, 0, h)),
            pl.BlockSpec((1, n_cb, hd), lambda b, h, i: (b, 0, h)),
            kv_spec(2), kv_spec(3), kv_spec(4), kv_spec(5),
            pl.BlockSpec((1, 1, n_cb, gw), lambda b, h, i: (h, i, 0, 0)),
            pl.BlockSpec((1, nd, LANES, gw), lambda b, h, i: (h, 0, 0, 0)),
            pl.BlockSpec((tq, LANES), lambda b, h, i: (b * nblk + i, gate_col // LANES + h)),
            pl.BlockSpec((tq, gw), lambda b, h, i: (b * nblk + i, path_col // gw + h)),
        ],
        out_specs=pl.BlockSpec((tq, gw), lambda b, h, i: (b * nblk + i, h)),
        out_shape=jax.ShapeDtypeStruct((B * T, NSA_KVH * gw), BF16),
        scratch_shapes=[pltpu.VMEM((G * tq, hd), BF16), pltpu.VMEM((hd, G * tq), F32),
                        pltpu.VMEM((2, 1, G * tq), F32), pltpu.VMEM((2, 1, G * tq), F32),
                        pltpu.VMEM((2, hd, G * tq), F32)],
        compiler_params=_params("parallel", "parallel", "arbitrary"),
        name="nsa_prompt",
    )(main, kcmp, vcmp, main, main, main, main, cb, tb, tail, tail)


_TN = (((0,), (0,)), ((), ()))


def _cumsum_rows(x, n):
    row = lax.broadcasted_iota(jnp.int32, x.shape, 0)
    sh = 1
    while sh < n:
        x = x + jnp.where(row >= sh, pltpu.roll(x, sh, axis=0), 0.0)
        sh *= 2
    return x


def _hgrn_kernel(q_ref, f_ref, v_ref, gate_ref, lb_ref, gain_ref, s0_ref, y_ref, s_out_ref, st_s,
                 *, C, SB, c_eff, dk, HB, n_sub):
    c = pl.program_id(2)

    @pl.when(c == 0)
    def _():
        for hh in range(HB):
            st_s[hh] = s0_ref[0, hh].T

    states = [st_s[hh] for hh in range(HB)]
    ys = [[None] * HB for _ in range(n_sub)]
    for hh in range(HB):
        for j in range(n_sub):
            ys[j][hh], states[hh] = _hgrn_head(slice(j * C, (j + 1) * C), slice(hh * dk, (hh + 1) * dk),
                                               states[hh], q_ref, f_ref, v_ref, gate_ref, lb_ref, gain_ref,
                                               C=C, SB=SB, c_eff=c_eff, dk=dk)
    y_ref[...] = jnp.concatenate([jnp.concatenate(row, axis=1) for row in ys], axis=0)
    for hh in range(HB):
        st_s[hh] = states[hh]

    @pl.when(c == pl.num_programs(2) - 1)
    def _():
        for hh in range(HB):
            s_out_ref[0, hh] = st_s[hh].T


def _hgrn_head(rows, sl, st, q_ref, f_ref, v_ref, gate_ref, lb_ref, gain_ref, *, C, SB, c_eff, dk):
    lb = lb_ref[:, sl]
    f = lb + (1.0 - lb) * jax.nn.sigmoid(f_ref[rows, sl])
    logf = jnp.log(f)
    k = 1.0 - f
    if c_eff < C:
        real = lax.broadcasted_iota(jnp.int32, (C, dk), 0) < c_eff
        logf = jnp.where(real, logf, 0.0)
        k = jnp.where(real, k, 0.0)
    q = q_ref[rows, sl] * dk ** -0.5
    v_b = v_ref[rows, sl].astype(BF16)
    b = _cumsum_rows(logf, C)
    o = lax.dot_general((q * jnp.exp(b)).astype(BF16), st.astype(BF16), _NT, preferred_element_type=F32)

    lane_c = lax.broadcasted_iota(jnp.int32, (SB, C), 1)
    row_c = lax.broadcasted_iota(jnp.int32, (SB, C), 0)
    outs = []
    for I in range(C // SB):
        r0 = I * SB
        q_i = q[r0:r0 + SB]
        b_i = b[r0:r0 + SB]
        if I > 0:
            b_r = b[r0 - 1:r0]
            qq = (q_i * jnp.exp(b_i - b_r)).astype(BF16)
            kk = (k * jnp.exp(jnp.minimum(b_r - b, 0.0))).astype(BF16)
            a = lax.dot_general(qq, kk, _NT, preferred_element_type=F32)
            a = jnp.where(lane_c < r0, a, 0.0)
        else:
            a = jnp.zeros((SB, C), F32)
        for s in range(SB):
            z = q_i * k[r0 + s:r0 + s + 1] * jnp.exp(jnp.minimum(b_i - b[r0 + s:r0 + s + 1], 0.0))
            col = jnp.sum(z, axis=-1, keepdims=True)
            a = jnp.where((lane_c == r0 + s) & (row_c >= s), col, a)
        outs.append(jnp.dot(a.astype(BF16), v_b, preferred_element_type=F32))
    o = o + jnp.concatenate(outs, axis=0)

    b_last = b[C - 1:C]
    kd = (k * jnp.exp(b_last - b)).astype(BF16)
    st_new = jnp.exp(b_last) * st + lax.dot_general(v_b, kd, _TN, preferred_element_type=F32)
    y = o * lax.rsqrt(jnp.mean(o * o, axis=-1, keepdims=True) + EPS) * gain_ref[:, sl]
    return (y * jax.nn.silu(gate_ref[rows, sl])).astype(BF16), st_new


HGRN_HEADS_PER_STEP = 8
HGRN_CHUNKS_PER_STEP = 4


def hgrn_mix(main, lb, gain, s0, B, T, cols, C, SB, c_eff):
    q_col, f_col, v_col, g_col = cols
    _, H, dk, dv = s0.shape
    HB = HGRN_HEADS_PER_STEP
    W = HB * LANES
    assert dk == LANES and dv == LANES and T % C == 0 and C % SB == 0 and H % HB == 0
    assert all(off % W == 0 for off in cols)
    n_sub = HGRN_CHUNKS_PER_STEP if T % (HGRN_CHUNKS_PER_STEP * C) == 0 else 1
    R = n_sub * C
    nc = T // R
    col = lambda off: pl.BlockSpec((R, W), lambda b, h, c: (b * nc + c, off // W + h))
    vec = pl.BlockSpec((1, W), lambda b, h, c: (0, h))
    st = pl.BlockSpec((1, HB, dk, dv), lambda b, h, c: (b, h, 0, 0))
    return pl.pallas_call(
        functools.partial(_hgrn_kernel, C=C, SB=SB, c_eff=c_eff, dk=dk, HB=HB, n_sub=n_sub),
        grid=(B, H // HB, nc),
        in_specs=[col(q_col), col(f_col), col(v_col), col(g_col), vec, vec, st],
        out_specs=[pl.BlockSpec((R, W), lambda b, h, c: (b * nc + c, h)), st],
        out_shape=[jax.ShapeDtypeStruct((B * T, H * dv), BF16), jax.ShapeDtypeStruct(s0.shape, F32)],
        scratch_shapes=[pltpu.VMEM((HB, dv, dk), F32)],
        compiler_params=_params("parallel", "parallel", "arbitrary"),
        name="hgrn_mix",
    )(main, main, main, main, lb.reshape(1, -1), gain.reshape(1, -1), s0)


def _retention_kernel(q_ref, k_ref, v_ref, g_ref, cos_ref, sin_ref, lg_ref, gain_ref, s0_ref, y_ref, s_out_ref, s_s,
                      *, C, c_eff, dk, dv, HB, n_sub):
    c = pl.program_id(2)

    @pl.when(c == 0)
    def _():
        s_s[...] = s0_ref[0]

    for hh in range(HB):
        state = s_s[hh]
        for j in range(n_sub):
            state = _retention_head(slice(j * C, (j + 1) * C), hh, state, q_ref, k_ref, v_ref, g_ref, cos_ref,
                                    sin_ref, lg_ref, gain_ref, y_ref, C=C, c_eff=c_eff, dk=dk, dv=dv)
        s_s[hh] = state

    @pl.when(c == pl.num_programs(2) - 1)
    def _():
        s_out_ref[0] = s_s[...]


def _retention_head(rows, hh, s_old, q_ref, k_ref, v_ref, g_ref, cos_ref, sin_ref, lg_ref, gain_ref, y_ref,
                    *, C, c_eff, dk, dv):
    half = dk // 2
    vsl = slice(hh * dv, (hh + 1) * dv)
    cos = cos_ref[rows]
    sin = sin_ref[rows]
    lg_w = lg_ref[hh]
    lg = lg_w[:, :LANES]
    row = lax.broadcasted_iota(jnp.int32, (C, LANES), 0).astype(F32)

    def rot(ref, w):
        x1 = ref[rows, hh * dk:hh * dk + half]
        x2 = ref[rows, hh * dk + half:(hh + 1) * dk]
        return jnp.concatenate([(x1 * cos - x2 * sin) * w, (x1 * sin + x2 * cos) * w], axis=1)

    q = rot(q_ref, dk ** -0.5)
    k = rot(k_ref, 1.0)
    v_b = v_ref[rows, vsl].astype(BF16)

    a = lax.dot_general(q.astype(BF16), k.astype(BF16), _NT, preferred_element_type=F32)
    ti = lax.broadcasted_iota(jnp.int32, (C, C), 0)
    si = lax.broadcasted_iota(jnp.int32, (C, C), 1)
    diff = (ti - si).astype(F32)
    a = a * jnp.where(diff >= 0, jnp.exp(diff * lg_w[:, :C]), 0.0)
    inner = jnp.dot(a.astype(BF16), v_b, preferred_element_type=F32)
    q_w = jnp.exp((row + 1.0) * lg)
    q_dec = q * jnp.concatenate([q_w] * (dk // LANES), axis=1)
    cross = jnp.dot(q_dec.astype(BF16), s_old.astype(BF16), preferred_element_type=F32)
    k_w = jnp.where(row < c_eff, jnp.exp((c_eff - 1.0 - row) * lg), 0.0)
    k_dec = k * jnp.concatenate([k_w] * (dk // LANES), axis=1)
    s_new = jnp.exp(c_eff * lg_w) * s_old + lax.dot_general(k_dec.astype(BF16), v_b, _TN, preferred_element_type=F32)
    o = inner + cross
    cen = o - jnp.mean(o, axis=-1, keepdims=True)
    y = cen * lax.rsqrt(jnp.mean(cen * cen, axis=-1, keepdims=True) + EPS) * gain_ref[:, vsl]
    y_ref[rows, vsl] = (y * jax.nn.silu(g_ref[rows, vsl])).astype(y_ref.dtype)
    return s_new


RETENTION_HEADS_PER_STEP = 4
RETENTION_CHUNKS_PER_STEP = 2


def retention_mix(proj, pos, gain, s0, B, T, C, c_eff):
    _, H, dk, dv = s0.shape
    HB = RETENTION_HEADS_PER_STEP
    assert T % C == 0 and C <= LANES and dk % LANES == 0 and H % HB == 0
    n_sub = RETENTION_CHUNKS_PER_STEP if T % (RETENTION_CHUNKS_PER_STEP * C) == 0 else 1
    R = n_sub * C
    nc = T // R
    nh = H // HB
    half = dk // 2
    inv = ROPE_BASE ** (-jnp.arange(half, dtype=F32) / half)
    ang = pos.astype(F32)[:, None] * inv[None, :]
    log_gamma = jnp.log1p(-jnp.exp2(-5.0 - jnp.arange(H, dtype=F32)))
    lg = jnp.broadcast_to(log_gamma[:, None, None], (H, 1, dv))
    qk = lambda j: pl.BlockSpec((R, HB * dk), lambda b, h, c: (b * nc + c, j * nh + h))
    vg = lambda j: pl.BlockSpec((R, HB * dv), lambda b, h, c: (b * nc + c, (2 * H * dk) // (HB * dv) + j * nh + h))
    tab = pl.BlockSpec((R, half), lambda b, h, c: (c, 0))
    st = pl.BlockSpec((1, HB, dk, dv), lambda b, h, c: (b, h, 0, 0))
    return pl.pallas_call(
        functools.partial(_retention_kernel, C=C, c_eff=c_eff, dk=dk, dv=dv, HB=HB, n_sub=n_sub),
        grid=(B, nh, nc),
        in_specs=[qk(0), qk(1), vg(0), vg(1), tab, tab,
                  pl.BlockSpec((HB, 1, dv), lambda b, h, c: (h, 0, 0)),
                  pl.BlockSpec((1, HB * dv), lambda b, h, c: (0, h)), st],
        out_specs=[pl.BlockSpec((R, HB * dv), lambda b, h, c: (b * nc + c, h)), st],
        out_shape=[jax.ShapeDtypeStruct((B * T, H * dv), BF16), jax.ShapeDtypeStruct(s0.shape, F32)],
        scratch_shapes=[pltpu.VMEM((HB, dk, dv), F32)],
        compiler_params=_params("parallel", "parallel", "arbitrary"),
        name="retention_mix",
    )(proj, proj, proj, proj, jnp.cos(ang), jnp.sin(ang), lg, gain.reshape(1, -1), s0)


def _mem_kernel(qg_ref, mk_ref, mv_ref, o_ref, *, heads, hd):
    for h in range(heads):
        sl = slice(h * hd, (h + 1) * hd)
        q = (qg_ref[:, sl] * hd ** -0.5).astype(BF16)
        s = lax.dot_general(q, mk_ref[0, :, sl].astype(BF16), _NT, preferred_element_type=F32)
        e = jnp.exp(s - jnp.max(s, axis=-1, keepdims=True))
        p = e * (1.0 / jnp.sum(e, axis=-1, keepdims=True))
        o = jnp.dot(p.astype(BF16), mv_ref[0, :, sl].astype(BF16), preferred_element_type=F32)
        gate = qg_ref[:, heads * hd + h * hd:heads * hd + (h + 1) * hd]
        o_ref[:, sl] = (o * jax.nn.silu(gate)).astype(o_ref.dtype)


def mem_mix(qg, mk, mv, B, T):
    W = mk.shape[2]
    tq = min(T, 256)
    nb = T // tq
    kv = pl.BlockSpec((1, mk.shape[1], W), lambda b, i: (b, 0, 0))
    return pl.pallas_call(
        functools.partial(_mem_kernel, heads=MEM_HEADS, hd=W // MEM_HEADS),
        grid=(B, nb),
        in_specs=[pl.BlockSpec((tq, 2 * W), lambda b, i: (b * nb + i, 0)), kv, kv],
        out_specs=pl.BlockSpec((tq, W), lambda b, i: (b * nb + i, 0)),
        out_shape=jax.ShapeDtypeStruct((B * T, W), BF16),
        compiler_params=_params("parallel", "arbitrary"),
        name="mem_mix",
    )(qg, mk, mv)


def _compress_rows(x_ref, n_blocks, pe_ref, w1_ref, w2_ref, kvh):
    stride = CMP_BLOCK * kvh
    outs = []
    for h in range(kvh):
        acc = jnp.zeros((n_blocks, w1_ref.shape[2]), F32)
        for j in range(0, CMP_BLOCK, 2):
            xa = x_ref[pl.ds(j * kvh + h, n_blocks, stride=stride), :] + pe_ref[j:j + 1]
            xb = x_ref[pl.ds((j + 1) * kvh + h, n_blocks, stride=stride), :] + pe_ref[j + 1:j + 2]
            x2 = jnp.concatenate([xa, xb], axis=1).astype(BF16)
            acc = acc + jnp.dot(x2, w1_ref[j // 2], preferred_element_type=F32)
        outs.append(jnp.dot(jax.nn.silu(acc).astype(BF16), w2_ref[...], preferred_element_type=F32))
    return jnp.concatenate(outs, axis=1)


def _compress_kernel(x_ref, pe_ref, w1_ref, w2_ref, o_ref, *, kvh):
    o_ref[...] = _compress_rows(x_ref, o_ref.shape[0], pe_ref, w1_ref, w2_ref, kvh)


def _compress_paged_kernel(pt_ref, pool_ref, pe_ref, w1_ref, w2_ref, o_ref, x_s, sem, *, n_pages, rows_per_page,
                           kvh):
    b = pl.program_id(0)

    def page_copy(p):
        return pltpu.make_async_copy(pool_ref.at[pt_ref[b * n_pages + p]],
                                     x_s.at[pl.ds(p * rows_per_page, rows_per_page)], sem.at[0])

    def start(p, carry):
        page_copy(p).start()
        return carry

    def wait(p, carry):
        page_copy(p).wait()
        return carry

    lax.fori_loop(0, n_pages, start, 0)
    lax.fori_loop(0, n_pages, wait, 0)
    o_ref[0] = _compress_rows(x_s, o_ref.shape[1], pe_ref, w1_ref, w2_ref, kvh)


def _compress_weights(pe, w1, w2, hd):
    return pe.astype(F32), w1.reshape(CMP_BLOCK // 2, 2 * hd, w1.shape[1]).astype(BF16), w2.astype(BF16)


def _kv_split_kernel(*refs, n, kvh, hd):
    for x_ref, o_ref in zip(refs[:n], refs[n:]):
        for h in range(kvh):
            o_ref[pl.ds(h, x_ref.shape[0], stride=kvh), :] = x_ref[:, h * hd:(h + 1) * hd]


def kv_split(main, col, n, kvh, hd):
    M = main.shape[0]
    tm = min(M, 512)
    w = kvh * hd
    assert col % w == 0 and M % tm == 0
    return pl.pallas_call(
        functools.partial(_kv_split_kernel, n=n, kvh=kvh, hd=hd),
        grid=(M // tm,),
        in_specs=[pl.BlockSpec((tm, w), functools.partial(lambda i, j: (i, col // w + j), j=j)) for j in range(n)],
        out_specs=[pl.BlockSpec((tm * kvh, hd), lambda i: (i, 0))] * n,
        out_shape=[jax.ShapeDtypeStruct((M * kvh, hd), F32)] * n,
        compiler_params=_params("parallel"),
        name="kv_split",
    )(*([main] * n))


def compress_blocks(x, B, L, kvh, pe, w1, w2):
    hd = x.shape[1]
    n = B * (L // CMP_BLOCK)
    assert hd == LANES and L % CMP_BLOCK == 0
    peb, w1b, w2b = _compress_weights(pe, w1, w2, hd)
    full = lambda a: pl.BlockSpec(a.shape, lambda i: (0,) * a.ndim)
    out = pl.pallas_call(
        functools.partial(_compress_kernel, kvh=kvh),
        grid=(1,),
        in_specs=[full(x), full(peb), full(w1b), full(w2b)],
        out_specs=pl.BlockSpec((n, kvh * hd), lambda i: (0, 0)),
        out_shape=jax.ShapeDtypeStruct((n, kvh * hd), F32),
        compiler_params=_params("arbitrary"),
        name="compress_blocks",
    )(x, peb, w1b, w2b)
    return out.reshape(B, L // CMP_BLOCK, kvh * hd)


def compress_paged(pool, page_table, pe, w1, w2):
    n_pool, page, kvh, hd = pool.shape
    B, NP = page_table.shape
    rpp = page * kvh
    n_blocks = NP * page // CMP_BLOCK
    assert hd == LANES and page % CMP_BLOCK == 0
    peb, w1b, w2b = _compress_weights(pe, w1, w2, hd)
    full = lambda a: pl.BlockSpec(a.shape, lambda b, pt: (0,) * a.ndim)
    return pl.pallas_call(
        functools.partial(_compress_paged_kernel, n_pages=NP, rows_per_page=rpp, kvh=kvh),
        grid_spec=pltpu.PrefetchScalarGridSpec(
            num_scalar_prefetch=1,
            grid=(B,),
            in_specs=[pl.BlockSpec(memory_space=pl.ANY), full(peb), full(w1b), full(w2b)],
            out_specs=pl.BlockSpec((1, n_blocks, kvh * hd), lambda b, pt: (b, 0, 0)),
            scratch_shapes=[pltpu.VMEM((NP * rpp, hd), F32), pltpu.SemaphoreType.DMA((1,))],
        ),
        out_shape=jax.ShapeDtypeStruct((B, n_blocks, kvh * hd), F32),
        compiler_params=_params("arbitrary"),
        name="compress_paged",
    )(page_table.reshape(-1), pool.reshape(n_pool, rpp, hd), peb, w1b, w2b)


def _bias_rows(dist, thr_ref, tab):
    bias = jnp.broadcast_to(tab[:, 0:1], dist.shape)
    for k in range(1, NUM_BUCKETS):
        bias = jnp.where(dist >= thr_ref[k], tab[:, k:k + 1], bias)
    return bias


def _sample_cmp_kernel(thr_ref, q_ref, kc_ref, vc_ref, tab_ref, oc_ref, idx_ref, *, qpos, n_pick, hd):
    n_cb = kc_ref.shape[1]
    qg = (q_ref[0, 0] * hd ** -0.5).astype(BF16)
    lane = lax.broadcasted_iota(jnp.int32, (NSA_G, n_cb), 1)
    dist = qpos - ((lane + 1) * CMP_BLOCK - 1)
    valid = dist >= 0
    s = lax.dot_general(qg, kc_ref[0].astype(BF16), _NT, preferred_element_type=F32)
    s = jnp.where(valid, s + _bias_rows(dist, thr_ref, tab_ref[0]), NEG_INF)
    e = jnp.exp(s - jnp.max(s, axis=-1, keepdims=True))
    p = e / jnp.sum(e, axis=-1, keepdims=True) * valid.astype(F32)
    oc_ref[0, 0] = jnp.dot(p.astype(BF16), vc_ref[0].astype(BF16), preferred_element_type=F32)

    imp = jnp.sum(p, axis=0, keepdims=True)
    imp2 = imp + pltpu.roll(imp, n_cb - 1, axis=1)
    lane1 = lax.broadcasted_iota(jnp.int32, (1, n_cb), 1)
    blk = lane1 >> 1
    cur = qpos // SEL_BLOCK
    forced = (blk == 0) | (blk == cur) | (blk == cur - 1)
    usable = ((lane1 & 1) == 0) & (blk * SEL_BLOCK <= qpos)
    work = jnp.where(usable, jnp.where(forced, jnp.inf, imp2), -jnp.inf)
    out_lane = lax.broadcasted_iota(jnp.int32, (1, LANES), 1)
    picks = jnp.zeros((1, LANES), jnp.int32)
    for it in range(n_pick):
        mx = jnp.max(work, axis=-1, keepdims=True)
        first = jnp.min(jnp.where(work == mx, lane1, n_cb), axis=-1, keepdims=True)
        picks = jnp.where(out_lane == it, first >> 1, picks)
        work = jnp.where(lane1 == first, -jnp.inf, work)
    idx_ref[0, 0] = picks


def _sample_attn_kernel(idx_ref, pt_ref, thr_ref, q_ref, ksn_ref, vsn_ref, kwin_ref, vwin_ref, kwn_ref, vwn_ref,
                        oc_ref, gate_ref, path_ref, tab_ref, *rest, qpos, hd, n_cached, n_pick, kvh):
    ksel_refs, vsel_refs, o_ref = rest[:n_pick], rest[n_pick:2 * n_pick], rest[2 * n_pick]
    b, h = pl.program_id(0), pl.program_id(1)
    G = NSA_G
    qg = (q_ref[0, 0] * hd ** -0.5).astype(BF16)
    qf = qg.astype(F32)
    tab = tab_ref[0]
    bias_new = _bias_rows(jnp.zeros((G, 1), jnp.int32), thr_ref, tab)

    def new_token_score(k_ref):
        return jnp.sum(qf * k_ref[0, 0].astype(BF16).astype(F32), axis=-1, keepdims=True) + bias_new

    def attend(k_tiles, v_tiles, dists, k_new_ref, v_new_ref, extra_valid):
        scores, valids = [], []
        for k_t, dist in zip(k_tiles, dists):
            row = lax.broadcasted_iota(jnp.int32, dist.shape, 1)
            valid = (row % kvh == h) & (dist >= 0) & extra_valid(dist)
            s = lax.dot_general(qg, k_t.astype(BF16), _NT, preferred_element_type=F32)
            scores.append(jnp.where(valid, s + _bias_rows(dist, thr_ref, tab), NEG_INF))
            valids.append(valid)
        s_new = new_token_score(k_new_ref)
        m = s_new
        for s in scores:
            m = jnp.maximum(m, jnp.max(s, axis=-1, keepdims=True))
        p_new = jnp.exp(s_new - m)
        l = p_new
        acc = p_new.astype(BF16).astype(F32) * v_new_ref[0, 0].astype(BF16).astype(F32)
        for s, valid, v_t in zip(scores, valids, v_tiles):
            p = jnp.where(valid, jnp.exp(s - m), 0.0)
            l = l + jnp.sum(p, axis=-1, keepdims=True)
            acc = acc + jnp.dot(p.astype(BF16), v_t.astype(BF16), preferred_element_type=F32)
        return acc / l

    rows = SEL_BLOCK * kvh
    tok = lax.broadcasted_iota(jnp.int32, (G, rows), 1) // kvh
    sel_dists = [qpos - (idx_ref[(b * kvh + h) * LANES + j] * SEL_BLOCK + tok) for j in range(n_pick)]
    o_sel = attend([r[...] for r in ksel_refs], [r[...] for r in vsel_refs], sel_dists, ksn_ref, vsn_ref,
                   lambda dist: dist >= 0)
    slot = lax.broadcasted_iota(jnp.int32, (G, n_cached * kvh), 1) // kvh
    wdist = n_cached - slot
    o_win = attend([kwin_ref[0]], [vwin_ref[0]], [wdist], kwn_ref, vwn_ref,
                   lambda dist: (dist < WINDOW) & (qpos - dist >= 0))
    gates = jax.nn.sigmoid(gate_ref[0, 0])
    o = gates[:, 0:1] * oc_ref[0, 0] + gates[:, 1:2] * o_sel + gates[:, 2:3] * o_win
    o_ref[0, 0] = o * jax.nn.silu(path_ref[0, 0])


def nsa_sample(q, kcmp, vcmp, pool_sk, pool_sv, page_table, new_kv, win_k, win_v, gate, path, table, past_len):
    B, KVH, G, hd = q.shape
    n_cb = kcmp.shape[1]
    n_pool, page = pool_sk.shape[:2]
    NP = page_table.shape[1]
    n_cached = win_k.shape[1]
    qpos = past_len
    n_sel = -(-(past_len + 1) // SEL_BLOCK)
    n_pick = min(N_SELECT, n_sel) - 1
    assert past_len % SEL_BLOCK == 0 and n_cb * CMP_BLOCK == past_len and 1 <= n_pick <= LANES
    bucket = rel_bucket(jnp.arange(past_len + 1, dtype=jnp.int32))
    thr = jnp.sum(bucket[None, :] < jnp.arange(NUM_BUCKETS, dtype=jnp.int32)[:, None], axis=1).astype(jnp.int32)
    tab = table.astype(F32).T.reshape(KVH, G, NUM_BUCKETS)
    head = lambda shape: pl.BlockSpec((1, 1) + shape, lambda b, h, *_: (b, h, 0, 0))
    cmp_spec = pl.BlockSpec((1, n_cb, hd), lambda b, h, *_: (b, 0, h))
    tab_spec = pl.BlockSpec((1, G, NUM_BUCKETS), lambda b, h, *_: (h, 0, 0))
    oc, idx = pl.pallas_call(
        functools.partial(_sample_cmp_kernel, qpos=qpos, n_pick=n_pick, hd=hd),
        grid_spec=pltpu.PrefetchScalarGridSpec(
            num_scalar_prefetch=1,
            grid=(B, KVH),
            in_specs=[head((G, hd)), cmp_spec, cmp_spec, tab_spec],
            out_specs=[head((G, hd)), head((1, LANES))],
        ),
        out_shape=[jax.ShapeDtypeStruct((B, KVH, G, hd), F32), jax.ShapeDtypeStruct((B, KVH, 1, LANES), jnp.int32)],
        compiler_params=_params("parallel", "parallel"),
        name="nsa_sample_cmp",
    )(thr, q, kcmp, vcmp, tab)

    halves = page // SEL_BLOCK
    rows = SEL_BLOCK * KVH

    def sel_spec(j):
        def index(b, h, idx_ref, pt_ref, thr_ref):
            blk = idx_ref[(b * KVH + h) * LANES + j]
            return (pt_ref[b * NP + blk // halves] * halves + blk % halves, 0)
        return pl.BlockSpec((rows, hd), index)

    win_spec = pl.BlockSpec((1, n_cached * KVH, hd), lambda b, h, *_: (b, 0, 0))
    new_spec = pl.BlockSpec((1, 1, 1, hd), lambda b, h, *_: (b, h, 0, 0))
    ks_n, vs_n, kw_n, vw_n = new_kv
    sel_k = pool_sk.reshape(n_pool * page * KVH, hd)
    sel_v = pool_sv.reshape(n_pool * page * KVH, hd)
    return pl.pallas_call(
        functools.partial(_sample_attn_kernel, qpos=qpos, hd=hd, n_cached=n_cached, n_pick=n_pick, kvh=KVH),
        grid_spec=pltpu.PrefetchScalarGridSpec(
            num_scalar_prefetch=3,
            grid=(B, KVH),
            in_specs=[head((G, hd)), new_spec, new_spec, win_spec, win_spec, new_spec, new_spec,
                      head((G, hd)), head((G, 3)), head((G, hd)), tab_spec]
                     + [sel_spec(j) for j in range(n_pick)] * 2,
            out_specs=head((G, hd)),
        ),
        out_shape=jax.ShapeDtypeStruct((B, KVH, G, hd), F32),
        compiler_params=_params("parallel", "parallel"),
        name="nsa_sample_attn",
    )(idx.reshape(-1), page_table.reshape(-1), thr, q, ks_n, vs_n, win_k.reshape(B, n_cached * KVH, hd),
      win_v.reshape(B, n_cached * KVH, hd), kw_n, vw_n, oc, gate, path, tab,
      *([sel_k] * n_pick), *([sel_v] * n_pick))


def rel_bucket(dist):
    d = jnp.maximum(dist, 0)
    me = NUM_BUCKETS // 2
    logd = jnp.log(jnp.maximum(d, 1).astype(F32) / me)
    large = me + (logd / math.log(MAX_DISTANCE / me) * (NUM_BUCKETS - me)).astype(jnp.int32)
    return jnp.where(d < me, d, jnp.minimum(large, NUM_BUCKETS - 1))


SAMPLE_ROWS = 16


class EvenCols:
    def __init__(self, hg_qk, hg_w, nsa_w, nsa_kvw):
        self.hd = nsa_w // NSA_HEADS
        self.kvw = nsa_kvw
        self.nsa_w = nsa_w
        self.q, self.f, self.v, self.g = 0, hg_qk, 2 * hg_qk, 2 * hg_qk + hg_w
        self.nq = 2 * hg_qk + 2 * hg_w
        self.kv = self.nq + nsa_w
        self.main_w = self.kv + 6 * nsa_kvw
        self.gate_w = 3 * NSA_G
        self.tail_path, self.tail_gate = 0, nsa_w
        used = nsa_w + NSA_KVH * LANES
        self.tail_w = -(-used // MATMUL_TN) * MATMUL_TN
        assert self.main_w % MATMUL_TN == 0

    def tail_weight(self, w_in_t):
        gate0 = self.main_w
        path0 = gate0 + NSA_KVH * self.gate_w
        parts = [w_in_t[path0:path0 + self.nsa_w]]
        for h in range(NSA_KVH):
            parts.append(jnp.pad(w_in_t[gate0 + h * self.gate_w:gate0 + (h + 1) * self.gate_w],
                                 ((0, LANES - self.gate_w), (0, 0))))
        w = jnp.concatenate(parts, axis=0)
        return jnp.pad(w, ((0, self.tail_w - w.shape[0]), (0, 0)))


def even_prompt(main, tail, B, T, lb, hg_gain, pe, w1k, w2k, w1v, w2v, table, ec):
    hd = ec.hd
    S0 = jnp.zeros((B, HG_HEADS, HG_DK, hg_gain.shape[0] // HG_HEADS), F32)
    y_hg, S = hgrn_mix(main, lb, hg_gain, S0, B, T, (ec.q, ec.f, ec.v, ec.g), HG_CHUNK, HGRN_SUB, HG_CHUNK)
    rows = kv_split(main, ec.kv, 6, NSA_KVH, hd)
    kcmp = compress_blocks(rows[0], B, T, NSA_KVH, pe, w1k, w2k)
    vcmp = compress_blocks(rows[1], B, T, NSA_KVH, pe, w1v, w2v)
    kc, vc, ks, vs, kw, vw = [r.reshape(B, T, NSA_KVH, hd) for r in rows]
    y_nsa = nsa_prompt(main, tail, kcmp, vcmp, table, B, T, (ec.nq, ec.kv, ec.tail_path, ec.tail_gate, hd))
    wb = min(WINDOW, T)
    return jnp.concatenate([y_hg, y_nsa], axis=-1), S, kc, vc, ks, vs, kw[:, T - wb:], vw[:, T - wb:]


def even_sample(main, tail, B, T, past_len, S0, page_table, pool_ck, pool_cv, pool_sk, pool_sv, win_k, win_v,
                lb, hg_gain, pe, w1k, w2k, w1v, w2v, table, ec):
    hd = ec.hd
    R = SAMPLE_ROWS
    y_hg, S = hgrn_mix(main, lb, hg_gain, S0.astype(F32), B, R, (ec.q, ec.f, ec.v, ec.g), R, R, T)
    m3 = main.reshape(B, R, -1)[:, :T]
    t3 = tail.reshape(B, R, -1)[:, :T]
    assert T == 1
    heads = lambda a, w: a.reshape(B, NSA_KVH, NSA_G, w)
    q = heads(m3[..., ec.nq:ec.kv], hd)
    path = heads(t3[..., ec.tail_path:ec.tail_path + ec.nsa_w], hd)
    gate = jnp.stack([t3[:, 0, ec.tail_gate + h * LANES:ec.tail_gate + h * LANES + ec.gate_w].reshape(B, NSA_G, 3)
                      for h in range(NSA_KVH)], axis=1)
    kvs = m3[..., ec.kv:ec.main_w].reshape(B, T, 6, NSA_KVH, hd)
    kc, vc, ks, vs, kw, vw = [kvs[:, :, j] for j in range(6)]
    new_kv = tuple(jnp.swapaxes(a, 1, 2) for a in (ks, vs, kw, vw))
    kcmp = compress_paged(pool_ck, page_table, pe, w1k, w2k)
    vcmp = compress_paged(pool_cv, page_table, pe, w1v, w2v)
    y_nsa = nsa_sample(q, kcmp, vcmp, pool_sk, pool_sv, page_table, new_kv, win_k, win_v, gate, path, table,
                       past_len)
    y_nsa = jnp.pad(y_nsa.reshape(B, T, -1).astype(BF16), ((0, 0), (0, R - T), (0, 0))).reshape(B * R, -1)
    kw_all = jnp.concatenate([win_k, kw], axis=1)
    vw_all = jnp.concatenate([win_v, vw], axis=1)
    return jnp.concatenate([y_hg, y_nsa], axis=-1), S, kc, vc, ks, vs, kw_all[:, T:], vw_all[:, T:]


def mem_attend(qg, B, T, mk, mv):
    slots = mk.shape[1]
    return mem_mix(qg, mk.reshape(B, slots, -1), mv.reshape(B, slots, -1), B, T)


def kernel(x_prompt, x_sample, mem_prompt, state_hgrn, cache_cmp_k, cache_cmp_v, cache_sel_k, cache_sel_v,
           cache_win_k, cache_win_v, state_ret, cache_mem_k, cache_mem_v, page_table, rel_table,
           norm_mix_pre, norm_mix_post, norm_mem_pre, norm_mem_post, ev_w_in, ev_w_out, hgrn_lb, hgrn_norm,
           cmp_pe, cmp_w1_k, cmp_w2_k, cmp_w1_v, cmp_w2_v, od_w_in, od_w_out, ret_norm,
           mem_w_q, mem_w_k, mem_w_v, mem_w_o):
    B, T, D = x_prompt.shape
    Bs, Ts, _ = x_sample.shape
    depth = norm_mix_pre.shape[0]
    past_len = page_table.shape[1] * PAGE_SIZE
    pos_p = jnp.arange(T, dtype=jnp.int32)
    pos_s = past_len + jnp.arange(Ts, dtype=jnp.int32)
    lb_all = jnp.cumsum(jax.nn.softmax(hgrn_lb.astype(F32), axis=0), axis=0)
    hg_qk = hgrn_lb.shape[1]
    hg_w = hgrn_norm.shape[1]
    nsa_w = ev_w_out.shape[1] - hg_w
    nsa_kvw = NSA_KVH * (nsa_w // NSA_HEADS)
    ec = EvenCols(hg_qk, hg_w, nsa_w, nsa_kvw)
    ev_w_in_t = jnp.swapaxes(ev_w_in, 1, 2)
    R = SAMPLE_ROWS
    assert Ts <= R

    even_p, even_s, ret_p, ret_s, memk_p, memv_p = [], [], [], [], [], []
    xp = x_prompt.reshape(B * T, D)
    xs = jnp.pad(x_sample, ((0, 0), (0, R - Ts), (0, 0))).reshape(Bs * R, D)
    pos_sr = past_len + jnp.arange(R, dtype=jnp.int32)
    mem2 = mem_prompt.reshape(-1, D).astype(BF16)
    ML = mem_prompt.shape[1]
    hp = rmsnorm_bf16(xp, norm_mix_pre[0])
    hs = rmsnorm_bf16(xs, norm_mix_pre[0])
    for l in range(depth):
        if l % 2 == 0:
            e = l // 2
            w = (lb_all[e], hgrn_norm[e], cmp_pe[e], cmp_w1_k[e], cmp_w2_k[e], cmp_w1_v[e], cmp_w2_v[e], rel_table)
            main_p, main_s = matmul(hp, ev_w_in_t, e, n_cols=ec.main_w, transposed=True, xs=hs)
            tail_p, tail_s = matmul(hp, ec.tail_weight(ev_w_in_t[e]), transposed=True, xs=hs)
            ap, *sp = even_prompt(main_p, tail_p, B, T, *w, ec)
            as_, *ss = even_sample(main_s, tail_s, Bs, Ts, past_len, state_hgrn[e], page_table, cache_cmp_k[e],
                                   cache_cmp_v[e], cache_sel_k[e], cache_sel_v[e], cache_win_k[e], cache_win_v[e],
                                   *w, ec)
            yp, ys = matmul(ap, ev_w_out, e, xs=as_)
            even_p.append(sp)
            even_s.append(ss)
        else:
            o = l // 2
            S0p = jnp.zeros((B,) + state_ret.shape[2:], F32)
            proj_p, proj_s = matmul(hp, od_w_in, o, xs=hs)
            ap, sp = retention_mix(proj_p, pos_p, ret_norm[o], S0p, B, T, RET_CHUNK, RET_CHUNK)
            as_, ss = retention_mix(proj_s, pos_sr, ret_norm[o], state_ret[o].astype(F32), Bs, R, R, Ts)
            yp, ys = matmul(ap, od_w_out, o, xs=as_)
            ret_p.append(sp)
            ret_s.append(ss)
        xp, hp = residual_post_pre(xp, yp, norm_mix_post[l], norm_mem_pre[l])
        xs, hs = residual_post_pre(xs, ys, norm_mix_post[l], norm_mem_pre[l])
        mk_p = matmul(mem2, mem_w_k, l).reshape(B, ML, MEM_HEADS, MEM_HD)
        mv_p = matmul(mem2, mem_w_v, l).reshape(B, ML, MEM_HEADS, MEM_HD)
        memk_p.append(mk_p)
        memv_p.append(mv_p)
        qg_p, qg_s = matmul(hp, mem_w_q, l, xs=hs)
        g_next = norm_mix_pre[l + 1] if l + 1 < depth else None
        xp, hp = matmul_residual(mem_attend(qg_p, B, T, mk_p, mv_p), mem_w_o, l, xp, norm_mem_post[l], g_next)
        xs, hs = matmul_residual(mem_attend(qg_s, Bs, R, cache_mem_k[l], cache_mem_v[l]), mem_w_o, l, xs,
                                 norm_mem_post[l], g_next)
    p_hgrn, p_cmp_k, p_cmp_v, p_sel_k, p_sel_v, p_win_k, p_win_v = [jnp.stack(a) for a in zip(*even_p)]
    s_hgrn, s_cmp_k, s_cmp_v, s_sel_k, s_sel_v, s_win_k, s_win_v = [jnp.stack(a) for a in zip(*even_s)]
    p_ret = jnp.stack(ret_p)
    s_ret = jnp.stack(ret_s)
    p_mem_k = jnp.stack(memk_p)
    p_mem_v = jnp.stack(memv_p)
    return (xp.reshape(B, T, D), xs.reshape(Bs, R, D)[:, :Ts], p_hgrn, p_cmp_k, p_cmp_v, p_sel_k, p_sel_v,
            p_win_k, p_win_v, p_ret, p_mem_k, p_mem_v,
            s_hgrn, s_cmp_k, s_cmp_v, s_sel_k, s_sel_v, s_win_k, s_win_v, s_ret)
```

```python
import functools
import math

import jax
import jax.numpy as jnp
import numpy as np
from jax import lax
from jax.experimental import pallas as pl
from jax.experimental.pallas import tpu as pltpu

F32 = jnp.float32
BF16 = jnp.bfloat16
EPS = 1e-6
NEG_INF = -1e30
LOG2E = math.log2(math.e)

PAGE_SIZE = 128
HG_HEADS = 16
HG_DK = 128
HG_CHUNK = 64
HGRN_SUB = 16
NSA_HEADS = 16
NSA_KVH = 2
NSA_G = NSA_HEADS // NSA_KVH
CMP_BLOCK = 32
SEL_BLOCK = 64
N_SELECT = 16
WINDOW = 512
SEL_QBLOCK = 64
WIN_QBLOCK = 128
NUM_BUCKETS = 32
MAX_DISTANCE = 1024
RET_HEADS = 16
RET_CHUNK = 128
ROPE_BASE = 10000.0
MEM_HEADS = 4
MEM_HD = 128
MEM_W = MEM_HEADS * MEM_HD

VMEM_LIMIT_BYTES = 56 * 1024 * 1024


def _params(*sem):
    return pltpu.CompilerParams(dimension_semantics=sem, vmem_limit_bytes=VMEM_LIMIT_BYTES)


def _rmsnorm_kernel(x_ref, g_ref, o_ref):
    x = x_ref[...]
    y = x * lax.rsqrt(jnp.mean(x * x, axis=-1, keepdims=True) + EPS)
    o_ref[...] = (y * g_ref[...]).astype(o_ref.dtype)


def rmsnorm_bf16(x, g):
    M, D = x.shape
    tm = min(M, 512)
    return pl.pallas_call(
        _rmsnorm_kernel,
        grid=(M // tm,),
        in_specs=[pl.BlockSpec((tm, D), lambda i: (i, 0)), pl.BlockSpec((1, D), lambda i: (0, 0))],
        out_specs=pl.BlockSpec((tm, D), lambda i: (i, 0)),
        out_shape=jax.ShapeDtypeStruct((M, D), BF16),
        compiler_params=_params("parallel"),
        name="rmsnorm_bf16",
    )(x, g.reshape(1, D))


def _post_pre_kernel(x_ref, y_ref, g_ref, gn_ref, o_ref, h_ref):
    y = y_ref[...]
    x = x_ref[...] + y * lax.rsqrt(jnp.mean(y * y, axis=-1, keepdims=True) + EPS) * g_ref[...]
    o_ref[...] = x
    h_ref[...] = (x * lax.rsqrt(jnp.mean(x * x, axis=-1, keepdims=True) + EPS) * gn_ref[...]).astype(h_ref.dtype)


def residual_post_pre(x, y, g, g_next):
    M, D = x.shape
    tm = min(M, 256)
    row = pl.BlockSpec((tm, D), lambda i: (i, 0))
    vec = pl.BlockSpec((1, D), lambda i: (0, 0))
    return pl.pallas_call(
        _post_pre_kernel,
        grid=(M // tm,),
        in_specs=[row, row, vec, vec],
        out_specs=[row, row],
        out_shape=[jax.ShapeDtypeStruct((M, D), F32), jax.ShapeDtypeStruct((M, D), BF16)],
        compiler_params=_params("parallel"),
        name="residual_post_pre",
    )(x, y, g.reshape(1, D), g_next.reshape(1, D))


def _matmul_post_kernel(*refs, has_next):
    if has_next:
        a_ref, w_ref, x_ref, g_ref, gn_ref, o_ref, h_ref, wb_ref = refs
    else:
        a_ref, w_ref, x_ref, g_ref, o_ref, wb_ref = refs

    @pl.when(pl.program_id(0) == 0)
    def _():
        wb_ref[...] = w_ref[...].astype(BF16)

    y = jnp.dot(a_ref[...], wb_ref[...], preferred_element_type=F32)
    x = x_ref[...] + y * lax.rsqrt(jnp.mean(y * y, axis=-1, keepdims=True) + EPS) * g_ref[...]
    o_ref[...] = x
    if has_next:
        h_ref[...] = (x * lax.rsqrt(jnp.mean(x * x, axis=-1, keepdims=True) + EPS) * gn_ref[...]).astype(h_ref.dtype)


def matmul_residual(a, w, layer, x, g, g_next=None):
    M, K = a.shape
    D = w.shape[-1]
    tm = min(M, 256)
    has_next = g_next is not None
    row = lambda width: pl.BlockSpec((tm, width), lambda i: (i, 0))
    vec = pl.BlockSpec((1, D), lambda i: (0, 0))
    in_specs = [row(K), pl.BlockSpec((None, K, D), lambda i: (layer, 0, 0)), row(D), vec]
    args = [a, w, x, g.reshape(1, D)]
    out_specs = [row(D)]
    out_shape = [jax.ShapeDtypeStruct((M, D), F32)]
    if has_next:
        in_specs.append(vec)
        args.append(g_next.reshape(1, D))
        out_specs.append(row(D))
        out_shape.append(jax.ShapeDtypeStruct((M, D), BF16))
    out = pl.pallas_call(
        functools.partial(_matmul_post_kernel, has_next=has_next),
        grid=(M // tm,),
        in_specs=in_specs,
        out_specs=out_specs,
        out_shape=out_shape,
        scratch_shapes=[pltpu.VMEM((K, D), BF16)],
        compiler_params=_params("arbitrary"),
        name="matmul_residual",
    )(*args)
    return out if has_next else (out[0], None)


MATMUL_KB = 4096
MATMUL_TN = 512
MATMUL_TM = 1024


def _matmul_kernel(*refs, transposed, paired, chained):
    n_in = 1 + paired
    xs_refs = refs[:n_in]
    w_ref = refs[n_in]
    acc_refs = refs[n_in + 1:n_in + 1 + n_in] if chained else (None,) * n_in
    o_refs = refs[-1 - n_in:-1]
    wb_ref = refs[-1]

    def emit(x_ref, acc_ref, o_ref):
        y = jnp.dot(x_ref[...], wb_ref[...], preferred_element_type=F32)
        o_ref[...] = y if acc_ref is None else acc_ref[...] + y

    @pl.when(pl.program_id(1) == 0)
    def _():
        w = w_ref[...]
        wb_ref[...] = (w.T if transposed else w).astype(BF16)
        if paired:
            emit(xs_refs[1], acc_refs[1], o_refs[1])

    emit(xs_refs[0], acc_refs[0], o_refs[0])


def matmul(x, w, layer=None, n_cols=None, transposed=False, xs=None):
    M, K = x.shape
    N = (w.shape[-2] if transposed else w.shape[-1]) if n_cols is None else n_cols
    kb = min(K, MATMUL_KB)
    tn, tm = min(MATMUL_TN, N), min(MATMUL_TM, M)
    assert N % tn == 0 and M % tm == 0 and K % kb == 0, (M, K, N)
    operands = [x] if xs is None else [x, xs]
    out = None
    for kk in range(K // kb):
        blk = (tn, kb) if transposed else (kb, tn)
        pick = (lambda n, kk=kk: (n, kk)) if transposed else (lambda n, kk=kk: (kk, n))
        if w.ndim == 3:
            w_spec = pl.BlockSpec((None,) + blk, lambda n, m, pick=pick: (layer,) + pick(n))
        else:
            w_spec = pl.BlockSpec(blk, lambda n, m, pick=pick: pick(n))
        in_specs = [pl.BlockSpec((tm, kb), lambda n, m, kk=kk: (m, kk))]
        out_specs = [pl.BlockSpec((tm, tn), lambda n, m: (m, n))]
        if xs is not None:
            in_specs.append(pl.BlockSpec((xs.shape[0], kb), lambda n, m, kk=kk: (0, kk)))
            out_specs.append(pl.BlockSpec((xs.shape[0], tn), lambda n, m: (0, n)))
        out = pl.pallas_call(
            functools.partial(_matmul_kernel, transposed=transposed, paired=xs is not None, chained=out is not None),
            grid=(N // tn, M // tm),
            in_specs=in_specs + [w_spec] + (out_specs if out is not None else []),
            out_specs=out_specs,
            out_shape=[jax.ShapeDtypeStruct((a.shape[0], N), F32) for a in operands],
            scratch_shapes=[pltpu.VMEM((kb, tn), BF16)],
            compiler_params=_params("parallel", "arbitrary"),
            name="matmul",
        )(*operands, w, *(out if out is not None else []))
    return out if xs is not None else out[0]


LANES = 128
NSA_TQ = 128
NSA_KC = 256
_NT = (((1,), (1,)), ((), ()))


_TN = (((0,), (0,)), ((), ()))


def _bias_tile_kernel(thr_ref, table_ref, o_ref, *, row_stride, row_offset):
    kvh = pl.program_id(0)
    step = pl.program_id(1)
    rows = o_ref.shape[2]
    row = lax.broadcasted_iota(jnp.int32, (rows, LANES), 0)
    lane = lax.broadcasted_iota(jnp.int32, (rows, LANES), 1)
    dist = LANES * step + lane - row_stride * row - row_offset
    for g in range(NSA_G):
        h = kvh * NSA_G + g
        bias = jnp.full((rows, LANES), table_ref[0, h], F32)
        for k in range(1, NUM_BUCKETS):
            bias = jnp.where(dist >= thr_ref[k], table_ref[k, h], bias)
        o_ref[0, 0, :, g * LANES:(g + 1) * LANES] = bias


def bias_tiles(table, n_steps, rows, max_dist, row_stride, row_offset):
    bucket = rel_bucket(jnp.arange(max_dist, dtype=jnp.int32))
    thr = jnp.sum(bucket[None, :] < jnp.arange(NUM_BUCKETS, dtype=jnp.int32)[:, None], axis=1).astype(jnp.int32)
    smem = pl.BlockSpec(memory_space=pltpu.SMEM)
    return pl.pallas_call(
        functools.partial(_bias_tile_kernel, row_stride=row_stride, row_offset=row_offset),
        grid=(NSA_KVH, n_steps),
        in_specs=[smem, smem],
        out_specs=pl.BlockSpec((1, 1, rows, NSA_G * LANES), lambda h, s: (h, s, 0, 0)),
        out_shape=jax.ShapeDtypeStruct((NSA_KVH, n_steps, rows, NSA_G * LANES), F32),
        compiler_params=_params("parallel", "parallel"),
        name="bias_tiles",
    )(thr, table)


def _nsa_prompt_kernel(q_ref, kc_ref, vc_ref, ks_ref, vs_ref, kw_ref, vw_ref, cb_ref, tb_ref, gate_ref, path_ref,
                       o_ref, qb_s, oc_s, m_s, l_s, acc_s, *, hd, n_win_chunks, n_pick):
    i = pl.program_id(2)
    tq = NSA_TQ
    G = NSA_G
    scale = hd ** -0.5 * LOG2E
    n_cb = kc_ref.shape[1]
    row_c = lax.broadcasted_iota(jnp.int32, (n_cb, tq), 0)
    qpos_c = i * tq + lax.broadcasted_iota(jnp.int32, (n_cb, tq), 1)
    row = lax.broadcasted_iota(jnp.int32, (NSA_KC, tq), 0)
    qpos = i * tq + lax.broadcasted_iota(jnp.int32, (NSA_KC, tq), 1)
    slab = lambda g: slice(g * tq, (g + 1) * tq)

    for g in range(G):
        qb_s[slab(g), :] = (q_ref[:, g * hd:(g + 1) * hd] * scale).astype(BF16)
    qb = qb_s[...]

    s_all = lax.dot_general(kc_ref[0].astype(BF16), qb, _NT, preferred_element_type=F32) + cb_ref[0, 0]
    valid_c = qpos_c >= (row_c + 1) * CMP_BLOCK - 1
    valid_cf = valid_c.astype(F32)
    imp = jnp.zeros((n_cb, tq), F32)
    ps = []
    for g in range(G):
        s = jnp.where(valid_c, s_all[:, slab(g)], NEG_INF)
        e = jnp.exp2(s - jnp.max(s, axis=0, keepdims=True))
        p = e * (1.0 / jnp.sum(e, axis=0, keepdims=True)) * valid_cf
        imp = imp + p
        ps.append(p.astype(BF16))
    oc_s[...] = lax.dot_general(vc_ref[0].astype(BF16), jnp.concatenate(ps, axis=1), _TN,
                                preferred_element_type=F32)

    imp2 = imp + pltpu.roll(imp, n_cb - 1, axis=0)
    blk = row_c >> 1
    cur = qpos_c // SEL_BLOCK
    forced = (blk == 0) | (blk == cur) | (blk == cur - 1)
    usable = ((row_c & 1) == 0) & (blk * SEL_BLOCK <= qpos_c)
    work = jnp.where(usable, jnp.where(forced, jnp.inf, imp2), -jnp.inf)
    sel = jnp.zeros((n_cb, tq), F32)
    for _ in range(n_pick):
        mx = jnp.max(work, axis=0, keepdims=True)
        first = jnp.min(jnp.where(work == mx, row_c, n_cb), axis=0, keepdims=True)
        pick = row_c == first
        sel = jnp.where(pick, 1.0, sel)
        work = jnp.where(pick, -jnp.inf, work)
    sel_b = sel.astype(BF16)

    m_s[...] = jnp.full(m_s.shape, NEG_INF, F32)
    l_s[...] = jnp.zeros(l_s.shape, F32)
    acc_s[...] = jnp.zeros(acc_s.shape, F32)
    ekey = lax.broadcasted_iota(jnp.int32, (NSA_KC, n_cb), 0)
    eblk = lax.broadcasted_iota(jnp.int32, (NSA_KC, n_cb), 1)
    tiles_per_chunk = NSA_KC // LANES

    def online(slot, k_b, v_b, bias, valid):
        s_all = lax.dot_general(k_b, qb, _NT, preferred_element_type=F32) + bias
        ps, alphas = [], []
        for g in range(G):
            s = jnp.where(valid, s_all[:, slab(g)], NEG_INF)
            m_old = m_s[slot, :, slab(g)]
            m_new = jnp.maximum(m_old, jnp.max(s, axis=0, keepdims=True))
            alpha = jnp.exp2(m_old - m_new)
            p = jnp.exp2(s - m_new)
            l_s[slot, :, slab(g)] = alpha * l_s[slot, :, slab(g)] + jnp.sum(p, axis=0, keepdims=True)
            m_s[slot, :, slab(g)] = m_new
            ps.append(p.astype(BF16))
            alphas.append(alpha)
        pv = lax.dot_general(v_b, jnp.concatenate(ps, axis=1), _TN, preferred_element_type=F32)
        acc_s[slot] = jnp.concatenate(alphas, axis=1) * acc_s[slot] + pv

    def chunk(c, carry):
        k0 = pl.multiple_of(c * NSA_KC, NSA_KC)
        delta = i - tiles_per_chunk * c
        dist = qpos - (k0 + row)
        causal = dist >= 0
        expand = (eblk == 2 * (c * (NSA_KC // SEL_BLOCK) + (ekey // SEL_BLOCK))).astype(BF16)
        chosen = jnp.dot(expand, sel_b, preferred_element_type=F32) > 0.5
        bias = jnp.concatenate([tb_ref[0, jnp.maximum(delta - t, 0)] for t in range(tiles_per_chunk)], axis=0)
        online(0, ks_ref[pl.ds(k0, NSA_KC), :].astype(BF16), vs_ref[pl.ds(k0, NSA_KC), :].astype(BF16), bias,
               chosen & causal)

        @pl.when(delta < n_win_chunks + tiles_per_chunk - 1)
        def _():
            online(1, kw_ref[pl.ds(k0, NSA_KC), :].astype(BF16), vw_ref[pl.ds(k0, NSA_KC), :].astype(BF16), bias,
                   causal & (dist < WINDOW))

        return carry

    lax.fori_loop(0, i // tiles_per_chunk + 1, chunk, 0)

    gates = jax.nn.sigmoid(gate_ref[...]).T
    for g in range(G):
        o_sel = acc_s[0, :, slab(g)] * (1.0 / l_s[0, :, slab(g)])
        o_win = acc_s[1, :, slab(g)] * (1.0 / l_s[1, :, slab(g)])
        o = (gates[3 * g:3 * g + 1] * oc_s[:, slab(g)] + gates[3 * g + 1:3 * g + 2] * o_sel
             + gates[3 * g + 2:3 * g + 3] * o_win)
        o_ref[:, g * hd:(g + 1) * hd] = (o.T * jax.nn.silu(path_ref[:, g * hd:(g + 1) * hd])).astype(o_ref.dtype)


def nsa_prompt(main, tail, kcmp, vcmp, table, B, T, cols):
    q_col, kv_col, path_col, gate_col, hd = cols
    G = NSA_G
    tq = NSA_TQ
    n_cb = kcmp.shape[1]
    assert n_cb % 8 == 0 and tq == LANES and T % NSA_KC == 0 and NSA_KC % LANES == 0 and hd == LANES
    nd = T // LANES
    n_win_chunks = WINDOW // LANES + 1
    table2 = table.astype(F32) * LOG2E
    tb = bias_tiles(table2, nd, LANES, T, 1, 0)
    cb = bias_tiles(table2, T // tq, n_cb, T, CMP_BLOCK, CMP_BLOCK - 1)
    nblk = T // tq
    gw = G * hd
    kv_spec = lambda j: pl.BlockSpec((T, hd), lambda b, h, i: (b, kv_col // hd + 2 * j + h))
    return pl.pallas_call(
        functools.partial(_nsa_prompt_kernel, hd=hd, n_win_chunks=n_win_chunks,
                          n_pick=min(N_SELECT, -(-T // SEL_BLOCK))),
        grid=(B, NSA_KVH, nblk),
        in_specs=[
            pl.BlockSpec((tq, gw), lambda b, h, i: (b * nblk + i, q_col // gw + h)),
            pl.BlockSpec((1, n_cb, hd), lambda b, h, i: (b, 0, h)),
            pl.BlockSpec((1, n_cb, hd), lambda b, h, i: (b, 0, h)),
            kv_spec(2), kv_spec(3), kv_spec(4), kv_spec(5),
            pl.BlockSpec((1, 1, n_cb, gw), lambda b, h, i: (h, i, 0, 0)),
            pl.BlockSpec((1, nd, LANES, gw), lambda b, h, i: (h, 0, 0, 0)),
            pl.BlockSpec((tq, LANES), lambda b, h, i: (b * nblk + i, gate_col // LANES + h)),
            pl.BlockSpec((tq, gw), lambda b, h, i: (b * nblk + i, path_col // gw + h)),
        ],
        out_specs=pl.BlockSpec((tq, gw), lambda b, h, i: (b * nblk + i, h)),
        out_shape=jax.ShapeDtypeStruct((B * T, NSA_KVH * gw), BF16),
        scratch_shapes=[pltpu.VMEM((G * tq, hd), BF16), pltpu.VMEM((hd, G * tq), F32),
                        pltpu.VMEM((2, 1, G * tq), F32), pltpu.VMEM((2, 1, G * tq), F32),
                        pltpu.VMEM((2, hd, G * tq), F32)],
        compiler_params=_params("parallel", "parallel", "arbitrary"),
        name="nsa_prompt",
    )(main, kcmp, vcmp, main, main, main, main, cb, tb, tail, tail)


_TN = (((0,), (0,)), ((), ()))


def _cumsum_rows(x, n):
    row = lax.broadcasted_iota(jnp.int32, x.shape, 0)
    sh = 1
    while sh < n:
        x = x + jnp.where(row >= sh, pltpu.roll(x, sh, axis=0), 0.0)
        sh *= 2
    return x


def _hgrn_kernel(q_ref, f_ref, v_ref, gate_ref, lb_ref, gain_ref, s0_ref, y_ref, s_out_ref, st_s,
                 *, C, SB, c_eff, dk, HB, n_sub):
    c = pl.program_id(2)

    @pl.when(c == 0)
    def _():
        for hh in range(HB):
            st_s[hh] = s0_ref[0, hh].T

    states = [st_s[hh] for hh in range(HB)]
    ys = [[None] * HB for _ in range(n_sub)]
    for hh in range(HB):
        for j in range(n_sub):
            ys[j][hh], states[hh] = _hgrn_head(slice(j * C, (j + 1) * C), slice(hh * dk, (hh + 1) * dk),
                                               states[hh], q_ref, f_ref, v_ref, gate_ref, lb_ref, gain_ref,
                                               C=C, SB=SB, c_eff=c_eff, dk=dk)
    y_ref[...] = jnp.concatenate([jnp.concatenate(row, axis=1) for row in ys], axis=0)
    for hh in range(HB):
        st_s[hh] = states[hh]

    @pl.when(c == pl.num_programs(2) - 1)
    def _():
        for hh in range(HB):
            s_out_ref[0, hh] = st_s[hh].T


def _hgrn_head(rows, sl, st, q_ref, f_ref, v_ref, gate_ref, lb_ref, gain_ref, *, C, SB, c_eff, dk):
    lb = lb_ref[:, sl]
    f = lb + (1.0 - lb) * jax.nn.sigmoid(f_ref[rows, sl])
    logf = jnp.log(f)
    k = 1.0 - f
    if c_eff < C:
        real = lax.broadcasted_iota(jnp.int32, (C, dk), 0) < c_eff
        logf = jnp.where(real, logf, 0.0)
        k = jnp.where(real, k, 0.0)
    q = q_ref[rows, sl] * dk ** -0.5
    v_b = v_ref[rows, sl].astype(BF16)
    b = _cumsum_rows(logf, C)
    o = lax.dot_general((q * jnp.exp(b)).astype(BF16), st.astype(BF16), _NT, preferred_element_type=F32)

    lane_c = lax.broadcasted_iota(jnp.int32, (SB, C), 1)
    row_c = lax.broadcasted_iota(jnp.int32, (SB, C), 0)
    outs = []
    for I in range(C // SB):
        r0 = I * SB
        q_i = q[r0:r0 + SB]
        b_i = b[r0:r0 + SB]
        if I > 0:
            b_r = b[r0 - 1:r0]
            qq = (q_i * jnp.exp(b_i - b_r)).astype(BF16)
            kk = (k * jnp.exp(jnp.minimum(b_r - b, 0.0))).astype(BF16)
            a = lax.dot_general(qq, kk, _NT, preferred_element_type=F32)
            a = jnp.where(lane_c < r0, a, 0.0)
        else:
            a = jnp.zeros((SB, C), F32)
        for s in range(SB):
            z = q_i * k[r0 + s:r0 + s + 1] * jnp.exp(jnp.minimum(b_i - b[r0 + s:r0 + s + 1], 0.0))
            col = jnp.sum(z, axis=-1, keepdims=True)
            a = jnp.where((lane_c == r0 + s) & (row_c >= s), col, a)
        outs.append(jnp.dot(a.astype(BF16), v_b, preferred_element_type=F32))
    o = o + jnp.concatenate(outs, axis=0)

    b_last = b[C - 1:C]
    kd = (k * jnp.exp(b_last - b)).astype(BF16)
    st_new = jnp.exp(b_last) * st + lax.dot_general(v_b, kd, _TN, preferred_element_type=F32)
    y = o * lax.rsqrt(jnp.mean(o * o, axis=-1, keepdims=True) + EPS) * gain_ref[:, sl]
    return (y * jax.nn.silu(gate_ref[rows, sl])).astype(BF16), st_new


HGRN_HEADS_PER_STEP = 8
HGRN_CHUNKS_PER_STEP = 8


def hgrn_mix(main, lb, gain, s0, B, T, cols, C, SB, c_eff):
    q_col, f_col, v_col, g_col = cols
    _, H, dk, dv = s0.shape
    HB = HGRN_HEADS_PER_STEP
    W = HB * LANES
    assert dk == LANES and dv == LANES and T % C == 0 and C % SB == 0 and H % HB == 0
    assert all(off % W == 0 for off in cols)
    n_sub = HGRN_CHUNKS_PER_STEP if T % (HGRN_CHUNKS_PER_STEP * C) == 0 else 1
    R = n_sub * C
    nc = T // R
    col = lambda off: pl.BlockSpec((R, W), lambda b, h, c: (b * nc + c, off // W + h))
    vec = pl.BlockSpec((1, W), lambda b, h, c: (0, h))
    st = pl.BlockSpec((1, HB, dk, dv), lambda b, h, c: (b, h, 0, 0))
    return pl.pallas_call(
        functools.partial(_hgrn_kernel, C=C, SB=SB, c_eff=c_eff, dk=dk, HB=HB, n_sub=n_sub),
        grid=(B, H // HB, nc),
        in_specs=[col(q_col), col(f_col), col(v_col), col(g_col), vec, vec, st],
        out_specs=[pl.BlockSpec((R, W), lambda b, h, c: (b * nc + c, h)), st],
        out_shape=[jax.ShapeDtypeStruct((B * T, H * dv), BF16), jax.ShapeDtypeStruct(s0.shape, F32)],
        scratch_shapes=[pltpu.VMEM((HB, dv, dk), F32)],
        compiler_params=_params("parallel", "parallel", "arbitrary"),
        name="hgrn_mix",
    )(main, main, main, main, lb.reshape(1, -1), gain.reshape(1, -1), s0)


def _retention_kernel(q_ref, k_ref, v_ref, g_ref, cos_ref, sin_ref, lg_ref, gain_ref, s0_ref, y_ref, s_out_ref, s_s,
                      *, C, c_eff, dk, dv, HB, n_sub):
    c = pl.program_id(2)

    @pl.when(c == 0)
    def _():
        s_s[...] = s0_ref[0]

    for hh in range(HB):
        state = s_s[hh]
        for j in range(n_sub):
            state = _retention_head(slice(j * C, (j + 1) * C), hh, state, q_ref, k_ref, v_ref, g_ref, cos_ref,
                                    sin_ref, lg_ref, gain_ref, y_ref, C=C, c_eff=c_eff, dk=dk, dv=dv)
        s_s[hh] = state

    @pl.when(c == pl.num_programs(2) - 1)
    def _():
        s_out_ref[0] = s_s[...]


def _retention_head(rows, hh, s_old, q_ref, k_ref, v_ref, g_ref, cos_ref, sin_ref, lg_ref, gain_ref, y_ref,
                    *, C, c_eff, dk, dv):
    half = dk // 2
    vsl = slice(hh * dv, (hh + 1) * dv)
    cos = cos_ref[rows]
    sin = sin_ref[rows]
    lg_w = lg_ref[hh]
    lg = lg_w[:, :LANES]
    row = lax.broadcasted_iota(jnp.int32, (C, LANES), 0).astype(F32)

    def rot(ref, w):
        x1 = ref[rows, hh * dk:hh * dk + half]
        x2 = ref[rows, hh * dk + half:(hh + 1) * dk]
        return jnp.concatenate([(x1 * cos - x2 * sin) * w, (x1 * sin + x2 * cos) * w], axis=1)

    q = rot(q_ref, dk ** -0.5)
    k = rot(k_ref, 1.0)
    v_b = v_ref[rows, vsl].astype(BF16)

    a = lax.dot_general(q.astype(BF16), k.astype(BF16), _NT, preferred_element_type=F32)
    ti = lax.broadcasted_iota(jnp.int32, (C, C), 0)
    si = lax.broadcasted_iota(jnp.int32, (C, C), 1)
    diff = (ti - si).astype(F32)
    a = a * jnp.where(diff >= 0, jnp.exp(diff * lg_w[:, :C]), 0.0)
    inner = jnp.dot(a.astype(BF16), v_b, preferred_element_type=F32)
    q_w = jnp.exp((row + 1.0) * lg)
    q_dec = q * jnp.concatenate([q_w] * (dk // LANES), axis=1)
    cross = jnp.dot(q_dec.astype(BF16), s_old.astype(BF16), preferred_element_type=F32)
    k_w = jnp.where(row < c_eff, jnp.exp((c_eff - 1.0 - row) * lg), 0.0)
    k_dec = k * jnp.concatenate([k_w] * (dk // LANES), axis=1)
    s_new = jnp.exp(c_eff * lg_w) * s_old + lax.dot_general(k_dec.astype(BF16), v_b, _TN, preferred_element_type=F32)
    o = inner + cross
    cen = o - jnp.mean(o, axis=-1, keepdims=True)
    y = cen * lax.rsqrt(jnp.mean(cen * cen, axis=-1, keepdims=True) + EPS) * gain_ref[:, vsl]
    y_ref[rows, vsl] = (y * jax.nn.silu(g_ref[rows, vsl])).astype(y_ref.dtype)
    return s_new


RETENTION_HEADS_PER_STEP = 4
RETENTION_CHUNKS_PER_STEP = 4


def retention_mix(proj, pos, gain, s0, B, T, C, c_eff):
    _, H, dk, dv = s0.shape
    HB = RETENTION_HEADS_PER_STEP
    assert T % C == 0 and C <= LANES and dk % LANES == 0 and H % HB == 0
    n_sub = RETENTION_CHUNKS_PER_STEP if T % (RETENTION_CHUNKS_PER_STEP * C) == 0 else 1
    R = n_sub * C
    nc = T // R
    nh = H // HB
    half = dk // 2
    inv = ROPE_BASE ** (-jnp.arange(half, dtype=F32) / half)
    ang = pos.astype(F32)[:, None] * inv[None, :]
    log_gamma = jnp.log1p(-jnp.exp2(-5.0 - jnp.arange(H, dtype=F32)))
    lg = jnp.broadcast_to(log_gamma[:, None, None], (H, 1, dv))
    qk = lambda j: pl.BlockSpec((R, HB * dk), lambda b, h, c: (b * nc + c, j * nh + h))
    vg = lambda j: pl.BlockSpec((R, HB * dv), lambda b, h, c: (b * nc + c, (2 * H * dk) // (HB * dv) + j * nh + h))
    tab = pl.BlockSpec((R, half), lambda b, h, c: (c, 0))
    st = pl.BlockSpec((1, HB, dk, dv), lambda b, h, c: (b, h, 0, 0))
    return pl.pallas_call(
        functools.partial(_retention_kernel, C=C, c_eff=c_eff, dk=dk, dv=dv, HB=HB, n_sub=n_sub),
        grid=(B, nh, nc),
        in_specs=[qk(0), qk(1), vg(0), vg(1), tab, tab,
                  pl.BlockSpec((HB, 1, dv), lambda b, h, c: (h, 0, 0)),
                  pl.BlockSpec((1, HB * dv), lambda b, h, c: (0, h)), st],
        out_specs=[pl.BlockSpec((R, HB * dv), lambda b, h, c: (b * nc + c, h)), st],
        out_shape=[jax.ShapeDtypeStruct((B * T, H * dv), BF16), jax.ShapeDtypeStruct(s0.shape, F32)],
        scratch_shapes=[pltpu.VMEM((HB, dk, dv), F32)],
        compiler_params=_params("parallel", "parallel", "arbitrary"),
        name="retention_mix",
    )(proj, proj, proj, proj, jnp.cos(ang), jnp.sin(ang), lg, gain.reshape(1, -1), s0)


def _mem_kernel(qg_ref, mk_ref, mv_ref, o_ref, *, heads, hd):
    for h in range(heads):
        sl = slice(h * hd, (h + 1) * hd)
        q = (qg_ref[:, sl] * hd ** -0.5).astype(BF16)
        s = lax.dot_general(q, mk_ref[0, :, sl].astype(BF16), _NT, preferred_element_type=F32)
        e = jnp.exp(s - jnp.max(s, axis=-1, keepdims=True))
        p = e * (1.0 / jnp.sum(e, axis=-1, keepdims=True))
        o = jnp.dot(p.astype(BF16), mv_ref[0, :, sl].astype(BF16), preferred_element_type=F32)
        gate = qg_ref[:, heads * hd + h * hd:heads * hd + (h + 1) * hd]
        o_ref[:, sl] = (o * jax.nn.silu(gate)).astype(o_ref.dtype)


def mem_mix(qg, mk, mv, B, T):
    W = mk.shape[2]
    tq = min(T, 256)
    nb = T // tq
    kv = pl.BlockSpec((1, mk.shape[1], W), lambda b, i: (b, 0, 0))
    return pl.pallas_call(
        functools.partial(_mem_kernel, heads=MEM_HEADS, hd=W // MEM_HEADS),
        grid=(B, nb),
        in_specs=[pl.BlockSpec((tq, 2 * W), lambda b, i: (b * nb + i, 0)), kv, kv],
        out_specs=pl.BlockSpec((tq, W), lambda b, i: (b * nb + i, 0)),
        out_shape=jax.ShapeDtypeStruct((B * T, W), BF16),
        compiler_params=_params("parallel", "arbitrary"),
        name="mem_mix",
    )(qg, mk, mv)


def _compress_rows(x_ref, n_blocks, pe_ref, w1_ref, w2_ref, kvh):
    stride = CMP_BLOCK * kvh
    outs = []
    for h in range(kvh):
        acc = jnp.zeros((n_blocks, w1_ref.shape[2]), F32)
        for j in range(0, CMP_BLOCK, 2):
            xa = x_ref[pl.ds(j * kvh + h, n_blocks, stride=stride), :] + pe_ref[j:j + 1]
            xb = x_ref[pl.ds((j + 1) * kvh + h, n_blocks, stride=stride), :] + pe_ref[j + 1:j + 2]
            x2 = jnp.concatenate([xa, xb], axis=1).astype(BF16)
            acc = acc + jnp.dot(x2, w1_ref[j // 2], preferred_element_type=F32)
        outs.append(jnp.dot(jax.nn.silu(acc).astype(BF16), w2_ref[...], preferred_element_type=F32))
    return jnp.concatenate(outs, axis=1)


def _compress_kernel(x_ref, pe_ref, w1_ref, w2_ref, o_ref, *, kvh):
    o_ref[...] = _compress_rows(x_ref, o_ref.shape[0], pe_ref, w1_ref, w2_ref, kvh)


def _compress_paged_kernel(pt_ref, pool_ref, pe_ref, w1_ref, w2_ref, o_ref, x_s, sem, *, n_pages, rows_per_page,
                           kvh):
    b = pl.program_id(0)

    def page_copy(p):
        return pltpu.make_async_copy(pool_ref.at[pt_ref[b * n_pages + p]],
                                     x_s.at[pl.ds(p * rows_per_page, rows_per_page)], sem.at[0])

    def start(p, carry):
        page_copy(p).start()
        return carry

    def wait(p, carry):
        page_copy(p).wait()
        return carry

    lax.fori_loop(0, n_pages, start, 0)
    lax.fori_loop(0, n_pages, wait, 0)
    o_ref[0] = _compress_rows(x_s, o_ref.shape[1], pe_ref, w1_ref, w2_ref, kvh)


def _compress_weights(pe, w1, w2, hd):
    return pe.astype(F32), w1.reshape(CMP_BLOCK // 2, 2 * hd, w1.shape[1]).astype(BF16), w2.astype(BF16)


def _kv_split_kernel(*refs, n, kvh, hd):
    for x_ref, o_ref in zip(refs[:n], refs[n:]):
        for h in range(kvh):
            o_ref[pl.ds(h, x_ref.shape[0], stride=kvh), :] = x_ref[:, h * hd:(h + 1) * hd]


def kv_split(main, col, n, kvh, hd):
    M = main.shape[0]
    tm = min(M, 512)
    w = kvh * hd
    assert col % w == 0 and M % tm == 0
    return pl.pallas_call(
        functools.partial(_kv_split_kernel, n=n, kvh=kvh, hd=hd),
        grid=(M // tm,),
        in_specs=[pl.BlockSpec((tm, w), functools.partial(lambda i, j: (i, col // w + j), j=j)) for j in range(n)],
        out_specs=[pl.BlockSpec((tm * kvh, hd), lambda i: (i, 0))] * n,
        out_shape=[jax.ShapeDtypeStruct((M * kvh, hd), F32)] * n,
        compiler_params=_params("parallel"),
        name="kv_split",
    )(*([main] * n))


def compress_blocks(x, B, L, kvh, pe, w1, w2):
    hd = x.shape[1]
    n = B * (L // CMP_BLOCK)
    assert hd == LANES and L % CMP_BLOCK == 0
    peb, w1b, w2b = _compress_weights(pe, w1, w2, hd)
    full = lambda a: pl.BlockSpec(a.shape, lambda i: (0,) * a.ndim)
    out = pl.pallas_call(
        functools.partial(_compress_kernel, kvh=kvh),
        grid=(1,),
        in_specs=[full(x), full(peb), full(w1b), full(w2b)],
        out_specs=pl.BlockSpec((n, kvh * hd), lambda i: (0, 0)),
        out_shape=jax.ShapeDtypeStruct((n, kvh * hd), F32),
        compiler_params=_params("arbitrary"),
        name="compress_blocks",
    )(x, peb, w1b, w2b)
    return out.reshape(B, L // CMP_BLOCK, kvh * hd)


def compress_paged(pool, page_table, pe, w1, w2):
    n_pool, page, kvh, hd = pool.shape
    B, NP = page_table.shape
    rpp = page * kvh
    n_blocks = NP * page // CMP_BLOCK
    assert hd == LANES and page % CMP_BLOCK == 0
    peb, w1b, w2b = _compress_weights(pe, w1, w2, hd)
    full = lambda a: pl.BlockSpec(a.shape, lambda b, pt: (0,) * a.ndim)
    return pl.pallas_call(
        functools.partial(_compress_paged_kernel, n_pages=NP, rows_per_page=rpp, kvh=kvh),
        grid_spec=pltpu.PrefetchScalarGridSpec(
            num_scalar_prefetch=1,
            grid=(B,),
            in_specs=[pl.BlockSpec(memory_space=pl.ANY), full(peb), full(w1b), full(w2b)],
            out_specs=pl.BlockSpec((1, n_blocks, kvh * hd), lambda b, pt: (b, 0, 0)),
            scratch_shapes=[pltpu.VMEM((NP * rpp, hd), F32), pltpu.SemaphoreType.DMA((1,))],
        ),
        out_shape=jax.ShapeDtypeStruct((B, n_blocks, kvh * hd), F32),
        compiler_params=_params("arbitrary"),
        name="compress_paged",
    )(page_table.reshape(-1), pool.reshape(n_pool, rpp, hd), peb, w1b, w2b)


def _bias_rows(dist, thr_ref, tab):
    bias = jnp.broadcast_to(tab[:, 0:1], dist.shape)
    for k in range(1, NUM_BUCKETS):
        bias = jnp.where(dist >= thr_ref[k], tab[:, k:k + 1], bias)
    return bias


def _sample_cmp_kernel(thr_ref, q_ref, kc_ref, vc_ref, tab_ref, oc_ref, idx_ref, *, qpos, n_pick, hd):
    n_cb = kc_ref.shape[1]
    qg = (q_ref[0, 0] * hd ** -0.5).astype(BF16)
    lane = lax.broadcasted_iota(jnp.int32, (NSA_G, n_cb), 1)
    dist = qpos - ((lane + 1) * CMP_BLOCK - 1)
    valid = dist >= 0
    s = lax.dot_general(qg, kc_ref[0].astype(BF16), _NT, preferred_element_type=F32)
    s = jnp.where(valid, s + _bias_rows(dist, thr_ref, tab_ref[0]), NEG_INF)
    e = jnp.exp(s - jnp.max(s, axis=-1, keepdims=True))
    p = e / jnp.sum(e, axis=-1, keepdims=True) * valid.astype(F32)
    oc_ref[0, 0] = jnp.dot(p.astype(BF16), vc_ref[0].astype(BF16), preferred_element_type=F32)

    imp = jnp.sum(p, axis=0, keepdims=True)
    imp2 = imp + pltpu.roll(imp, n_cb - 1, axis=1)
    lane1 = lax.broadcasted_iota(jnp.int32, (1, n_cb), 1)
    blk = lane1 >> 1
    cur = qpos // SEL_BLOCK
    forced = (blk == 0) | (blk == cur) | (blk == cur - 1)
    usable = ((lane1 & 1) == 0) & (blk * SEL_BLOCK <= qpos)
    work = jnp.where(usable, jnp.where(forced, jnp.inf, imp2), -jnp.inf)
    out_lane = lax.broadcasted_iota(jnp.int32, (1, LANES), 1)
    picks = jnp.zeros((1, LANES), jnp.int32)
    for it in range(n_pick):
        mx = jnp.max(work, axis=-1, keepdims=True)
        first = jnp.min(jnp.where(work == mx, lane1, n_cb), axis=-1, keepdims=True)
        picks = jnp.where(out_lane == it, first >> 1, picks)
        work = jnp.where(lane1 == first, -jnp.inf, work)
    idx_ref[0, 0] = picks


def _sample_attn_kernel(idx_ref, pt_ref, thr_ref, q_ref, ksn_ref, vsn_ref, kwin_ref, vwin_ref, kwn_ref, vwn_ref,
                        oc_ref, gate_ref, path_ref, tab_ref, *rest, qpos, hd, n_cached, n_pick, kvh):
    ksel_refs, vsel_refs, o_ref = rest[:n_pick], rest[n_pick:2 * n_pick], rest[2 * n_pick]
    b, h = pl.program_id(0), pl.program_id(1)
    G = NSA_G
    qg = (q_ref[0, 0] * hd ** -0.5).astype(BF16)
    qf = qg.astype(F32)
    tab = tab_ref[0]
    bias_new = _bias_rows(jnp.zeros((G, 1), jnp.int32), thr_ref, tab)

    def new_token_score(k_ref):
        return jnp.sum(qf * k_ref[0, 0].astype(BF16).astype(F32), axis=-1, keepdims=True) + bias_new

    def attend(k_tiles, v_tiles, dists, k_new_ref, v_new_ref, extra_valid):
        scores, valids = [], []
        for k_t, dist in zip(k_tiles, dists):
            row = lax.broadcasted_iota(jnp.int32, dist.shape, 1)
            valid = (row % kvh == h) & (dist >= 0) & extra_valid(dist)
            s = lax.dot_general(qg, k_t.astype(BF16), _NT, preferred_element_type=F32)
            scores.append(jnp.where(valid, s + _bias_rows(dist, thr_ref, tab), NEG_INF))
            valids.append(valid)
        s_new = new_token_score(k_new_ref)
        m = s_new
        for s in scores:
            m = jnp.maximum(m, jnp.max(s, axis=-1, keepdims=True))
        p_new = jnp.exp(s_new - m)
        l = p_new
        acc = p_new.astype(BF16).astype(F32) * v_new_ref[0, 0].astype(BF16).astype(F32)
        for s, valid, v_t in zip(scores, valids, v_tiles):
            p = jnp.where(valid, jnp.exp(s - m), 0.0)
            l = l + jnp.sum(p, axis=-1, keepdims=True)
            acc = acc + jnp.dot(p.astype(BF16), v_t.astype(BF16), preferred_element_type=F32)
        return acc / l

    rows = SEL_BLOCK * kvh
    tok = lax.broadcasted_iota(jnp.int32, (G, rows), 1) // kvh
    sel_dists = [qpos - (idx_ref[(b * kvh + h) * LANES + j] * SEL_BLOCK + tok) for j in range(n_pick)]
    o_sel = attend([r[...] for r in ksel_refs], [r[...] for r in vsel_refs], sel_dists, ksn_ref, vsn_ref,
                   lambda dist: dist >= 0)
    slot = lax.broadcasted_iota(jnp.int32, (G, n_cached * kvh), 1) // kvh
    wdist = n_cached - slot
    o_win = attend([kwin_ref[0]], [vwin_ref[0]], [wdist], kwn_ref, vwn_ref,
                   lambda dist: (dist < WINDOW) & (qpos - dist >= 0))
    gates = jax.nn.sigmoid(gate_ref[0, 0])
    o = gates[:, 0:1] * oc_ref[0, 0] + gates[:, 1:2] * o_sel + gates[:, 2:3] * o_win
    o_ref[0, 0] = o * jax.nn.silu(path_ref[0, 0])


def nsa_sample(q, kcmp, vcmp, pool_sk, pool_sv, page_table, new_kv, win_k, win_v, gate, path, table, past_len):
    B, KVH, G, hd = q.shape
    n_cb = kcmp.shape[1]
    n_pool, page = pool_sk.shape[:2]
    NP = page_table.shape[1]
    n_cached = win_k.shape[1]
    qpos = past_len
    n_sel = -(-(past_len + 1) // SEL_BLOCK)
    n_pick = min(N_SELECT, n_sel) - 1
    assert past_len % SEL_BLOCK == 0 and n_cb * CMP_BLOCK == past_len and 1 <= n_pick <= LANES
    bucket = rel_bucket(jnp.arange(past_len + 1, dtype=jnp.int32))
    thr = jnp.sum(bucket[None, :] < jnp.arange(NUM_BUCKETS, dtype=jnp.int32)[:, None], axis=1).astype(jnp.int32)
    tab = table.astype(F32).T.reshape(KVH, G, NUM_BUCKETS)
    head = lambda shape: pl.BlockSpec((1, 1) + shape, lambda b, h, *_: (b, h, 0, 0))
    cmp_spec = pl.BlockSpec((1, n_cb, hd), lambda b, h, *_: (b, 0, h))
    tab_spec = pl.BlockSpec((1, G, NUM_BUCKETS), lambda b, h, *_: (h, 0, 0))
    oc, idx = pl.pallas_call(
        functools.partial(_sample_cmp_kernel, qpos=qpos, n_pick=n_pick, hd=hd),
        grid_spec=pltpu.PrefetchScalarGridSpec(
            num_scalar_prefetch=1,
            grid=(B, KVH),
            in_specs=[head((G, hd)), cmp_spec, cmp_spec, tab_spec],
            out_specs=[head((G, hd)), head((1, LANES))],
        ),
        out_shape=[jax.ShapeDtypeStruct((B, KVH, G, hd), F32), jax.ShapeDtypeStruct((B, KVH, 1, LANES), jnp.int32)],
        compiler_params=_params("parallel", "parallel"),
        name="nsa_sample_cmp",
    )(thr, q, kcmp, vcmp, tab)

    halves = page // SEL_BLOCK
    rows = SEL_BLOCK * KVH

    def sel_spec(j):
        def index(b, h, idx_ref, pt_ref, thr_ref):
            blk = idx_ref[(b * KVH + h) * LANES + j]
            return (pt_ref[b * NP + blk // halves] * halves + blk % halves, 0)
        return pl.BlockSpec((rows, hd), index)

    win_spec = pl.BlockSpec((1, n_cached * KVH, hd), lambda b, h, *_: (b, 0, 0))
    new_spec = pl.BlockSpec((1, 1, 1, hd), lambda b, h, *_: (b, h, 0, 0))
    ks_n, vs_n, kw_n, vw_n = new_kv
    sel_k = pool_sk.reshape(n_pool * page * KVH, hd)
    sel_v = pool_sv.reshape(n_pool * page * KVH, hd)
    return pl.pallas_call(
        functools.partial(_sample_attn_kernel, qpos=qpos, hd=hd, n_cached=n_cached, n_pick=n_pick, kvh=KVH),
        grid_spec=pltpu.PrefetchScalarGridSpec(
            num_scalar_prefetch=3,
            grid=(B, KVH),
            in_specs=[head((G, hd)), new_spec, new_spec, win_spec, win_spec, new_spec, new_spec,
                      head((G, hd)), head((G, 3)), head((G, hd)), tab_spec]
                     + [sel_spec(j) for j in range(n_pick)] * 2,
            out_specs=head((G, hd)),
        ),
        out_shape=jax.ShapeDtypeStruct((B, KVH, G, hd), F32),
        compiler_params=_params("parallel", "parallel"),
        name="nsa_sample_attn",
    )(idx.reshape(-1), page_table.reshape(-1), thr, q, ks_n, vs_n, win_k.reshape(B, n_cached * KVH, hd),
      win_v.reshape(B, n_cached * KVH, hd), kw_n, vw_n, oc, gate, path, tab,
      *([sel_k] * n_pick), *([sel_v] * n_pick))


def rel_bucket(dist):
    d = jnp.maximum(dist, 0)
    me = NUM_BUCKETS // 2
    logd = jnp.log(jnp.maximum(d, 1).astype(F32) / me)
    large = me + (logd / math.log(MAX_DISTANCE / me) * (NUM_BUCKETS - me)).astype(jnp.int32)
    return jnp.where(d < me, d, jnp.minimum(large, NUM_BUCKETS - 1))


SAMPLE_ROWS = 16


class EvenCols:
    def __init__(self, hg_qk, hg_w, nsa_w, nsa_kvw):
        self.hd = nsa_w // NSA_HEADS
        self.kvw = nsa_kvw
        self.nsa_w = nsa_w
        self.q, self.f, self.v, self.g = 0, hg_qk, 2 * hg_qk, 2 * hg_qk + hg_w
        self.nq = 2 * hg_qk + 2 * hg_w
        self.kv = self.nq + nsa_w
        self.main_w = self.kv + 6 * nsa_kvw
        self.gate_w = 3 * NSA_G
        self.tail_path, self.tail_gate = 0, nsa_w
        used = nsa_w + NSA_KVH * LANES
        self.tail_w = -(-used // MATMUL_TN) * MATMUL_TN
        assert self.main_w % MATMUL_TN == 0

    def tail_weight(self, w_in_t):
        gate0 = self.main_w
        path0 = gate0 + NSA_KVH * self.gate_w
        parts = [w_in_t[path0:path0 + self.nsa_w]]
        for h in range(NSA_KVH):
            parts.append(jnp.pad(w_in_t[gate0 + h * self.gate_w:gate0 + (h + 1) * self.gate_w],
                                 ((0, LANES - self.gate_w), (0, 0))))
        w = jnp.concatenate(parts, axis=0)
        return jnp.pad(w, ((0, self.tail_w - w.shape[0]), (0, 0)))


def even_prompt(main, tail, B, T, lb, hg_gain, pe, w1k, w2k, w1v, w2v, table, ec):
    hd = ec.hd
    S0 = jnp.zeros((B, HG_HEADS, HG_DK, hg_gain.shape[0] // HG_HEADS), F32)
    y_hg, S = hgrn_mix(main, lb, hg_gain, S0, B, T, (ec.q, ec.f, ec.v, ec.g), HG_CHUNK, HGRN_SUB, HG_CHUNK)
    rows = kv_split(main, ec.kv, 6, NSA_KVH, hd)
    kcmp = compress_blocks(rows[0], B, T, NSA_KVH, pe, w1k, w2k)
    vcmp = compress_blocks(rows[1], B, T, NSA_KVH, pe, w1v, w2v)
    kc, vc, ks, vs, kw, vw = [r.reshape(B, T, NSA_KVH, hd) for r in rows]
    y_nsa = nsa_prompt(main, tail, kcmp, vcmp, table, B, T, (ec.nq, ec.kv, ec.tail_path, ec.tail_gate, hd))
    wb = min(WINDOW, T)
    return jnp.concatenate([y_hg, y_nsa], axis=-1), S, kc, vc, ks, vs, kw[:, T - wb:], vw[:, T - wb:]


def even_sample(main, tail, B, T, past_len, S0, page_table, pool_ck, pool_cv, pool_sk, pool_sv, win_k, win_v,
                lb, hg_gain, pe, w1k, w2k, w1v, w2v, table, ec):
    hd = ec.hd
    R = SAMPLE_ROWS
    y_hg, S = hgrn_mix(main, lb, hg_gain, S0.astype(F32), B, R, (ec.q, ec.f, ec.v, ec.g), R, R, T)
    m3 = main.reshape(B, R, -1)[:, :T]
    t3 = tail.reshape(B, R, -1)[:, :T]
    assert T == 1
    heads = lambda a, w: a.reshape(B, NSA_KVH, NSA_G, w)
    q = heads(m3[..., ec.nq:ec.kv], hd)
    path = heads(t3[..., ec.tail_path:ec.tail_path + ec.nsa_w], hd)
    gate = jnp.stack([t3[:, 0, ec.tail_gate + h * LANES:ec.tail_gate + h * LANES + ec.gate_w].reshape(B, NSA_G, 3)
                      for h in range(NSA_KVH)], axis=1)
    kvs = m3[..., ec.kv:ec.main_w].reshape(B, T, 6, NSA_KVH, hd)
    kc, vc, ks, vs, kw, vw = [kvs[:, :, j] for j in range(6)]
    new_kv = tuple(jnp.swapaxes(a, 1, 2) for a in (ks, vs, kw, vw))
    kcmp = compress_paged(pool_ck, page_table, pe, w1k, w2k)
    vcmp = compress_paged(pool_cv, page_table, pe, w1v, w2v)
    y_nsa = nsa_sample(q, kcmp, vcmp, pool_sk, pool_sv, page_table, new_kv, win_k, win_v, gate, path, table,
                       past_len)
    y_nsa = jnp.pad(y_nsa.reshape(B, T, -1).astype(BF16), ((0, 0), (0, R - T), (0, 0))).reshape(B * R, -1)
    kw_all = jnp.concatenate([win_k, kw], axis=1)
    vw_all = jnp.concatenate([win_v, vw], axis=1)
    return jnp.concatenate([y_hg, y_nsa], axis=-1), S, kc, vc, ks, vs, kw_all[:, T:], vw_all[:, T:]


def mem_attend(qg, B, T, mk, mv):
    slots = mk.shape[1]
    return mem_mix(qg, mk.reshape(B, slots, -1), mv.reshape(B, slots, -1), B, T)


def kernel(x_prompt, x_sample, mem_prompt, state_hgrn, cache_cmp_k, cache_cmp_v, cache_sel_k, cache_sel_v,
           cache_win_k, cache_win_v, state_ret, cache_mem_k, cache_mem_v, page_table, rel_table,
           norm_mix_pre, norm_mix_post, norm_mem_pre, norm_mem_post, ev_w_in, ev_w_out, hgrn_lb, hgrn_norm,
           cmp_pe, cmp_w1_k, cmp_w2_k, cmp_w1_v, cmp_w2_v, od_w_in, od_w_out, ret_norm,
           mem_w_q, mem_w_k, mem_w_v, mem_w_o):
    B, T, D = x_prompt.shape
    Bs, Ts, _ = x_sample.shape
    depth = norm_mix_pre.shape[0]
    past_len = page_table.shape[1] * PAGE_SIZE
    pos_p = jnp.arange(T, dtype=jnp.int32)
    pos_s = past_len + jnp.arange(Ts, dtype=jnp.int32)
    lb_all = jnp.cumsum(jax.nn.softmax(hgrn_lb.astype(F32), axis=0), axis=0)
    hg_qk = hgrn_lb.shape[1]
    hg_w = hgrn_norm.shape[1]
    nsa_w = ev_w_out.shape[1] - hg_w
    nsa_kvw = NSA_KVH * (nsa_w // NSA_HEADS)
    ec = EvenCols(hg_qk, hg_w, nsa_w, nsa_kvw)
    ev_w_in_t = jnp.swapaxes(ev_w_in, 1, 2)
    R = SAMPLE_ROWS
    assert Ts <= R

    even_p, even_s, ret_p, ret_s, memk_p, memv_p = [], [], [], [], [], []
    xp = x_prompt.reshape(B * T, D)
    xs = jnp.pad(x_sample, ((0, 0), (0, R - Ts), (0, 0))).reshape(Bs * R, D)
    pos_sr = past_len + jnp.arange(R, dtype=jnp.int32)
    mem2 = mem_prompt.reshape(-1, D).astype(BF16)
    ML = mem_prompt.shape[1]
    hp = rmsnorm_bf16(xp, norm_mix_pre[0])
    hs = rmsnorm_bf16(xs, norm_mix_pre[0])
    for l in range(depth):
        if l % 2 == 0:
            e = l // 2
            w = (lb_all[e], hgrn_norm[e], cmp_pe[e], cmp_w1_k[e], cmp_w2_k[e], cmp_w1_v[e], cmp_w2_v[e], rel_table)
            main_p, main_s = matmul(hp, ev_w_in_t, e, n_cols=ec.main_w, transposed=True, xs=hs)
            tail_p, tail_s = matmul(hp, ec.tail_weight(ev_w_in_t[e]), transposed=True, xs=hs)
            ap, *sp = even_prompt(main_p, tail_p, B, T, *w, ec)
            as_, *ss = even_sample(main_s, tail_s, Bs, Ts, past_len, state_hgrn[e], page_table, cache_cmp_k[e],
                                   cache_cmp_v[e], cache_sel_k[e], cache_sel_v[e], cache_win_k[e], cache_win_v[e],
                                   *w, ec)
            yp, ys = matmul(ap, ev_w_out, e, xs=as_)
            even_p.append(sp)
            even_s.append(ss)
        else:
            o = l // 2
            S0p = jnp.zeros((B,) + state_ret.shape[2:], F32)
            proj_p, proj_s = matmul(hp, od_w_in, o, xs=hs)
            ap, sp = retention_mix(proj_p, pos_p, ret_norm[o], S0p, B, T, RET_CHUNK, RET_CHUNK)
            as_, ss = retention_mix(proj_s, pos_sr, ret_norm[o], state_ret[o].astype(F32), Bs, R, R, Ts)
            yp, ys = matmul(ap, od_w_out, o, xs=as_)
            ret_p.append(sp)
            ret_s.append(ss)
        xp, hp = residual_post_pre(xp, yp, norm_mix_post[l], norm_mem_pre[l])
        xs, hs = residual_post_pre(xs, ys, norm_mix_post[l], norm_mem_pre[l])
        mk_p = matmul(mem2, mem_w_k, l).reshape(B, ML, MEM_HEADS, MEM_HD)
        mv_p = matmul(mem2, mem_w_v, l).reshape(B, ML, MEM_HEADS, MEM_HD)
        memk_p.append(mk_p)
        memv_p.append(mv_p)
        qg_p, qg_s = matmul(hp, mem_w_q, l, xs=hs)
        g_next = norm_mix_pre[l + 1] if l + 1 < depth else None
        xp, hp = matmul_residual(mem_attend(qg_p, B, T, mk_p, mv_p), mem_w_o, l, xp, norm_mem_post[l], g_next)
        xs, hs = matmul_residual(mem_attend(qg_s, Bs, R, cache_mem_k[l], cache_mem_v[l]), mem_w_o, l, xs,
                                 norm_mem_post[l], g_next)
    p_hgrn, p_cmp_k, p_cmp_v, p_sel_k, p_sel_v, p_win_k, p_win_v = [jnp.stack(a) for a in zip(*even_p)]
    s_hgrn, s_cmp_k, s_cmp_v, s_sel_k, s_sel_v, s_win_k, s_win_v = [jnp.stack(a) for a in zip(*even_s)]
    p_ret = jnp.stack(ret_p)
    s_ret = jnp.stack(ret_s)
    p_mem_k = jnp.stack(memk_p)
    p_mem_v = jnp.stack(memv_p)
    return (xp.reshape(B, T, D), xs.reshape(Bs, R, D)[:, :Ts], p_hgrn, p_cmp_k, p_cmp_v, p_sel_k, p_sel_v,
            p_win_k, p_win_v, p_ret, p_mem_k, p_mem_v,
            s_hgrn, s_cmp_k, s_cmp_v, s_sel_k, s_sel_v, s_win_k, s_win_v, s_ret)
```
